```python
import math
import jax, jax.numpy as jnp
from jax import lax
import numpy as np

D_MODEL = 1024
BATCH = 8
SEQ = 2048
DEPTH = 1
DEC_BATCH = 32
DEC_SEQ = 1
PAST_LEN = 16384
PAGE_SIZE = 128

HEAD_DIM = 64
ROT_DIM = HEAD_DIM // 4
ROPE_THETA = 500000.0
NSA_HEADS = 8
NSA_KV_HEADS = 2
NSA_GROUP = NSA_HEADS // NSA_KV_HEADS
NSA_BRANCHES = 3
NSA_BLOCK = 64
NSA_TOPK = 16
NSA_WINDOW = 512
NSA_WIDTH = NSA_HEADS * HEAD_DIM
SB_HEADS = 8
SB_WIDTH = SB_HEADS * HEAD_DIM
N_MERGE = 2
Q_BLOCK = 128
RMS_EPS = 1e-6
NEG_INF = -1e30
FORCE_SCORE = 1e3
D_IN = (NSA_WIDTH + 2 * NSA_BRANCHES * NSA_KV_HEADS * HEAD_DIM + NSA_WIDTH + NSA_BRANCHES * NSA_HEADS
        + 3 * SB_WIDTH + SB_WIDTH + N_MERGE * D_MODEL)

kernel_name = "nsa_stickbreaking_parallel_gated_decoder_step"


def _split_points():
    sizes = (NSA_WIDTH, 2 * NSA_BRANCHES * NSA_KV_HEADS * HEAD_DIM, NSA_WIDTH, NSA_BRANCHES * NSA_HEADS,
             3 * SB_WIDTH, SB_WIDTH, N_MERGE * D_MODEL)
    pts, acc = [], 0
    for s in sizes[:-1]:
        acc += s
        pts.append(acc)
    return pts


def rms_norm(x, g):
    xf = x.astype(jnp.float32)
    y = xf * lax.rsqrt(jnp.mean(xf * xf, axis=-1, keepdims=True) + RMS_EPS)
    return (y * g.astype(jnp.float32)).astype(x.dtype)


def partial_rope(x, pos):
    half = ROT_DIM // 2
    inv_freq = ROPE_THETA ** (-jnp.arange(half, dtype=jnp.float32) / half)
    ang = pos.astype(jnp.float32)[:, None] * inv_freq[None, :]
    shape = (1, pos.shape[0]) + (1,) * (x.ndim - 3) + (half,)
    cos = jnp.cos(ang).reshape(shape)
    sin = jnp.sin(ang).reshape(shape)
    xr = x[..., :ROT_DIM].astype(jnp.float32)
    x1, x2 = xr[..., :half], xr[..., half:]
    rot = jnp.concatenate([x1 * cos - x2 * sin, x2 * cos + x1 * sin], axis=-1).astype(x.dtype)
    return jnp.concatenate([rot, x[..., ROT_DIM:]], axis=-1)


def masked_softmax(s, mask, axis):
    s = jnp.where(mask, s.astype(jnp.float32), NEG_INF)
    return jax.nn.softmax(s, axis=axis) * mask


def map_query_blocks(fn, tq, *arrs):
    qb = min(Q_BLOCK, tq)
    nblk = -(-tq // qb)
    pad = nblk * qb - tq

    def prep(a):
        a = jnp.pad(a, [(0, 0), (0, pad)] + [(0, 0)] * (a.ndim - 2))
        a = a.reshape((a.shape[0], nblk, qb) + a.shape[2:])
        return jnp.moveaxis(a, 1, 0)

    blocks = [prep(a) for a in arrs]
    starts = jnp.arange(nblk, dtype=jnp.int32) * qb
    out = lax.map(lambda a: fn(*a), (starts, *blocks))
    out = jnp.moveaxis(out, 0, 1)
    out = out.reshape((out.shape[0], nblk * qb) + out.shape[3:])
    return out[:, :tq]


def gather_pages(pool, page_table):
    g = pool[page_table]
    return g.reshape((g.shape[0], g.shape[1] * g.shape[2]) + g.shape[3:])


def nsa_attention(q, k_cmp, v_cmp, k_slc, v_slc, k_win, v_win, gates, w_cmp, q_pos0, win_pos0):
    bsz, tq = q.shape[0], q.shape[1]
    tk = k_cmp.shape[1]
    scale = HEAD_DIM ** -0.5
    qg = q.reshape(bsz, tq, NSA_KV_HEADS, NSA_GROUP, HEAD_DIM)
    tpos = q_pos0 + jnp.arange(tq, dtype=jnp.int32)
    nb = -(-tk // NSA_BLOCK)
    kpad = nb * NSA_BLOCK - tk

    def to_blocks(a):
        a = jnp.pad(a, ((0, 0), (0, kpad), (0, 0), (0, 0)))
        return a.reshape(bsz, nb, NSA_BLOCK, NSA_KV_HEADS, HEAD_DIM)

    kc = jnp.einsum('bnlgd,l->bngd', to_blocks(k_cmp), w_cmp[0])
    vc = jnp.einsum('bnlgd,l->bngd', to_blocks(v_cmp), w_cmp[1])
    blk = jnp.arange(nb, dtype=jnp.int32)
    cmp_mask = ((blk + 1) * NSA_BLOCK - 1)[None, :] <= tpos[:, None]
    s_cmp = jnp.einsum('bqghd,bngd->bghqn', qg, kc) * scale
    p_cmp = masked_softmax(s_cmp, cmp_mask, -1)
    o_cmp = jnp.einsum('bghqn,bngd->bqghd', p_cmp.astype(vc.dtype), vc)

    importance = jnp.sum(p_cmp, axis=2)
    cur = tpos // NSA_BLOCK
    valid = blk[None, :] <= cur[:, None]
    forced = (blk[None, :] == 0) | (blk[None, :] == cur[:, None]) | (blk[None, :] == cur[:, None] - 1)
    score = jnp.where(valid, jnp.where(forced, FORCE_SCORE, importance), NEG_INF)
    n_sel = min(NSA_TOPK, nb)
    _, sel = lax.top_k(score, n_sel)
    sel_t = jnp.transpose(sel, (0, 2, 1, 3)).astype(jnp.int32)

    kb = jnp.moveaxis(to_blocks(k_slc), 3, 1)
    vb = jnp.moveaxis(to_blocks(v_slc), 3, 1)
    bi = jnp.arange(bsz)[:, None, None, None]
    gi = jnp.arange(NSA_KV_HEADS)[None, None, :, None]
    offs = jnp.arange(NSA_BLOCK, dtype=jnp.int32)

    tw = k_win.shape[1]
    qb = min(Q_BLOCK, tq)
    nblk = -(-tq // qb)
    span = NSA_WINDOW + qb
    end_pad = max(0, q_pos0 - win_pos0 + nblk * qb - tw)
    kw = jnp.pad(k_win, ((0, 0), (NSA_WINDOW, end_pad), (0, 0), (0, 0)))
    vw = jnp.pad(v_win, ((0, 0), (NSA_WINDOW, end_pad), (0, 0), (0, 0)))

    def block(start, q_blk, sel_blk):
        t = q_pos0 + start + jnp.arange(qb, dtype=jnp.int32)
        ks = kb[bi, gi, sel_blk]
        vs = vb[bi, gi, sel_blk]
        kpos = sel_blk[..., None] * NSA_BLOCK + offs
        m_slc = (kpos <= t[None, :, None, None, None]) & \
            (sel_blk <= (t // NSA_BLOCK)[None, :, None, None])[..., None]
        s = jnp.einsum('bqghd,bqgnld->bqghnl', q_blk, ks) * scale
        s = s.reshape(bsz, qb, NSA_KV_HEADS, NSA_GROUP, n_sel * NSA_BLOCK)
        p = masked_softmax(s, m_slc.reshape(bsz, qb, NSA_KV_HEADS, 1, n_sel * NSA_BLOCK), -1)
        o_slc = jnp.einsum('bqghm,bqgmd->bqghd', p.astype(vs.dtype),
                           vs.reshape(bsz, qb, NSA_KV_HEADS, n_sel * NSA_BLOCK, HEAD_DIM))
        st = q_pos0 + start - win_pos0
        kwb = lax.dynamic_slice_in_dim(kw, st, span, axis=1)
        vwb = lax.dynamic_slice_in_dim(vw, st, span, axis=1)
        wpos = q_pos0 + start - NSA_WINDOW + jnp.arange(span, dtype=jnp.int32)
        d = t[:, None] - wpos[None, :]
        m_win = (wpos[None, :] >= win_pos0) & (d >= 0) & (d < NSA_WINDOW)
        s = jnp.einsum('bqghd,bkgd->bqghk', q_blk, kwb) * scale
        p = masked_softmax(s, m_win[None, :, None, None, :], -1)
        o_win = jnp.einsum('bqghk,bkgd->bqghd', p.astype(vwb.dtype), vwb)
        return jnp.stack([o_slc, o_win], axis=2)

    o_sw = map_query_blocks(block, tq, qg, sel_t)
    g = gates.reshape(bsz, tq, NSA_BRANCHES, NSA_KV_HEADS, NSA_GROUP, 1)
    o = g[:, :, 0] * o_cmp + g[:, :, 1] * o_sw[:, :, 0] + g[:, :, 2] * o_sw[:, :, 1]
    return o.reshape(bsz, tq, NSA_WIDTH)


def stick_breaking_attention(q, k, v, q_pos0):
    tk = k.shape[1]
    kpos = jnp.arange(tk, dtype=jnp.int32)
    scale = HEAD_DIM ** -0.5

    def block(start, q_blk):
        n = q_blk.shape[1]
        tpos = q_pos0 + start + jnp.arange(n, dtype=jnp.int32)
        z = jnp.einsum('bqhd,bkhd->bhqk', q_blk, k).astype(jnp.float32) * scale
        mask = kpos[None, :] < tpos[:, None]
        log_keep = jnp.where(mask, jax.nn.log_sigmoid(-z), 0.0)
        after = lax.cumsum(log_keep, axis=3, reverse=True) - log_keep
        a = jnp.where(mask, jnp.exp(jax.nn.log_sigmoid(z) + after), 0.0)
        return jnp.einsum('bhqk,bkhd->bqhd', a.astype(v.dtype), v)

    return map_query_blocks(block, q.shape[1], q)


def hybrid_layer(x, pos0, nsa_past, win_past, sb_past, w_in, w_cmp, w_branch_a, w_branch_b, w_out, g_pre, g_post):
    bsz, t_new, _ = x.shape
    h = rms_norm(x, g_pre)
    proj = jnp.einsum('btd,de->bte', h, w_in)
    q_a, kv_a, z_a, gate_a, qkv_b, z_b, gate_m = jnp.split(proj, _split_points(), axis=-1)
    pos = pos0 + jnp.arange(t_new, dtype=jnp.int32)

    q_a = partial_rope(q_a.reshape(bsz, t_new, NSA_HEADS, HEAD_DIM), pos)
    kv_a = kv_a.reshape(bsz, t_new, NSA_BRANCHES, 2, NSA_KV_HEADS, HEAD_DIM)
    k_rot = partial_rope(kv_a[:, :, :, 0], pos)
    kv_a = jnp.stack([k_rot, kv_a[:, :, :, 1]], axis=3).reshape(
        bsz, t_new, 2 * NSA_BRANCHES, NSA_KV_HEADS, HEAD_DIM)
    nsa_new = kv_a[:, :, :4]
    win_new = kv_a[:, :, 4:]
    nsa_full = nsa_new if nsa_past is None else jnp.concatenate([nsa_past, nsa_new], axis=1)
    win_all = win_new if win_past is None else jnp.concatenate([win_past, win_new], axis=1)
    win_pos0 = pos0 - (win_all.shape[1] - t_new)
    gates_a = jax.nn.sigmoid(gate_a).reshape(bsz, t_new, NSA_BRANCHES, NSA_HEADS)
    o_a = nsa_attention(q_a, nsa_full[:, :, 0], nsa_full[:, :, 1], nsa_full[:, :, 2], nsa_full[:, :, 3],
                        win_all[:, :, 0], win_all[:, :, 1], gates_a, w_cmp, pos0, win_pos0)

    qkv_b = qkv_b.reshape(bsz, t_new, 3, SB_HEADS, HEAD_DIM)
    sb_new = qkv_b[:, :, 1:]
    sb_full = sb_new if sb_past is None else jnp.concatenate([sb_past, sb_new], axis=1)
    o_b = stick_breaking_attention(qkv_b[:, :, 0], sb_full[:, :, 0], sb_full[:, :, 1], pos0)
    o_b = o_b.reshape(bsz, t_new, SB_WIDTH)

    y_a = jnp.einsum('bte,ed->btd', o_a * jax.nn.silu(z_a), w_branch_a)
    y_b = jnp.einsum('bte,ed->btd', o_b * jax.nn.silu(z_b), w_branch_b)
    g_m = jax.nn.sigmoid(gate_m).reshape(bsz, t_new, N_MERGE, D_MODEL)
    mixed = g_m[:, :, 0] * y_a + g_m[:, :, 1] * y_b
    out = rms_norm(jnp.einsum('btd,de->bte', mixed, w_out), g_post)
    win_keep = min(NSA_WINDOW, win_all.shape[1])
    return x + out, nsa_new, win_all[:, win_all.shape[1] - win_keep:], sb_new


def setup_inputs(seed: int = 0) -> dict:
    key = jax.random.key(seed)
    ks = jax.random.split(key, 16)
    n_pages = PAST_LEN // PAGE_SIZE
    n_used = DEC_BATCH * n_pages
    n_phys = n_used + (n_used + 3) // 4
    w_buf = min(NSA_WINDOW, PAST_LEN)
    nrm = jax.random.normal
    f32 = jnp.float32
    x_prompt = nrm(ks[0], (BATCH, SEQ, D_MODEL), f32)
    x_sample = nrm(ks[1], (DEC_BATCH, DEC_SEQ, D_MODEL), f32)
    cache_nsa_kv = nrm(ks[2], (DEPTH, n_phys, PAGE_SIZE, 4, NSA_KV_HEADS, HEAD_DIM), f32)
    cache_nsa_win_kv = nrm(ks[3], (DEPTH, DEC_BATCH, w_buf, 2, NSA_KV_HEADS, HEAD_DIM), f32)
    cache_sb_kv = nrm(ks[4], (DEPTH, n_phys, PAGE_SIZE, 2, SB_HEADS, HEAD_DIM), f32)
    page_table = jax.random.permutation(ks[5], n_phys)[:n_used].reshape(DEC_BATCH, n_pages).astype(jnp.int32)
    w_in = nrm(ks[6], (DEPTH, D_MODEL, D_IN), f32) * D_MODEL ** -0.5
    w_cmp = (1.0 + 0.1 * nrm(ks[7], (DEPTH, 2, NSA_BLOCK), f32)) / NSA_BLOCK
    w_branch_a = nrm(ks[8], (DEPTH, NSA_WIDTH, D_MODEL), f32) * NSA_WIDTH ** -0.5
    w_branch_b = nrm(ks[9], (DEPTH, SB_WIDTH, D_MODEL), f32) * SB_WIDTH ** -0.5
    w_out = nrm(ks[10], (DEPTH, D_MODEL, D_MODEL), f32) * D_MODEL ** -0.5
    g_pre = 1.0 + 0.1 * nrm(ks[11], (DEPTH, D_MODEL), f32)
    g_post = 1.0 + 0.1 * nrm(ks[12], (DEPTH, D_MODEL), f32)
    return {"x_prompt": x_prompt, "x_sample": x_sample, "cache_nsa_kv": cache_nsa_kv,
            "cache_nsa_win_kv": cache_nsa_win_kv, "cache_sb_kv": cache_sb_kv, "page_table": page_table,
            "w_in": w_in, "w_cmp": w_cmp, "w_branch_a": w_branch_a, "w_branch_b": w_branch_b,
            "w_out": w_out, "g_pre": g_pre, "g_post": g_post}


def reference(x_prompt, x_sample, cache_nsa_kv, cache_nsa_win_kv, cache_sb_kv, page_table,
              w_in, w_cmp, w_branch_a, w_branch_b, w_out, g_pre, g_post):
    past_len = page_table.shape[1] * PAGE_SIZE
    hp, hs = x_prompt, x_sample
    nsa_p, win_p, sb_p, nsa_s, win_s, sb_s = [], [], [], [], [], []
    for layer in range(DEPTH):
        params = (w_in[layer], w_cmp[layer], w_branch_a[layer], w_branch_b[layer],
                  w_out[layer], g_pre[layer], g_post[layer])
        hp, a, b, c = hybrid_layer(hp, 0, None, None, None, *params)
        nsa_past = gather_pages(cache_nsa_kv[layer], page_table)
        sb_past = gather_pages(cache_sb_kv[layer], page_table)
        hs, d, e, f = hybrid_layer(hs, past_len, nsa_past, cache_nsa_win_kv[layer], sb_past, *params)
        nsa_p.append(a); win_p.append(b); sb_p.append(c)
        nsa_s.append(d); win_s.append(e); sb_s.append(f)
    return (hp, hs, jnp.stack(nsa_p), jnp.stack(win_p), jnp.stack(sb_p),
            jnp.stack(nsa_s), jnp.stack(win_s), jnp.stack(sb_s))
```

```python
import functools

import jax
import jax.numpy as jnp
from jax import lax
from jax.experimental import pallas as pl
from jax.experimental.pallas import tpu as pltpu

HEAD_DIM = 64
ROT_DIM = HEAD_DIM // 4
ROPE_THETA = 500000.0
NSA_HEADS = 8
NSA_KV_HEADS = 2
NSA_GROUP = NSA_HEADS // NSA_KV_HEADS
NSA_BRANCHES = 3
NSA_BLOCK = 64
NSA_TOPK = 16
NSA_WINDOW = 512
NSA_WIDTH = NSA_HEADS * HEAD_DIM
SB_HEADS = 8
SB_WIDTH = SB_HEADS * HEAD_DIM
N_MERGE = 2
PAGE_SIZE = 128
RMS_EPS = 1e-6
NEG_INF = -1e30
FORCE_SCORE = 1e3
SCALE = HEAD_DIM ** -0.5

LANES = 128
SUBLANES = 8
KV_PAIR = NSA_KV_HEADS * HEAD_DIM
assert KV_PAIR == LANES and PAGE_SIZE == LANES and PAGE_SIZE % NSA_BLOCK == 0
KVA_WIDTH = 2 * NSA_BRANCHES * KV_PAIR
GATE_A = NSA_BRANCHES * NSA_HEADS
BLOCKS_PER_PAGE = PAGE_SIZE // NSA_BLOCK
VMEM_LIMIT = 56 * 1024 * 1024

C_QA = 0
C_KVA = C_QA + NSA_WIDTH
C_ZA = C_KVA + KVA_WIDTH
C_QKVB = C_ZA + NSA_WIDTH
C_ZB = C_QKVB + 3 * SB_WIDTH
C_GM = C_ZB + SB_WIDTH

BF16 = jnp.bfloat16
F32 = jnp.float32


def _dot(a, b):
    return jnp.dot(a, b, preferred_element_type=F32)


def _dot_tn(a, b):
    return lax.dot_general(a, b, (((0,), (0,)), ((), ())), preferred_element_type=F32)


def _dot_nt(a, b):
    return lax.dot_general(a, b, (((1,), (1,)), ((), ())), preferred_element_type=F32)


def _sigmoid(x):
    return 1.0 / (1.0 + jnp.exp(-x))


def _rms_scale(x, g):
    return x * lax.rsqrt(jnp.mean(x * x, axis=-1, keepdims=True) + RMS_EPS) * g


def _rope_rows(v, cos, sin):
    half = ROT_DIM // 2
    parts = []
    for base in range(0, v.shape[0], HEAD_DIM):
        x1, x2 = v[base:base + half], v[base + half:base + 2 * half]
        parts += [x1 * cos - x2 * sin, x2 * cos + x1 * sin, v[base + 2 * half:base + HEAD_DIM]]
    return jnp.concatenate(parts, axis=0)


def _proj_prompt_kernel(x_ref, g_ref, wt_ref, cos_ref, sin_ref, wp_ref,
                        nsa_ref, win_ref, sb_ref, za_ref, zb_ref, gm_ref, qa_ref, ga_ref, kb_ref, vt_ref,
                        pool_ref, qb_ref, kbb_ref, vbt_ref, *, d_model):
    hb = _rms_scale(x_ref[...], g_ref[...]).astype(BF16)
    cos, sin = cos_ref[...], sin_ref[...]

    def seg(lo, width):
        return _dot_nt(wt_ref[lo:lo + width, :], hb)

    qa_ref[0] = (_rope_rows(seg(C_QA, NSA_WIDTH), cos, sin) * SCALE).astype(BF16)
    kv = seg(C_KVA, KVA_WIDTH)
    cmp_k = _rope_rows(kv[0 * LANES:1 * LANES], cos, sin)
    cmp_v = kv[1 * LANES:2 * LANES]
    slc_k = _rope_rows(kv[2 * LANES:3 * LANES], cos, sin)
    slc_v = kv[3 * LANES:4 * LANES]
    win_k = _rope_rows(kv[4 * LANES:5 * LANES], cos, sin)
    win_v = kv[5 * LANES:6 * LANES]
    nsa_ref[0] = jnp.concatenate([cmp_k, cmp_v, slc_k, slc_v], axis=0)
    win_ref[0] = jnp.concatenate([win_k, win_v], axis=0)
    kb_ref[...] = jnp.concatenate([slc_k.T, win_k.T], axis=1).astype(BF16)
    vt_ref[0] = jnp.concatenate([slc_v, win_v], axis=0).astype(BF16)
    wp = wp_ref[...]
    nblk = pool_ref.shape[1]
    pooled = jnp.concatenate([_dot_nt(wp[:SUBLANES], cmp_k.astype(BF16)),
                              _dot_nt(wp[SUBLANES:], cmp_v.astype(BF16))], axis=1)
    pool_ref[0] = pooled[:nblk]

    za = seg(C_ZA, NSA_WIDTH)
    za_ref[0] = za * _sigmoid(za)
    qkvb = seg(C_QKVB, 3 * SB_WIDTH)
    qb_ref[0] = (qkvb[:SB_WIDTH] * SCALE).astype(BF16)
    sb_ref[0] = qkvb[SB_WIDTH:]
    kbb_ref[...] = qkvb[SB_WIDTH:2 * SB_WIDTH].T.astype(BF16)
    vbt_ref[0] = qkvb[2 * SB_WIDTH:].astype(BF16)
    zb = seg(C_ZB, SB_WIDTH)
    zb_ref[0] = zb * _sigmoid(zb)
    gm_ref[0] = _sigmoid(seg(C_GM, N_MERGE * d_model))
    ga_ref[0] = _sigmoid(seg(C_GM + N_MERGE * d_model, LANES))


def _project_prompt(x2d, g_pre, wt_perm, cos_t, sin_t, wp, *, batch, seq, tm):
    m, d_model = x2d.shape
    nt = seq // tm
    n_rows = wt_perm.shape[0]
    nblk = tm // NSA_BLOCK
    row = lambda i: (i, 0)
    full = lambda i: (0, 0)
    tab = lambda i: (0, i % nt)
    tr = lambda i: (i // nt, 0, i % nt)
    half = ROT_DIM // 2
    in_specs = [pl.BlockSpec((tm, d_model), row), pl.BlockSpec((1, d_model), full),
                pl.BlockSpec((n_rows, d_model), full),
                pl.BlockSpec((half, tm), tab), pl.BlockSpec((half, tm), tab),
                pl.BlockSpec((2 * SUBLANES, tm), full)]
    sds = jax.ShapeDtypeStruct

    def feat(width, dtype):
        return sds((batch, width, seq), dtype), pl.BlockSpec((1, width, tm), tr)

    def rows(width, dtype):
        return sds((m, width), dtype), pl.BlockSpec((tm, width), row)

    outs = [feat(4 * KV_PAIR, F32), feat(2 * KV_PAIR, F32), feat(2 * SB_WIDTH, F32),
            feat(NSA_WIDTH, F32), feat(SB_WIDTH, F32), feat(N_MERGE * d_model, F32),
            feat(NSA_WIDTH, BF16), feat(LANES, F32), rows(2 * KV_PAIR, BF16), feat(2 * KV_PAIR, BF16),
            (sds((m // tm, nblk, 2 * KV_PAIR), F32), pl.BlockSpec((1, nblk, 2 * KV_PAIR), lambda i: (i, 0, 0))),
            feat(SB_WIDTH, BF16), rows(SB_WIDTH, BF16), feat(SB_WIDTH, BF16)]
    return pl.pallas_call(
        functools.partial(_proj_prompt_kernel, d_model=d_model),
        grid=(m // tm,), in_specs=in_specs, out_specs=[o[1] for o in outs], out_shape=[o[0] for o in outs],
        compiler_params=pltpu.CompilerParams(dimension_semantics=("arbitrary",), vmem_limit_bytes=VMEM_LIMIT),
        name="proj_prompt",
    )(x2d, g_pre, wt_perm, cos_t, sin_t, wp)


def _rope_lanes(v, c, s_up, s_dn):
    half = ROT_DIM // 2
    outs = []
    for j in range(v.shape[1] // LANES):
        blk = v[:, j * LANES:(j + 1) * LANES]
        outs.append(blk * c + pltpu.roll(blk, LANES - half, 1) * s_up + pltpu.roll(blk, half, 1) * s_dn)
    return outs[0] if len(outs) == 1 else jnp.concatenate(outs, axis=1)


def _proj_sample_kernel(x_ref, g_ref, wt_ref, cos_ref, sup_ref, sdn_ref,
                        nsa_ref, win_ref, sbkv_ref, za_ref, zb_ref, gm_ref, qa_ref, ga_ref, qb_ref, *, d_model):
    hb = _rms_scale(x_ref[...], g_ref[...]).astype(BF16)
    cos, s_up, s_dn = cos_ref[...], sup_ref[...], sdn_ref[...]

    def seg(lo, width):
        return _dot_nt(hb, wt_ref[lo:lo + width, :])

    qa_ref[...] = _rope_lanes(seg(C_QA, NSA_WIDTH), cos, s_up, s_dn) * SCALE
    kv = seg(C_KVA, KVA_WIDTH)
    parts = []
    for j in range(2 * NSA_BRANCHES):
        blk = kv[:, j * LANES:(j + 1) * LANES]
        parts.append(_rope_lanes(blk, cos, s_up, s_dn) if j % 2 == 0 else blk)
    nsa_ref[...] = jnp.concatenate(parts[:4], axis=1)
    win_ref[...] = jnp.concatenate(parts[4:], axis=1)
    za = seg(C_ZA, NSA_WIDTH)
    za_ref[...] = za * _sigmoid(za)
    qkvb = seg(C_QKVB, 3 * SB_WIDTH)
    qb_ref[...] = qkvb[:, :SB_WIDTH] * SCALE
    sbkv_ref[...] = qkvb[:, SB_WIDTH:]
    zb = seg(C_ZB, SB_WIDTH)
    zb_ref[...] = zb * _sigmoid(zb)
    gm_ref[...] = _sigmoid(seg(C_GM, N_MERGE * d_model))
    ga_ref[...] = _sigmoid(seg(C_GM + N_MERGE * d_model, LANES))


def _project_sample(x2d, g_pre, wt_perm, tables):
    m, d_model = x2d.shape
    full = lambda i: (0, 0)
    widths = [4 * KV_PAIR, 2 * KV_PAIR, 2 * SB_WIDTH, NSA_WIDTH, SB_WIDTH, N_MERGE * d_model,
              NSA_WIDTH, LANES, SB_WIDTH]
    return pl.pallas_call(
        functools.partial(_proj_sample_kernel, d_model=d_model),
        grid=(1,),
        in_specs=[pl.BlockSpec((m, d_model), full), pl.BlockSpec((1, d_model), full),
                  pl.BlockSpec(wt_perm.shape, full)] + [pl.BlockSpec((m, LANES), full)] * 3,
        out_specs=[pl.BlockSpec((m, w), full) for w in widths],
        out_shape=[jax.ShapeDtypeStruct((m, w), F32) for w in widths],
        compiler_params=pltpu.CompilerParams(dimension_semantics=("arbitrary",), vmem_limit_bytes=VMEM_LIMIT),
        name="proj_sample",
    )(x2d, g_pre, wt_perm, *tables)


def _rope_angles(pos):
    half = ROT_DIM // 2
    inv_freq = ROPE_THETA ** (-jnp.arange(half, dtype=F32) / half)
    ang = pos.astype(F32)[:, None] * inv_freq[None, :]
    return jnp.cos(ang), jnp.sin(ang)


def _rope_lane_tables(pos):
    cos, sin = _rope_angles(pos)
    n = pos.shape[0]
    half = ROT_DIM // 2
    ones = jnp.ones((n, HEAD_DIM - ROT_DIM), F32)
    zeros = jnp.zeros((n, HEAD_DIM - ROT_DIM), F32)
    zh = jnp.zeros((n, half), F32)
    rep = LANES // HEAD_DIM
    return tuple(jnp.tile(t, (1, rep)) for t in (jnp.concatenate([cos, cos, ones], axis=1),
                                                 jnp.concatenate([-sin, zh, zeros], axis=1),
                                                 jnp.concatenate([zh, sin, zeros], axis=1)))


def _permute_w_in_t(w_in, d_model):
    c = [NSA_WIDTH, KVA_WIDTH, NSA_WIDTH, GATE_A, 3 * SB_WIDTH, SB_WIDTH, N_MERGE * d_model]
    o = [0]
    for s in c:
        o.append(o[-1] + s)
    wt = w_in.T
    pad = jnp.zeros((LANES - GATE_A, d_model), w_in.dtype)
    return jnp.concatenate([wt[:o[3]], wt[o[4]:], wt[o[3]:o[4]], pad], axis=0).astype(BF16)


def _pool_weights(w_cmp, n_blocks, rows):
    eye = jnp.eye(rows, n_blocks, dtype=w_cmp.dtype)
    return (eye[None, :, :, None] * w_cmp[:, None, None, :]).reshape(2, rows, n_blocks * NSA_BLOCK)


def _nsa_prompt_kernel(q_ref, kslc_ref, kwin_ref, vslc_ref, vwin_ref, pool_ref, ga_ref, o_ref, sel_ref, *, tq, tk, nb):
    g = pl.program_id(1)
    i = pl.program_id(2)
    lanes = NSA_GROUP * tq
    qt = q_ref[0]
    q4 = jnp.concatenate([qt[hh * HEAD_DIM:(hh + 1) * HEAD_DIM] for hh in range(NSA_GROUP)], axis=1)
    row = lax.broadcasted_iota(jnp.int32, (KV_PAIR, lanes), 0)
    qpad = jnp.where(row // HEAD_DIM == g, jnp.concatenate([q4, q4], axis=0), jnp.zeros((), BF16))

    def pick(o):
        return jnp.where(g == 0, o[:HEAD_DIM], o[HEAD_DIM:])

    pos1 = i * tq + lax.broadcasted_iota(jnp.int32, (1, tq), 1)
    pos = jnp.concatenate([pos1] * NSA_GROUP, axis=1)

    pooled = pool_ref[0]
    kc = pooled[:, :KV_PAIR].astype(BF16)
    vc = pooled[:, KV_PAIR:].astype(BF16)
    blk = lax.broadcasted_iota(jnp.int32, (nb, 1), 0)
    s = _dot(kc, qpad)
    cmask = ((blk + 1) * NSA_BLOCK - 1) <= pos
    s = jnp.where(cmask, s, NEG_INF)
    e = jnp.exp(s - jnp.max(s, axis=0, keepdims=True))
    p = jnp.where(cmask, e / jnp.sum(e, axis=0, keepdims=True), 0.0)
    o_cmp = pick(_dot_tn(vc, p.astype(BF16)))

    imp = p[:, 0:tq]
    for hh in range(1, NSA_GROUP):
        imp = imp + p[:, hh * tq:(hh + 1) * tq]
    cur = pos1 // NSA_BLOCK
    valid = blk <= cur
    forced = (blk == 0) | (blk == cur) | (blk == cur - 1)
    score = jnp.where(valid, jnp.where(forced, FORCE_SCORE, imp), NEG_INF)
    n_sel = min(NSA_TOPK, nb)
    for j in range(nb):
        sj = score[j:j + 1, :]
        beats = (score > sj) | ((score == sj) & (blk < j))
        cnt = jnp.sum(jnp.where(beats, 1.0, 0.0), axis=0, keepdims=True)
        sel_ref[j] = jnp.where(cnt < n_sel, 1.0, 0.0)

    def attend(k_ref, v_ref, lo, hi, mask_fn):
        def body(kt, carry):
            m, l, acc = carry
            off = pl.multiple_of(kt * tk, tk)
            kk = k_ref[0, pl.ds(off, tk), :]
            sc = _dot(kk, qpad)
            kpos = off + lax.broadcasted_iota(jnp.int32, (tk, 1), 0)
            mask = mask_fn(kt, kpos)
            sc = jnp.where(mask, sc, NEG_INF)
            m_new = jnp.maximum(m, jnp.max(sc, axis=0, keepdims=True))
            alpha = jnp.exp(m - m_new)
            pp = jnp.where(mask, jnp.exp(sc - m_new), 0.0)
            l = alpha * l + jnp.sum(pp, axis=0, keepdims=True)
            acc = alpha * acc + _dot(v_ref[0, :, pl.ds(off, tk)], pp.astype(BF16))
            return m_new, l, acc
        init = (jnp.full((1, lanes), NEG_INF, F32), jnp.zeros((1, lanes), F32), jnp.zeros((KV_PAIR, lanes), F32))
        _, l, acc = lax.fori_loop(lo, hi, body, init)
        return pick(jnp.where(l > 0.0, acc / jnp.where(l > 0.0, l, 1.0), 0.0))

    bpt = tk // NSA_BLOCK

    def slc_mask(kt, kpos):
        rows = [jnp.broadcast_to(sel_ref[kt * bpt + r], (NSA_BLOCK, tq)) for r in range(bpt)]
        selt = jnp.concatenate(rows, axis=0)
        selt = jnp.concatenate([selt] * NSA_GROUP, axis=1)
        return (kpos <= pos) & (selt > 0.5)

    def win_mask(kt, kpos):
        d = pos - kpos
        return (d >= 0) & (d < NSA_WINDOW)

    last = (i * tq + tq - 1) // tk + 1
    o_slc = attend(kslc_ref, vslc_ref, 0, last, slc_mask)
    first = jnp.maximum(i * tq - (NSA_WINDOW - 1), 0) // tk
    o_win = attend(kwin_ref, vwin_ref, first, last, win_mask)

    ga = ga_ref[0]
    outs = []
    for hh in range(NSA_GROUP):
        def gate(c):
            r0 = c * NSA_HEADS + hh
            r1 = r0 + NSA_GROUP
            return jnp.where(g == 0, ga[r0:r0 + 1], ga[r1:r1 + 1])
        sl = slice(hh * tq, (hh + 1) * tq)
        outs.append(gate(0) * o_cmp[:, sl] + gate(1) * o_slc[:, sl] + gate(2) * o_win[:, sl])
    o_ref[0] = jnp.concatenate(outs, axis=0)


def _nsa_prompt(qat, kb, vt, pooled, gat, *, batch, seq, tq=128, tk=128):
    nb = seq // NSA_BLOCK
    nq = seq // tq
    gw = NSA_GROUP * HEAD_DIM
    return pl.pallas_call(
        functools.partial(_nsa_prompt_kernel, tq=tq, tk=tk, nb=nb),
        grid=(batch, NSA_KV_HEADS, nq),
        in_specs=[pl.BlockSpec((1, gw, tq), lambda b, g, i: (b, g, i)),
                  pl.BlockSpec((1, seq, KV_PAIR), lambda b, g, i: (b, 0, 0)),
                  pl.BlockSpec((1, seq, KV_PAIR), lambda b, g, i: (b, 0, 1)),
                  pl.BlockSpec((1, KV_PAIR, seq), lambda b, g, i: (b, 0, 0)),
                  pl.BlockSpec((1, KV_PAIR, seq), lambda b, g, i: (b, 1, 0)),
                  pl.BlockSpec((1, nb, 2 * KV_PAIR), lambda b, g, i: (b, 0, 0)),
                  pl.BlockSpec((1, LANES, tq), lambda b, g, i: (b, 0, i))],
        out_specs=pl.BlockSpec((1, gw, tq), lambda b, g, i: (b, g, i)),
        out_shape=jax.ShapeDtypeStruct((batch, NSA_WIDTH, seq), F32),
        scratch_shapes=[pltpu.VMEM((nb, 1, tq), F32)],
        compiler_params=pltpu.CompilerParams(dimension_semantics=("arbitrary",) * 3, vmem_limit_bytes=VMEM_LIMIT),
        name="nsa_prompt",
    )(qat, kb, kb, vt, vt, pooled, gat)


def _log_keep(z):
    return -(jnp.maximum(z, 0.0) + jnp.log1p(jnp.exp(-jnp.abs(z))))


def _split_bf16(x):
    hi = x.astype(BF16)
    return hi, (x - hi.astype(F32)).astype(BF16)


def _sb_prompt_kernel(q_ref, k_ref, v_ref, o_ref, *, tq, tk):
    i = pl.program_id(2)
    lanes = 2 * tq
    qt = q_ref[0]
    z0 = jnp.zeros((HEAD_DIM, tq), BF16)
    qpad = jnp.concatenate([jnp.concatenate([qt[:HEAD_DIM], z0], axis=0),
                            jnp.concatenate([z0, qt[HEAD_DIM:]], axis=0)], axis=1)
    pos1 = i * tq + lax.broadcasted_iota(jnp.int32, (1, tq), 1)
    pos = jnp.concatenate([pos1, pos1], axis=1)
    r = lax.broadcasted_iota(jnp.int32, (tk, tk), 0)
    c = lax.broadcasted_iota(jnp.int32, (tk, tk), 1)
    upper = jnp.where(c > r, 1.0, 0.0).astype(BF16)
    n_kt = (i * tq + tq - 1) // tk + 1

    def body(n, carry):
        run, acc = carry
        kt = n_kt - 1 - n
        off = pl.multiple_of(kt * tk, tk)
        z = _dot(k_ref[0, pl.ds(off, tk), :], qpad)
        kpos = off + lax.broadcasted_iota(jnp.int32, (tk, 1), 0)
        mask = kpos < pos
        lk = jnp.where(mask, _log_keep(z), 0.0)
        hi, lo = _split_bf16(lk)
        after = _dot(upper, hi) + _dot(upper, lo) + run
        a = jnp.where(mask, jnp.exp(z + lk + after), 0.0)
        acc = acc + _dot(v_ref[0, :, pl.ds(off, tk)], a.astype(BF16))
        return run + jnp.sum(lk, axis=0, keepdims=True), acc

    init = (jnp.zeros((1, lanes), F32), jnp.zeros((2 * HEAD_DIM, lanes), F32))
    _, acc = lax.fori_loop(0, n_kt, body, init)
    o_ref[0] = jnp.concatenate([acc[:HEAD_DIM, :tq], acc[HEAD_DIM:, tq:]], axis=0)


def _sb_prompt(qbt, kbb, vbt, *, batch, seq, tq=128, tk=128):
    nq = seq // tq
    pairs = SB_HEADS // 2
    return pl.pallas_call(
        functools.partial(_sb_prompt_kernel, tq=tq, tk=tk),
        grid=(batch, pairs, nq),
        in_specs=[pl.BlockSpec((1, LANES, tq), lambda b, j, i: (b, j, i)),
                  pl.BlockSpec((1, seq, LANES), lambda b, j, i: (b, 0, j)),
                  pl.BlockSpec((1, LANES, seq), lambda b, j, i: (b, j, 0))],
        out_specs=pl.BlockSpec((1, LANES, tq), lambda b, j, i: (b, j, i)),
        out_shape=jax.ShapeDtypeStruct((batch, SB_WIDTH, seq), F32),
        compiler_params=pltpu.CompilerParams(dimension_semantics=("arbitrary",) * 3, vmem_limit_bytes=VMEM_LIMIT),
        name="sb_prompt",
    )(qbt, kbb, vbt)


def _mix_prompt_kernel(x_ref, oa_ref, za_ref, ob_ref, zb_ref, gm_ref, wat_ref, wbt_ref, wot_ref, g_ref, y_ref, *,
                       d_model):
    ya = _dot(wat_ref[...], (oa_ref[0] * za_ref[0]).astype(BF16))
    yb = _dot(wbt_ref[...], (ob_ref[0] * zb_ref[0]).astype(BF16))
    gm = gm_ref[0]
    mixed = gm[:d_model] * ya + gm[d_model:] * yb
    out = _dot(wot_ref[...], mixed.astype(BF16))
    out = out * lax.rsqrt(jnp.mean(out * out, axis=0, keepdims=True) + RMS_EPS)
    y_ref[...] = x_ref[...] + out.T * g_ref[...]


def _mix_prompt(x2d, oat, zat, obt, zbt, gmt, wat, wbt, wot, g_post, *, batch, seq, tm):
    m, d_model = x2d.shape
    nt = seq // tm
    row = lambda i: (i, 0)
    full = lambda i: (0, 0)
    tr = lambda i: (i // nt, 0, i % nt)
    return pl.pallas_call(
        functools.partial(_mix_prompt_kernel, d_model=d_model),
        grid=(m // tm,),
        in_specs=[pl.BlockSpec((tm, d_model), row), pl.BlockSpec((1, NSA_WIDTH, tm), tr),
                  pl.BlockSpec((1, NSA_WIDTH, tm), tr), pl.BlockSpec((1, SB_WIDTH, tm), tr),
                  pl.BlockSpec((1, SB_WIDTH, tm), tr), pl.BlockSpec((1, N_MERGE * d_model, tm), tr),
                  pl.BlockSpec((d_model, NSA_WIDTH), full), pl.BlockSpec((d_model, SB_WIDTH), full),
                  pl.BlockSpec((d_model, d_model), full), pl.BlockSpec((1, d_model), full)],
        out_specs=pl.BlockSpec((tm, d_model), row),
        out_shape=jax.ShapeDtypeStruct((m, d_model), F32),
        compiler_params=pltpu.CompilerParams(dimension_semantics=("arbitrary",), vmem_limit_bytes=VMEM_LIMIT),
        name="mix_prompt",
    )(x2d, oat, zat, obt, zbt, gmt, wat, wbt, wot, g_post)


def _mix_sample_kernel(x_ref, oa_ref, za_ref, ob_ref, zb_ref, gm_ref, wat_ref, wbt_ref, wot_ref, g_ref, y_ref, *,
                       d_model):
    ya = _dot_nt((oa_ref[...] * za_ref[...]).astype(BF16), wat_ref[...])
    yb = _dot_nt((ob_ref[...] * zb_ref[...]).astype(BF16), wbt_ref[...])
    gm = gm_ref[...]
    mixed = gm[:, :d_model] * ya + gm[:, d_model:] * yb
    out = _dot_nt(mixed.astype(BF16), wot_ref[...])
    y_ref[...] = x_ref[...] + _rms_scale(out, g_ref[...])


def _mix_sample(x2d, o_a, za, o_b, zb, gm, wat, wbt, wot, g_post):
    m, d_model = x2d.shape
    args = (x2d, o_a, za, o_b, zb, gm, wat, wbt, wot, g_post)
    return pl.pallas_call(
        functools.partial(_mix_sample_kernel, d_model=d_model),
        grid=(1,),
        in_specs=[pl.BlockSpec(a.shape, lambda i: (0, 0)) for a in args],
        out_specs=pl.BlockSpec((m, d_model), lambda i: (0, 0)),
        out_shape=jax.ShapeDtypeStruct((m, d_model), F32),
        compiler_params=pltpu.CompilerParams(dimension_semantics=("arbitrary",), vmem_limit_bytes=VMEM_LIMIT),
        name="mix_sample",
    )(*args)


def _head_pad(q):
    q2 = jnp.concatenate([q, q], axis=1)
    row = lax.broadcasted_iota(jnp.int32, q2.shape, 0)
    lane = lax.broadcasted_iota(jnp.int32, q2.shape, 1)
    return jnp.where(row // NSA_GROUP == lane // HEAD_DIM, q2, 0.0)


def _sb_decode_kernel(pt_ref, q_ref, *refs, ppg):
    page_refs = refs[:ppg]
    o_ref = refs[ppg]
    run_ref, acc_ref = refs[ppg + 1:]
    s = pl.program_id(1)

    @pl.when(s == 0)
    def _():
        run_ref[...] = jnp.zeros_like(run_ref)
        acc_ref[...] = jnp.zeros_like(acc_ref)

    q = q_ref[0]
    qrep = jnp.concatenate([q] * SB_HEADS, axis=1)
    row = lax.broadcasted_iota(jnp.int32, qrep.shape, 0)
    lane = lax.broadcasted_iota(jnp.int32, qrep.shape, 1)
    diag = row == lane // HEAD_DIM
    qbd = jnp.where(diag, qrep, 0.0).astype(BF16)

    z = jnp.concatenate([_dot(qbd, pr[0, :SB_WIDTH, :].astype(BF16)) for pr in page_refs], axis=0)
    lk = _log_keep(z)
    r = lax.broadcasted_iota(jnp.int32, (PAGE_SIZE, PAGE_SIZE), 0)
    c = lax.broadcasted_iota(jnp.int32, (PAGE_SIZE, PAGE_SIZE), 1)
    lower = jnp.where(r > c, 1.0, 0.0).astype(BF16)
    hi, lo = _split_bf16(lk)
    after = _dot(hi, lower) + _dot(lo, lower)
    tot = jnp.sum(lk, axis=1, keepdims=True)
    run = run_ref[...]
    acc = acc_ref[...]
    for n in range(ppg):
        sl = slice(n * SB_HEADS, (n + 1) * SB_HEADS)
        a = jnp.exp(z[sl] + lk[sl] + after[sl] + run)
        acc = acc + _dot_nt(a.astype(BF16), page_refs[n][0, SB_WIDTH:, :].astype(BF16))
        run = run + tot[sl]
    run_ref[...] = run
    acc_ref[...] = acc

    @pl.when(s == pl.num_programs(1) - 1)
    def _():
        o_ref[0] = jnp.sum(jnp.where(diag, acc, 0.0), axis=0, keepdims=True)


def _sb_decode(page_table_flat, q_b, cache_t, *, dec_batch, n_pages, ppg=8):
    steps = n_pages // ppg

    def page_map(n):
        return lambda b, s, pt: (pt[b * n_pages + n_pages - 1 - (s * ppg + n)], 0, 0)

    grid_spec = pltpu.PrefetchScalarGridSpec(
        num_scalar_prefetch=1, grid=(dec_batch, steps),
        in_specs=[pl.BlockSpec((1, SB_HEADS, HEAD_DIM), lambda b, s, pt: (b, 0, 0))]
        + [pl.BlockSpec((1, 2 * SB_WIDTH, PAGE_SIZE), page_map(n)) for n in range(ppg)],
        out_specs=pl.BlockSpec((1, 1, SB_WIDTH), lambda b, s, pt: (b, 0, 0)),
        scratch_shapes=[pltpu.VMEM((SB_HEADS, 1), F32), pltpu.VMEM((SB_HEADS, SB_WIDTH), F32)])
    return pl.pallas_call(
        functools.partial(_sb_decode_kernel, ppg=ppg),
        grid_spec=grid_spec,
        out_shape=jax.ShapeDtypeStruct((dec_batch, 1, SB_WIDTH), F32),
        compiler_params=pltpu.CompilerParams(dimension_semantics=("arbitrary", "arbitrary"),
                                             vmem_limit_bytes=VMEM_LIMIT),
        name="sb_decode",
    )(page_table_flat, q_b, *([cache_t] * ppg))


def _nsa_cmp_decode_kernel(pt_ref, q_ref, wp_ref, *refs, ppg, nbp):
    page_refs = refs[:ppg]
    ocmp_ref, sel_ref = refs[ppg:ppg + 2]
    pool_ref = refs[ppg + 2]
    s = pl.program_id(1)
    rows = ppg * BLOCKS_PER_PAGE
    kc = jnp.zeros((rows, KV_PAIR), F32)
    vc = jnp.zeros((rows, KV_PAIR), F32)
    for n, pr in enumerate(page_refs):
        kc = kc + _dot_nt(wp_ref[0, n], pr[0, :KV_PAIR, :].astype(BF16))
        vc = vc + _dot_nt(wp_ref[1, n], pr[0, KV_PAIR:, :].astype(BF16))
    base = pl.multiple_of(s * rows, rows)
    pool_ref[pl.ds(base, rows), :] = jnp.concatenate([kc, vc], axis=1)

    @pl.when(s == pl.num_programs(1) - 1)
    def _():
        qpad = _head_pad(q_ref[0]).astype(BF16)
        pooled = pool_ref[...]
        sc = _dot_nt(qpad, pooled[:, :KV_PAIR].astype(BF16))
        e = jnp.exp(sc - jnp.max(sc, axis=1, keepdims=True))
        p = e / jnp.sum(e, axis=1, keepdims=True)
        o = _dot(p.astype(BF16), pooled[:, KV_PAIR:].astype(BF16))
        hrow = lax.broadcasted_iota(jnp.int32, o.shape, 0)
        ocmp_ref[0] = jnp.where(hrow < NSA_GROUP, o, pltpu.roll(o, HEAD_DIM, 1))[:, :HEAD_DIM]

        blk_l = lax.broadcasted_iota(jnp.int32, (1, nbp), 1)
        blk_s = lax.broadcasted_iota(jnp.int32, (nbp, 1), 0)
        eye = lax.broadcasted_iota(jnp.int32, (nbp, nbp), 0) == lax.broadcasted_iota(jnp.int32, (nbp, nbp), 1)
        n_sel = NSA_TOPK - 1
        kslot = lax.broadcasted_iota(jnp.int32, (NSA_TOPK, nbp), 0)
        for gi in range(NSA_KV_HEADS):
            imp = jnp.sum(p[gi * NSA_GROUP:(gi + 1) * NSA_GROUP], axis=0, keepdims=True)
            forced = (blk_l == 0) | (blk_l == nbp - 1)
            srow = jnp.where(forced, FORCE_SCORE, imp)
            scol = jnp.sum(jnp.where(eye, srow, 0.0), axis=1, keepdims=True)
            beats = (scol > srow) | ((scol == srow) & (blk_s < blk_l))
            rank = jnp.sum(jnp.where(beats, 1.0, 0.0), axis=0, keepdims=True)
            sel = rank < n_sel
            selcol = jnp.sum(jnp.where(eye & sel, 1.0, 0.0), axis=1, keepdims=True) > 0.5
            slot = jnp.sum(jnp.where(selcol & (blk_s < blk_l), 1.0, 0.0), axis=0, keepdims=True)
            onehot = sel & (slot.astype(jnp.int32) == kslot)
            idx = jnp.sum(jnp.where(onehot, blk_l, 0), axis=1, keepdims=True)
            sel_ref[0, gi] = jnp.broadcast_to(idx, (NSA_TOPK, LANES))


def _nsa_cmp_decode(page_table_flat, q_a, wp, cache_t, *, dec_batch, n_pages, ppg=8):
    steps = n_pages // ppg
    nbp = n_pages * BLOCKS_PER_PAGE

    def page_map(n):
        return lambda b, s, pt: (pt[b * n_pages + s * ppg + n], 0, 0)

    grid_spec = pltpu.PrefetchScalarGridSpec(
        num_scalar_prefetch=1, grid=(dec_batch, steps),
        in_specs=[pl.BlockSpec((1, NSA_HEADS, HEAD_DIM), lambda b, s, pt: (b, 0, 0)),
                  pl.BlockSpec(wp.shape, lambda b, s, pt: (0, 0, 0, 0))]
        + [pl.BlockSpec((1, 2 * KV_PAIR, PAGE_SIZE), page_map(n)) for n in range(ppg)],
        out_specs=[pl.BlockSpec((1, NSA_HEADS, HEAD_DIM), lambda b, s, pt: (b, 0, 0)),
                   pl.BlockSpec((1, NSA_KV_HEADS, NSA_TOPK, LANES), lambda b, s, pt: (b, 0, 0, 0))],
        scratch_shapes=[pltpu.VMEM((nbp, 2 * KV_PAIR), F32)])
    return pl.pallas_call(
        functools.partial(_nsa_cmp_decode_kernel, ppg=ppg, nbp=nbp),
        grid_spec=grid_spec,
        out_shape=[jax.ShapeDtypeStruct((dec_batch, NSA_HEADS, HEAD_DIM), F32),
                   jax.ShapeDtypeStruct((dec_batch, NSA_KV_HEADS, NSA_TOPK, LANES), jnp.int32)],
        compiler_params=pltpu.CompilerParams(dimension_semantics=("arbitrary", "arbitrary"),
                                             vmem_limit_bytes=VMEM_LIMIT),
        name="nsa_cmp_decode",
    )(page_table_flat, q_a, wp, *([cache_t] * ppg))


def _nsa_sel_decode_kernel(sel_ref, pt_ref, q_ref, ocmp_ref, ga_ref, new_ref, winp_ref, *refs, n_blk):
    blk_refs = refs[:NSA_KV_HEADS * n_blk]
    o_ref, wino_ref = refs[NSA_KV_HEADS * n_blk:]
    b = pl.program_id(0)
    qb = _head_pad(q_ref[0]).astype(BF16)
    new = new_ref[0]
    hrow = lax.broadcasted_iota(jnp.int32, (NSA_HEADS, KV_PAIR), 0)
    top = hrow < NSA_GROUP

    def fold(o):
        return jnp.where(top, o, pltpu.roll(o, HEAD_DIM, 1))[:, :HEAD_DIM]

    def softmax(sc, mask):
        sc = jnp.where(mask, sc, NEG_INF)
        e = jnp.where(mask, jnp.exp(sc - jnp.max(sc, axis=1, keepdims=True)), 0.0)
        return e / jnp.sum(e, axis=1, keepdims=True)

    lane = lax.broadcasted_iota(jnp.int32, (1, PAGE_SIZE), 1)
    first = lax.broadcasted_iota(jnp.int32, (KV_PAIR, PAGE_SIZE), 1) == 0
    k_new = jnp.where(first, new[2 * KV_PAIR:3 * KV_PAIR], 0.0).astype(BF16)
    v_new = jnp.where(first, new[3 * KV_PAIR:4 * KV_PAIR], 0.0).astype(BF16)
    o_g = []
    for gi in range(NSA_KV_HEADS):
        scs, masks = [], []
        for n in range(n_blk):
            half = sel_ref[(b * NSA_KV_HEADS + gi) * NSA_TOPK + n] % BLOCKS_PER_PAGE
            scs.append(_dot(qb, blk_refs[gi * n_blk + n][0, :KV_PAIR, :].astype(BF16)))
            masks.append(lane // NSA_BLOCK == half)
        scs.append(_dot(qb, k_new))
        masks.append(lane == 0)
        p = softmax(jnp.concatenate(scs, axis=1), jnp.concatenate(masks, axis=1)).astype(BF16)
        o = _dot_nt(p[:, n_blk * PAGE_SIZE:], v_new)
        for n in range(n_blk):
            o = o + _dot_nt(p[:, n * PAGE_SIZE:(n + 1) * PAGE_SIZE],
                            blk_refs[gi * n_blk + n][0, KV_PAIR:, :].astype(BF16))
        o_g.append(o)
    o_slc = fold(jnp.where(top, o_g[0], o_g[1]))

    wp = winp_ref[0]
    w = wp.shape[1]
    wl = lax.broadcasted_iota(jnp.int32, wp.shape, 1)
    shifted = jnp.where(wl == w - 1, new[4 * KV_PAIR:], pltpu.roll(wp, w - 1, 1))
    wino_ref[0] = shifted
    sc = _dot(qb, shifted[:KV_PAIR].astype(BF16))
    p = softmax(sc, jnp.full((1, w), True)).astype(BF16)
    o_win = fold(_dot_nt(p, shifted[KV_PAIR:].astype(BF16)))

    ga = ga_ref[0]
    o_ref[0] = ga[0] * ocmp_ref[0] + ga[1] * o_slc + ga[2] * o_win


def _nsa_sel_decode(sel_flat, page_table_flat, q_a, o_cmp, ga, new_kv, win_past_t, cache_t, *, dec_batch, n_pages):
    n_blk = NSA_TOPK - 1
    w = win_past_t.shape[2]

    def blk_map(gi, n):
        def f(b, sel, pt):
            blk = sel[(b * NSA_KV_HEADS + gi) * NSA_TOPK + n]
            return (pt[b * n_pages + blk // BLOCKS_PER_PAGE], 1, 0)
        return f

    grid_spec = pltpu.PrefetchScalarGridSpec(
        num_scalar_prefetch=2, grid=(dec_batch,),
        in_specs=[pl.BlockSpec((1, NSA_HEADS, HEAD_DIM), lambda b, sel, pt: (b, 0, 0)),
                  pl.BlockSpec((1, NSA_HEADS, HEAD_DIM), lambda b, sel, pt: (b, 0, 0)),
                  pl.BlockSpec((1, NSA_BRANCHES, NSA_HEADS, 1), lambda b, sel, pt: (b, 0, 0, 0)),
                  pl.BlockSpec((1, KVA_WIDTH, 1), lambda b, sel, pt: (b, 0, 0)),
                  pl.BlockSpec((1, 2 * KV_PAIR, w), lambda b, sel, pt: (b, 0, 0))]
        + [pl.BlockSpec((1, 2 * KV_PAIR, PAGE_SIZE), blk_map(gi, n))
           for gi in range(NSA_KV_HEADS) for n in range(n_blk)],
        out_specs=[pl.BlockSpec((1, NSA_HEADS, HEAD_DIM), lambda b, sel, pt: (b, 0, 0)),
                   pl.BlockSpec((1, 2 * KV_PAIR, w), lambda b, sel, pt: (b, 0, 0))])
    return pl.pallas_call(
        functools.partial(_nsa_sel_decode_kernel, n_blk=n_blk),
        grid_spec=grid_spec,
        out_shape=[jax.ShapeDtypeStruct((dec_batch, NSA_HEADS, HEAD_DIM), F32),
                   jax.ShapeDtypeStruct((dec_batch, 2 * KV_PAIR, w), F32)],
        compiler_params=pltpu.CompilerParams(dimension_semantics=("arbitrary",), vmem_limit_bytes=VMEM_LIMIT),
        name="nsa_sel_decode",
    )(sel_flat, page_table_flat, q_a, o_cmp, ga, new_kv, win_past_t, *([cache_t] * (NSA_KV_HEADS * n_blk)))


def _feature_major(a, lead):
    nl = len(lead)
    t = jnp.moveaxis(a, nl, -1)
    return t.reshape(lead + (-1, a.shape[nl]))


def _time_major(a_t, feat_shape):
    lead, _, time = a_t.shape
    return jnp.moveaxis(a_t.reshape((lead,) + feat_shape + (time,)), -1, 1)


def _layer(x_prompt, x_sample, nsa_cache, win_cache, sb_cache, page_table, w_in, w_cmp, w_a, w_b, w_o, g_pre, g_post):
    batch, seq, d_model = x_prompt.shape
    dec_batch, dec_seq, _ = x_sample.shape
    n_pages = page_table.shape[1]
    past_len = n_pages * PAGE_SIZE
    n_phys = nsa_cache.shape[0]
    tm, ppg = 256, 8
    assert dec_seq == 1 and seq % tm == 0 and seq >= NSA_WINDOW and past_len >= NSA_WINDOW
    assert n_pages % ppg == 0 and past_len // NSA_BLOCK >= NSA_TOPK
    assert win_cache.shape[1] == NSA_WINDOW and nsa_cache.shape[1] == PAGE_SIZE

    wt_perm = _permute_w_in_t(w_in, d_model)
    wat, wbt, wot = w_a.T.astype(BF16), w_b.T.astype(BF16), w_o.T.astype(BF16)
    g_pre2, g_post2 = g_pre.reshape(1, d_model), g_post.reshape(1, d_model)
    nsa_feat = (4, NSA_KV_HEADS, HEAD_DIM)
    win_feat = (2, NSA_KV_HEADS, HEAD_DIM)
    sb_feat = (2, SB_HEADS, HEAD_DIM)

    xp = x_prompt.reshape(batch * seq, d_model)
    cos, sin = _rope_angles(jnp.arange(seq, dtype=jnp.int32))
    wp_prompt = _pool_weights(w_cmp, tm // NSA_BLOCK, SUBLANES).reshape(2 * SUBLANES, tm).astype(BF16)
    (nsa_t, win_t, sb_t, za_t, zb_t, gm_t, qa_t, ga_t, kb, v_t, pooled, qb_t, kbb, vb_t) = _project_prompt(
        xp, g_pre2, wt_perm, cos.T, sin.T, wp_prompt, batch=batch, seq=seq, tm=tm)
    oa_t = _nsa_prompt(qa_t, kb.reshape(batch, seq, 2 * KV_PAIR), v_t,
                       pooled.reshape(batch, seq // NSA_BLOCK, 2 * KV_PAIR), ga_t, batch=batch, seq=seq)
    ob_t = _sb_prompt(qb_t, kbb.reshape(batch, seq, SB_WIDTH), vb_t, batch=batch, seq=seq)
    y_prompt = _mix_prompt(xp, oa_t, za_t, ob_t, zb_t, gm_t, wat, wbt, wot, g_post2,
                           batch=batch, seq=seq, tm=tm).reshape(batch, seq, d_model)
    nsa_kv_prompt = _time_major(nsa_t, nsa_feat)
    win_kv_prompt = _time_major(win_t[:, :, seq - NSA_WINDOW:], win_feat)
    sb_kv_prompt = _time_major(sb_t, sb_feat)

    xs = x_sample.reshape(dec_batch, d_model)
    tabs_s = _rope_lane_tables(jnp.full((dec_batch,), past_len, jnp.int32))
    (nsa_s, win_s, sb_s, za_s, zb_s, gm_s, qa_s, ga_s, qb_s) = _project_sample(xs, g_pre2, wt_perm, tabs_s)
    pt_flat = page_table.reshape(-1)
    sb_cache_t = _feature_major(sb_cache, (n_phys,))
    nsa_cache_t = _feature_major(nsa_cache, (n_phys,))
    win_cache_t = _feature_major(win_cache, (dec_batch,))
    o_b_s = _sb_decode(pt_flat, qb_s.reshape(dec_batch, SB_HEADS, HEAD_DIM), sb_cache_t,
                       dec_batch=dec_batch, n_pages=n_pages, ppg=ppg)
    wp_dec = _pool_weights(w_cmp, BLOCKS_PER_PAGE, ppg * BLOCKS_PER_PAGE)
    wp_dec = jnp.stack([jnp.roll(wp_dec, BLOCKS_PER_PAGE * n, axis=1) for n in range(ppg)], axis=1).astype(BF16)
    qa3 = qa_s.reshape(dec_batch, NSA_HEADS, HEAD_DIM)
    o_cmp_s, sel = _nsa_cmp_decode(pt_flat, qa3, wp_dec, nsa_cache_t, dec_batch=dec_batch, n_pages=n_pages, ppg=ppg)
    ga3 = ga_s[:, :GATE_A].reshape(dec_batch, NSA_BRANCHES, NSA_HEADS, 1)
    new_kv = jnp.concatenate([nsa_s, win_s], axis=1).reshape(dec_batch, KVA_WIDTH, 1)
    o_a_s, win_out_t = _nsa_sel_decode(sel[:, :, :, 0].reshape(-1), pt_flat, qa3, o_cmp_s, ga3, new_kv,
                                       win_cache_t, nsa_cache_t, dec_batch=dec_batch, n_pages=n_pages)
    y_sample = _mix_sample(xs, o_a_s.reshape(dec_batch, NSA_WIDTH), za_s, o_b_s.reshape(dec_batch, SB_WIDTH), zb_s,
                           gm_s, wat, wbt, wot, g_post2).reshape(dec_batch, 1, d_model)
    nsa_kv_sample = nsa_s.reshape((dec_batch, 1) + nsa_feat)
    win_kv_sample = _time_major(win_out_t, win_feat)
    sb_kv_sample = sb_s.reshape((dec_batch, 1) + sb_feat)
    return (y_prompt, y_sample, nsa_kv_prompt, win_kv_prompt, sb_kv_prompt, nsa_kv_sample, win_kv_sample,
            sb_kv_sample)


def kernel(x_prompt, x_sample, cache_nsa_kv, cache_nsa_win_kv, cache_sb_kv, page_table, w_in, w_cmp, w_branch_a,
           w_branch_b, w_out, g_pre, g_post):
    hp, hs = x_prompt, x_sample
    caches = [[] for _ in range(6)]
    for layer in range(w_in.shape[0]):
        outs = _layer(hp, hs, cache_nsa_kv[layer], cache_nsa_win_kv[layer], cache_sb_kv[layer], page_table,
                      w_in[layer], w_cmp[layer], w_branch_a[layer], w_branch_b[layer], w_out[layer],
                      g_pre[layer], g_post[layer])
        hp, hs = outs[0], outs[1]
        for acc, o in zip(caches, outs[2:]):
            acc.append(o)
    return (hp, hs) + tuple(jnp.stack(c) for c in caches)
```

```python
import functools

import jax
import jax.numpy as jnp
from jax import lax
from jax.experimental import pallas as pl
from jax.experimental.pallas import tpu as pltpu

HEAD_DIM = 64
ROT_DIM = HEAD_DIM // 4
ROPE_THETA = 500000.0
NSA_HEADS = 8
NSA_KV_HEADS = 2
NSA_GROUP = NSA_HEADS // NSA_KV_HEADS
NSA_BRANCHES = 3
NSA_BLOCK = 64
NSA_TOPK = 16
NSA_WINDOW = 512
NSA_WIDTH = NSA_HEADS * HEAD_DIM
SB_HEADS = 8
SB_WIDTH = SB_HEADS * HEAD_DIM
N_MERGE = 2
PAGE_SIZE = 128
RMS_EPS = 1e-6
NEG_INF = -1e30
FORCE_SCORE = 1e3
SCALE = HEAD_DIM ** -0.5

LANES = 128
SUBLANES = 8
KV_PAIR = NSA_KV_HEADS * HEAD_DIM
assert KV_PAIR == LANES and PAGE_SIZE == LANES and PAGE_SIZE % NSA_BLOCK == 0
KVA_WIDTH = 2 * NSA_BRANCHES * KV_PAIR
GATE_A = NSA_BRANCHES * NSA_HEADS
BLOCKS_PER_PAGE = PAGE_SIZE // NSA_BLOCK
VMEM_LIMIT = 56 * 1024 * 1024

C_QA = 0
C_KVA = C_QA + NSA_WIDTH
C_ZA = C_KVA + KVA_WIDTH
C_QKVB = C_ZA + NSA_WIDTH
C_ZB = C_QKVB + 3 * SB_WIDTH
C_GM = C_ZB + SB_WIDTH

BF16 = jnp.bfloat16
F32 = jnp.float32


def _dot(a, b):
    return jnp.dot(a, b, preferred_element_type=F32)


def _dot_tn(a, b):
    return lax.dot_general(a, b, (((0,), (0,)), ((), ())), preferred_element_type=F32)


def _dot_nt(a, b):
    return lax.dot_general(a, b, (((1,), (1,)), ((), ())), preferred_element_type=F32)


def _sigmoid(x):
    return 1.0 / (1.0 + jnp.exp(-x))


def _rms_scale(x, g):
    return x * lax.rsqrt(jnp.mean(x * x, axis=-1, keepdims=True) + RMS_EPS) * g


def _rope_rows(v, cos, sin):
    half = ROT_DIM // 2
    parts = []
    for base in range(0, v.shape[0], HEAD_DIM):
        x1, x2 = v[base:base + half], v[base + half:base + 2 * half]
        parts += [x1 * cos - x2 * sin, x2 * cos + x1 * sin, v[base + 2 * half:base + HEAD_DIM]]
    return jnp.concatenate(parts, axis=0)


def _proj_prompt_kernel(x_ref, g_ref, wt_ref, cos_ref, sin_ref, wp_ref,
                        nsa_ref, win_ref, sb_ref, za_ref, zb_ref, gm_ref, qa_ref, ga_ref, kb_ref, vt_ref,
                        pool_ref, qb_ref, kbb_ref, vbt_ref, *, d_model):
    hb = _rms_scale(x_ref[...], g_ref[...]).astype(BF16)
    cos, sin = cos_ref[...], sin_ref[...]

    def seg(lo, width):
        return _dot_nt(wt_ref[lo:lo + width, :], hb)

    qa_ref[0] = (_rope_rows(seg(C_QA, NSA_WIDTH), cos, sin) * SCALE).astype(BF16)
    kv = seg(C_KVA, KVA_WIDTH)
    cmp_k = _rope_rows(kv[0 * LANES:1 * LANES], cos, sin)
    cmp_v = kv[1 * LANES:2 * LANES]
    slc_k = _rope_rows(kv[2 * LANES:3 * LANES], cos, sin)
    slc_v = kv[3 * LANES:4 * LANES]
    win_k = _rope_rows(kv[4 * LANES:5 * LANES], cos, sin)
    win_v = kv[5 * LANES:6 * LANES]
    nsa_ref[0] = jnp.concatenate([cmp_k, cmp_v, slc_k, slc_v], axis=0)
    win_ref[0] = jnp.concatenate([win_k, win_v], axis=0)
    kb_ref[...] = jnp.concatenate([slc_k.T, win_k.T], axis=1).astype(BF16)
    vt_ref[0] = jnp.concatenate([slc_v, win_v], axis=0).astype(BF16)
    wp = wp_ref[...]
    nblk = pool_ref.shape[1]
    pooled = jnp.concatenate([_dot_nt(wp[:SUBLANES], cmp_k.astype(BF16)),
                              _dot_nt(wp[SUBLANES:], cmp_v.astype(BF16))], axis=1)
    pool_ref[0] = pooled[:nblk]

    za = seg(C_ZA, NSA_WIDTH)
    za_ref[0] = za * _sigmoid(za)
    qkvb = seg(C_QKVB, 3 * SB_WIDTH)
    qb_ref[0] = (qkvb[:SB_WIDTH] * SCALE).astype(BF16)
    sb_ref[0] = qkvb[SB_WIDTH:]
    kbb_ref[...] = qkvb[SB_WIDTH:2 * SB_WIDTH].T.astype(BF16)
    vbt_ref[0] = qkvb[2 * SB_WIDTH:].astype(BF16)
    zb = seg(C_ZB, SB_WIDTH)
    zb_ref[0] = zb * _sigmoid(zb)
    gm_ref[0] = _sigmoid(seg(C_GM, N_MERGE * d_model))
    ga_ref[0] = _sigmoid(seg(C_GM + N_MERGE * d_model, LANES))


def _project_prompt(x2d, g_pre, wt_perm, cos_t, sin_t, wp, *, batch, seq, tm):
    m, d_model = x2d.shape
    nt = seq // tm
    n_rows = wt_perm.shape[0]
    nblk = tm // NSA_BLOCK
    row = lambda i: (i, 0)
    full = lambda i: (0, 0)
    tab = lambda i: (0, i % nt)
    tr = lambda i: (i // nt, 0, i % nt)
    half = ROT_DIM // 2
    in_specs = [pl.BlockSpec((tm, d_model), row), pl.BlockSpec((1, d_model), full),
                pl.BlockSpec((n_rows, d_model), full),
                pl.BlockSpec((half, tm), tab), pl.BlockSpec((half, tm), tab),
                pl.BlockSpec((2 * SUBLANES, tm), full)]
    sds = jax.ShapeDtypeStruct

    def feat(width, dtype):
        return sds((batch, width, seq), dtype), pl.BlockSpec((1, width, tm), tr)

    def rows(width, dtype):
        return sds((m, width), dtype), pl.BlockSpec((tm, width), row)

    outs = [feat(4 * KV_PAIR, F32), feat(2 * KV_PAIR, F32), feat(2 * SB_WIDTH, F32),
            feat(NSA_WIDTH, F32), feat(SB_WIDTH, F32), feat(N_MERGE * d_model, F32),
            feat(NSA_WIDTH, BF16), feat(LANES, F32), rows(2 * KV_PAIR, BF16), feat(2 * KV_PAIR, BF16),
            (sds((m // tm, nblk, 2 * KV_PAIR), F32), pl.BlockSpec((1, nblk, 2 * KV_PAIR), lambda i: (i, 0, 0))),
            feat(SB_WIDTH, BF16), rows(SB_WIDTH, BF16), feat(SB_WIDTH, BF16)]
    return pl.pallas_call(
        functools.partial(_proj_prompt_kernel, d_model=d_model),
        grid=(m // tm,), in_specs=in_specs, out_specs=[o[1] for o in outs], out_shape=[o[0] for o in outs],
        compiler_params=pltpu.CompilerParams(dimension_semantics=("arbitrary",), vmem_limit_bytes=VMEM_LIMIT),
        name="proj_prompt",
    )(x2d, g_pre, wt_perm, cos_t, sin_t, wp)


def _rope_lanes(v, c, s_up, s_dn):
    half = ROT_DIM // 2
    outs = []
    for j in range(v.shape[1] // LANES):
        blk = v[:, j * LANES:(j + 1) * LANES]
        outs.append(blk * c + pltpu.roll(blk, LANES - half, 1) * s_up + pltpu.roll(blk, half, 1) * s_dn)
    return outs[0] if len(outs) == 1 else jnp.concatenate(outs, axis=1)


def _proj_sample_kernel(x_ref, g_ref, wt_ref, cos_ref, sup_ref, sdn_ref,
                        nsa_ref, win_ref, sbkv_ref, za_ref, zb_ref, gm_ref, qa_ref, ga_ref, qb_ref, *, d_model):
    hb = _rms_scale(x_ref[...], g_ref[...]).astype(BF16)
    cos, s_up, s_dn = cos_ref[...], sup_ref[...], sdn_ref[...]

    def seg(lo, width):
        return _dot_nt(hb, wt_ref[lo:lo + width, :])

    qa_ref[...] = _rope_lanes(seg(C_QA, NSA_WIDTH), cos, s_up, s_dn) * SCALE
    kv = seg(C_KVA, KVA_WIDTH)
    parts = []
    for j in range(2 * NSA_BRANCHES):
        blk = kv[:, j * LANES:(j + 1) * LANES]
        parts.append(_rope_lanes(blk, cos, s_up, s_dn) if j % 2 == 0 else blk)
    nsa_ref[...] = jnp.concatenate(parts[:4], axis=1)
    win_ref[...] = jnp.concatenate(parts[4:], axis=1)
    za = seg(C_ZA, NSA_WIDTH)
    za_ref[...] = za * _sigmoid(za)
    qkvb = seg(C_QKVB, 3 * SB_WIDTH)
    qb_ref[...] = qkvb[:, :SB_WIDTH] * SCALE
    sbkv_ref[...] = qkvb[:, SB_WIDTH:]
    zb = seg(C_ZB, SB_WIDTH)
    zb_ref[...] = zb * _sigmoid(zb)
    gm_ref[...] = _sigmoid(seg(C_GM, N_MERGE * d_model))
    ga_ref[...] = _sigmoid(seg(C_GM + N_MERGE * d_model, LANES))


def _project_sample(x2d, g_pre, wt_perm, tables):
    m, d_model = x2d.shape
    full = lambda i: (0, 0)
    widths = [4 * KV_PAIR, 2 * KV_PAIR, 2 * SB_WIDTH, NSA_WIDTH, SB_WIDTH, N_MERGE * d_model,
              NSA_WIDTH, LANES, SB_WIDTH]
    return pl.pallas_call(
        functools.partial(_proj_sample_kernel, d_model=d_model),
        grid=(1,),
        in_specs=[pl.BlockSpec((m, d_model), full), pl.BlockSpec((1, d_model), full),
                  pl.BlockSpec(wt_perm.shape, full)] + [pl.BlockSpec((m, LANES), full)] * 3,
        out_specs=[pl.BlockSpec((m, w), full) for w in widths],
        out_shape=[jax.ShapeDtypeStruct((m, w), F32) for w in widths],
        compiler_params=pltpu.CompilerParams(dimension_semantics=("arbitrary",), vmem_limit_bytes=VMEM_LIMIT),
        name="proj_sample",
    )(x2d, g_pre, wt_perm, *tables)


def _rope_angles(pos):
    half = ROT_DIM // 2
    inv_freq = ROPE_THETA ** (-jnp.arange(half, dtype=F32) / half)
    ang = pos.astype(F32)[:, None] * inv_freq[None, :]
    return jnp.cos(ang), jnp.sin(ang)


def _rope_lane_tables(pos):
    cos, sin = _rope_angles(pos)
    n = pos.shape[0]
    half = ROT_DIM // 2
    ones = jnp.ones((n, HEAD_DIM - ROT_DIM), F32)
    zeros = jnp.zeros((n, HEAD_DIM - ROT_DIM), F32)
    zh = jnp.zeros((n, half), F32)
    rep = LANES // HEAD_DIM
    return tuple(jnp.tile(t, (1, rep)) for t in (jnp.concatenate([cos, cos, ones], axis=1),
                                                 jnp.concatenate([-sin, zh, zeros], axis=1),
                                                 jnp.concatenate([zh, sin, zeros], axis=1)))


def _permute_w_in_t(w_in, d_model):
    c = [NSA_WIDTH, KVA_WIDTH, NSA_WIDTH, GATE_A, 3 * SB_WIDTH, SB_WIDTH, N_MERGE * d_model]
    o = [0]
    for s in c:
        o.append(o[-1] + s)
    wt = w_in.T
    pad = jnp.zeros((LANES - GATE_A, d_model), w_in.dtype)
    return jnp.concatenate([wt[:o[3]], wt[o[4]:], wt[o[3]:o[4]], pad], axis=0).astype(BF16)


def _pool_weights(w_cmp, n_tiles, blocks_per_tile, rows):
    lane = jnp.arange(blocks_per_tile * NSA_BLOCK)
    owner = jnp.arange(n_tiles)[:, None, None] * blocks_per_tile + (lane // NSA_BLOCK)[None, None, :]
    hit = jnp.arange(rows)[None, :, None] == owner
    w_lane = jnp.tile(w_cmp, (1, blocks_per_tile))
    return jnp.where(hit[None], w_lane[:, None, None, :], 0.0).astype(BF16)


def _nsa_prompt_kernel(q_ref, kslc_ref, kwin_ref, vslc_ref, vwin_ref, pool_ref, ga_ref, o_ref, sel_ref, acc_ref, *,
                       tq, tk, nb):
    i = pl.program_id(1)
    lanes = NSA_GROUP * tq
    kvh = NSA_KV_HEADS
    row = lax.broadcasted_iota(jnp.int32, (KV_PAIR, lanes), 0)
    pos1 = i * tq + lax.broadcasted_iota(jnp.int32, (1, tq), 1)
    pos = jnp.concatenate([pos1] * NSA_GROUP, axis=1)
    blk = lax.broadcasted_iota(jnp.int32, (nb, 1), 0)
    cmask = ((blk + 1) * NSA_BLOCK - 1) <= pos
    cur = pos1 // NSA_BLOCK
    valid = blk <= cur
    forced = (blk == 0) | (blk == cur) | (blk == cur - 1)
    n_sel = min(NSA_TOPK, nb)
    pooled = pool_ref[0]
    kc = pooled[:, :KV_PAIR].astype(BF16)
    vc = pooled[:, KV_PAIR:].astype(BF16)

    qpads, o_cmp = [], []
    for g in range(kvh):
        qt = q_ref[0, g * NSA_GROUP * HEAD_DIM:(g + 1) * NSA_GROUP * HEAD_DIM, :]
        q4 = jnp.concatenate([qt[hh * HEAD_DIM:(hh + 1) * HEAD_DIM] for hh in range(NSA_GROUP)], axis=1)
        qpad = jnp.where(row // HEAD_DIM == g, jnp.concatenate([q4, q4], axis=0), jnp.zeros((), BF16))
        qpads.append(qpad)
        s = jnp.where(cmask, _dot(kc, qpad), NEG_INF)
        e = jnp.exp(s - jnp.max(s, axis=0, keepdims=True))
        p = jnp.where(cmask, e / jnp.sum(e, axis=0, keepdims=True), 0.0)
        o_cmp.append(_dot_tn(vc, p.astype(BF16))[g * HEAD_DIM:(g + 1) * HEAD_DIM])
        imp = p[:, 0:tq]
        for hh in range(1, NSA_GROUP):
            imp = imp + p[:, hh * tq:(hh + 1) * tq]
        score = jnp.where(valid, jnp.where(forced, FORCE_SCORE, imp), NEG_INF)
        for j in range(nb):
            sj = score[j:j + 1, :]
            beats = (score > sj) | ((score == sj) & (blk < j))
            cnt = jnp.sum(jnp.where(beats, 1.0, 0.0), axis=0, keepdims=True)
            sel_ref[g, j] = jnp.where(cnt < n_sel, 1.0, 0.0)

    bpt = tk // NSA_BLOCK

    def sel_mask(g, kt):
        rows = [jnp.broadcast_to(sel_ref[g, kt * bpt + r], (NSA_BLOCK, tq)) for r in range(bpt)]
        selt = jnp.concatenate(rows, axis=0) > 0.5
        return jnp.concatenate([selt] * NSA_GROUP, axis=1)

    def step(kt, carry, near):
        ms, ls = list(carry[0]), list(carry[1])
        off = pl.multiple_of(kt * tk, tk)
        ks = kslc_ref[0, pl.ds(off, tk), :]
        chains = [(g, g, ks, vslc_ref) for g in range(kvh)]
        masks = [sel_mask(g, kt) for g in range(kvh)]
        if near:
            kw = kwin_ref[0, pl.ds(off, tk), :]
            d = pos - (off + lax.broadcasted_iota(jnp.int32, (tk, 1), 0))
            causal = d >= 0
            wmask = causal & (d < NSA_WINDOW)
            chains += [(kvh + g, g, kw, vwin_ref) for g in range(kvh)]
            masks = [mk & causal for mk in masks] + [wmask] * kvh
        scs = [jnp.where(mk, _dot(kk, qpads[g]), NEG_INF) for (_, g, kk, _), mk in zip(chains, masks)]
        m_new = [jnp.maximum(ms[c], jnp.max(sc, axis=0, keepdims=True)) for (c, _, _, _), sc in zip(chains, scs)]
        pps = [jnp.where(mk, jnp.exp(sc - mn), 0.0) for sc, mn, mk in zip(scs, m_new, masks)]
        pvs = [_dot(v_ref[0, g * HEAD_DIM:(g + 1) * HEAD_DIM, pl.ds(off, tk)], pp.astype(BF16))
               for (_, g, _, v_ref), pp in zip(chains, pps)]
        for (c, _, _, _), mn, pp, pv in zip(chains, m_new, pps, pvs):
            alpha = jnp.exp(ms[c] - mn)
            acc_ref[c] = alpha * acc_ref[c] + pv
            ls[c] = alpha * ls[c] + jnp.sum(pp, axis=0, keepdims=True)
            ms[c] = mn
        return tuple(ms), tuple(ls)

    n_chain = 2 * kvh
    acc_ref[...] = jnp.zeros_like(acc_ref)
    carry = ((jnp.full((1, lanes), NEG_INF, F32),) * n_chain, (jnp.zeros((1, lanes), F32),) * n_chain)
    first = jnp.maximum(i * tq - (NSA_WINDOW - 1), 0) // tk
    carry = lax.fori_loop(0, first, lambda kt, cr: step(kt, cr, False), carry)
    _, ls = lax.fori_loop(first, i + 1, lambda kt, cr: step(kt, cr, True), carry)

    ga = ga_ref[0]
    for g in range(kvh):
        o_br = [o_cmp[g]]
        for c in (g, kvh + g):
            l = ls[c]
            o_br.append(jnp.where(l > 0.0, acc_ref[c] / jnp.where(l > 0.0, l, 1.0), 0.0))
        for hh in range(NSA_GROUP):
            h = g * NSA_GROUP + hh
            sl = slice(hh * tq, (hh + 1) * tq)
            o = ga[h:h + 1] * o_br[0][:, sl]
            for br in range(1, NSA_BRANCHES):
                o = o + ga[br * NSA_HEADS + h:br * NSA_HEADS + h + 1] * o_br[br][:, sl]
            o_ref[0, h * HEAD_DIM:(h + 1) * HEAD_DIM, :] = o


def _nsa_prompt(qat, kb, vt, pooled, gat, *, batch, seq, tile=128):
    nb = seq // NSA_BLOCK
    nq = seq // tile
    return pl.pallas_call(
        functools.partial(_nsa_prompt_kernel, tq=tile, tk=tile, nb=nb),
        grid=(batch, nq),
        in_specs=[pl.BlockSpec((1, NSA_WIDTH, tile), lambda b, i: (b, 0, i)),
                  pl.BlockSpec((1, seq, KV_PAIR), lambda b, i: (b, 0, 0)),
                  pl.BlockSpec((1, seq, KV_PAIR), lambda b, i: (b, 0, 1)),
                  pl.BlockSpec((1, KV_PAIR, seq), lambda b, i: (b, 0, 0)),
                  pl.BlockSpec((1, KV_PAIR, seq), lambda b, i: (b, 1, 0)),
                  pl.BlockSpec((1, nb, 2 * KV_PAIR), lambda b, i: (b, 0, 0)),
                  pl.BlockSpec((1, LANES, tile), lambda b, i: (b, 0, i))],
        out_specs=pl.BlockSpec((1, NSA_WIDTH, tile), lambda b, i: (b, 0, i)),
        out_shape=jax.ShapeDtypeStruct((batch, NSA_WIDTH, seq), F32),
        scratch_shapes=[pltpu.VMEM((NSA_KV_HEADS, nb, 1, tile), F32),
                        pltpu.VMEM((2 * NSA_KV_HEADS, HEAD_DIM, NSA_GROUP * tile), F32)],
        compiler_params=pltpu.CompilerParams(dimension_semantics=("arbitrary",) * 2, vmem_limit_bytes=VMEM_LIMIT),
        name="nsa_prompt",
    )(qat, kb, kb, vt, vt, pooled, gat)


def _softplus(z):
    return jnp.maximum(z, 0.0) + jnp.log(1.0 + jnp.exp(-jnp.abs(z)))


def _split_bf16(x):
    hi = x.astype(BF16)
    return hi, (x - hi.astype(F32)).astype(BF16)


def _sb_prompt_kernel(q_ref, k_ref, v_ref, o_ref, acc_ref, *, tq, tk):
    i = pl.program_id(1)
    pairs = SB_HEADS // 2
    lanes = 2 * tq
    z0 = jnp.zeros((HEAD_DIM, tq), BF16)
    qpads = []
    for j in range(pairs):
        qt = q_ref[0, j * LANES:(j + 1) * LANES, :]
        qpads.append(jnp.concatenate([jnp.concatenate([qt[:HEAD_DIM], z0], axis=0),
                                      jnp.concatenate([z0, qt[HEAD_DIM:]], axis=0)], axis=1))
    r = lax.broadcasted_iota(jnp.int32, (tk, tk), 0)
    c = lax.broadcasted_iota(jnp.int32, (tk, tk), 1)
    upper = jnp.where(c > r, 1.0, 0.0).astype(BF16)

    def step(kt, runs, mask, first):
        off = pl.multiple_of(kt * tk, tk)
        kk = k_ref[0, pl.ds(off, tk), :]
        zs = [_dot(kk[:, j * LANES:(j + 1) * LANES], qpads[j]) for j in range(pairs)]
        sps = [_softplus(z) if mask is None else jnp.where(mask, _softplus(z), 0.0) for z in zs]
        parts = [_split_bf16(sp) for sp in sps]
        drops = [_dot(upper, hi) + _dot(upper, lo) + run for (hi, lo), run in zip(parts, runs)]
        ws = [jnp.exp(z - sp - drop) for z, sp, drop in zip(zs, sps, drops)]
        if mask is not None:
            ws = [jnp.where(mask, w, 0.0) for w in ws]
        for j in range(pairs):
            pv = _dot(v_ref[0, j * LANES:(j + 1) * LANES, pl.ds(off, tk)], ws[j].astype(BF16))
            acc_ref[j] = pv if first else acc_ref[j] + pv
        return tuple(run + jnp.sum(sp, axis=0, keepdims=True) for run, sp in zip(runs, sps))

    pos = i * tq + lax.broadcasted_iota(jnp.int32, (1, lanes), 1) % tq
    kpos = i * tk + lax.broadcasted_iota(jnp.int32, (tk, 1), 0)
    zero = jnp.zeros((1, lanes), F32)
    runs = step(i, (zero,) * pairs, kpos < pos, True)
    lax.fori_loop(0, i, lambda n, rs: step(i - 1 - n, rs, None, False), runs)
    for j in range(pairs):
        acc = acc_ref[j]
        o_ref[0, j * LANES:(j + 1) * LANES, :] = jnp.concatenate([acc[:HEAD_DIM, :tq], acc[HEAD_DIM:, tq:]], axis=0)


def _sb_prompt(qbt, kbb, vbt, *, batch, seq, tile=128):
    nq = seq // tile
    return pl.pallas_call(
        functools.partial(_sb_prompt_kernel, tq=tile, tk=tile),
        grid=(batch, nq),
        in_specs=[pl.BlockSpec((1, SB_WIDTH, tile), lambda b, i: (b, 0, i)),
                  pl.BlockSpec((1, seq, SB_WIDTH), lambda b, i: (b, 0, 0)),
                  pl.BlockSpec((1, SB_WIDTH, seq), lambda b, i: (b, 0, 0))],
        out_specs=pl.BlockSpec((1, SB_WIDTH, tile), lambda b, i: (b, 0, i)),
        out_shape=jax.ShapeDtypeStruct((batch, SB_WIDTH, seq), F32),
        scratch_shapes=[pltpu.VMEM((SB_HEADS // 2, 2 * HEAD_DIM, 2 * tile), F32)],
        compiler_params=pltpu.CompilerParams(dimension_semantics=("arbitrary",) * 2, vmem_limit_bytes=VMEM_LIMIT),
        name="sb_prompt",
    )(qbt, kbb, vbt)


def _mix_prompt_kernel(x_ref, oa_ref, za_ref, ob_ref, zb_ref, gm_ref, wat_ref, wbt_ref, wot_ref, g_ref, y_ref, *,
                       d_model):
    ya = _dot(wat_ref[...], (oa_ref[0] * za_ref[0]).astype(BF16))
    yb = _dot(wbt_ref[...], (ob_ref[0] * zb_ref[0]).astype(BF16))
    gm = gm_ref[0]
    mixed = gm[:d_model] * ya + gm[d_model:] * yb
    out = _dot(wot_ref[...], mixed.astype(BF16))
    out = out * lax.rsqrt(jnp.mean(out * out, axis=0, keepdims=True) + RMS_EPS)
    y_ref[...] = x_ref[...] + out.T * g_ref[...]


def _mix_prompt(x2d, oat, zat, obt, zbt, gmt, wat, wbt, wot, g_post, *, batch, seq, tm):
    m, d_model = x2d.shape
    nt = seq // tm
    row = lambda i: (i, 0)
    full = lambda i: (0, 0)
    tr = lambda i: (i // nt, 0, i % nt)
    return pl.pallas_call(
        functools.partial(_mix_prompt_kernel, d_model=d_model),
        grid=(m // tm,),
        in_specs=[pl.BlockSpec((tm, d_model), row), pl.BlockSpec((1, NSA_WIDTH, tm), tr),
                  pl.BlockSpec((1, NSA_WIDTH, tm), tr), pl.BlockSpec((1, SB_WIDTH, tm), tr),
                  pl.BlockSpec((1, SB_WIDTH, tm), tr), pl.BlockSpec((1, N_MERGE * d_model, tm), tr),
                  pl.BlockSpec((d_model, NSA_WIDTH), full), pl.BlockSpec((d_model, SB_WIDTH), full),
                  pl.BlockSpec((d_model, d_model), full), pl.BlockSpec((1, d_model), full)],
        out_specs=pl.BlockSpec((tm, d_model), row),
        out_shape=jax.ShapeDtypeStruct((m, d_model), F32),
        compiler_params=pltpu.CompilerParams(dimension_semantics=("arbitrary",), vmem_limit_bytes=VMEM_LIMIT),
        name="mix_prompt",
    )(x2d, oat, zat, obt, zbt, gmt, wat, wbt, wot, g_post)


def _mix_sample_kernel(x_ref, oa_ref, za_ref, ob_ref, zb_ref, gm_ref, wat_ref, wbt_ref, wot_ref, g_ref, y_ref, *,
                       d_model):
    ya = _dot_nt((oa_ref[...] * za_ref[...]).astype(BF16), wat_ref[...])
    yb = _dot_nt((ob_ref[...] * zb_ref[...]).astype(BF16), wbt_ref[...])
    gm = gm_ref[...]
    mixed = gm[:, :d_model] * ya + gm[:, d_model:] * yb
    out = _dot_nt(mixed.astype(BF16), wot_ref[...])
    y_ref[...] = x_ref[...] + _rms_scale(out, g_ref[...])


def _mix_sample(x2d, o_a, za, o_b, zb, gm, wat, wbt, wot, g_post):
    m, d_model = x2d.shape
    args = (x2d, o_a, za, o_b, zb, gm, wat, wbt, wot, g_post)
    return pl.pallas_call(
        functools.partial(_mix_sample_kernel, d_model=d_model),
        grid=(1,),
        in_specs=[pl.BlockSpec(a.shape, lambda i: (0, 0)) for a in args],
        out_specs=pl.BlockSpec((m, d_model), lambda i: (0, 0)),
        out_shape=jax.ShapeDtypeStruct((m, d_model), F32),
        compiler_params=pltpu.CompilerParams(dimension_semantics=("arbitrary",), vmem_limit_bytes=VMEM_LIMIT),
        name="mix_sample",
    )(*args)


def _head_pad(q):
    q2 = jnp.concatenate([q, q], axis=1)
    row = lax.broadcasted_iota(jnp.int32, q2.shape, 0)
    lane = lax.broadcasted_iota(jnp.int32, q2.shape, 1)
    return jnp.where(row // NSA_GROUP == lane // HEAD_DIM, q2, 0.0)


def _sb_decode_kernel(pt_ref, q_ref, *refs, ppg):
    page_refs = refs[:ppg]
    o_ref = refs[ppg]
    run_ref, acc_ref = refs[ppg + 1:]
    s = pl.program_id(1)

    @pl.when(s == 0)
    def _():
        run_ref[...] = jnp.zeros_like(run_ref)
        acc_ref[...] = jnp.zeros_like(acc_ref)

    q = q_ref[0]
    qrep = jnp.concatenate([q] * SB_HEADS, axis=1)
    row = lax.broadcasted_iota(jnp.int32, qrep.shape, 0)
    lane = lax.broadcasted_iota(jnp.int32, qrep.shape, 1)
    diag = row == lane // HEAD_DIM
    qbd = jnp.where(diag, qrep, 0.0).astype(BF16)

    z = jnp.concatenate([_dot(qbd, pr[0, :SB_WIDTH, :].astype(BF16)) for pr in page_refs], axis=0)
    sp = _softplus(z)
    r = lax.broadcasted_iota(jnp.int32, (PAGE_SIZE, PAGE_SIZE), 0)
    c = lax.broadcasted_iota(jnp.int32, (PAGE_SIZE, PAGE_SIZE), 1)
    lower = jnp.where(r > c, 1.0, 0.0).astype(BF16)
    hi, lo = _split_bf16(sp)
    drop = _dot(hi, lower) + _dot(lo, lower)
    tot = jnp.sum(sp, axis=1, keepdims=True)
    run = run_ref[...]
    acc = acc_ref[...]
    for n in range(ppg):
        sl = slice(n * SB_HEADS, (n + 1) * SB_HEADS)
        a = jnp.exp(z[sl] - sp[sl] - drop[sl] - run)
        acc = acc + _dot_nt(a.astype(BF16), page_refs[n][0, SB_WIDTH:, :].astype(BF16))
        run = run + tot[sl]
    run_ref[...] = run
    acc_ref[...] = acc

    @pl.when(s == pl.num_programs(1) - 1)
    def _():
        o_ref[0] = jnp.sum(jnp.where(diag, acc, 0.0), axis=0, keepdims=True)


def _sb_decode(page_table_flat, q_b, cache_t, *, dec_batch, n_pages, ppg=8):
    steps = n_pages // ppg

    def page_map(n):
        return lambda b, s, pt: (pt[b * n_pages + n_pages - 1 - (s * ppg + n)], 0, 0)

    grid_spec = pltpu.PrefetchScalarGridSpec(
        num_scalar_prefetch=1, grid=(dec_batch, steps),
        in_specs=[pl.BlockSpec((1, SB_HEADS, HEAD_DIM), lambda b, s, pt: (b, 0, 0))]
        + [pl.BlockSpec((1, 2 * SB_WIDTH, PAGE_SIZE), page_map(n)) for n in range(ppg)],
        out_specs=pl.BlockSpec((1, 1, SB_WIDTH), lambda b, s, pt: (b, 0, 0)),
        scratch_shapes=[pltpu.VMEM((SB_HEADS, 1), F32), pltpu.VMEM((SB_HEADS, SB_WIDTH), F32)])
    return pl.pallas_call(
        functools.partial(_sb_decode_kernel, ppg=ppg),
        grid_spec=grid_spec,
        out_shape=jax.ShapeDtypeStruct((dec_batch, 1, SB_WIDTH), F32),
        compiler_params=pltpu.CompilerParams(dimension_semantics=("arbitrary", "arbitrary"),
                                             vmem_limit_bytes=VMEM_LIMIT),
        name="sb_decode",
    )(page_table_flat, q_b, *([cache_t] * ppg))


def _nsa_cmp_decode_kernel(pt_ref, q_ref, wp_ref, *refs, ppg, nbp):
    page_refs = refs[:ppg]
    ocmp_ref, sel_ref = refs[ppg:ppg + 2]
    pool_ref = refs[ppg + 2]
    s = pl.program_id(1)
    rows = ppg * BLOCKS_PER_PAGE
    kc = jnp.zeros((rows, KV_PAIR), F32)
    vc = jnp.zeros((rows, KV_PAIR), F32)
    for n, pr in enumerate(page_refs):
        kc = kc + _dot_nt(wp_ref[0, n], pr[0, :KV_PAIR, :].astype(BF16))
        vc = vc + _dot_nt(wp_ref[1, n], pr[0, KV_PAIR:, :].astype(BF16))
    base = pl.multiple_of(s * rows, rows)
    pool_ref[pl.ds(base, rows), :] = jnp.concatenate([kc, vc], axis=1)

    @pl.when(s == pl.num_programs(1) - 1)
    def _():
        qpad = _head_pad(q_ref[0]).astype(BF16)
        pooled = pool_ref[...]
        sc = _dot_nt(qpad, pooled[:, :KV_PAIR].astype(BF16))
        e = jnp.exp(sc - jnp.max(sc, axis=1, keepdims=True))
        p = e / jnp.sum(e, axis=1, keepdims=True)
        o = _dot(p.astype(BF16), pooled[:, KV_PAIR:].astype(BF16))
        hrow = lax.broadcasted_iota(jnp.int32, o.shape, 0)
        ocmp_ref[0] = jnp.where(hrow < NSA_GROUP, o, pltpu.roll(o, HEAD_DIM, 1))[:, :HEAD_DIM]

        blk_l = lax.broadcasted_iota(jnp.int32, (1, nbp), 1)
        blk_s = lax.broadcasted_iota(jnp.int32, (nbp, 1), 0)
        eye = lax.broadcasted_iota(jnp.int32, (nbp, nbp), 0) == lax.broadcasted_iota(jnp.int32, (nbp, nbp), 1)
        n_sel = NSA_TOPK - 1
        kslot = lax.broadcasted_iota(jnp.int32, (NSA_TOPK, nbp), 0)
        for gi in range(NSA_KV_HEADS):
            imp = jnp.sum(p[gi * NSA_GROUP:(gi + 1) * NSA_GROUP], axis=0, keepdims=True)
            forced = (blk_l == 0) | (blk_l == nbp - 1)
            srow = jnp.where(forced, FORCE_SCORE, imp)
            scol = jnp.sum(jnp.where(eye, srow, 0.0), axis=1, keepdims=True)
            beats = (scol > srow) | ((scol == srow) & (blk_s < blk_l))
            rank = jnp.sum(jnp.where(beats, 1.0, 0.0), axis=0, keepdims=True)
            sel = rank < n_sel
            selcol = jnp.sum(jnp.where(eye & sel, 1.0, 0.0), axis=1, keepdims=True) > 0.5
            slot = jnp.sum(jnp.where(selcol & (blk_s < blk_l), 1.0, 0.0), axis=0, keepdims=True)
            onehot = sel & (slot.astype(jnp.int32) == kslot)
            idx = jnp.sum(jnp.where(onehot, blk_l, 0), axis=1, keepdims=True)
            sel_ref[0, gi] = jnp.broadcast_to(idx, (NSA_TOPK, LANES))


def _nsa_cmp_decode(page_table_flat, q_a, wp, cache_t, *, dec_batch, n_pages, ppg=8):
    steps = n_pages // ppg
    nbp = n_pages * BLOCKS_PER_PAGE

    def page_map(n):
        return lambda b, s, pt: (pt[b * n_pages + s * ppg + n], 0, 0)

    grid_spec = pltpu.PrefetchScalarGridSpec(
        num_scalar_prefetch=1, grid=(dec_batch, steps),
        in_specs=[pl.BlockSpec((1, NSA_HEADS, HEAD_DIM), lambda b, s, pt: (b, 0, 0)),
                  pl.BlockSpec(wp.shape, lambda b, s, pt: (0, 0, 0, 0))]
        + [pl.BlockSpec((1, 2 * KV_PAIR, PAGE_SIZE), page_map(n)) for n in range(ppg)],
        out_specs=[pl.BlockSpec((1, NSA_HEADS, HEAD_DIM), lambda b, s, pt: (b, 0, 0)),
                   pl.BlockSpec((1, NSA_KV_HEADS, NSA_TOPK, LANES), lambda b, s, pt: (b, 0, 0, 0))],
        scratch_shapes=[pltpu.VMEM((nbp, 2 * KV_PAIR), F32)])
    return pl.pallas_call(
        functools.partial(_nsa_cmp_decode_kernel, ppg=ppg, nbp=nbp),
        grid_spec=grid_spec,
        out_shape=[jax.ShapeDtypeStruct((dec_batch, NSA_HEADS, HEAD_DIM), F32),
                   jax.ShapeDtypeStruct((dec_batch, NSA_KV_HEADS, NSA_TOPK, LANES), jnp.int32)],
        compiler_params=pltpu.CompilerParams(dimension_semantics=("arbitrary", "arbitrary"),
                                             vmem_limit_bytes=VMEM_LIMIT),
        name="nsa_cmp_decode",
    )(page_table_flat, q_a, wp, *([cache_t] * ppg))


def _nsa_sel_decode_kernel(sel_ref, pt_ref, q_ref, ocmp_ref, ga_ref, new_ref, winp_ref, *refs, n_blk):
    blk_refs = refs[:NSA_KV_HEADS * n_blk]
    o_ref, wino_ref = refs[NSA_KV_HEADS * n_blk:]
    b = pl.program_id(0)
    qb = _head_pad(q_ref[0]).astype(BF16)
    new = new_ref[0]
    hrow = lax.broadcasted_iota(jnp.int32, (NSA_HEADS, KV_PAIR), 0)
    top = hrow < NSA_GROUP

    def fold(o):
        return jnp.where(top, o, pltpu.roll(o, HEAD_DIM, 1))[:, :HEAD_DIM]

    def softmax(sc, mask):
        sc = jnp.where(mask, sc, NEG_INF)
        e = jnp.where(mask, jnp.exp(sc - jnp.max(sc, axis=1, keepdims=True)), 0.0)
        return e / jnp.sum(e, axis=1, keepdims=True)

    lane = lax.broadcasted_iota(jnp.int32, (1, PAGE_SIZE), 1)
    first = lax.broadcasted_iota(jnp.int32, (KV_PAIR, PAGE_SIZE), 1) == 0
    k_new = jnp.where(first, new[2 * KV_PAIR:3 * KV_PAIR], 0.0).astype(BF16)
    v_new = jnp.where(first, new[3 * KV_PAIR:4 * KV_PAIR], 0.0).astype(BF16)
    o_g = []
    for gi in range(NSA_KV_HEADS):
        scs, masks = [], []
        for n in range(n_blk):
            half = sel_ref[(b * NSA_KV_HEADS + gi) * NSA_TOPK + n] % BLOCKS_PER_PAGE
            scs.append(_dot(qb, blk_refs[gi * n_blk + n][0, :KV_PAIR, :].astype(BF16)))
            masks.append(lane // NSA_BLOCK == half)
        scs.append(_dot(qb, k_new))
        masks.append(lane == 0)
        p = softmax(jnp.concatenate(scs, axis=1), jnp.concatenate(masks, axis=1)).astype(BF16)
        o = _dot_nt(p[:, n_blk * PAGE_SIZE:], v_new)
        for n in range(n_blk):
            o = o + _dot_nt(p[:, n * PAGE_SIZE:(n + 1) * PAGE_SIZE],
                            blk_refs[gi * n_blk + n][0, KV_PAIR:, :].astype(BF16))
        o_g.append(o)
    o_slc = fold(jnp.where(top, o_g[0], o_g[1]))

    wp = winp_ref[0]
    w = wp.shape[1]
    wl = lax.broadcasted_iota(jnp.int32, wp.shape, 1)
    shifted = jnp.where(wl == w - 1, new[4 * KV_PAIR:], pltpu.roll(wp, w - 1, 1))
    wino_ref[0] = shifted
    sc = _dot(qb, shifted[:KV_PAIR].astype(BF16))
    p = softmax(sc, jnp.full((1, w), True)).astype(BF16)
    o_win = fold(_dot_nt(p, shifted[KV_PAIR:].astype(BF16)))

    ga = ga_ref[0]
    o_ref[0] = ga[0] * ocmp_ref[0] + ga[1] * o_slc + ga[2] * o_win


def _nsa_sel_decode(sel_flat, page_table_flat, q_a, o_cmp, ga, new_kv, win_past_t, cache_t, *, dec_batch, n_pages):
    n_blk = NSA_TOPK - 1
    w = win_past_t.shape[2]

    def blk_map(gi, n):
        def f(b, sel, pt):
            blk = sel[(b * NSA_KV_HEADS + gi) * NSA_TOPK + n]
            return (pt[b * n_pages + blk // BLOCKS_PER_PAGE], 1, 0)
        return f

    grid_spec = pltpu.PrefetchScalarGridSpec(
        num_scalar_prefetch=2, grid=(dec_batch,),
        in_specs=[pl.BlockSpec((1, NSA_HEADS, HEAD_DIM), lambda b, sel, pt: (b, 0, 0)),
                  pl.BlockSpec((1, NSA_HEADS, HEAD_DIM), lambda b, sel, pt: (b, 0, 0)),
                  pl.BlockSpec((1, NSA_BRANCHES, NSA_HEADS, 1), lambda b, sel, pt: (b, 0, 0, 0)),
                  pl.BlockSpec((1, KVA_WIDTH, 1), lambda b, sel, pt: (b, 0, 0)),
                  pl.BlockSpec((1, 2 * KV_PAIR, w), lambda b, sel, pt: (b, 0, 0))]
        + [pl.BlockSpec((1, 2 * KV_PAIR, PAGE_SIZE), blk_map(gi, n))
           for gi in range(NSA_KV_HEADS) for n in range(n_blk)],
        out_specs=[pl.BlockSpec((1, NSA_HEADS, HEAD_DIM), lambda b, sel, pt: (b, 0, 0)),
                   pl.BlockSpec((1, 2 * KV_PAIR, w), lambda b, sel, pt: (b, 0, 0))])
    return pl.pallas_call(
        functools.partial(_nsa_sel_decode_kernel, n_blk=n_blk),
        grid_spec=grid_spec,
        out_shape=[jax.ShapeDtypeStruct((dec_batch, NSA_HEADS, HEAD_DIM), F32),
                   jax.ShapeDtypeStruct((dec_batch, 2 * KV_PAIR, w), F32)],
        compiler_params=pltpu.CompilerParams(dimension_semantics=("arbitrary",), vmem_limit_bytes=VMEM_LIMIT),
        name="nsa_sel_decode",
    )(sel_flat, page_table_flat, q_a, o_cmp, ga, new_kv, win_past_t, *([cache_t] * (NSA_KV_HEADS * n_blk)))


def _feature_major(a, lead):
    nl = len(lead)
    t = jnp.moveaxis(a, nl, -1)
    return t.reshape(lead + (-1, a.shape[nl]))


def _time_major(a_t, feat_shape):
    lead, _, time = a_t.shape
    return jnp.moveaxis(a_t.reshape((lead,) + feat_shape + (time,)), -1, 1)


def _layer(x_prompt, x_sample, nsa_cache, win_cache, sb_cache, page_table, w_in, w_cmp, w_a, w_b, w_o, g_pre, g_post):
    batch, seq, d_model = x_prompt.shape
    dec_batch, dec_seq, _ = x_sample.shape
    n_pages = page_table.shape[1]
    past_len = n_pages * PAGE_SIZE
    n_phys = nsa_cache.shape[0]
    tm, ppg = 256, 16
    assert dec_seq == 1 and seq % tm == 0 and seq >= NSA_WINDOW and past_len >= NSA_WINDOW
    assert n_pages % ppg == 0 and past_len // NSA_BLOCK >= NSA_TOPK
    assert win_cache.shape[1] == NSA_WINDOW and nsa_cache.shape[1] == PAGE_SIZE

    wt_perm = _permute_w_in_t(w_in, d_model)
    wat, wbt, wot = w_a.T.astype(BF16), w_b.T.astype(BF16), w_o.T.astype(BF16)
    g_pre2, g_post2 = g_pre.reshape(1, d_model), g_post.reshape(1, d_model)
    nsa_feat = (4, NSA_KV_HEADS, HEAD_DIM)
    win_feat = (2, NSA_KV_HEADS, HEAD_DIM)
    sb_feat = (2, SB_HEADS, HEAD_DIM)

    xp = x_prompt.reshape(batch * seq, d_model)
    cos, sin = _rope_angles(jnp.arange(seq, dtype=jnp.int32))
    wp_prompt = _pool_weights(w_cmp, 1, tm // NSA_BLOCK, SUBLANES).reshape(2 * SUBLANES, tm)
    (nsa_t, win_t, sb_t, za_t, zb_t, gm_t, qa_t, ga_t, kb, v_t, pooled, qb_t, kbb, vb_t) = _project_prompt(
        xp, g_pre2, wt_perm, cos.T, sin.T, wp_prompt, batch=batch, seq=seq, tm=tm)
    oa_t = _nsa_prompt(qa_t, kb.reshape(batch, seq, 2 * KV_PAIR), v_t,
                       pooled.reshape(batch, seq // NSA_BLOCK, 2 * KV_PAIR), ga_t, batch=batch, seq=seq)
    ob_t = _sb_prompt(qb_t, kbb.reshape(batch, seq, SB_WIDTH), vb_t, batch=batch, seq=seq)
    y_prompt = _mix_prompt(xp, oa_t, za_t, ob_t, zb_t, gm_t, wat, wbt, wot, g_post2,
                           batch=batch, seq=seq, tm=tm).reshape(batch, seq, d_model)
    nsa_kv_prompt = _time_major(nsa_t, nsa_feat)
    win_kv_prompt = _time_major(win_t[:, :, seq - NSA_WINDOW:], win_feat)
    sb_kv_prompt = _time_major(sb_t, sb_feat)

    xs = x_sample.reshape(dec_batch, d_model)
    tabs_s = _rope_lane_tables(jnp.full((dec_batch,), past_len, jnp.int32))
    (nsa_s, win_s, sb_s, za_s, zb_s, gm_s, qa_s, ga_s, qb_s) = _project_sample(xs, g_pre2, wt_perm, tabs_s)
    pt_flat = page_table.reshape(-1)
    sb_cache_t = _feature_major(sb_cache, (n_phys,))
    nsa_cache_t = _feature_major(nsa_cache, (n_phys,))
    win_cache_t = _feature_major(win_cache, (dec_batch,))
    o_b_s = _sb_decode(pt_flat, qb_s.reshape(dec_batch, SB_HEADS, HEAD_DIM), sb_cache_t,
                       dec_batch=dec_batch, n_pages=n_pages, ppg=ppg)
    wp_dec = _pool_weights(w_cmp, ppg, BLOCKS_PER_PAGE, ppg * BLOCKS_PER_PAGE)
    qa3 = qa_s.reshape(dec_batch, NSA_HEADS, HEAD_DIM)
    o_cmp_s, sel = _nsa_cmp_decode(pt_flat, qa3, wp_dec, nsa_cache_t, dec_batch=dec_batch, n_pages=n_pages, ppg=ppg)
    ga3 = ga_s[:, :GATE_A].reshape(dec_batch, NSA_BRANCHES, NSA_HEADS, 1)
    new_kv = jnp.concatenate([nsa_s, win_s], axis=1).reshape(dec_batch, KVA_WIDTH, 1)
    o_a_s, win_out_t = _nsa_sel_decode(sel[:, :, :, 0].reshape(-1), pt_flat, qa3, o_cmp_s, ga3, new_kv,
                                       win_cache_t, nsa_cache_t, dec_batch=dec_batch, n_pages=n_pages)
    y_sample = _mix_sample(xs, o_a_s.reshape(dec_batch, NSA_WIDTH), za_s, o_b_s.reshape(dec_batch, SB_WIDTH), zb_s,
                           gm_s, wat, wbt, wot, g_post2).reshape(dec_batch, 1, d_model)
    nsa_kv_sample = nsa_s.reshape((dec_batch, 1) + nsa_feat)
    win_kv_sample = _time_major(win_out_t, win_feat)
    sb_kv_sample = sb_s.reshape((dec_batch, 1) + sb_feat)
    return (y_prompt, y_sample, nsa_kv_prompt, win_kv_prompt, sb_kv_prompt, nsa_kv_sample, win_kv_sample,
            sb_kv_sample)


def kernel(x_prompt, x_sample, cache_nsa_kv, cache_nsa_win_kv, cache_sb_kv, page_table, w_in, w_cmp, w_branch_a,
           w_branch_b, w_out, g_pre, g_post):
    hp, hs = x_prompt, x_sample
    caches = [[] for _ in range(6)]
    for layer in range(w_in.shape[0]):
        outs = _layer(hp, hs, cache_nsa_kv[layer], cache_nsa_win_kv[layer], cache_sb_kv[layer], page_table,
                      w_in[layer], w_cmp[layer], w_branch_a[layer], w_branch_b[layer], w_out[layer],
                      g_pre[layer], g_post[layer])
        hp, hs = outs[0], outs[1]
        for acc, o in zip(caches, outs[2:]):
            acc.append(o)
    return (hp, hs) + tuple(jnp.stack(c) for c in caches)
```

```python
import functools

import jax
import jax.numpy as jnp
from jax import lax
from jax.experimental import pallas as pl
from jax.experimental.pallas import tpu as pltpu

HEAD_DIM = 64
ROT_DIM = HEAD_DIM // 4
ROPE_THETA = 500000.0
NSA_HEADS = 8
NSA_KV_HEADS = 2
NSA_GROUP = NSA_HEADS // NSA_KV_HEADS
NSA_BRANCHES = 3
NSA_BLOCK = 64
NSA_TOPK = 16
NSA_WINDOW = 512
NSA_WIDTH = NSA_HEADS * HEAD_DIM
SB_HEADS = 8
SB_WIDTH = SB_HEADS * HEAD_DIM
N_MERGE = 2
PAGE_SIZE = 128
RMS_EPS = 1e-6
NEG_INF = -1e30
FORCE_SCORE = 1e3
SCALE = HEAD_DIM ** -0.5
LOG2E = 1.4426950408889634

LANES = 128
SUBLANES = 8
KV_PAIR = NSA_KV_HEADS * HEAD_DIM
assert KV_PAIR == LANES and PAGE_SIZE == LANES and PAGE_SIZE % NSA_BLOCK == 0
KVA_WIDTH = 2 * NSA_BRANCHES * KV_PAIR
GATE_A = NSA_BRANCHES * NSA_HEADS
BLOCKS_PER_PAGE = PAGE_SIZE // NSA_BLOCK
VMEM_LIMIT = 56 * 1024 * 1024

C_QA = 0
C_KVA = C_QA + NSA_WIDTH
C_ZA = C_KVA + KVA_WIDTH
C_QKVB = C_ZA + NSA_WIDTH
C_ZB = C_QKVB + 3 * SB_WIDTH
C_GM = C_ZB + SB_WIDTH

BF16 = jnp.bfloat16
F32 = jnp.float32


def _dot(a, b):
    return jnp.dot(a, b, preferred_element_type=F32)


def _dot_tn(a, b):
    return lax.dot_general(a, b, (((0,), (0,)), ((), ())), preferred_element_type=F32)


def _dot_nt(a, b):
    return lax.dot_general(a, b, (((1,), (1,)), ((), ())), preferred_element_type=F32)


def _sigmoid(x):
    return 1.0 / (1.0 + jnp.exp(-x))


def _rms_scale(x, g):
    return x * lax.rsqrt(jnp.mean(x * x, axis=-1, keepdims=True) + RMS_EPS) * g


def _rope_rows(v, cos, sin):
    half = ROT_DIM // 2
    parts = []
    for base in range(0, v.shape[0], HEAD_DIM):
        x1, x2 = v[base:base + half], v[base + half:base + 2 * half]
        parts += [x1 * cos - x2 * sin, x2 * cos + x1 * sin, v[base + 2 * half:base + HEAD_DIM]]
    return jnp.concatenate(parts, axis=0)


def _proj_prompt_kernel(x_ref, g_ref, wt_ref, cos_ref, sin_ref, wp_ref,
                        nsa_ref, win_ref, sb_ref, za_ref, zb_ref, gm_ref, qa_ref, ga_ref, kb_ref, vt_ref,
                        pool_ref, qb_ref, kbb_ref, vbt_ref, *, d_model):
    hb = _rms_scale(x_ref[...], g_ref[...]).astype(BF16)
    cos, sin = cos_ref[...], sin_ref[...]

    def seg(lo, width):
        return _dot_nt(wt_ref[lo:lo + width, :], hb)

    qa_ref[0] = (_rope_rows(seg(C_QA, NSA_WIDTH), cos, sin) * SCALE).astype(BF16)
    kv = seg(C_KVA, KVA_WIDTH)
    cmp_k = _rope_rows(kv[0 * LANES:1 * LANES], cos, sin)
    cmp_v = kv[1 * LANES:2 * LANES]
    slc_k = _rope_rows(kv[2 * LANES:3 * LANES], cos, sin)
    slc_v = kv[3 * LANES:4 * LANES]
    win_k = _rope_rows(kv[4 * LANES:5 * LANES], cos, sin)
    win_v = kv[5 * LANES:6 * LANES]
    nsa_ref[0] = jnp.concatenate([cmp_k, cmp_v, slc_k, slc_v], axis=0)
    win_ref[0] = jnp.concatenate([win_k, win_v], axis=0)
    kb_ref[...] = jnp.concatenate([slc_k.T, win_k.T], axis=1).astype(BF16)
    vt_ref[0] = jnp.concatenate([slc_v, win_v], axis=0).astype(BF16)
    wp = wp_ref[...]
    nblk = pool_ref.shape[1]
    pooled = jnp.concatenate([_dot_nt(wp[:SUBLANES], cmp_k.astype(BF16)),
                              _dot_nt(wp[SUBLANES:], cmp_v.astype(BF16))], axis=1)
    pool_ref[0] = pooled[:nblk]

    za = seg(C_ZA, NSA_WIDTH)
    za_ref[0] = za * _sigmoid(za)
    qkvb = seg(C_QKVB, 3 * SB_WIDTH)
    qb_ref[0] = (qkvb[:SB_WIDTH] * SCALE).astype(BF16)
    sb_ref[0] = qkvb[SB_WIDTH:]
    kbb_ref[...] = qkvb[SB_WIDTH:2 * SB_WIDTH].T.astype(BF16)
    vbt_ref[0] = qkvb[2 * SB_WIDTH:].astype(BF16)
    zb = seg(C_ZB, SB_WIDTH)
    zb_ref[0] = zb * _sigmoid(zb)
    gm_ref[0] = _sigmoid(seg(C_GM, N_MERGE * d_model))
    ga_ref[0] = _sigmoid(seg(C_GM + N_MERGE * d_model, LANES))


def _project_prompt(x2d, g_pre, wt_perm, cos_t, sin_t, wp, *, batch, seq, tm):
    m, d_model = x2d.shape
    nt = seq // tm
    n_rows = wt_perm.shape[0]
    nblk = tm // NSA_BLOCK
    row = lambda i: (i, 0)
    full = lambda i: (0, 0)
    tab = lambda i: (0, i % nt)
    tr = lambda i: (i // nt, 0, i % nt)
    half = ROT_DIM // 2
    in_specs = [pl.BlockSpec((tm, d_model), row), pl.BlockSpec((1, d_model), full),
                pl.BlockSpec((n_rows, d_model), full),
                pl.BlockSpec((half, tm), tab), pl.BlockSpec((half, tm), tab),
                pl.BlockSpec((2 * SUBLANES, tm), full)]
    sds = jax.ShapeDtypeStruct

    def feat(width, dtype):
        return sds((batch, width, seq), dtype), pl.BlockSpec((1, width, tm), tr)

    def rows(width, dtype):
        return sds((m, width), dtype), pl.BlockSpec((tm, width), row)

    outs = [feat(4 * KV_PAIR, F32), feat(2 * KV_PAIR, F32), feat(2 * SB_WIDTH, F32),
            feat(NSA_WIDTH, F32), feat(SB_WIDTH, F32), feat(N_MERGE * d_model, F32),
            feat(NSA_WIDTH, BF16), feat(LANES, F32), rows(2 * KV_PAIR, BF16), feat(2 * KV_PAIR, BF16),
            (sds((m // tm, nblk, 2 * KV_PAIR), F32), pl.BlockSpec((1, nblk, 2 * KV_PAIR), lambda i: (i, 0, 0))),
            feat(SB_WIDTH, BF16), rows(SB_WIDTH, BF16), feat(SB_WIDTH, BF16)]
    return pl.pallas_call(
        functools.partial(_proj_prompt_kernel, d_model=d_model),
        grid=(m // tm,), in_specs=in_specs, out_specs=[o[1] for o in outs], out_shape=[o[0] for o in outs],
        compiler_params=pltpu.CompilerParams(dimension_semantics=("arbitrary",), vmem_limit_bytes=VMEM_LIMIT),
        name="proj_prompt",
    )(x2d, g_pre, wt_perm, cos_t, sin_t, wp)


def _rope_lanes(v, c, s_up, s_dn):
    half = ROT_DIM // 2
    outs = []
    for j in range(v.shape[1] // LANES):
        blk = v[:, j * LANES:(j + 1) * LANES]
        outs.append(blk * c + pltpu.roll(blk, LANES - half, 1) * s_up + pltpu.roll(blk, half, 1) * s_dn)
    return outs[0] if len(outs) == 1 else jnp.concatenate(outs, axis=1)


def _proj_sample_kernel(x_ref, g_ref, wt_ref, cos_ref, sup_ref, sdn_ref,
                        nsa_ref, win_ref, sbkv_ref, za_ref, zb_ref, gm_ref, qa_ref, ga_ref, qb_ref, *, d_model):
    hb = _rms_scale(x_ref[...], g_ref[...]).astype(BF16)
    cos, s_up, s_dn = cos_ref[...], sup_ref[...], sdn_ref[...]

    def seg(lo, width):
        return _dot_nt(hb, wt_ref[lo:lo + width, :])

    qa_ref[...] = _rope_lanes(seg(C_QA, NSA_WIDTH), cos, s_up, s_dn) * SCALE
    kv = seg(C_KVA, KVA_WIDTH)
    parts = []
    for j in range(2 * NSA_BRANCHES):
        blk = kv[:, j * LANES:(j + 1) * LANES]
        parts.append(_rope_lanes(blk, cos, s_up, s_dn) if j % 2 == 0 else blk)
    nsa_ref[...] = jnp.concatenate(parts[:4], axis=1)
    win_ref[...] = jnp.concatenate(parts[4:], axis=1)
    za = seg(C_ZA, NSA_WIDTH)
    za_ref[...] = za * _sigmoid(za)
    qkvb = seg(C_QKVB, 3 * SB_WIDTH)
    qb_ref[...] = qkvb[:, :SB_WIDTH] * SCALE
    sbkv_ref[...] = qkvb[:, SB_WIDTH:]
    zb = seg(C_ZB, SB_WIDTH)
    zb_ref[...] = zb * _sigmoid(zb)
    gm_ref[...] = _sigmoid(seg(C_GM, N_MERGE * d_model))
    ga_ref[...] = _sigmoid(seg(C_GM + N_MERGE * d_model, LANES))


def _project_sample(x2d, g_pre, wt_perm, tables):
    m, d_model = x2d.shape
    full = lambda i: (0, 0)
    widths = [4 * KV_PAIR, 2 * KV_PAIR, 2 * SB_WIDTH, NSA_WIDTH, SB_WIDTH, N_MERGE * d_model,
              NSA_WIDTH, LANES, SB_WIDTH]
    return pl.pallas_call(
        functools.partial(_proj_sample_kernel, d_model=d_model),
        grid=(1,),
        in_specs=[pl.BlockSpec((m, d_model), full), pl.BlockSpec((1, d_model), full),
                  pl.BlockSpec(wt_perm.shape, full)] + [pl.BlockSpec((m, LANES), full)] * 3,
        out_specs=[pl.BlockSpec((m, w), full) for w in widths],
        out_shape=[jax.ShapeDtypeStruct((m, w), F32) for w in widths],
        compiler_params=pltpu.CompilerParams(dimension_semantics=("arbitrary",), vmem_limit_bytes=VMEM_LIMIT),
        name="proj_sample",
    )(x2d, g_pre, wt_perm, *tables)


def _rope_angles(pos):
    half = ROT_DIM // 2
    inv_freq = ROPE_THETA ** (-jnp.arange(half, dtype=F32) / half)
    ang = pos.astype(F32)[:, None] * inv_freq[None, :]
    return jnp.cos(ang), jnp.sin(ang)


def _rope_lane_tables(pos):
    cos, sin = _rope_angles(pos)
    n = pos.shape[0]
    half = ROT_DIM // 2
    ones = jnp.ones((n, HEAD_DIM - ROT_DIM), F32)
    zeros = jnp.zeros((n, HEAD_DIM - ROT_DIM), F32)
    zh = jnp.zeros((n, half), F32)
    rep = LANES // HEAD_DIM
    return tuple(jnp.tile(t, (1, rep)) for t in (jnp.concatenate([cos, cos, ones], axis=1),
                                                 jnp.concatenate([-sin, zh, zeros], axis=1),
                                                 jnp.concatenate([zh, sin, zeros], axis=1)))


def _permute_w_in_t(w_in, d_model):
    c = [NSA_WIDTH, KVA_WIDTH, NSA_WIDTH, GATE_A, 3 * SB_WIDTH, SB_WIDTH, N_MERGE * d_model]
    o = [0]
    for s in c:
        o.append(o[-1] + s)
    wt = w_in.T
    pad = jnp.zeros((LANES - GATE_A, d_model), w_in.dtype)
    return jnp.concatenate([wt[:o[3]], wt[o[4]:], wt[o[3]:o[4]], pad], axis=0).astype(BF16)


def _pool_weights(w_cmp, n_tiles, blocks_per_tile, rows):
    lane = jnp.arange(blocks_per_tile * NSA_BLOCK)
    owner = jnp.arange(n_tiles)[:, None, None] * blocks_per_tile + (lane // NSA_BLOCK)[None, None, :]
    hit = jnp.arange(rows)[None, :, None] == owner
    w_lane = jnp.tile(w_cmp, (1, blocks_per_tile))
    return jnp.where(hit[None], w_lane[:, None, None, :], 0.0).astype(BF16)


def _nsa_prompt_kernel(q_ref, kslc_ref, kwin_ref, vslc_ref, vwin_ref, pool_ref, ga_ref, o_ref, sel_ref, acc_ref, *,
                       tq, tk, nb):
    i = pl.program_id(1)
    lanes = NSA_GROUP * tq
    kvh = NSA_KV_HEADS
    row = lax.broadcasted_iota(jnp.int32, (KV_PAIR, lanes), 0)
    pos1 = i * tq + lax.broadcasted_iota(jnp.int32, (1, tq), 1)
    pos = jnp.concatenate([pos1] * NSA_GROUP, axis=1)
    blk = lax.broadcasted_iota(jnp.int32, (nb, 1), 0)
    cmask = ((blk + 1) * NSA_BLOCK - 1) <= pos
    cur = pos1 // NSA_BLOCK
    valid = blk <= cur
    forced = (blk == 0) | (blk == cur) | (blk == cur - 1)
    n_sel = min(NSA_TOPK, nb)
    pooled = pool_ref[0]
    kc = pooled[:, :KV_PAIR].astype(BF16)
    vc = pooled[:, KV_PAIR:].astype(BF16)

    qpads, o_cmp = [], []
    for g in range(kvh):
        qt = q_ref[0, g * NSA_GROUP * HEAD_DIM:(g + 1) * NSA_GROUP * HEAD_DIM, :]
        q4 = jnp.concatenate([qt[hh * HEAD_DIM:(hh + 1) * HEAD_DIM] for hh in range(NSA_GROUP)], axis=1)
        qpad = jnp.where(row // HEAD_DIM == g, jnp.concatenate([q4, q4], axis=0), jnp.zeros((), BF16))
        qpads.append(qpad)
        s = jnp.where(cmask, _dot(kc, qpad), NEG_INF)
        e = jnp.exp(s - jnp.max(s, axis=0, keepdims=True))
        p = jnp.where(cmask, e / jnp.sum(e, axis=0, keepdims=True), 0.0)
        o_cmp.append(_dot_tn(vc, p.astype(BF16))[g * HEAD_DIM:(g + 1) * HEAD_DIM])
        imp = p[:, 0:tq]
        for hh in range(1, NSA_GROUP):
            imp = imp + p[:, hh * tq:(hh + 1) * tq]
        score = jnp.where(valid, jnp.where(forced, FORCE_SCORE, imp), NEG_INF)
        for j in range(nb):
            sj = score[j:j + 1, :]
            beats = (score > sj) | ((score == sj) & (blk < j))
            cnt = jnp.sum(jnp.where(beats, 1.0, 0.0), axis=0, keepdims=True)
            sel_ref[g, j] = jnp.where(cnt < n_sel, 1.0, 0.0)

    bpt = tk // NSA_BLOCK

    def sel_mask(g, kt):
        rows = [jnp.broadcast_to(sel_ref[g, kt * bpt + r], (NSA_BLOCK, tq)) for r in range(bpt)]
        selt = jnp.concatenate(rows, axis=0) > 0.5
        return jnp.concatenate([selt] * NSA_GROUP, axis=1)

    def step(kt, carry, near):
        ms, ls = list(carry[0]), list(carry[1])
        off = pl.multiple_of(kt * tk, tk)
        ks = kslc_ref[0, pl.ds(off, tk), :]
        chains = [(g, g, ks, vslc_ref) for g in range(kvh)]
        masks = [sel_mask(g, kt) for g in range(kvh)]
        if near:
            kw = kwin_ref[0, pl.ds(off, tk), :]
            d = pos - (off + lax.broadcasted_iota(jnp.int32, (tk, 1), 0))
            causal = d >= 0
            wmask = causal & (d < NSA_WINDOW)
            chains += [(kvh + g, g, kw, vwin_ref) for g in range(kvh)]
            masks = [mk & causal for mk in masks] + [wmask] * kvh
        scs = [jnp.where(mk, _dot(kk, qpads[g]), NEG_INF) for (_, g, kk, _), mk in zip(chains, masks)]
        m_new = [jnp.maximum(ms[c], jnp.max(sc, axis=0, keepdims=True)) for (c, _, _, _), sc in zip(chains, scs)]
        pps = [jnp.where(mk, jnp.exp(sc - mn), 0.0) for sc, mn, mk in zip(scs, m_new, masks)]
        pvs = [_dot(v_ref[0, g * HEAD_DIM:(g + 1) * HEAD_DIM, pl.ds(off, tk)], pp.astype(BF16))
               for (_, g, _, v_ref), pp in zip(chains, pps)]
        for (c, _, _, _), mn, pp, pv in zip(chains, m_new, pps, pvs):
            alpha = jnp.exp(ms[c] - mn)
            acc_ref[c] = alpha * acc_ref[c] + pv
            ls[c] = alpha * ls[c] + jnp.sum(pp, axis=0, keepdims=True)
            ms[c] = mn
        return tuple(ms), tuple(ls)

    n_chain = 2 * kvh
    acc_ref[...] = jnp.zeros_like(acc_ref)
    carry = ((jnp.full((1, lanes), NEG_INF, F32),) * n_chain, (jnp.zeros((1, lanes), F32),) * n_chain)
    first = jnp.maximum(i * tq - (NSA_WINDOW - 1), 0) // tk
    carry = lax.fori_loop(0, first, lambda kt, cr: step(kt, cr, False), carry)
    _, ls = lax.fori_loop(first, (i + 1) * (tq // tk), lambda kt, cr: step(kt, cr, True), carry)

    ga = ga_ref[0]
    for g in range(kvh):
        o_br = [o_cmp[g]]
        for c in (g, kvh + g):
            l = ls[c]
            o_br.append(jnp.where(l > 0.0, acc_ref[c] / jnp.where(l > 0.0, l, 1.0), 0.0))
        for hh in range(NSA_GROUP):
            h = g * NSA_GROUP + hh
            sl = slice(hh * tq, (hh + 1) * tq)
            o = ga[h:h + 1] * o_br[0][:, sl]
            for br in range(1, NSA_BRANCHES):
                o = o + ga[br * NSA_HEADS + h:br * NSA_HEADS + h + 1] * o_br[br][:, sl]
            o_ref[0, h * HEAD_DIM:(h + 1) * HEAD_DIM, :] = o


def _nsa_prompt(qat, kb, vt, pooled, gat, *, batch, seq, tq=256, tk=128):
    assert tq % tk == 0 and seq % tq == 0 and tk % NSA_BLOCK == 0
    nb = seq // NSA_BLOCK
    nq = seq // tq
    return pl.pallas_call(
        functools.partial(_nsa_prompt_kernel, tq=tq, tk=tk, nb=nb),
        grid=(batch, nq),
        in_specs=[pl.BlockSpec((1, NSA_WIDTH, tq), lambda b, i: (b, 0, i)),
                  pl.BlockSpec((1, seq, KV_PAIR), lambda b, i: (b, 0, 0)),
                  pl.BlockSpec((1, seq, KV_PAIR), lambda b, i: (b, 0, 1)),
                  pl.BlockSpec((1, KV_PAIR, seq), lambda b, i: (b, 0, 0)),
                  pl.BlockSpec((1, KV_PAIR, seq), lambda b, i: (b, 1, 0)),
                  pl.BlockSpec((1, nb, 2 * KV_PAIR), lambda b, i: (b, 0, 0)),
                  pl.BlockSpec((1, LANES, tq), lambda b, i: (b, 0, i))],
        out_specs=pl.BlockSpec((1, NSA_WIDTH, tq), lambda b, i: (b, 0, i)),
        out_shape=jax.ShapeDtypeStruct((batch, NSA_WIDTH, seq), F32),
        scratch_shapes=[pltpu.VMEM((NSA_KV_HEADS, nb, 1, tq), F32),
                        pltpu.VMEM((2 * NSA_KV_HEADS, HEAD_DIM, NSA_GROUP * tq), F32)],
        compiler_params=pltpu.CompilerParams(dimension_semantics=("arbitrary",) * 2, vmem_limit_bytes=VMEM_LIMIT),
        name="nsa_prompt",
    )(qat, kb, kb, vt, vt, pooled, gat)


def _softplus(z):
    return jnp.maximum(z, 0.0) + jnp.log(1.0 + jnp.exp2(jnp.abs(z) * -LOG2E))


def _split_bf16(x):
    hi = x.astype(BF16)
    return hi, (x - hi.astype(F32)).astype(BF16)


def _sb_prompt_kernel(q_ref, k_ref, v_ref, o_ref, acc_ref, *, tq, tk):
    i = pl.program_id(1)
    pairs = SB_HEADS // 2
    lanes = 2 * tq
    z0 = jnp.zeros((HEAD_DIM, tq), BF16)
    qpads = []
    for j in range(pairs):
        qt = q_ref[0, j * LANES:(j + 1) * LANES, :]
        qpads.append(jnp.concatenate([jnp.concatenate([qt[:HEAD_DIM], z0], axis=0),
                                      jnp.concatenate([z0, qt[HEAD_DIM:]], axis=0)], axis=1))
    r = lax.broadcasted_iota(jnp.int32, (tk, 2 * tk), 0)
    c = lax.broadcasted_iota(jnp.int32, (tk, 2 * tk), 1) % tk
    upper2 = jnp.where(c > r, 1.0, 0.0).astype(BF16)
    pos = i * tq + lax.broadcasted_iota(jnp.int32, (1, lanes), 1) % tq

    def step(kt, runs, masked, first):
        off = pl.multiple_of(kt * tk, tk)
        kk = k_ref[0, pl.ds(off, tk), :]
        mask = (off + lax.broadcasted_iota(jnp.int32, (tk, 1), 0)) < pos if masked else None
        zs = [_dot(kk[:, j * LANES:(j + 1) * LANES], qpads[j]) for j in range(pairs)]
        sps = [jnp.where(mask, _softplus(z), 0.0) if masked else _softplus(z) for z in zs]
        drops = [_dot(upper2, jnp.concatenate(_split_bf16(sp), axis=0)) + run for sp, run in zip(sps, runs)]
        ws = [jnp.exp(z - sp - drop) for z, sp, drop in zip(zs, sps, drops)]
        if masked:
            ws = [jnp.where(mask, w, 0.0) for w in ws]
        for h in range(SB_HEADS):
            w = ws[h // 2][:, (h % 2) * tq:(h % 2 + 1) * tq].astype(BF16)
            pv = _dot(v_ref[0, h * HEAD_DIM:(h + 1) * HEAD_DIM, pl.ds(off, tk)], w)
            acc_ref[h] = pv if first else acc_ref[h] + pv
        return tuple(run + jnp.sum(sp, axis=0, keepdims=True) for run, sp in zip(runs, sps))

    diag = tq // tk
    runs = (jnp.zeros((1, lanes), F32),) * pairs
    for n in range(diag):
        runs = step((i + 1) * diag - 1 - n, runs, True, n == 0)
    lax.fori_loop(0, i * diag, lambda n, rs: step(i * diag - 1 - n, rs, False, False), runs)
    for h in range(SB_HEADS):
        o_ref[0, h * HEAD_DIM:(h + 1) * HEAD_DIM, :] = acc_ref[h]


def _sb_prompt(qbt, kbb, vbt, *, batch, seq, tq=256, tk=128):
    assert tq % tk == 0 and seq % tq == 0
    return pl.pallas_call(
        functools.partial(_sb_prompt_kernel, tq=tq, tk=tk),
        grid=(batch, seq // tq),
        in_specs=[pl.BlockSpec((1, SB_WIDTH, tq), lambda b, i: (b, 0, i)),
                  pl.BlockSpec((1, seq, SB_WIDTH), lambda b, i: (b, 0, 0)),
                  pl.BlockSpec((1, SB_WIDTH, seq), lambda b, i: (b, 0, 0))],
        out_specs=pl.BlockSpec((1, SB_WIDTH, tq), lambda b, i: (b, 0, i)),
        out_shape=jax.ShapeDtypeStruct((batch, SB_WIDTH, seq), F32),
        scratch_shapes=[pltpu.VMEM((SB_HEADS, HEAD_DIM, tq), F32)],
        compiler_params=pltpu.CompilerParams(dimension_semantics=("arbitrary",) * 2, vmem_limit_bytes=VMEM_LIMIT),
        name="sb_prompt",
    )(qbt, kbb, vbt)


def _mix_prompt_kernel(x_ref, oa_ref, za_ref, ob_ref, zb_ref, gm_ref, wat_ref, wbt_ref, wot_ref, g_ref, y_ref, *,
                       d_model):
    ya = _dot(wat_ref[...], (oa_ref[0] * za_ref[0]).astype(BF16))
    yb = _dot(wbt_ref[...], (ob_ref[0] * zb_ref[0]).astype(BF16))
    gm = gm_ref[0]
    mixed = gm[:d_model] * ya + gm[d_model:] * yb
    out = _dot(wot_ref[...], mixed.astype(BF16))
    out = out * lax.rsqrt(jnp.mean(out * out, axis=0, keepdims=True) + RMS_EPS)
    y_ref[...] = x_ref[...] + out.T * g_ref[...]


def _mix_prompt(x2d, oat, zat, obt, zbt, gmt, wat, wbt, wot, g_post, *, batch, seq, tm):
    m, d_model = x2d.shape
    nt = seq // tm
    row = lambda i: (i, 0)
    full = lambda i: (0, 0)
    tr = lambda i: (i // nt, 0, i % nt)
    return pl.pallas_call(
        functools.partial(_mix_prompt_kernel, d_model=d_model),
        grid=(m // tm,),
        in_specs=[pl.BlockSpec((tm, d_model), row), pl.BlockSpec((1, NSA_WIDTH, tm), tr),
                  pl.BlockSpec((1, NSA_WIDTH, tm), tr), pl.BlockSpec((1, SB_WIDTH, tm), tr),
                  pl.BlockSpec((1, SB_WIDTH, tm), tr), pl.BlockSpec((1, N_MERGE * d_model, tm), tr),
                  pl.BlockSpec((d_model, NSA_WIDTH), full), pl.BlockSpec((d_model, SB_WIDTH), full),
                  pl.BlockSpec((d_model, d_model), full), pl.BlockSpec((1, d_model), full)],
        out_specs=pl.BlockSpec((tm, d_model), row),
        out_shape=jax.ShapeDtypeStruct((m, d_model), F32),
        compiler_params=pltpu.CompilerParams(dimension_semantics=("arbitrary",), vmem_limit_bytes=VMEM_LIMIT),
        name="mix_prompt",
    )(x2d, oat, zat, obt, zbt, gmt, wat, wbt, wot, g_post)


def _mix_sample_kernel(x_ref, oa_ref, za_ref, ob_ref, zb_ref, gm_ref, wat_ref, wbt_ref, wot_ref, g_ref, y_ref, *,
                       d_model):
    ya = _dot_nt((oa_ref[...] * za_ref[...]).astype(BF16), wat_ref[...])
    yb = _dot_nt((ob_ref[...] * zb_ref[...]).astype(BF16), wbt_ref[...])
    gm = gm_ref[...]
    mixed = gm[:, :d_model] * ya + gm[:, d_model:] * yb
    out = _dot_nt(mixed.astype(BF16), wot_ref[...])
    y_ref[...] = x_ref[...] + _rms_scale(out, g_ref[...])


def _mix_sample(x2d, o_a, za, o_b, zb, gm, wat, wbt, wot, g_post):
    m, d_model = x2d.shape
    args = (x2d, o_a, za, o_b, zb, gm, wat, wbt, wot, g_post)
    return pl.pallas_call(
        functools.partial(_mix_sample_kernel, d_model=d_model),
        grid=(1,),
        in_specs=[pl.BlockSpec(a.shape, lambda i: (0, 0)) for a in args],
        out_specs=pl.BlockSpec((m, d_model), lambda i: (0, 0)),
        out_shape=jax.ShapeDtypeStruct((m, d_model), F32),
        compiler_params=pltpu.CompilerParams(dimension_semantics=("arbitrary",), vmem_limit_bytes=VMEM_LIMIT),
        name="mix_sample",
    )(*args)


def _head_pad(q):
    q2 = jnp.concatenate([q, q], axis=1)
    row = lax.broadcasted_iota(jnp.int32, q2.shape, 0)
    lane = lax.broadcasted_iota(jnp.int32, q2.shape, 1)
    return jnp.where(row // NSA_GROUP == lane // HEAD_DIM, q2, 0.0)


def _sb_decode_kernel(pt_ref, q_ref, *refs, ppg):
    page_refs = refs[:ppg]
    o_ref = refs[ppg]
    run_ref, acc_ref = refs[ppg + 1:]
    s = pl.program_id(1)

    @pl.when(s == 0)
    def _():
        run_ref[...] = jnp.zeros_like(run_ref)
        acc_ref[...] = jnp.zeros_like(acc_ref)

    q = q_ref[0]
    qrep = jnp.concatenate([q] * SB_HEADS, axis=1)
    row = lax.broadcasted_iota(jnp.int32, qrep.shape, 0)
    lane = lax.broadcasted_iota(jnp.int32, qrep.shape, 1)
    diag = row == lane // HEAD_DIM
    qbd = jnp.where(diag, qrep, 0.0).astype(BF16)

    z = jnp.concatenate([_dot(qbd, pr[0, :SB_WIDTH, :].astype(BF16)) for pr in page_refs], axis=0)
    sp = _softplus(z)
    r = lax.broadcasted_iota(jnp.int32, (PAGE_SIZE, PAGE_SIZE), 0)
    c = lax.broadcasted_iota(jnp.int32, (PAGE_SIZE, PAGE_SIZE), 1)
    lower = jnp.where(r > c, 1.0, 0.0).astype(BF16)
    hi, lo = _split_bf16(sp)
    drop = _dot(hi, lower) + _dot(lo, lower)
    tot = jnp.sum(sp, axis=1, keepdims=True)
    run = run_ref[...]
    acc = acc_ref[...]
    for n in range(ppg):
        sl = slice(n * SB_HEADS, (n + 1) * SB_HEADS)
        a = jnp.exp(z[sl] - sp[sl] - drop[sl] - run)
        acc = acc + _dot_nt(a.astype(BF16), page_refs[n][0, SB_WIDTH:, :].astype(BF16))
        run = run + tot[sl]
    run_ref[...] = run
    acc_ref[...] = acc

    @pl.when(s == pl.num_programs(1) - 1)
    def _():
        o_ref[0] = jnp.sum(jnp.where(diag, acc, 0.0), axis=0, keepdims=True)


def _sb_decode(page_table_flat, q_b, cache_t, *, dec_batch, n_pages, ppg=8):
    steps = n_pages // ppg

    def page_map(n):
        return lambda b, s, pt: (pt[b * n_pages + n_pages - 1 - (s * ppg + n)], 0, 0)

    grid_spec = pltpu.PrefetchScalarGridSpec(
        num_scalar_prefetch=1, grid=(dec_batch, steps),
        in_specs=[pl.BlockSpec((1, SB_HEADS, HEAD_DIM), lambda b, s, pt: (b, 0, 0))]
        + [pl.BlockSpec((1, 2 * SB_WIDTH, PAGE_SIZE), page_map(n)) for n in range(ppg)],
        out_specs=pl.BlockSpec((1, 1, SB_WIDTH), lambda b, s, pt: (b, 0, 0)),
        scratch_shapes=[pltpu.VMEM((SB_HEADS, 1), F32), pltpu.VMEM((SB_HEADS, SB_WIDTH), F32)])
    return pl.pallas_call(
        functools.partial(_sb_decode_kernel, ppg=ppg),
        grid_spec=grid_spec,
        out_shape=jax.ShapeDtypeStruct((dec_batch, 1, SB_WIDTH), F32),
        compiler_params=pltpu.CompilerParams(dimension_semantics=("arbitrary", "arbitrary"),
                                             vmem_limit_bytes=VMEM_LIMIT),
        name="sb_decode",
    )(page_table_flat, q_b, *([cache_t] * ppg))


def _nsa_cmp_decode_kernel(pt_ref, q_ref, wp_ref, *refs, ppg, nbp):
    page_refs = refs[:ppg]
    ocmp_ref, sel_ref = refs[ppg:ppg + 2]
    pool_ref = refs[ppg + 2]
    s = pl.program_id(1)
    rows = ppg * BLOCKS_PER_PAGE
    kc = jnp.zeros((rows, KV_PAIR), F32)
    vc = jnp.zeros((rows, KV_PAIR), F32)
    for n, pr in enumerate(page_refs):
        kc = kc + _dot_nt(wp_ref[0, n], pr[0, :KV_PAIR, :].astype(BF16))
        vc = vc + _dot_nt(wp_ref[1, n], pr[0, KV_PAIR:, :].astype(BF16))
    base = pl.multiple_of(s * rows, rows)
    pool_ref[pl.ds(base, rows), :] = jnp.concatenate([kc, vc], axis=1)

    @pl.when(s == pl.num_programs(1) - 1)
    def _():
        qpad = _head_pad(q_ref[0]).astype(BF16)
        pooled = pool_ref[...]
        sc = _dot_nt(qpad, pooled[:, :KV_PAIR].astype(BF16))
        e = jnp.exp(sc - jnp.max(sc, axis=1, keepdims=True))
        p = e / jnp.sum(e, axis=1, keepdims=True)
        o = _dot(p.astype(BF16), pooled[:, KV_PAIR:].astype(BF16))
        hrow = lax.broadcasted_iota(jnp.int32, o.shape, 0)
        ocmp_ref[0] = jnp.where(hrow < NSA_GROUP, o, pltpu.roll(o, HEAD_DIM, 1))[:, :HEAD_DIM]

        blk_l = lax.broadcasted_iota(jnp.int32, (1, nbp), 1)
        blk_s = lax.broadcasted_iota(jnp.int32, (nbp, 1), 0)
        eye = lax.broadcasted_iota(jnp.int32, (nbp, nbp), 0) == lax.broadcasted_iota(jnp.int32, (nbp, nbp), 1)
        n_sel = NSA_TOPK - 1
        kslot = lax.broadcasted_iota(jnp.int32, (NSA_TOPK, nbp), 0)
        for gi in range(NSA_KV_HEADS):
            imp = jnp.sum(p[gi * NSA_GROUP:(gi + 1) * NSA_GROUP], axis=0, keepdims=True)
            forced = (blk_l == 0) | (blk_l == nbp - 1)
            srow = jnp.where(forced, FORCE_SCORE, imp)
            scol = jnp.sum(jnp.where(eye, srow, 0.0), axis=1, keepdims=True)
            beats = (scol > srow) | ((scol == srow) & (blk_s < blk_l))
            rank = jnp.sum(jnp.where(beats, 1.0, 0.0), axis=0, keepdims=True)
            sel = rank < n_sel
            selcol = jnp.sum(jnp.where(eye & sel, 1.0, 0.0), axis=1, keepdims=True) > 0.5
            slot = jnp.sum(jnp.where(selcol & (blk_s < blk_l), 1.0, 0.0), axis=0, keepdims=True)
            onehot = sel & (slot.astype(jnp.int32) == kslot)
            idx = jnp.sum(jnp.where(onehot, blk_l, 0), axis=1, keepdims=True)
            sel_ref[0, gi] = jnp.broadcast_to(idx, (NSA_TOPK, LANES))


def _nsa_cmp_decode(page_table_flat, q_a, wp, cache_t, *, dec_batch, n_pages, ppg=8):
    steps = n_pages // ppg
    nbp = n_pages * BLOCKS_PER_PAGE

    def page_map(n):
        return lambda b, s, pt: (pt[b * n_pages + s * ppg + n], 0, 0)

    grid_spec = pltpu.PrefetchScalarGridSpec(
        num_scalar_prefetch=1, grid=(dec_batch, steps),
        in_specs=[pl.BlockSpec((1, NSA_HEADS, HEAD_DIM), lambda b, s, pt: (b, 0, 0)),
                  pl.BlockSpec(wp.shape, lambda b, s, pt: (0, 0, 0, 0))]
        + [pl.BlockSpec((1, 2 * KV_PAIR, PAGE_SIZE), page_map(n)) for n in range(ppg)],
        out_specs=[pl.BlockSpec((1, NSA_HEADS, HEAD_DIM), lambda b, s, pt: (b, 0, 0)),
                   pl.BlockSpec((1, NSA_KV_HEADS, NSA_TOPK, LANES), lambda b, s, pt: (b, 0, 0, 0))],
        scratch_shapes=[pltpu.VMEM((nbp, 2 * KV_PAIR), F32)])
    return pl.pallas_call(
        functools.partial(_nsa_cmp_decode_kernel, ppg=ppg, nbp=nbp),
        grid_spec=grid_spec,
        out_shape=[jax.ShapeDtypeStruct((dec_batch, NSA_HEADS, HEAD_DIM), F32),
                   jax.ShapeDtypeStruct((dec_batch, NSA_KV_HEADS, NSA_TOPK, LANES), jnp.int32)],
        compiler_params=pltpu.CompilerParams(dimension_semantics=("arbitrary", "arbitrary"),
                                             vmem_limit_bytes=VMEM_LIMIT),
        name="nsa_cmp_decode",
    )(page_table_flat, q_a, wp, *([cache_t] * ppg))


def _nsa_sel_decode_kernel(sel_ref, pt_ref, q_ref, ocmp_ref, ga_ref, new_ref, winp_ref, *refs, n_blk):
    blk_refs = refs[:NSA_KV_HEADS * n_blk]
    o_ref, wino_ref = refs[NSA_KV_HEADS * n_blk:]
    b = pl.program_id(0)
    qb = _head_pad(q_ref[0]).astype(BF16)
    new = new_ref[0]
    hrow = lax.broadcasted_iota(jnp.int32, (NSA_HEADS, KV_PAIR), 0)
    top = hrow < NSA_GROUP

    def fold(o):
        return jnp.where(top, o, pltpu.roll(o, HEAD_DIM, 1))[:, :HEAD_DIM]

    def softmax(sc, mask):
        sc = jnp.where(mask, sc, NEG_INF)
        e = jnp.where(mask, jnp.exp(sc - jnp.max(sc, axis=1, keepdims=True)), 0.0)
        return e / jnp.sum(e, axis=1, keepdims=True)

    lane = lax.broadcasted_iota(jnp.int32, (1, PAGE_SIZE), 1)
    first = lax.broadcasted_iota(jnp.int32, (KV_PAIR, PAGE_SIZE), 1) == 0
    k_new = jnp.where(first, new[2 * KV_PAIR:3 * KV_PAIR], 0.0).astype(BF16)
    v_new = jnp.where(first, new[3 * KV_PAIR:4 * KV_PAIR], 0.0).astype(BF16)
    o_g = []
    for gi in range(NSA_KV_HEADS):
        scs, masks = [], []
        for n in range(n_blk):
            half = sel_ref[(b * NSA_KV_HEADS + gi) * NSA_TOPK + n] % BLOCKS_PER_PAGE
            scs.append(_dot(qb, blk_refs[gi * n_blk + n][0, :KV_PAIR, :].astype(BF16)))
            masks.append(lane // NSA_BLOCK == half)
        scs.append(_dot(qb, k_new))
        masks.append(lane == 0)
        p = softmax(jnp.concatenate(scs, axis=1), jnp.concatenate(masks, axis=1)).astype(BF16)
        o = _dot_nt(p[:, n_blk * PAGE_SIZE:], v_new)
        for n in range(n_blk):
            o = o + _dot_nt(p[:, n * PAGE_SIZE:(n + 1) * PAGE_SIZE],
                            blk_refs[gi * n_blk + n][0, KV_PAIR:, :].astype(BF16))
        o_g.append(o)
    o_slc = fold(jnp.where(top, o_g[0], o_g[1]))

    wp = winp_ref[0]
    w = wp.shape[1]
    wl = lax.broadcasted_iota(jnp.int32, wp.shape, 1)
    shifted = jnp.where(wl == w - 1, new[4 * KV_PAIR:], pltpu.roll(wp, w - 1, 1))
    wino_ref[0] = shifted
    sc = _dot(qb, shifted[:KV_PAIR].astype(BF16))
    p = softmax(sc, jnp.full((1, w), True)).astype(BF16)
    o_win = fold(_dot_nt(p, shifted[KV_PAIR:].astype(BF16)))

    ga = ga_ref[0]
    o_ref[0] = ga[0] * ocmp_ref[0] + ga[1] * o_slc + ga[2] * o_win


def _nsa_sel_decode(sel_flat, page_table_flat, q_a, o_cmp, ga, new_kv, win_past_t, cache_t, *, dec_batch, n_pages):
    n_blk = NSA_TOPK - 1
    w = win_past_t.shape[2]

    def blk_map(gi, n):
        def f(b, sel, pt):
            blk = sel[(b * NSA_KV_HEADS + gi) * NSA_TOPK + n]
            return (pt[b * n_pages + blk // BLOCKS_PER_PAGE], 1, 0)
        return f

    grid_spec = pltpu.PrefetchScalarGridSpec(
        num_scalar_prefetch=2, grid=(dec_batch,),
        in_specs=[pl.BlockSpec((1, NSA_HEADS, HEAD_DIM), lambda b, sel, pt: (b, 0, 0)),
                  pl.BlockSpec((1, NSA_HEADS, HEAD_DIM), lambda b, sel, pt: (b, 0, 0)),
                  pl.BlockSpec((1, NSA_BRANCHES, NSA_HEADS, 1), lambda b, sel, pt: (b, 0, 0, 0)),
                  pl.BlockSpec((1, KVA_WIDTH, 1), lambda b, sel, pt: (b, 0, 0)),
                  pl.BlockSpec((1, 2 * KV_PAIR, w), lambda b, sel, pt: (b, 0, 0))]
        + [pl.BlockSpec((1, 2 * KV_PAIR, PAGE_SIZE), blk_map(gi, n))
           for gi in range(NSA_KV_HEADS) for n in range(n_blk)],
        out_specs=[pl.BlockSpec((1, NSA_HEADS, HEAD_DIM), lambda b, sel, pt: (b, 0, 0)),
                   pl.BlockSpec((1, 2 * KV_PAIR, w), lambda b, sel, pt: (b, 0, 0))])
    return pl.pallas_call(
        functools.partial(_nsa_sel_decode_kernel, n_blk=n_blk),
        grid_spec=grid_spec,
        out_shape=[jax.ShapeDtypeStruct((dec_batch, NSA_HEADS, HEAD_DIM), F32),
                   jax.ShapeDtypeStruct((dec_batch, 2 * KV_PAIR, w), F32)],
        compiler_params=pltpu.CompilerParams(dimension_semantics=("arbitrary",), vmem_limit_bytes=VMEM_LIMIT),
        name="nsa_sel_decode",
    )(sel_flat, page_table_flat, q_a, o_cmp, ga, new_kv, win_past_t, *([cache_t] * (NSA_KV_HEADS * n_blk)))


def _feature_major(a, lead):
    nl = len(lead)
    t = jnp.moveaxis(a, nl, -1)
    return t.reshape(lead + (-1, a.shape[nl]))


def _time_major(a_t, feat_shape):
    lead, _, time = a_t.shape
    return jnp.moveaxis(a_t.reshape((lead,) + feat_shape + (time,)), -1, 1)


def _layer(x_prompt, x_sample, nsa_cache, win_cache, sb_cache, page_table, w_in, w_cmp, w_a, w_b, w_o, g_pre, g_post):
    batch, seq, d_model = x_prompt.shape
    dec_batch, dec_seq, _ = x_sample.shape
    n_pages = page_table.shape[1]
    past_len = n_pages * PAGE_SIZE
    n_phys = nsa_cache.shape[0]
    tm, ppg = 256, 16
    assert dec_seq == 1 and seq % tm == 0 and seq >= NSA_WINDOW and past_len >= NSA_WINDOW
    assert n_pages % ppg == 0 and past_len // NSA_BLOCK >= NSA_TOPK
    assert win_cache.shape[1] == NSA_WINDOW and nsa_cache.shape[1] == PAGE_SIZE

    wt_perm = _permute_w_in_t(w_in, d_model)
    wat, wbt, wot = w_a.T.astype(BF16), w_b.T.astype(BF16), w_o.T.astype(BF16)
    g_pre2, g_post2 = g_pre.reshape(1, d_model), g_post.reshape(1, d_model)
    nsa_feat = (4, NSA_KV_HEADS, HEAD_DIM)
    win_feat = (2, NSA_KV_HEADS, HEAD_DIM)
    sb_feat = (2, SB_HEADS, HEAD_DIM)

    xp = x_prompt.reshape(batch * seq, d_model)
    cos, sin = _rope_angles(jnp.arange(seq, dtype=jnp.int32))
    wp_prompt = _pool_weights(w_cmp, 1, tm // NSA_BLOCK, SUBLANES).reshape(2 * SUBLANES, tm)
    (nsa_t, win_t, sb_t, za_t, zb_t, gm_t, qa_t, ga_t, kb, v_t, pooled, qb_t, kbb, vb_t) = _project_prompt(
        xp, g_pre2, wt_perm, cos.T, sin.T, wp_prompt, batch=batch, seq=seq, tm=tm)
    oa_t = _nsa_prompt(qa_t, kb.reshape(batch, seq, 2 * KV_PAIR), v_t,
                       pooled.reshape(batch, seq // NSA_BLOCK, 2 * KV_PAIR), ga_t, batch=batch, seq=seq)
    ob_t = _sb_prompt(qb_t, kbb.reshape(batch, seq, SB_WIDTH), vb_t, batch=batch, seq=seq)
    y_prompt = _mix_prompt(xp, oa_t, za_t, ob_t, zb_t, gm_t, wat, wbt, wot, g_post2,
                           batch=batch, seq=seq, tm=tm).reshape(batch, seq, d_model)
    nsa_kv_prompt = _time_major(nsa_t, nsa_feat)
    win_kv_prompt = _time_major(win_t[:, :, seq - NSA_WINDOW:], win_feat)
    sb_kv_prompt = _time_major(sb_t, sb_feat)

    xs = x_sample.reshape(dec_batch, d_model)
    tabs_s = _rope_lane_tables(jnp.full((dec_batch,), past_len, jnp.int32))
    (nsa_s, win_s, sb_s, za_s, zb_s, gm_s, qa_s, ga_s, qb_s) = _project_sample(xs, g_pre2, wt_perm, tabs_s)
    pt_flat = page_table.reshape(-1)
    sb_cache_t = _feature_major(sb_cache, (n_phys,))
    nsa_cache_t = _feature_major(nsa_cache, (n_phys,))
    win_cache_t = _feature_major(win_cache, (dec_batch,))
    o_b_s = _sb_decode(pt_flat, qb_s.reshape(dec_batch, SB_HEADS, HEAD_DIM), sb_cache_t,
                       dec_batch=dec_batch, n_pages=n_pages, ppg=ppg)
    wp_dec = _pool_weights(w_cmp, ppg, BLOCKS_PER_PAGE, ppg * BLOCKS_PER_PAGE)
    qa3 = qa_s.reshape(dec_batch, NSA_HEADS, HEAD_DIM)
    o_cmp_s, sel = _nsa_cmp_decode(pt_flat, qa3, wp_dec, nsa_cache_t, dec_batch=dec_batch, n_pages=n_pages, ppg=ppg)
    ga3 = ga_s[:, :GATE_A].reshape(dec_batch, NSA_BRANCHES, NSA_HEADS, 1)
    new_kv = jnp.concatenate([nsa_s, win_s], axis=1).reshape(dec_batch, KVA_WIDTH, 1)
    o_a_s, win_out_t = _nsa_sel_decode(sel[:, :, :, 0].reshape(-1), pt_flat, qa3, o_cmp_s, ga3, new_kv,
                                       win_cache_t, nsa_cache_t, dec_batch=dec_batch, n_pages=n_pages)
    y_sample = _mix_sample(xs, o_a_s.reshape(dec_batch, NSA_WIDTH), za_s, o_b_s.reshape(dec_batch, SB_WIDTH), zb_s,
                           gm_s, wat, wbt, wot, g_post2).reshape(dec_batch, 1, d_model)
    nsa_kv_sample = nsa_s.reshape((dec_batch, 1) + nsa_feat)
    win_kv_sample = _time_major(win_out_t, win_feat)
    sb_kv_sample = sb_s.reshape((dec_batch, 1) + sb_feat)
    return (y_prompt, y_sample, nsa_kv_prompt, win_kv_prompt, sb_kv_prompt, nsa_kv_sample, win_kv_sample,
            sb_kv_sample)


def kernel(x_prompt, x_sample, cache_nsa_kv, cache_nsa_win_kv, cache_sb_kv, page_table, w_in, w_cmp, w_branch_a,
           w_branch_b, w_out, g_pre, g_post):
    hp, hs = x_prompt, x_sample
    caches = [[] for _ in range(6)]
    for layer in range(w_in.shape[0]):
        outs = _layer(hp, hs, cache_nsa_kv[layer], cache_nsa_win_kv[layer], cache_sb_kv[layer], page_table,
                      w_in[layer], w_cmp[layer], w_branch_a[layer], w_branch_b[layer], w_out[layer],
                      g_pre[layer], g_post[layer])
        hp, hs = outs[0], outs[1]
        for acc, o in zip(caches, outs[2:]):
            acc.append(o)
    return (hp, hs) + tuple(jnp.stack(c) for c in caches)
```

```python
import functools

import jax
import jax.numpy as jnp
from jax import lax
from jax.experimental import pallas as pl
from jax.experimental.pallas import tpu as pltpu

HEAD_DIM = 64
ROT_DIM = HEAD_DIM // 4
ROPE_THETA = 500000.0
NSA_HEADS = 8
NSA_KV_HEADS = 2
NSA_GROUP = NSA_HEADS // NSA_KV_HEADS
NSA_BRANCHES = 3
NSA_BLOCK = 64
NSA_TOPK = 16
NSA_WINDOW = 512
NSA_WIDTH = NSA_HEADS * HEAD_DIM
SB_HEADS = 8
SB_WIDTH = SB_HEADS * HEAD_DIM
N_MERGE = 2
PAGE_SIZE = 128
RMS_EPS = 1e-6
NEG_INF = -1e30
FORCE_SCORE = 1e3
SCALE = HEAD_DIM ** -0.5
LOG2E = 1.4426950408889634

LANES = 128
SUBLANES = 8
KV_PAIR = NSA_KV_HEADS * HEAD_DIM
assert KV_PAIR == LANES and PAGE_SIZE == LANES and PAGE_SIZE % NSA_BLOCK == 0
KVA_WIDTH = 2 * NSA_BRANCHES * KV_PAIR
GATE_A = NSA_BRANCHES * NSA_HEADS
BLOCKS_PER_PAGE = PAGE_SIZE // NSA_BLOCK
VMEM_LIMIT = 56 * 1024 * 1024

C_QA = 0
C_KVA = C_QA + NSA_WIDTH
C_ZA = C_KVA + KVA_WIDTH
C_QKVB = C_ZA + NSA_WIDTH
C_ZB = C_QKVB + 3 * SB_WIDTH
C_GM = C_ZB + SB_WIDTH

BF16 = jnp.bfloat16
F32 = jnp.float32


def _dot(a, b):
    return jnp.dot(a, b, preferred_element_type=F32)


def _dot_tn(a, b):
    return lax.dot_general(a, b, (((0,), (0,)), ((), ())), preferred_element_type=F32)


def _dot_nt(a, b):
    return lax.dot_general(a, b, (((1,), (1,)), ((), ())), preferred_element_type=F32)


def _sigmoid(x):
    return 1.0 / (1.0 + jnp.exp(-x))


def _rms_scale(x, g):
    return x * lax.rsqrt(jnp.mean(x * x, axis=-1, keepdims=True) + RMS_EPS) * g


def _rope_rows(v, cos, sin):
    half = ROT_DIM // 2
    parts = []
    for base in range(0, v.shape[0], HEAD_DIM):
        x1, x2 = v[base:base + half], v[base + half:base + 2 * half]
        parts += [x1 * cos - x2 * sin, x2 * cos + x1 * sin, v[base + 2 * half:base + HEAD_DIM]]
    return jnp.concatenate(parts, axis=0)


def _proj_prompt_kernel(x_ref, g_ref, wt_ref, cos_ref, sin_ref, wp_ref,
                        nsa_ref, win_ref, sb_ref, za_ref, zb_ref, gm_ref, qa_ref, ga_ref, kb_ref, vt_ref,
                        pool_ref, qb_ref, kbb_ref, vbt_ref, *, d_model):
    hb = _rms_scale(x_ref[...], g_ref[...]).astype(BF16)
    cos, sin = cos_ref[...], sin_ref[...]

    def seg(lo, width):
        return _dot_nt(wt_ref[lo:lo + width, :], hb)

    qa_ref[0] = (_rope_rows(seg(C_QA, NSA_WIDTH), cos, sin) * SCALE).astype(BF16)
    kv = seg(C_KVA, KVA_WIDTH)
    cmp_k = _rope_rows(kv[0 * LANES:1 * LANES], cos, sin)
    cmp_v = kv[1 * LANES:2 * LANES]
    slc_k = _rope_rows(kv[2 * LANES:3 * LANES], cos, sin)
    slc_v = kv[3 * LANES:4 * LANES]
    win_k = _rope_rows(kv[4 * LANES:5 * LANES], cos, sin)
    win_v = kv[5 * LANES:6 * LANES]
    nsa_ref[0] = jnp.concatenate([cmp_k, cmp_v, slc_k, slc_v], axis=0)
    win_ref[0] = jnp.concatenate([win_k, win_v], axis=0)
    kb_ref[...] = jnp.concatenate([slc_k.T, win_k.T], axis=1).astype(BF16)
    vt_ref[0] = jnp.concatenate([slc_v, win_v], axis=0).astype(BF16)
    wp = wp_ref[...]
    nblk = pool_ref.shape[1]
    pooled = jnp.concatenate([_dot_nt(wp[:SUBLANES], cmp_k.astype(BF16)),
                              _dot_nt(wp[SUBLANES:], cmp_v.astype(BF16))], axis=1)
    pool_ref[0] = pooled[:nblk]

    za = seg(C_ZA, NSA_WIDTH)
    za_ref[0] = (za * _sigmoid(za)).astype(za_ref.dtype)
    qkvb = seg(C_QKVB, 3 * SB_WIDTH)
    qb_ref[0] = (qkvb[:SB_WIDTH] * SCALE).astype(BF16)
    sb_ref[0] = qkvb[SB_WIDTH:]
    kbb_ref[...] = qkvb[SB_WIDTH:2 * SB_WIDTH].T.astype(BF16)
    vbt_ref[0] = qkvb[2 * SB_WIDTH:].astype(BF16)
    zb = seg(C_ZB, SB_WIDTH)
    zb_ref[0] = (zb * _sigmoid(zb)).astype(zb_ref.dtype)
    gm_ref[0] = _sigmoid(seg(C_GM, N_MERGE * d_model)).astype(gm_ref.dtype)
    ga_ref[0] = _sigmoid(seg(C_GM + N_MERGE * d_model, LANES))


def _project_prompt(x2d, g_pre, wt_perm, cos_t, sin_t, wp, *, batch, seq, tm):
    m, d_model = x2d.shape
    nt = seq // tm
    n_rows = wt_perm.shape[0]
    nblk = tm // NSA_BLOCK
    row = lambda i: (i, 0)
    full = lambda i: (0, 0)
    tab = lambda i: (0, i % nt)
    tr = lambda i: (i // nt, 0, i % nt)
    half = ROT_DIM // 2
    in_specs = [pl.BlockSpec((tm, d_model), row), pl.BlockSpec((1, d_model), full),
                pl.BlockSpec((n_rows, d_model), full),
                pl.BlockSpec((half, tm), tab), pl.BlockSpec((half, tm), tab),
                pl.BlockSpec((2 * SUBLANES, tm), full)]
    sds = jax.ShapeDtypeStruct

    def feat(width, dtype):
        return sds((batch, width, seq), dtype), pl.BlockSpec((1, width, tm), tr)

    def rows(width, dtype):
        return sds((m, width), dtype), pl.BlockSpec((tm, width), row)

    outs = [feat(4 * KV_PAIR, F32), feat(2 * KV_PAIR, F32), feat(2 * SB_WIDTH, F32),
            feat(NSA_WIDTH, BF16), feat(SB_WIDTH, BF16), feat(N_MERGE * d_model, BF16),
            feat(NSA_WIDTH, BF16), feat(LANES, F32), rows(2 * KV_PAIR, BF16), feat(2 * KV_PAIR, BF16),
            (sds((m // tm, nblk, 2 * KV_PAIR), F32), pl.BlockSpec((1, nblk, 2 * KV_PAIR), lambda i: (i, 0, 0))),
            feat(SB_WIDTH, BF16), rows(SB_WIDTH, BF16), feat(SB_WIDTH, BF16)]
    return pl.pallas_call(
        functools.partial(_proj_prompt_kernel, d_model=d_model),
        grid=(m // tm,), in_specs=in_specs, out_specs=[o[1] for o in outs], out_shape=[o[0] for o in outs],
        compiler_params=pltpu.CompilerParams(dimension_semantics=("arbitrary",), vmem_limit_bytes=VMEM_LIMIT),
        name="proj_prompt",
    )(x2d, g_pre, wt_perm, cos_t, sin_t, wp)


def _rope_lanes(v, c, s_up, s_dn):
    half = ROT_DIM // 2
    outs = []
    for j in range(v.shape[1] // LANES):
        blk = v[:, j * LANES:(j + 1) * LANES]
        outs.append(blk * c + pltpu.roll(blk, LANES - half, 1) * s_up + pltpu.roll(blk, half, 1) * s_dn)
    return outs[0] if len(outs) == 1 else jnp.concatenate(outs, axis=1)


def _proj_sample_kernel(x_ref, g_ref, wt_ref, cos_ref, sup_ref, sdn_ref,
                        nsa_ref, win_ref, sbkv_ref, za_ref, zb_ref, gm_ref, qa_ref, ga_ref, qb_ref, *, d_model):
    hb = _rms_scale(x_ref[...], g_ref[...]).astype(BF16)
    cos, s_up, s_dn = cos_ref[...], sup_ref[...], sdn_ref[...]

    def seg(lo, width):
        return _dot_nt(hb, wt_ref[lo:lo + width, :])

    qa_ref[...] = _rope_lanes(seg(C_QA, NSA_WIDTH), cos, s_up, s_dn) * SCALE
    kv = seg(C_KVA, KVA_WIDTH)
    parts = []
    for j in range(2 * NSA_BRANCHES):
        blk = kv[:, j * LANES:(j + 1) * LANES]
        parts.append(_rope_lanes(blk, cos, s_up, s_dn) if j % 2 == 0 else blk)
    nsa_ref[...] = jnp.concatenate(parts[:4], axis=1)
    win_ref[...] = jnp.concatenate(parts[4:], axis=1)
    za = seg(C_ZA, NSA_WIDTH)
    za_ref[...] = za * _sigmoid(za)
    qkvb = seg(C_QKVB, 3 * SB_WIDTH)
    qb_ref[...] = qkvb[:, :SB_WIDTH] * SCALE
    sbkv_ref[...] = qkvb[:, SB_WIDTH:]
    zb = seg(C_ZB, SB_WIDTH)
    zb_ref[...] = zb * _sigmoid(zb)
    gm_ref[...] = _sigmoid(seg(C_GM, N_MERGE * d_model))
    ga_ref[...] = _sigmoid(seg(C_GM + N_MERGE * d_model, LANES))


def _project_sample(x2d, g_pre, wt_perm, tables):
    m, d_model = x2d.shape
    full = lambda i: (0, 0)
    widths = [4 * KV_PAIR, 2 * KV_PAIR, 2 * SB_WIDTH, NSA_WIDTH, SB_WIDTH, N_MERGE * d_model,
              NSA_WIDTH, LANES, SB_WIDTH]
    return pl.pallas_call(
        functools.partial(_proj_sample_kernel, d_model=d_model),
        grid=(1,),
        in_specs=[pl.BlockSpec((m, d_model), full), pl.BlockSpec((1, d_model), full),
                  pl.BlockSpec(wt_perm.shape, full)] + [pl.BlockSpec((m, LANES), full)] * 3,
        out_specs=[pl.BlockSpec((m, w), full) for w in widths],
        out_shape=[jax.ShapeDtypeStruct((m, w), F32) for w in widths],
        compiler_params=pltpu.CompilerParams(dimension_semantics=("arbitrary",), vmem_limit_bytes=VMEM_LIMIT),
        name="proj_sample",
    )(x2d, g_pre, wt_perm, *tables)


def _rope_angles(pos):
    half = ROT_DIM // 2
    inv_freq = ROPE_THETA ** (-jnp.arange(half, dtype=F32) / half)
    ang = pos.astype(F32)[:, None] * inv_freq[None, :]
    return jnp.cos(ang), jnp.sin(ang)


def _rope_lane_tables(pos):
    cos, sin = _rope_angles(pos)
    n = pos.shape[0]
    half = ROT_DIM // 2
    ones = jnp.ones((n, HEAD_DIM - ROT_DIM), F32)
    zeros = jnp.zeros((n, HEAD_DIM - ROT_DIM), F32)
    zh = jnp.zeros((n, half), F32)
    rep = LANES // HEAD_DIM
    return tuple(jnp.tile(t, (1, rep)) for t in (jnp.concatenate([cos, cos, ones], axis=1),
                                                 jnp.concatenate([-sin, zh, zeros], axis=1),
                                                 jnp.concatenate([zh, sin, zeros], axis=1)))


def _permute_w_in_t(w_in, d_model):
    c = [NSA_WIDTH, KVA_WIDTH, NSA_WIDTH, GATE_A, 3 * SB_WIDTH, SB_WIDTH, N_MERGE * d_model]
    o = [0]
    for s in c:
        o.append(o[-1] + s)
    wt = w_in.T
    pad = jnp.zeros((LANES - GATE_A, d_model), w_in.dtype)
    return jnp.concatenate([wt[:o[3]], wt[o[4]:], wt[o[3]:o[4]], pad], axis=0).astype(BF16)


def _pool_weights(w_cmp, n_tiles, blocks_per_tile, rows):
    lane = jnp.arange(blocks_per_tile * NSA_BLOCK)
    owner = jnp.arange(n_tiles)[:, None, None] * blocks_per_tile + (lane // NSA_BLOCK)[None, None, :]
    hit = jnp.arange(rows)[None, :, None] == owner
    w_lane = jnp.tile(w_cmp, (1, blocks_per_tile))
    return jnp.where(hit[None], w_lane[:, None, None, :], 0.0).astype(BF16)


def _nsa_prompt_kernel(q_ref, kslc_ref, kwin_ref, vslc_ref, vwin_ref, pool_ref, ga_ref, o_ref, sel_ref, acc_ref, *,
                       tq, tk, nb):
    i = pl.program_id(1)
    lanes = NSA_GROUP * tq
    kvh = NSA_KV_HEADS
    row = lax.broadcasted_iota(jnp.int32, (KV_PAIR, lanes), 0)
    pos1 = i * tq + lax.broadcasted_iota(jnp.int32, (1, tq), 1)
    pos = jnp.concatenate([pos1] * NSA_GROUP, axis=1)
    blk = lax.broadcasted_iota(jnp.int32, (nb, 1), 0)
    cmask = ((blk + 1) * NSA_BLOCK - 1) <= pos
    cur = pos1 // NSA_BLOCK
    valid = blk <= cur
    forced = (blk == 0) | (blk == cur) | (blk == cur - 1)
    n_sel = min(NSA_TOPK, nb)
    pooled = pool_ref[0]
    kc = pooled[:, :KV_PAIR].astype(BF16)
    vc = pooled[:, KV_PAIR:].astype(BF16)

    qpads, o_cmp = [], []
    for g in range(kvh):
        qt = q_ref[0, g * NSA_GROUP * HEAD_DIM:(g + 1) * NSA_GROUP * HEAD_DIM, :]
        q4 = jnp.concatenate([qt[hh * HEAD_DIM:(hh + 1) * HEAD_DIM] for hh in range(NSA_GROUP)], axis=1)
        qpad = jnp.where(row // HEAD_DIM == g, jnp.concatenate([q4, q4], axis=0), jnp.zeros((), BF16))
        qpads.append(qpad)
        s = jnp.where(cmask, _dot(kc, qpad), NEG_INF)
        e = jnp.exp(s - jnp.max(s, axis=0, keepdims=True))
        p = jnp.where(cmask, e / jnp.sum(e, axis=0, keepdims=True), 0.0)
        o_cmp.append(_dot_tn(vc, p.astype(BF16))[g * HEAD_DIM:(g + 1) * HEAD_DIM])
        imp = p[:, 0:tq]
        for hh in range(1, NSA_GROUP):
            imp = imp + p[:, hh * tq:(hh + 1) * tq]
        score = jnp.where(valid, jnp.where(forced, FORCE_SCORE, imp), NEG_INF)
        for j in range(nb):
            sj = score[j:j + 1, :]
            beats = (score > sj) | ((score == sj) & (blk < j))
            cnt = jnp.sum(jnp.where(beats, 1.0, 0.0), axis=0, keepdims=True)
            sel_ref[g, j] = jnp.where(cnt < n_sel, 1.0, 0.0)

    bpt = tk // NSA_BLOCK

    def sel_mask(g, kt):
        rows = [jnp.broadcast_to(sel_ref[g, kt * bpt + r], (NSA_BLOCK, tq)) for r in range(bpt)]
        selt = jnp.concatenate(rows, axis=0) > 0.5
        return jnp.concatenate([selt] * NSA_GROUP, axis=1)

    def step(kt, carry, near):
        ms, ls = list(carry[0]), list(carry[1])
        off = pl.multiple_of(kt * tk, tk)
        ks = kslc_ref[0, pl.ds(off, tk), :]
        chains = [(g, g, ks, vslc_ref) for g in range(kvh)]
        masks = [sel_mask(g, kt) for g in range(kvh)]
        if near:
            kw = kwin_ref[0, pl.ds(off, tk), :]
            d = pos - (off + lax.broadcasted_iota(jnp.int32, (tk, 1), 0))
            causal = d >= 0
            wmask = causal & (d < NSA_WINDOW)
            chains += [(kvh + g, g, kw, vwin_ref) for g in range(kvh)]
            masks = [mk & causal for mk in masks] + [wmask] * kvh
        scs = [jnp.where(mk, _dot(kk, qpads[g]), NEG_INF) for (_, g, kk, _), mk in zip(chains, masks)]
        m_new = [jnp.maximum(ms[c], jnp.max(sc, axis=0, keepdims=True)) for (c, _, _, _), sc in zip(chains, scs)]
        pps = [jnp.where(mk, jnp.exp(sc - mn), 0.0) for sc, mn, mk in zip(scs, m_new, masks)]
        pvs = [_dot(v_ref[0, g * HEAD_DIM:(g + 1) * HEAD_DIM, pl.ds(off, tk)], pp.astype(BF16))
               for (_, g, _, v_ref), pp in zip(chains, pps)]
        for (c, _, _, _), mn, pp, pv in zip(chains, m_new, pps, pvs):
            alpha = jnp.exp(ms[c] - mn)
            acc_ref[c] = alpha * acc_ref[c] + pv
            ls[c] = alpha * ls[c] + jnp.sum(pp, axis=0, keepdims=True)
            ms[c] = mn
        return tuple(ms), tuple(ls)

    n_chain = 2 * kvh
    acc_ref[...] = jnp.zeros_like(acc_ref)
    carry = ((jnp.full((1, lanes), NEG_INF, F32),) * n_chain, (jnp.zeros((1, lanes), F32),) * n_chain)
    first = jnp.maximum(i * tq - (NSA_WINDOW - 1), 0) // tk
    carry = lax.fori_loop(0, first, lambda kt, cr: step(kt, cr, False), carry)
    _, ls = lax.fori_loop(first, (i + 1) * (tq // tk), lambda kt, cr: step(kt, cr, True), carry)

    ga = ga_ref[0]
    for g in range(kvh):
        o_br = [o_cmp[g]]
        for c in (g, kvh + g):
            l = ls[c]
            o_br.append(jnp.where(l > 0.0, acc_ref[c] / jnp.where(l > 0.0, l, 1.0), 0.0))
        for hh in range(NSA_GROUP):
            h = g * NSA_GROUP + hh
            sl = slice(hh * tq, (hh + 1) * tq)
            o = ga[h:h + 1] * o_br[0][:, sl]
            for br in range(1, NSA_BRANCHES):
                o = o + ga[br * NSA_HEADS + h:br * NSA_HEADS + h + 1] * o_br[br][:, sl]
            o_ref[0, h * HEAD_DIM:(h + 1) * HEAD_DIM, :] = o


def _nsa_prompt(qat, kb, vt, pooled, gat, *, batch, seq, tq=256, tk=128):
    assert tq % tk == 0 and seq % tq == 0 and tk % NSA_BLOCK == 0
    nb = seq // NSA_BLOCK
    nq = seq // tq
    return pl.pallas_call(
        functools.partial(_nsa_prompt_kernel, tq=tq, tk=tk, nb=nb),
        grid=(batch, nq),
        in_specs=[pl.BlockSpec((1, NSA_WIDTH, tq), lambda b, i: (b, 0, i)),
                  pl.BlockSpec((1, seq, KV_PAIR), lambda b, i: (b, 0, 0)),
                  pl.BlockSpec((1, seq, KV_PAIR), lambda b, i: (b, 0, 1)),
                  pl.BlockSpec((1, KV_PAIR, seq), lambda b, i: (b, 0, 0)),
                  pl.BlockSpec((1, KV_PAIR, seq), lambda b, i: (b, 1, 0)),
                  pl.BlockSpec((1, nb, 2 * KV_PAIR), lambda b, i: (b, 0, 0)),
                  pl.BlockSpec((1, LANES, tq), lambda b, i: (b, 0, i))],
        out_specs=pl.BlockSpec((1, NSA_WIDTH, tq), lambda b, i: (b, 0, i)),
        out_shape=jax.ShapeDtypeStruct((batch, NSA_WIDTH, seq), F32),
        scratch_shapes=[pltpu.VMEM((NSA_KV_HEADS, nb, 1, tq), F32),
                        pltpu.VMEM((2 * NSA_KV_HEADS, HEAD_DIM, NSA_GROUP * tq), F32)],
        compiler_params=pltpu.CompilerParams(dimension_semantics=("arbitrary",) * 2, vmem_limit_bytes=VMEM_LIMIT),
        name="nsa_prompt",
    )(qat, kb, kb, vt, vt, pooled, gat)


def _softplus(z):
    return jnp.maximum(z, 0.0) + jnp.log(1.0 + jnp.exp2(jnp.abs(z) * -LOG2E))


def _split_bf16(x):
    hi = x.astype(BF16)
    return hi, (x - hi.astype(F32)).astype(BF16)


def _sb_prompt_kernel(q_ref, k_ref, v_ref, o_ref, acc_ref, *, tq, tk):
    i = pl.program_id(1)
    pairs = SB_HEADS // 2
    lanes = 2 * tq
    z0 = jnp.zeros((HEAD_DIM, tq), BF16)
    qpads = []
    for j in range(pairs):
        qt = q_ref[0, j * LANES:(j + 1) * LANES, :]
        qpads.append(jnp.concatenate([jnp.concatenate([qt[:HEAD_DIM], z0], axis=0),
                                      jnp.concatenate([z0, qt[HEAD_DIM:]], axis=0)], axis=1))
    r = lax.broadcasted_iota(jnp.int32, (tk, 2 * tk), 0)
    c = lax.broadcasted_iota(jnp.int32, (tk, 2 * tk), 1) % tk
    upper2 = jnp.where(c > r, 1.0, 0.0).astype(BF16)
    pos = i * tq + lax.broadcasted_iota(jnp.int32, (1, lanes), 1) % tq

    def step(kt, runs, masked, first):
        off = pl.multiple_of(kt * tk, tk)
        kk = k_ref[0, pl.ds(off, tk), :]
        mask = (off + lax.broadcasted_iota(jnp.int32, (tk, 1), 0)) < pos if masked else None
        zs = [_dot(kk[:, j * LANES:(j + 1) * LANES], qpads[j]) for j in range(pairs)]
        sps = [jnp.where(mask, _softplus(z), 0.0) if masked else _softplus(z) for z in zs]
        drops = [_dot(upper2, jnp.concatenate(_split_bf16(sp), axis=0)) + run for sp, run in zip(sps, runs)]
        ws = [jnp.exp(z - sp - drop) for z, sp, drop in zip(zs, sps, drops)]
        if masked:
            ws = [jnp.where(mask, w, 0.0) for w in ws]
        for h in range(SB_HEADS):
            w = ws[h // 2][:, (h % 2) * tq:(h % 2 + 1) * tq].astype(BF16)
            pv = _dot(v_ref[0, h * HEAD_DIM:(h + 1) * HEAD_DIM, pl.ds(off, tk)], w)
            acc_ref[h] = pv if first else acc_ref[h] + pv
        return tuple(run + jnp.sum(sp, axis=0, keepdims=True) for run, sp in zip(runs, sps))

    diag = tq // tk
    runs = (jnp.zeros((1, lanes), F32),) * pairs
    for n in range(diag):
        runs = step((i + 1) * diag - 1 - n, runs, True, n == 0)
    lax.fori_loop(0, i * diag, lambda n, rs: step(i * diag - 1 - n, rs, False, False), runs)
    for h in range(SB_HEADS):
        o_ref[0, h * HEAD_DIM:(h + 1) * HEAD_DIM, :] = acc_ref[h]


def _sb_prompt(qbt, kbb, vbt, *, batch, seq, tq=256, tk=128):
    assert tq % tk == 0 and seq % tq == 0
    return pl.pallas_call(
        functools.partial(_sb_prompt_kernel, tq=tq, tk=tk),
        grid=(batch, seq // tq),
        in_specs=[pl.BlockSpec((1, SB_WIDTH, tq), lambda b, i: (b, 0, i)),
                  pl.BlockSpec((1, seq, SB_WIDTH), lambda b, i: (b, 0, 0)),
                  pl.BlockSpec((1, SB_WIDTH, seq), lambda b, i: (b, 0, 0))],
        out_specs=pl.BlockSpec((1, SB_WIDTH, tq), lambda b, i: (b, 0, i)),
        out_shape=jax.ShapeDtypeStruct((batch, SB_WIDTH, seq), F32),
        scratch_shapes=[pltpu.VMEM((SB_HEADS, HEAD_DIM, tq), F32)],
        compiler_params=pltpu.CompilerParams(dimension_semantics=("arbitrary",) * 2, vmem_limit_bytes=VMEM_LIMIT),
        name="sb_prompt",
    )(qbt, kbb, vbt)


def _mix_prompt_kernel(x_ref, oa_ref, za_ref, ob_ref, zb_ref, gm_ref, wat_ref, wbt_ref, wot_ref, g_ref, y_ref, *,
                       d_model):
    ya = _dot(wat_ref[...], (oa_ref[0] * za_ref[0]).astype(BF16))
    yb = _dot(wbt_ref[...], (ob_ref[0] * zb_ref[0]).astype(BF16))
    gm = gm_ref[0]
    mixed = gm[:d_model] * ya + gm[d_model:] * yb
    out = _dot(wot_ref[...], mixed.astype(BF16))
    out = out * lax.rsqrt(jnp.mean(out * out, axis=0, keepdims=True) + RMS_EPS)
    y_ref[...] = x_ref[...] + out.T * g_ref[...]


def _mix_prompt(x2d, oat, zat, obt, zbt, gmt, wat, wbt, wot, g_post, *, batch, seq, tm):
    m, d_model = x2d.shape
    nt = seq // tm
    row = lambda i: (i, 0)
    full = lambda i: (0, 0)
    tr = lambda i: (i // nt, 0, i % nt)
    return pl.pallas_call(
        functools.partial(_mix_prompt_kernel, d_model=d_model),
        grid=(m // tm,),
        in_specs=[pl.BlockSpec((tm, d_model), row), pl.BlockSpec((1, NSA_WIDTH, tm), tr),
                  pl.BlockSpec((1, NSA_WIDTH, tm), tr), pl.BlockSpec((1, SB_WIDTH, tm), tr),
                  pl.BlockSpec((1, SB_WIDTH, tm), tr), pl.BlockSpec((1, N_MERGE * d_model, tm), tr),
                  pl.BlockSpec((d_model, NSA_WIDTH), full), pl.BlockSpec((d_model, SB_WIDTH), full),
                  pl.BlockSpec((d_model, d_model), full), pl.BlockSpec((1, d_model), full)],
        out_specs=pl.BlockSpec((tm, d_model), row),
        out_shape=jax.ShapeDtypeStruct((m, d_model), F32),
        compiler_params=pltpu.CompilerParams(dimension_semantics=("arbitrary",), vmem_limit_bytes=VMEM_LIMIT),
        name="mix_prompt",
    )(x2d, oat, zat, obt, zbt, gmt, wat, wbt, wot, g_post)


def _mix_sample_kernel(x_ref, oa_ref, za_ref, ob_ref, zb_ref, gm_ref, wat_ref, wbt_ref, wot_ref, g_ref, y_ref, *,
                       d_model):
    ya = _dot_nt((oa_ref[...] * za_ref[...]).astype(BF16), wat_ref[...])
    yb = _dot_nt((ob_ref[...] * zb_ref[...]).astype(BF16), wbt_ref[...])
    gm = gm_ref[...]
    mixed = gm[:, :d_model] * ya + gm[:, d_model:] * yb
    out = _dot_nt(mixed.astype(BF16), wot_ref[...])
    y_ref[...] = x_ref[...] + _rms_scale(out, g_ref[...])


def _mix_sample(x2d, o_a, za, o_b, zb, gm, wat, wbt, wot, g_post):
    m, d_model = x2d.shape
    args = (x2d, o_a, za, o_b, zb, gm, wat, wbt, wot, g_post)
    return pl.pallas_call(
        functools.partial(_mix_sample_kernel, d_model=d_model),
        grid=(1,),
        in_specs=[pl.BlockSpec(a.shape, lambda i: (0, 0)) for a in args],
        out_specs=pl.BlockSpec((m, d_model), lambda i: (0, 0)),
        out_shape=jax.ShapeDtypeStruct((m, d_model), F32),
        compiler_params=pltpu.CompilerParams(dimension_semantics=("arbitrary",), vmem_limit_bytes=VMEM_LIMIT),
        name="mix_sample",
    )(*args)


def _head_pad(q):
    q2 = jnp.concatenate([q, q], axis=1)
    row = lax.broadcasted_iota(jnp.int32, q2.shape, 0)
    lane = lax.broadcasted_iota(jnp.int32, q2.shape, 1)
    return jnp.where(row // NSA_GROUP == lane // HEAD_DIM, q2, 0.0)


def _sb_decode_kernel(pt_ref, q_ref, *refs, ppg):
    page_refs = refs[:ppg]
    o_ref = refs[ppg]
    run_ref, acc_ref = refs[ppg + 1:]
    s = pl.program_id(1)

    @pl.when(s == 0)
    def _():
        run_ref[...] = jnp.zeros_like(run_ref)
        acc_ref[...] = jnp.zeros_like(acc_ref)

    q = q_ref[0]
    qrep = jnp.concatenate([q] * SB_HEADS, axis=1)
    row = lax.broadcasted_iota(jnp.int32, qrep.shape, 0)
    lane = lax.broadcasted_iota(jnp.int32, qrep.shape, 1)
    diag = row == lane // HEAD_DIM
    qbd = jnp.where(diag, qrep, 0.0).astype(BF16)

    z = jnp.concatenate([_dot(qbd, pr[0, :SB_WIDTH, :].astype(BF16)) for pr in page_refs], axis=0)
    sp = _softplus(z)
    r = lax.broadcasted_iota(jnp.int32, (PAGE_SIZE, PAGE_SIZE), 0)
    c = lax.broadcasted_iota(jnp.int32, (PAGE_SIZE, PAGE_SIZE), 1)
    lower = jnp.where(r > c, 1.0, 0.0).astype(BF16)
    hi, lo = _split_bf16(sp)
    drop = _dot(hi, lower) + _dot(lo, lower)
    tot = jnp.sum(sp, axis=1, keepdims=True)
    run = run_ref[...]
    acc = acc_ref[...]
    for n in range(ppg):
        sl = slice(n * SB_HEADS, (n + 1) * SB_HEADS)
        a = jnp.exp(z[sl] - sp[sl] - drop[sl] - run)
        acc = acc + _dot_nt(a.astype(BF16), page_refs[n][0, SB_WIDTH:, :].astype(BF16))
        run = run + tot[sl]
    run_ref[...] = run
    acc_ref[...] = acc

    @pl.when(s == pl.num_programs(1) - 1)
    def _():
        o_ref[0] = jnp.sum(jnp.where(diag, acc, 0.0), axis=0, keepdims=True)


def _sb_decode(page_table_flat, q_b, cache_t, *, dec_batch, n_pages, ppg=8):
    steps = n_pages // ppg

    def page_map(n):
        return lambda b, s, pt: (pt[b * n_pages + n_pages - 1 - (s * ppg + n)], 0, 0)

    grid_spec = pltpu.PrefetchScalarGridSpec(
        num_scalar_prefetch=1, grid=(dec_batch, steps),
        in_specs=[pl.BlockSpec((1, SB_HEADS, HEAD_DIM), lambda b, s, pt: (b, 0, 0))]
        + [pl.BlockSpec((1, 2 * SB_WIDTH, PAGE_SIZE), page_map(n)) for n in range(ppg)],
        out_specs=pl.BlockSpec((1, 1, SB_WIDTH), lambda b, s, pt: (b, 0, 0)),
        scratch_shapes=[pltpu.VMEM((SB_HEADS, 1), F32), pltpu.VMEM((SB_HEADS, SB_WIDTH), F32)])
    return pl.pallas_call(
        functools.partial(_sb_decode_kernel, ppg=ppg),
        grid_spec=grid_spec,
        out_shape=jax.ShapeDtypeStruct((dec_batch, 1, SB_WIDTH), F32),
        compiler_params=pltpu.CompilerParams(dimension_semantics=("arbitrary", "arbitrary"),
                                             vmem_limit_bytes=VMEM_LIMIT),
        name="sb_decode",
    )(page_table_flat, q_b, *([cache_t] * ppg))


def _nsa_cmp_decode_kernel(pt_ref, q_ref, wb_ref, *refs, ppg, nbp):
    page_refs = refs[:ppg]
    ocmp_ref, sel_ref = refs[ppg:ppg + 2]
    pool_ref = refs[ppg + 2]
    s = pl.program_id(1)
    group = LANES // (ppg * BLOCKS_PER_PAGE)
    slot = s % group
    chunk = s // group
    xk = jnp.concatenate([pr[0, :KV_PAIR, :].astype(BF16) for pr in page_refs], axis=1)
    xv = jnp.concatenate([pr[0, KV_PAIR:, :].astype(BF16) for pr in page_refs], axis=1)
    part = jnp.concatenate([_dot(xk, wb_ref[0, slot]), _dot(xv, wb_ref[1, slot])], axis=0)

    @pl.when(slot == 0)
    def _():
        pool_ref[chunk] = part

    @pl.when(slot != 0)
    def _():
        pool_ref[chunk] = pool_ref[chunk] + part

    @pl.when(s == pl.num_programs(1) - 1)
    def _():
        n_chunks = pool_ref.shape[0]
        nl = n_chunks * LANES
        qpad = _head_pad(q_ref[0]).astype(BF16)
        pooled = jnp.concatenate([pool_ref[ch] for ch in range(n_chunks)], axis=1)
        blk_l = lax.broadcasted_iota(jnp.int32, (1, nl), 1)
        live = blk_l < nbp
        sc = jnp.where(live, _dot(qpad, pooled[:KV_PAIR].astype(BF16)), NEG_INF)
        e = jnp.where(live, jnp.exp(sc - jnp.max(sc, axis=1, keepdims=True)), 0.0)
        p = e / jnp.sum(e, axis=1, keepdims=True)
        o = _dot_nt(p.astype(BF16), pooled[KV_PAIR:].astype(BF16))
        hrow = lax.broadcasted_iota(jnp.int32, o.shape, 0)
        ocmp_ref[0] = jnp.where(hrow < NSA_GROUP, o, pltpu.roll(o, HEAD_DIM, 1))[:, :HEAD_DIM]

        blk_s = lax.broadcasted_iota(jnp.int32, (nl, 1), 0)
        eye = lax.broadcasted_iota(jnp.int32, (nl, nl), 0) == lax.broadcasted_iota(jnp.int32, (nl, nl), 1)
        n_sel = NSA_TOPK - 1
        kslot = lax.broadcasted_iota(jnp.int32, (NSA_TOPK, nl), 0)
        for gi in range(NSA_KV_HEADS):
            imp = jnp.sum(p[gi * NSA_GROUP:(gi + 1) * NSA_GROUP], axis=0, keepdims=True)
            forced = (blk_l == 0) | (blk_l == nbp - 1)
            srow = jnp.where(live, jnp.where(forced, FORCE_SCORE, imp), NEG_INF)
            scol = jnp.sum(jnp.where(eye, srow, 0.0), axis=1, keepdims=True)
            beats = (scol > srow) | ((scol == srow) & (blk_s < blk_l))
            rank = jnp.sum(jnp.where(beats, 1.0, 0.0), axis=0, keepdims=True)
            sel = rank < n_sel
            selcol = jnp.sum(jnp.where(eye & sel, 1.0, 0.0), axis=1, keepdims=True) > 0.5
            rank_sel = jnp.sum(jnp.where(selcol & (blk_s < blk_l), 1.0, 0.0), axis=0, keepdims=True)
            onehot = sel & (rank_sel.astype(jnp.int32) == kslot)
            idx = jnp.sum(jnp.where(onehot, blk_l, 0), axis=1, keepdims=True)
            sel_ref[0, gi] = jnp.broadcast_to(idx, (NSA_TOPK, LANES))


def _decode_pool_weights(w_cmp, ppg):
    per_step = ppg * BLOCKS_PER_PAGE
    group = LANES // per_step
    k = jnp.arange(ppg * PAGE_SIZE)
    target = (k // PAGE_SIZE) * BLOCKS_PER_PAGE + (k % PAGE_SIZE) // NSA_BLOCK
    hit = jnp.arange(LANES)[None, None, :] == (jnp.arange(group)[:, None, None] * per_step + target[None, :, None])
    w_row = w_cmp[:, k % NSA_BLOCK]
    return jnp.where(hit[None], w_row[:, None, :, None], 0.0).astype(BF16)


def _nsa_cmp_decode(page_table_flat, q_a, w_cmp, cache_t, *, dec_batch, n_pages, ppg):
    assert LANES % (ppg * BLOCKS_PER_PAGE) == 0
    steps = n_pages // ppg
    nbp = n_pages * BLOCKS_PER_PAGE
    wb = _decode_pool_weights(w_cmp, ppg)

    def page_map(n):
        return lambda b, s, pt: (pt[b * n_pages + s * ppg + n], 0, 0)

    grid_spec = pltpu.PrefetchScalarGridSpec(
        num_scalar_prefetch=1, grid=(dec_batch, steps),
        in_specs=[pl.BlockSpec((1, NSA_HEADS, HEAD_DIM), lambda b, s, pt: (b, 0, 0)),
                  pl.BlockSpec(wb.shape, lambda b, s, pt: (0, 0, 0, 0))]
        + [pl.BlockSpec((1, 2 * KV_PAIR, PAGE_SIZE), page_map(n)) for n in range(ppg)],
        out_specs=[pl.BlockSpec((1, NSA_HEADS, HEAD_DIM), lambda b, s, pt: (b, 0, 0)),
                   pl.BlockSpec((1, NSA_KV_HEADS, NSA_TOPK, LANES), lambda b, s, pt: (b, 0, 0, 0))],
        scratch_shapes=[pltpu.VMEM((pl.cdiv(nbp, LANES), 2 * KV_PAIR, LANES), F32)])
    return pl.pallas_call(
        functools.partial(_nsa_cmp_decode_kernel, ppg=ppg, nbp=nbp),
        grid_spec=grid_spec,
        out_shape=[jax.ShapeDtypeStruct((dec_batch, NSA_HEADS, HEAD_DIM), F32),
                   jax.ShapeDtypeStruct((dec_batch, NSA_KV_HEADS, NSA_TOPK, LANES), jnp.int32)],
        compiler_params=pltpu.CompilerParams(dimension_semantics=("arbitrary", "arbitrary"),
                                             vmem_limit_bytes=VMEM_LIMIT),
        name="nsa_cmp_decode",
    )(page_table_flat, q_a, wb, *([cache_t] * ppg))


def _nsa_sel_decode_kernel(sel_ref, pt_ref, q_ref, ocmp_ref, ga_ref, new_ref, winp_ref, *refs, n_blk):
    blk_refs = refs[:NSA_KV_HEADS * n_blk]
    o_ref, wino_ref = refs[NSA_KV_HEADS * n_blk:]
    b = pl.program_id(0)
    qb = _head_pad(q_ref[0]).astype(BF16)
    new = new_ref[0]
    hrow = lax.broadcasted_iota(jnp.int32, (NSA_HEADS, KV_PAIR), 0)
    top = hrow < NSA_GROUP

    def fold(o):
        return jnp.where(top, o, pltpu.roll(o, HEAD_DIM, 1))[:, :HEAD_DIM]

    def softmax(sc, mask):
        sc = jnp.where(mask, sc, NEG_INF)
        e = jnp.where(mask, jnp.exp(sc - jnp.max(sc, axis=1, keepdims=True)), 0.0)
        return e / jnp.sum(e, axis=1, keepdims=True)

    lane = lax.broadcasted_iota(jnp.int32, (1, PAGE_SIZE), 1)
    first = lax.broadcasted_iota(jnp.int32, (KV_PAIR, PAGE_SIZE), 1) == 0
    k_new = jnp.where(first, new[2 * KV_PAIR:3 * KV_PAIR], 0.0).astype(BF16)
    v_new = jnp.where(first, new[3 * KV_PAIR:4 * KV_PAIR], 0.0).astype(BF16)
    o_g = []
    for gi in range(NSA_KV_HEADS):
        scs, masks = [], []
        for n in range(n_blk):
            half = sel_ref[(b * NSA_KV_HEADS + gi) * NSA_TOPK + n] % BLOCKS_PER_PAGE
            scs.append(_dot(qb, blk_refs[gi * n_blk + n][0, :KV_PAIR, :].astype(BF16)))
            masks.append(lane // NSA_BLOCK == half)
        scs.append(_dot(qb, k_new))
        masks.append(lane == 0)
        p = softmax(jnp.concatenate(scs, axis=1), jnp.concatenate(masks, axis=1)).astype(BF16)
        o = _dot_nt(p[:, n_blk * PAGE_SIZE:], v_new)
        for n in range(n_blk):
            o = o + _dot_nt(p[:, n * PAGE_SIZE:(n + 1) * PAGE_SIZE],
                            blk_refs[gi * n_blk + n][0, KV_PAIR:, :].astype(BF16))
        o_g.append(o)
    o_slc = fold(jnp.where(top, o_g[0], o_g[1]))

    wp = winp_ref[0]
    w = wp.shape[1]
    wl = lax.broadcasted_iota(jnp.int32, wp.shape, 1)
    shifted = jnp.where(wl == w - 1, new[4 * KV_PAIR:], pltpu.roll(wp, w - 1, 1))
    wino_ref[0] = shifted
    sc = _dot(qb, shifted[:KV_PAIR].astype(BF16))
    p = softmax(sc, jnp.full((1, w), True)).astype(BF16)
    o_win = fold(_dot_nt(p, shifted[KV_PAIR:].astype(BF16)))

    ga = ga_ref[0]
    o_ref[0] = ga[0] * ocmp_ref[0] + ga[1] * o_slc + ga[2] * o_win


def _nsa_sel_decode(sel_flat, page_table_flat, q_a, o_cmp, ga, new_kv, win_past_t, cache_t, *, dec_batch, n_pages):
    n_blk = NSA_TOPK - 1
    w = win_past_t.shape[2]

    def blk_map(gi, n):
        def f(b, sel, pt):
            blk = sel[(b * NSA_KV_HEADS + gi) * NSA_TOPK + n]
            return (pt[b * n_pages + blk // BLOCKS_PER_PAGE], 1, 0)
        return f

    grid_spec = pltpu.PrefetchScalarGridSpec(
        num_scalar_prefetch=2, grid=(dec_batch,),
        in_specs=[pl.BlockSpec((1, NSA_HEADS, HEAD_DIM), lambda b, sel, pt: (b, 0, 0)),
                  pl.BlockSpec((1, NSA_HEADS, HEAD_DIM), lambda b, sel, pt: (b, 0, 0)),
                  pl.BlockSpec((1, NSA_BRANCHES, NSA_HEADS, 1), lambda b, sel, pt: (b, 0, 0, 0)),
                  pl.BlockSpec((1, KVA_WIDTH, 1), lambda b, sel, pt: (b, 0, 0)),
                  pl.BlockSpec((1, 2 * KV_PAIR, w), lambda b, sel, pt: (b, 0, 0))]
        + [pl.BlockSpec((1, 2 * KV_PAIR, PAGE_SIZE), blk_map(gi, n))
           for gi in range(NSA_KV_HEADS) for n in range(n_blk)],
        out_specs=[pl.BlockSpec((1, NSA_HEADS, HEAD_DIM), lambda b, sel, pt: (b, 0, 0)),
                   pl.BlockSpec((1, 2 * KV_PAIR, w), lambda b, sel, pt: (b, 0, 0))])
    return pl.pallas_call(
        functools.partial(_nsa_sel_decode_kernel, n_blk=n_blk),
        grid_spec=grid_spec,
        out_shape=[jax.ShapeDtypeStruct((dec_batch, NSA_HEADS, HEAD_DIM), F32),
                   jax.ShapeDtypeStruct((dec_batch, 2 * KV_PAIR, w), F32)],
        compiler_params=pltpu.CompilerParams(dimension_semantics=("arbitrary",), vmem_limit_bytes=VMEM_LIMIT),
        name="nsa_sel_decode",
    )(sel_flat, page_table_flat, q_a, o_cmp, ga, new_kv, win_past_t, *([cache_t] * (NSA_KV_HEADS * n_blk)))


def _feature_major(a, lead):
    nl = len(lead)
    t = jnp.moveaxis(a, nl, -1)
    return t.reshape(lead + (-1, a.shape[nl]))


def _time_major(a_t, feat_shape):
    lead, _, time = a_t.shape
    return jnp.moveaxis(a_t.reshape((lead,) + feat_shape + (time,)), -1, 1)


def _layer(x_prompt, x_sample, nsa_cache, win_cache, sb_cache, page_table, w_in, w_cmp, w_a, w_b, w_o, g_pre, g_post):
    batch, seq, d_model = x_prompt.shape
    dec_batch, dec_seq, _ = x_sample.shape
    n_pages = page_table.shape[1]
    past_len = n_pages * PAGE_SIZE
    n_phys = nsa_cache.shape[0]
    tm, ppg = 256, 16
    assert dec_seq == 1 and seq % tm == 0 and seq >= NSA_WINDOW and past_len >= NSA_WINDOW
    assert n_pages % ppg == 0 and past_len // NSA_BLOCK >= NSA_TOPK
    assert win_cache.shape[1] == NSA_WINDOW and nsa_cache.shape[1] == PAGE_SIZE

    wt_perm = _permute_w_in_t(w_in, d_model)
    wat, wbt, wot = w_a.T.astype(BF16), w_b.T.astype(BF16), w_o.T.astype(BF16)
    g_pre2, g_post2 = g_pre.reshape(1, d_model), g_post.reshape(1, d_model)
    nsa_feat = (4, NSA_KV_HEADS, HEAD_DIM)
    win_feat = (2, NSA_KV_HEADS, HEAD_DIM)
    sb_feat = (2, SB_HEADS, HEAD_DIM)

    xp = x_prompt.reshape(batch * seq, d_model)
    cos, sin = _rope_angles(jnp.arange(seq, dtype=jnp.int32))
    wp_prompt = _pool_weights(w_cmp, 1, tm // NSA_BLOCK, SUBLANES).reshape(2 * SUBLANES, tm)
    (nsa_t, win_t, sb_t, za_t, zb_t, gm_t, qa_t, ga_t, kb, v_t, pooled, qb_t, kbb, vb_t) = _project_prompt(
        xp, g_pre2, wt_perm, cos.T, sin.T, wp_prompt, batch=batch, seq=seq, tm=tm)
    oa_t = _nsa_prompt(qa_t, kb.reshape(batch, seq, 2 * KV_PAIR), v_t,
                       pooled.reshape(batch, seq // NSA_BLOCK, 2 * KV_PAIR), ga_t, batch=batch, seq=seq)
    ob_t = _sb_prompt(qb_t, kbb.reshape(batch, seq, SB_WIDTH), vb_t, batch=batch, seq=seq)
    y_prompt = _mix_prompt(xp, oa_t, za_t, ob_t, zb_t, gm_t, wat, wbt, wot, g_post2,
                           batch=batch, seq=seq, tm=tm).reshape(batch, seq, d_model)
    nsa_kv_prompt = _time_major(nsa_t, nsa_feat)
    win_kv_prompt = _time_major(win_t[:, :, seq - NSA_WINDOW:], win_feat)
    sb_kv_prompt = _time_major(sb_t, sb_feat)

    xs = x_sample.reshape(dec_batch, d_model)
    tabs_s = _rope_lane_tables(jnp.full((dec_batch,), past_len, jnp.int32))
    (nsa_s, win_s, sb_s, za_s, zb_s, gm_s, qa_s, ga_s, qb_s) = _project_sample(xs, g_pre2, wt_perm, tabs_s)
    pt_flat = page_table.reshape(-1)
    sb_cache_t = _feature_major(sb_cache, (n_phys,))
    nsa_cache_t = _feature_major(nsa_cache, (n_phys,))
    win_cache_t = _feature_major(win_cache, (dec_batch,))
    o_b_s = _sb_decode(pt_flat, qb_s.reshape(dec_batch, SB_HEADS, HEAD_DIM), sb_cache_t,
                       dec_batch=dec_batch, n_pages=n_pages, ppg=ppg)
    qa3 = qa_s.reshape(dec_batch, NSA_HEADS, HEAD_DIM)
    o_cmp_s, sel = _nsa_cmp_decode(pt_flat, qa3, w_cmp, nsa_cache_t, dec_batch=dec_batch, n_pages=n_pages, ppg=ppg)
    ga3 = ga_s[:, :GATE_A].reshape(dec_batch, NSA_BRANCHES, NSA_HEADS, 1)
    new_kv = jnp.concatenate([nsa_s, win_s], axis=1).reshape(dec_batch, KVA_WIDTH, 1)
    o_a_s, win_out_t = _nsa_sel_decode(sel[:, :, :, 0].reshape(-1), pt_flat, qa3, o_cmp_s, ga3, new_kv,
                                       win_cache_t, nsa_cache_t, dec_batch=dec_batch, n_pages=n_pages)
    y_sample = _mix_sample(xs, o_a_s.reshape(dec_batch, NSA_WIDTH), za_s, o_b_s.reshape(dec_batch, SB_WIDTH), zb_s,
                           gm_s, wat, wbt, wot, g_post2).reshape(dec_batch, 1, d_model)
    nsa_kv_sample = nsa_s.reshape((dec_batch, 1) + nsa_feat)
    win_kv_sample = _time_major(win_out_t, win_feat)
    sb_kv_sample = sb_s.reshape((dec_batch, 1) + sb_feat)
    return (y_prompt, y_sample, nsa_kv_prompt, win_kv_prompt, sb_kv_prompt, nsa_kv_sample, win_kv_sample,
            sb_kv_sample)


def kernel(x_prompt, x_sample, cache_nsa_kv, cache_nsa_win_kv, cache_sb_kv, page_table, w_in, w_cmp, w_branch_a,
           w_branch_b, w_out, g_pre, g_post):
    hp, hs = x_prompt, x_sample
    caches = [[] for _ in range(6)]
    for layer in range(w_in.shape[0]):
        outs = _layer(hp, hs, cache_nsa_kv[layer], cache_nsa_win_kv[layer], cache_sb_kv[layer], page_table,
                      w_in[layer], w_cmp[layer], w_branch_a[layer], w_branch_b[layer], w_out[layer],
                      g_pre[layer], g_post[layer])
        hp, hs = outs[0], outs[1]
        for acc, o in zip(caches, outs[2:]):
            acc.append(o)
    return (hp, hs) + tuple(jnp.stack(c) for c in caches)
```

```python
import functools

import jax
import jax.numpy as jnp
from jax import lax
from jax.experimental import pallas as pl
from jax.experimental.pallas import tpu as pltpu

HEAD_DIM = 64
ROT_DIM = HEAD_DIM // 4
ROPE_THETA = 500000.0
NSA_HEADS = 8
NSA_KV_HEADS = 2
NSA_GROUP = NSA_HEADS // NSA_KV_HEADS
NSA_BRANCHES = 3
NSA_BLOCK = 64
NSA_TOPK = 16
NSA_WINDOW = 512
NSA_WIDTH = NSA_HEADS * HEAD_DIM
SB_HEADS = 8
SB_WIDTH = SB_HEADS * HEAD_DIM
N_MERGE = 2
PAGE_SIZE = 128
RMS_EPS = 1e-6
NEG_INF = -1e30
FORCE_SCORE = 1e3
SCALE = HEAD_DIM ** -0.5
LOG2E = 1.4426950408889634

LANES = 128
SUBLANES = 8
KV_PAIR = NSA_KV_HEADS * HEAD_DIM
assert KV_PAIR == LANES and PAGE_SIZE == LANES and PAGE_SIZE % NSA_BLOCK == 0
KVA_WIDTH = 2 * NSA_BRANCHES * KV_PAIR
GATE_A = NSA_BRANCHES * NSA_HEADS
BLOCKS_PER_PAGE = PAGE_SIZE // NSA_BLOCK
VMEM_LIMIT = 56 * 1024 * 1024

C_QA = 0
C_KVA = C_QA + NSA_WIDTH
C_ZA = C_KVA + KVA_WIDTH
C_QKVB = C_ZA + NSA_WIDTH
C_ZB = C_QKVB + 3 * SB_WIDTH
C_GM = C_ZB + SB_WIDTH

BF16 = jnp.bfloat16
F32 = jnp.float32


def _dot(a, b):
    return jnp.dot(a, b, preferred_element_type=F32)


def _dot_tn(a, b):
    return lax.dot_general(a, b, (((0,), (0,)), ((), ())), preferred_element_type=F32)


def _dot_nt(a, b):
    return lax.dot_general(a, b, (((1,), (1,)), ((), ())), preferred_element_type=F32)


def _sigmoid(x):
    return 1.0 / (1.0 + jnp.exp(-x))


def _rms_scale(x, g):
    return x * lax.rsqrt(jnp.mean(x * x, axis=-1, keepdims=True) + RMS_EPS) * g


def _rope_rows(v, cos, sin):
    half = ROT_DIM // 2
    parts = []
    for base in range(0, v.shape[0], HEAD_DIM):
        x1, x2 = v[base:base + half], v[base + half:base + 2 * half]
        parts += [x1 * cos - x2 * sin, x2 * cos + x1 * sin, v[base + 2 * half:base + HEAD_DIM]]
    return jnp.concatenate(parts, axis=0)


def _proj_prompt_kernel(x_ref, g_ref, wt_ref, cos_ref, sin_ref, wp_ref,
                        nsa_ref, win_ref, sb_ref, za_ref, zb_ref, gm_ref, qa_ref, ga_ref, kb_ref, vt_ref,
                        pool_ref, qb_ref, kbb_ref, vbt_ref, *, d_model):
    hb = _rms_scale(x_ref[...], g_ref[...]).astype(BF16)
    cos, sin = cos_ref[...], sin_ref[...]

    def seg(lo, width):
        return _dot_nt(wt_ref[lo:lo + width, :], hb)

    qa_ref[0] = (_rope_rows(seg(C_QA, NSA_WIDTH), cos, sin) * SCALE).astype(BF16)
    kv = seg(C_KVA, KVA_WIDTH)
    cmp_k = _rope_rows(kv[0 * LANES:1 * LANES], cos, sin)
    cmp_v = kv[1 * LANES:2 * LANES]
    slc_k = _rope_rows(kv[2 * LANES:3 * LANES], cos, sin)
    slc_v = kv[3 * LANES:4 * LANES]
    win_k = _rope_rows(kv[4 * LANES:5 * LANES], cos, sin)
    win_v = kv[5 * LANES:6 * LANES]
    nsa_ref[0] = jnp.concatenate([cmp_k, cmp_v, slc_k, slc_v], axis=0)
    win_ref[0] = jnp.concatenate([win_k, win_v], axis=0)
    kb_ref[...] = jnp.concatenate([slc_k.T, win_k.T], axis=1).astype(BF16)
    vt_ref[0] = jnp.concatenate([slc_v, win_v], axis=0).astype(BF16)
    wp = wp_ref[...]
    nblk = pool_ref.shape[1]
    pooled = jnp.concatenate([_dot_nt(wp[:SUBLANES], cmp_k.astype(BF16)),
                              _dot_nt(wp[SUBLANES:], cmp_v.astype(BF16))], axis=1)
    pool_ref[0] = pooled[:nblk]

    za = seg(C_ZA, NSA_WIDTH)
    za_ref[0] = (za * _sigmoid(za)).astype(za_ref.dtype)
    qkvb = seg(C_QKVB, 3 * SB_WIDTH)
    qb_ref[0] = (qkvb[:SB_WIDTH] * SCALE).astype(BF16)
    sb_ref[0] = qkvb[SB_WIDTH:]
    kbb_ref[...] = qkvb[SB_WIDTH:2 * SB_WIDTH].T.astype(BF16)
    vbt_ref[0] = qkvb[2 * SB_WIDTH:].astype(BF16)
    zb = seg(C_ZB, SB_WIDTH)
    zb_ref[0] = (zb * _sigmoid(zb)).astype(zb_ref.dtype)
    gm_ref[0] = _sigmoid(seg(C_GM, N_MERGE * d_model)).astype(gm_ref.dtype)
    ga_ref[0] = _sigmoid(seg(C_GM + N_MERGE * d_model, LANES))


def _project_prompt(x2d, g_pre, wt_perm, cos_t, sin_t, wp, *, batch, seq, tm):
    m, d_model = x2d.shape
    nt = seq // tm
    n_rows = wt_perm.shape[0]
    nblk = tm // NSA_BLOCK
    row = lambda i: (i, 0)
    full = lambda i: (0, 0)
    tab = lambda i: (0, i % nt)
    tr = lambda i: (i // nt, 0, i % nt)
    half = ROT_DIM // 2
    in_specs = [pl.BlockSpec((tm, d_model), row), pl.BlockSpec((1, d_model), full),
                pl.BlockSpec((n_rows, d_model), full),
                pl.BlockSpec((half, tm), tab), pl.BlockSpec((half, tm), tab),
                pl.BlockSpec((2 * SUBLANES, tm), full)]
    sds = jax.ShapeDtypeStruct

    def feat(width, dtype):
        return sds((batch, width, seq), dtype), pl.BlockSpec((1, width, tm), tr)

    def rows(width, dtype):
        return sds((m, width), dtype), pl.BlockSpec((tm, width), row)

    outs = [feat(4 * KV_PAIR, F32), feat(2 * KV_PAIR, F32), feat(2 * SB_WIDTH, F32),
            feat(NSA_WIDTH, BF16), feat(SB_WIDTH, BF16), feat(N_MERGE * d_model, BF16),
            feat(NSA_WIDTH, BF16), feat(LANES, F32), rows(2 * KV_PAIR, BF16), feat(2 * KV_PAIR, BF16),
            (sds((m // tm, nblk, 2 * KV_PAIR), F32), pl.BlockSpec((1, nblk, 2 * KV_PAIR), lambda i: (i, 0, 0))),
            feat(SB_WIDTH, BF16), rows(SB_WIDTH, BF16), feat(SB_WIDTH, BF16)]
    return pl.pallas_call(
        functools.partial(_proj_prompt_kernel, d_model=d_model),
        grid=(m // tm,), in_specs=in_specs, out_specs=[o[1] for o in outs], out_shape=[o[0] for o in outs],
        compiler_params=pltpu.CompilerParams(dimension_semantics=("arbitrary",), vmem_limit_bytes=VMEM_LIMIT),
        name="proj_prompt",
    )(x2d, g_pre, wt_perm, cos_t, sin_t, wp)


def _rope_lanes(v, c, s_up, s_dn):
    half = ROT_DIM // 2
    outs = []
    for j in range(v.shape[1] // LANES):
        blk = v[:, j * LANES:(j + 1) * LANES]
        outs.append(blk * c + pltpu.roll(blk, LANES - half, 1) * s_up + pltpu.roll(blk, half, 1) * s_dn)
    return outs[0] if len(outs) == 1 else jnp.concatenate(outs, axis=1)


def _proj_sample_kernel(x_ref, g_ref, wt_ref, cos_ref, sup_ref, sdn_ref,
                        nsa_ref, win_ref, sbkv_ref, za_ref, zb_ref, gm_ref, qa_ref, ga_ref, qb_ref, *, d_model):
    hb = _rms_scale(x_ref[...], g_ref[...]).astype(BF16)
    cos, s_up, s_dn = cos_ref[...], sup_ref[...], sdn_ref[...]

    def seg(lo, width):
        return _dot_nt(hb, wt_ref[lo:lo + width, :])

    qa_ref[...] = _rope_lanes(seg(C_QA, NSA_WIDTH), cos, s_up, s_dn) * SCALE
    kv = seg(C_KVA, KVA_WIDTH)
    parts = []
    for j in range(2 * NSA_BRANCHES):
        blk = kv[:, j * LANES:(j + 1) * LANES]
        parts.append(_rope_lanes(blk, cos, s_up, s_dn) if j % 2 == 0 else blk)
    nsa_ref[...] = jnp.concatenate(parts[:4], axis=1)
    win_ref[...] = jnp.concatenate(parts[4:], axis=1)
    za = seg(C_ZA, NSA_WIDTH)
    za_ref[...] = za * _sigmoid(za)
    qkvb = seg(C_QKVB, 3 * SB_WIDTH)
    qb_ref[...] = qkvb[:, :SB_WIDTH] * SCALE
    sbkv_ref[...] = qkvb[:, SB_WIDTH:]
    zb = seg(C_ZB, SB_WIDTH)
    zb_ref[...] = zb * _sigmoid(zb)
    gm_ref[...] = _sigmoid(seg(C_GM, N_MERGE * d_model))
    ga_ref[...] = _sigmoid(seg(C_GM + N_MERGE * d_model, LANES))


def _project_sample(x2d, g_pre, wt_perm, tables):
    m, d_model = x2d.shape
    full = lambda i: (0, 0)
    widths = [4 * KV_PAIR, 2 * KV_PAIR, 2 * SB_WIDTH, NSA_WIDTH, SB_WIDTH, N_MERGE * d_model,
              NSA_WIDTH, LANES, SB_WIDTH]
    return pl.pallas_call(
        functools.partial(_proj_sample_kernel, d_model=d_model),
        grid=(1,),
        in_specs=[pl.BlockSpec((m, d_model), full), pl.BlockSpec((1, d_model), full),
                  pl.BlockSpec(wt_perm.shape, full)] + [pl.BlockSpec((m, LANES), full)] * 3,
        out_specs=[pl.BlockSpec((m, w), full) for w in widths],
        out_shape=[jax.ShapeDtypeStruct((m, w), F32) for w in widths],
        compiler_params=pltpu.CompilerParams(dimension_semantics=("arbitrary",), vmem_limit_bytes=VMEM_LIMIT),
        name="proj_sample",
    )(x2d, g_pre, wt_perm, *tables)


def _rope_angles(pos):
    half = ROT_DIM // 2
    inv_freq = ROPE_THETA ** (-jnp.arange(half, dtype=F32) / half)
    ang = pos.astype(F32)[:, None] * inv_freq[None, :]
    return jnp.cos(ang), jnp.sin(ang)


def _rope_lane_tables(pos):
    cos, sin = _rope_angles(pos)
    n = pos.shape[0]
    half = ROT_DIM // 2
    ones = jnp.ones((n, HEAD_DIM - ROT_DIM), F32)
    zeros = jnp.zeros((n, HEAD_DIM - ROT_DIM), F32)
    zh = jnp.zeros((n, half), F32)
    rep = LANES // HEAD_DIM
    return tuple(jnp.tile(t, (1, rep)) for t in (jnp.concatenate([cos, cos, ones], axis=1),
                                                 jnp.concatenate([-sin, zh, zeros], axis=1),
                                                 jnp.concatenate([zh, sin, zeros], axis=1)))


def _permute_w_in_t(w_in, d_model):
    c = [NSA_WIDTH, KVA_WIDTH, NSA_WIDTH, GATE_A, 3 * SB_WIDTH, SB_WIDTH, N_MERGE * d_model]
    o = [0]
    for s in c:
        o.append(o[-1] + s)
    wt = w_in.T
    pad = jnp.zeros((LANES - GATE_A, d_model), w_in.dtype)
    return jnp.concatenate([wt[:o[3]], wt[o[4]:], wt[o[3]:o[4]], pad], axis=0).astype(BF16)


def _pool_weights(w_cmp, n_tiles, blocks_per_tile, rows):
    lane = jnp.arange(blocks_per_tile * NSA_BLOCK)
    owner = jnp.arange(n_tiles)[:, None, None] * blocks_per_tile + (lane // NSA_BLOCK)[None, None, :]
    hit = jnp.arange(rows)[None, :, None] == owner
    w_lane = jnp.tile(w_cmp, (1, blocks_per_tile))
    return jnp.where(hit[None], w_lane[:, None, None, :], 0.0).astype(BF16)


def _nsa_prompt_kernel(q_ref, kslc_ref, kwin_ref, vslc_ref, vwin_ref, pool_ref, ga_ref, o_ref, sel_ref, acc_ref, *,
                       tq, tk, nb):
    i = pl.program_id(1)
    lanes = NSA_GROUP * tq
    kvh = NSA_KV_HEADS
    row = lax.broadcasted_iota(jnp.int32, (KV_PAIR, lanes), 0)
    pos1 = i * tq + lax.broadcasted_iota(jnp.int32, (1, tq), 1)
    pos = jnp.concatenate([pos1] * NSA_GROUP, axis=1)
    blk = lax.broadcasted_iota(jnp.int32, (nb, 1), 0)
    cmask = ((blk + 1) * NSA_BLOCK - 1) <= pos
    cur = pos1 // NSA_BLOCK
    valid = blk <= cur
    forced = (blk == 0) | (blk == cur) | (blk == cur - 1)
    n_sel = min(NSA_TOPK, nb)
    pooled = pool_ref[0]
    kc = pooled[:, :KV_PAIR].astype(BF16)
    vc = pooled[:, KV_PAIR:].astype(BF16)

    qpads, o_cmp = [], []
    for g in range(kvh):
        qt = q_ref[0, g * NSA_GROUP * HEAD_DIM:(g + 1) * NSA_GROUP * HEAD_DIM, :]
        q4 = jnp.concatenate([qt[hh * HEAD_DIM:(hh + 1) * HEAD_DIM] for hh in range(NSA_GROUP)], axis=1)
        qpad = jnp.where(row // HEAD_DIM == g, jnp.concatenate([q4, q4], axis=0), jnp.zeros((), BF16))
        qpads.append(qpad)
        s = jnp.where(cmask, _dot(kc, qpad), NEG_INF)
        e = jnp.exp(s - jnp.max(s, axis=0, keepdims=True))
        p = jnp.where(cmask, e / jnp.sum(e, axis=0, keepdims=True), 0.0)
        o_cmp.append(_dot_tn(vc, p.astype(BF16))[g * HEAD_DIM:(g + 1) * HEAD_DIM])
        imp = p[:, 0:tq]
        for hh in range(1, NSA_GROUP):
            imp = imp + p[:, hh * tq:(hh + 1) * tq]
        score = jnp.where(valid, jnp.where(forced, FORCE_SCORE, imp), NEG_INF)
        for j in range(nb):
            sj = score[j:j + 1, :]
            beats = (score > sj) | ((score == sj) & (blk < j))
            cnt = jnp.sum(jnp.where(beats, 1.0, 0.0), axis=0, keepdims=True)
            sel_ref[g, j] = jnp.where(cnt < n_sel, 0.0, NEG_INF)

    bpt = tk // NSA_BLOCK

    def sel_bias(g, kt):
        rows = [jnp.broadcast_to(sel_ref[g, kt * bpt + r], (NSA_BLOCK, tq)) for r in range(bpt)]
        return jnp.concatenate(rows, axis=0)

    def step(kt, carry, near):
        ms, ls = list(carry[0]), list(carry[1])
        off = pl.multiple_of(kt * tk, tk)
        ks = kslc_ref[0, pl.ds(off, tk), :]
        chains = [(g, g, ks, vslc_ref) for g in range(kvh)]
        biases = [sel_bias(g, kt) for g in range(kvh)]
        if near:
            kw = kwin_ref[0, pl.ds(off, tk), :]
            d = pos1 - (off + lax.broadcasted_iota(jnp.int32, (tk, 1), 0))
            causal = jnp.where(d >= 0, 0.0, NEG_INF)
            window = jnp.where((d >= 0) & (d < NSA_WINDOW), 0.0, NEG_INF)
            chains += [(kvh + g, g, kw, vwin_ref) for g in range(kvh)]
            biases = [bs + causal for bs in biases] + [window] * kvh
        scs = [_dot(kk, qpads[g]) + jnp.concatenate([bs] * NSA_GROUP, axis=1)
               for (_, g, kk, _), bs in zip(chains, biases)]
        m_new = [jnp.maximum(ms[c], jnp.max(sc, axis=0, keepdims=True)) for (c, _, _, _), sc in zip(chains, scs)]
        pps = [jnp.exp(sc - jnp.maximum(mn, 0.5 * NEG_INF)) for sc, mn in zip(scs, m_new)]
        pvs = [_dot(v_ref[0, g * HEAD_DIM:(g + 1) * HEAD_DIM, pl.ds(off, tk)], pp.astype(BF16))
               for (_, g, _, v_ref), pp in zip(chains, pps)]
        for (c, _, _, _), mn, pp, pv in zip(chains, m_new, pps, pvs):
            alpha = jnp.exp(ms[c] - mn)
            acc_ref[c] = alpha * acc_ref[c] + pv
            ls[c] = alpha * ls[c] + jnp.sum(pp, axis=0, keepdims=True)
            ms[c] = mn
        return tuple(ms), tuple(ls)

    n_chain = 2 * kvh
    acc_ref[...] = jnp.zeros_like(acc_ref)
    carry = ((jnp.full((1, lanes), NEG_INF, F32),) * n_chain, (jnp.zeros((1, lanes), F32),) * n_chain)
    first = jnp.maximum(i * tq - (NSA_WINDOW - 1), 0) // tk
    carry = lax.fori_loop(0, first, lambda kt, cr: step(kt, cr, False), carry)
    _, ls = lax.fori_loop(first, (i + 1) * (tq // tk), lambda kt, cr: step(kt, cr, True), carry)

    ga = ga_ref[0]
    for g in range(kvh):
        o_br = [o_cmp[g]]
        for c in (g, kvh + g):
            l = ls[c]
            o_br.append(jnp.where(l > 0.0, acc_ref[c] / jnp.where(l > 0.0, l, 1.0), 0.0))
        for hh in range(NSA_GROUP):
            h = g * NSA_GROUP + hh
            sl = slice(hh * tq, (hh + 1) * tq)
            o = ga[h:h + 1] * o_br[0][:, sl]
            for br in range(1, NSA_BRANCHES):
                o = o + ga[br * NSA_HEADS + h:br * NSA_HEADS + h + 1] * o_br[br][:, sl]
            o_ref[0, h * HEAD_DIM:(h + 1) * HEAD_DIM, :] = o


def _nsa_prompt(qat, kb, vt, pooled, gat, *, batch, seq, tq=256, tk=128):
    assert tq % tk == 0 and seq % tq == 0 and tk % NSA_BLOCK == 0
    nb = seq // NSA_BLOCK
    nq = seq // tq
    return pl.pallas_call(
        functools.partial(_nsa_prompt_kernel, tq=tq, tk=tk, nb=nb),
        grid=(batch, nq),
        in_specs=[pl.BlockSpec((1, NSA_WIDTH, tq), lambda b, i: (b, 0, i)),
                  pl.BlockSpec((1, seq, KV_PAIR), lambda b, i: (b, 0, 0)),
                  pl.BlockSpec((1, seq, KV_PAIR), lambda b, i: (b, 0, 1)),
                  pl.BlockSpec((1, KV_PAIR, seq), lambda b, i: (b, 0, 0)),
                  pl.BlockSpec((1, KV_PAIR, seq), lambda b, i: (b, 1, 0)),
                  pl.BlockSpec((1, nb, 2 * KV_PAIR), lambda b, i: (b, 0, 0)),
                  pl.BlockSpec((1, LANES, tq), lambda b, i: (b, 0, i))],
        out_specs=pl.BlockSpec((1, NSA_WIDTH, tq), lambda b, i: (b, 0, i)),
        out_shape=jax.ShapeDtypeStruct((batch, NSA_WIDTH, seq), F32),
        scratch_shapes=[pltpu.VMEM((NSA_KV_HEADS, nb, 1, tq), F32),
                        pltpu.VMEM((2 * NSA_KV_HEADS, HEAD_DIM, NSA_GROUP * tq), F32)],
        compiler_params=pltpu.CompilerParams(dimension_semantics=("arbitrary",) * 2, vmem_limit_bytes=VMEM_LIMIT),
        name="nsa_prompt",
    )(qat, kb, kb, vt, vt, pooled, gat)


def _softplus(z):
    return jnp.maximum(z, 0.0) + jnp.log(1.0 + jnp.exp2(jnp.abs(z) * -LOG2E))


def _split_bf16(x):
    hi = x.astype(BF16)
    return hi, (x - hi.astype(F32)).astype(BF16)


def _sb_prompt_kernel(q_ref, k_ref, v_ref, o_ref, acc_ref, *, tq, tk):
    i = pl.program_id(1)
    pairs = SB_HEADS // 2
    lanes = 2 * tq
    z0 = jnp.zeros((HEAD_DIM, tq), BF16)
    qpads = []
    for j in range(pairs):
        qt = q_ref[0, j * LANES:(j + 1) * LANES, :]
        qpads.append(jnp.concatenate([jnp.concatenate([qt[:HEAD_DIM], z0], axis=0),
                                      jnp.concatenate([z0, qt[HEAD_DIM:]], axis=0)], axis=1))
    r = lax.broadcasted_iota(jnp.int32, (tk, 2 * tk), 0)
    c = lax.broadcasted_iota(jnp.int32, (tk, 2 * tk), 1) % tk
    upper2 = jnp.where(c > r, 1.0, 0.0).astype(BF16)
    pos = i * tq + lax.broadcasted_iota(jnp.int32, (1, lanes), 1) % tq

    def step(kt, runs, masked, first):
        off = pl.multiple_of(kt * tk, tk)
        kk = k_ref[0, pl.ds(off, tk), :]
        mask = (off + lax.broadcasted_iota(jnp.int32, (tk, 1), 0)) < pos if masked else None
        zs = [_dot(kk[:, j * LANES:(j + 1) * LANES], qpads[j]) for j in range(pairs)]
        sps = [jnp.where(mask, _softplus(z), 0.0) if masked else _softplus(z) for z in zs]
        drops = [_dot(upper2, jnp.concatenate(_split_bf16(sp), axis=0)) + run for sp, run in zip(sps, runs)]
        ws = [jnp.exp(z - sp - drop) for z, sp, drop in zip(zs, sps, drops)]
        if masked:
            ws = [jnp.where(mask, w, 0.0) for w in ws]
        for h in range(SB_HEADS):
            w = ws[h // 2][:, (h % 2) * tq:(h % 2 + 1) * tq].astype(BF16)
            pv = _dot(v_ref[0, h * HEAD_DIM:(h + 1) * HEAD_DIM, pl.ds(off, tk)], w)
            acc_ref[h] = pv if first else acc_ref[h] + pv
        return tuple(run + jnp.sum(sp, axis=0, keepdims=True) for run, sp in zip(runs, sps))

    diag = tq // tk
    runs = (jnp.zeros((1, lanes), F32),) * pairs
    for n in range(diag):
        runs = step((i + 1) * diag - 1 - n, runs, True, n == 0)
    lax.fori_loop(0, i * diag, lambda n, rs: step(i * diag - 1 - n, rs, False, False), runs)
    for h in range(SB_HEADS):
        o_ref[0, h * HEAD_DIM:(h + 1) * HEAD_DIM, :] = acc_ref[h]


def _sb_prompt(qbt, kbb, vbt, *, batch, seq, tq=256, tk=128):
    assert tq % tk == 0 and seq % tq == 0
    return pl.pallas_call(
        functools.partial(_sb_prompt_kernel, tq=tq, tk=tk),
        grid=(batch, seq // tq),
        in_specs=[pl.BlockSpec((1, SB_WIDTH, tq), lambda b, i: (b, 0, i)),
                  pl.BlockSpec((1, seq, SB_WIDTH), lambda b, i: (b, 0, 0)),
                  pl.BlockSpec((1, SB_WIDTH, seq), lambda b, i: (b, 0, 0))],
        out_specs=pl.BlockSpec((1, SB_WIDTH, tq), lambda b, i: (b, 0, i)),
        out_shape=jax.ShapeDtypeStruct((batch, SB_WIDTH, seq), F32),
        scratch_shapes=[pltpu.VMEM((SB_HEADS, HEAD_DIM, tq), F32)],
        compiler_params=pltpu.CompilerParams(dimension_semantics=("arbitrary",) * 2, vmem_limit_bytes=VMEM_LIMIT),
        name="sb_prompt",
    )(qbt, kbb, vbt)


def _mix_prompt_kernel(x_ref, oa_ref, za_ref, ob_ref, zb_ref, gm_ref, wat_ref, wbt_ref, wot_ref, g_ref, y_ref, *,
                       d_model):
    ya = _dot(wat_ref[...], (oa_ref[0] * za_ref[0]).astype(BF16))
    yb = _dot(wbt_ref[...], (ob_ref[0] * zb_ref[0]).astype(BF16))
    gm = gm_ref[0]
    mixed = gm[:d_model] * ya + gm[d_model:] * yb
    out = _dot(wot_ref[...], mixed.astype(BF16))
    out = out * lax.rsqrt(jnp.mean(out * out, axis=0, keepdims=True) + RMS_EPS)
    y_ref[...] = x_ref[...] + out.T * g_ref[...]


def _mix_prompt(x2d, oat, zat, obt, zbt, gmt, wat, wbt, wot, g_post, *, batch, seq, tm):
    m, d_model = x2d.shape
    nt = seq // tm
    row = lambda i: (i, 0)
    full = lambda i: (0, 0)
    tr = lambda i: (i // nt, 0, i % nt)
    return pl.pallas_call(
        functools.partial(_mix_prompt_kernel, d_model=d_model),
        grid=(m // tm,),
        in_specs=[pl.BlockSpec((tm, d_model), row), pl.BlockSpec((1, NSA_WIDTH, tm), tr),
                  pl.BlockSpec((1, NSA_WIDTH, tm), tr), pl.BlockSpec((1, SB_WIDTH, tm), tr),
                  pl.BlockSpec((1, SB_WIDTH, tm), tr), pl.BlockSpec((1, N_MERGE * d_model, tm), tr),
                  pl.BlockSpec((d_model, NSA_WIDTH), full), pl.BlockSpec((d_model, SB_WIDTH), full),
                  pl.BlockSpec((d_model, d_model), full), pl.BlockSpec((1, d_model), full)],
        out_specs=pl.BlockSpec((tm, d_model), row),
        out_shape=jax.ShapeDtypeStruct((m, d_model), F32),
        compiler_params=pltpu.CompilerParams(dimension_semantics=("arbitrary",), vmem_limit_bytes=VMEM_LIMIT),
        name="mix_prompt",
    )(x2d, oat, zat, obt, zbt, gmt, wat, wbt, wot, g_post)


def _mix_sample_kernel(x_ref, oa_ref, za_ref, ob_ref, zb_ref, gm_ref, wat_ref, wbt_ref, wot_ref, g_ref, y_ref, *,
                       d_model):
    ya = _dot_nt((oa_ref[...] * za_ref[...]).astype(BF16), wat_ref[...])
    yb = _dot_nt((ob_ref[...] * zb_ref[...]).astype(BF16), wbt_ref[...])
    gm = gm_ref[...]
    mixed = gm[:, :d_model] * ya + gm[:, d_model:] * yb
    out = _dot_nt(mixed.astype(BF16), wot_ref[...])
    y_ref[...] = x_ref[...] + _rms_scale(out, g_ref[...])


def _mix_sample(x2d, o_a, za, o_b, zb, gm, wat, wbt, wot, g_post):
    m, d_model = x2d.shape
    args = (x2d, o_a, za, o_b, zb, gm, wat, wbt, wot, g_post)
    return pl.pallas_call(
        functools.partial(_mix_sample_kernel, d_model=d_model),
        grid=(1,),
        in_specs=[pl.BlockSpec(a.shape, lambda i: (0, 0)) for a in args],
        out_specs=pl.BlockSpec((m, d_model), lambda i: (0, 0)),
        out_shape=jax.ShapeDtypeStruct((m, d_model), F32),
        compiler_params=pltpu.CompilerParams(dimension_semantics=("arbitrary",), vmem_limit_bytes=VMEM_LIMIT),
        name="mix_sample",
    )(*args)


def _head_pad(q):
    q2 = jnp.concatenate([q, q], axis=1)
    row = lax.broadcasted_iota(jnp.int32, q2.shape, 0)
    lane = lax.broadcasted_iota(jnp.int32, q2.shape, 1)
    return jnp.where(row // NSA_GROUP == lane // HEAD_DIM, q2, 0.0)


def _sb_decode_kernel(pt_ref, q_ref, *refs, ppg):
    page_refs = refs[:ppg]
    o_ref = refs[ppg]
    run_ref, acc_ref = refs[ppg + 1:]
    s = pl.program_id(1)

    @pl.when(s == 0)
    def _():
        run_ref[...] = jnp.zeros_like(run_ref)
        acc_ref[...] = jnp.zeros_like(acc_ref)

    q = q_ref[0]
    qrep = jnp.concatenate([q] * SB_HEADS, axis=1)
    row = lax.broadcasted_iota(jnp.int32, qrep.shape, 0)
    lane = lax.broadcasted_iota(jnp.int32, qrep.shape, 1)
    diag = row == lane // HEAD_DIM
    qbd = jnp.where(diag, qrep, 0.0).astype(BF16)

    z = jnp.concatenate([_dot(qbd, pr[0, :SB_WIDTH, :].astype(BF16)) for pr in page_refs], axis=0)
    sp = _softplus(z)
    r = lax.broadcasted_iota(jnp.int32, (PAGE_SIZE, PAGE_SIZE), 0)
    c = lax.broadcasted_iota(jnp.int32, (PAGE_SIZE, PAGE_SIZE), 1)
    lower = jnp.where(r > c, 1.0, 0.0).astype(BF16)
    hi, lo = _split_bf16(sp)
    drop = _dot(hi, lower) + _dot(lo, lower)
    tot = jnp.sum(sp, axis=1, keepdims=True)
    run = run_ref[...]
    acc = acc_ref[...]
    for n in range(ppg):
        sl = slice(n * SB_HEADS, (n + 1) * SB_HEADS)
        a = jnp.exp(z[sl] - sp[sl] - drop[sl] - run)
        acc = acc + _dot_nt(a.astype(BF16), page_refs[n][0, SB_WIDTH:, :].astype(BF16))
        run = run + tot[sl]
    run_ref[...] = run
    acc_ref[...] = acc

    @pl.when(s == pl.num_programs(1) - 1)
    def _():
        o_ref[0] = jnp.sum(jnp.where(diag, acc, 0.0), axis=0, keepdims=True)


def _sb_decode(page_table_flat, q_b, cache_t, *, dec_batch, n_pages, ppg=8):
    steps = n_pages // ppg

    def page_map(n):
        return lambda b, s, pt: (pt[b * n_pages + n_pages - 1 - (s * ppg + n)], 0, 0)

    grid_spec = pltpu.PrefetchScalarGridSpec(
        num_scalar_prefetch=1, grid=(dec_batch, steps),
        in_specs=[pl.BlockSpec((1, SB_HEADS, HEAD_DIM), lambda b, s, pt: (b, 0, 0))]
        + [pl.BlockSpec((1, 2 * SB_WIDTH, PAGE_SIZE), page_map(n)) for n in range(ppg)],
        out_specs=pl.BlockSpec((1, 1, SB_WIDTH), lambda b, s, pt: (b, 0, 0)),
        scratch_shapes=[pltpu.VMEM((SB_HEADS, 1), F32), pltpu.VMEM((SB_HEADS, SB_WIDTH), F32)])
    return pl.pallas_call(
        functools.partial(_sb_decode_kernel, ppg=ppg),
        grid_spec=grid_spec,
        out_shape=jax.ShapeDtypeStruct((dec_batch, 1, SB_WIDTH), F32),
        compiler_params=pltpu.CompilerParams(dimension_semantics=("arbitrary", "arbitrary"),
                                             vmem_limit_bytes=VMEM_LIMIT),
        name="sb_decode",
    )(page_table_flat, q_b, *([cache_t] * ppg))


def _nsa_cmp_decode_kernel(pt_ref, q_ref, wb_ref, *refs, ppg, nbp):
    page_refs = refs[:ppg]
    ocmp_ref, sel_ref = refs[ppg:ppg + 2]
    pool_ref = refs[ppg + 2]
    s = pl.program_id(1)
    group = LANES // (ppg * BLOCKS_PER_PAGE)
    slot = s % group
    chunk = s // group
    xk = jnp.concatenate([pr[0, :KV_PAIR, :].astype(BF16) for pr in page_refs], axis=1)
    xv = jnp.concatenate([pr[0, KV_PAIR:, :].astype(BF16) for pr in page_refs], axis=1)
    part = jnp.concatenate([_dot(xk, wb_ref[0, slot]), _dot(xv, wb_ref[1, slot])], axis=0)

    @pl.when(slot == 0)
    def _():
        pool_ref[chunk] = part

    @pl.when(slot != 0)
    def _():
        pool_ref[chunk] = pool_ref[chunk] + part

    @pl.when(s == pl.num_programs(1) - 1)
    def _():
        n_chunks = pool_ref.shape[0]
        nl = n_chunks * LANES
        qpad = _head_pad(q_ref[0]).astype(BF16)
        pooled = jnp.concatenate([pool_ref[ch] for ch in range(n_chunks)], axis=1)
        blk_l = lax.broadcasted_iota(jnp.int32, (1, nl), 1)
        live = blk_l < nbp
        sc = jnp.where(live, _dot(qpad, pooled[:KV_PAIR].astype(BF16)), NEG_INF)
        e = jnp.where(live, jnp.exp(sc - jnp.max(sc, axis=1, keepdims=True)), 0.0)
        p = e / jnp.sum(e, axis=1, keepdims=True)
        o = _dot_nt(p.astype(BF16), pooled[KV_PAIR:].astype(BF16))
        hrow = lax.broadcasted_iota(jnp.int32, o.shape, 0)
        ocmp_ref[0] = jnp.where(hrow < NSA_GROUP, o, pltpu.roll(o, HEAD_DIM, 1))[:, :HEAD_DIM]

        blk_s = lax.broadcasted_iota(jnp.int32, (nl, 1), 0)
        eye = lax.broadcasted_iota(jnp.int32, (nl, nl), 0) == lax.broadcasted_iota(jnp.int32, (nl, nl), 1)
        n_sel = NSA_TOPK - 1
        kslot = lax.broadcasted_iota(jnp.int32, (NSA_TOPK, nl), 0)
        for gi in range(NSA_KV_HEADS):
            imp = jnp.sum(p[gi * NSA_GROUP:(gi + 1) * NSA_GROUP], axis=0, keepdims=True)
            forced = (blk_l == 0) | (blk_l == nbp - 1)
            srow = jnp.where(live, jnp.where(forced, FORCE_SCORE, imp), NEG_INF)
            scol = jnp.sum(jnp.where(eye, srow, 0.0), axis=1, keepdims=True)
            beats = (scol > srow) | ((scol == srow) & (blk_s < blk_l))
            rank = jnp.sum(jnp.where(beats, 1.0, 0.0), axis=0, keepdims=True)
            sel = rank < n_sel
            selcol = jnp.sum(jnp.where(eye & sel, 1.0, 0.0), axis=1, keepdims=True) > 0.5
            rank_sel = jnp.sum(jnp.where(selcol & (blk_s < blk_l), 1.0, 0.0), axis=0, keepdims=True)
            onehot = sel & (rank_sel.astype(jnp.int32) == kslot)
            idx = jnp.sum(jnp.where(onehot, blk_l, 0), axis=1, keepdims=True)
            sel_ref[0, gi] = jnp.broadcast_to(idx, (NSA_TOPK, LANES))


def _decode_pool_weights(w_cmp, ppg):
    per_step = ppg * BLOCKS_PER_PAGE
    group = LANES // per_step
    k = jnp.arange(ppg * PAGE_SIZE)
    target = (k // PAGE_SIZE) * BLOCKS_PER_PAGE + (k % PAGE_SIZE) // NSA_BLOCK
    hit = jnp.arange(LANES)[None, None, :] == (jnp.arange(group)[:, None, None] * per_step + target[None, :, None])
    w_row = w_cmp[:, k % NSA_BLOCK]
    return jnp.where(hit[None], w_row[:, None, :, None], 0.0).astype(BF16)


def _nsa_cmp_decode(page_table_flat, q_a, w_cmp, cache_t, *, dec_batch, n_pages, ppg):
    assert LANES % (ppg * BLOCKS_PER_PAGE) == 0
    steps = n_pages // ppg
    nbp = n_pages * BLOCKS_PER_PAGE
    wb = _decode_pool_weights(w_cmp, ppg)

    def page_map(n):
        return lambda b, s, pt: (pt[b * n_pages + s * ppg + n], 0, 0)

    grid_spec = pltpu.PrefetchScalarGridSpec(
        num_scalar_prefetch=1, grid=(dec_batch, steps),
        in_specs=[pl.BlockSpec((1, NSA_HEADS, HEAD_DIM), lambda b, s, pt: (b, 0, 0)),
                  pl.BlockSpec(wb.shape, lambda b, s, pt: (0, 0, 0, 0))]
        + [pl.BlockSpec((1, 2 * KV_PAIR, PAGE_SIZE), page_map(n)) for n in range(ppg)],
        out_specs=[pl.BlockSpec((1, NSA_HEADS, HEAD_DIM), lambda b, s, pt: (b, 0, 0)),
                   pl.BlockSpec((1, NSA_KV_HEADS, NSA_TOPK, LANES), lambda b, s, pt: (b, 0, 0, 0))],
        scratch_shapes=[pltpu.VMEM((pl.cdiv(nbp, LANES), 2 * KV_PAIR, LANES), F32)])
    return pl.pallas_call(
        functools.partial(_nsa_cmp_decode_kernel, ppg=ppg, nbp=nbp),
        grid_spec=grid_spec,
        out_shape=[jax.ShapeDtypeStruct((dec_batch, NSA_HEADS, HEAD_DIM), F32),
                   jax.ShapeDtypeStruct((dec_batch, NSA_KV_HEADS, NSA_TOPK, LANES), jnp.int32)],
        compiler_params=pltpu.CompilerParams(dimension_semantics=("arbitrary", "arbitrary"),
                                             vmem_limit_bytes=VMEM_LIMIT),
        name="nsa_cmp_decode",
    )(page_table_flat, q_a, wb, *([cache_t] * ppg))


def _nsa_sel_decode_kernel(sel_ref, pt_ref, q_ref, ocmp_ref, ga_ref, new_ref, winp_ref, *refs, n_blk):
    blk_refs = refs[:NSA_KV_HEADS * n_blk]
    o_ref, wino_ref = refs[NSA_KV_HEADS * n_blk:]
    b = pl.program_id(0)
    qb = _head_pad(q_ref[0]).astype(BF16)
    new = new_ref[0]
    hrow = lax.broadcasted_iota(jnp.int32, (NSA_HEADS, KV_PAIR), 0)
    top = hrow < NSA_GROUP

    def fold(o):
        return jnp.where(top, o, pltpu.roll(o, HEAD_DIM, 1))[:, :HEAD_DIM]

    def softmax(sc, mask):
        sc = jnp.where(mask, sc, NEG_INF)
        e = jnp.where(mask, jnp.exp(sc - jnp.max(sc, axis=1, keepdims=True)), 0.0)
        return e / jnp.sum(e, axis=1, keepdims=True)

    lane = lax.broadcasted_iota(jnp.int32, (1, PAGE_SIZE), 1)
    first = lax.broadcasted_iota(jnp.int32, (KV_PAIR, PAGE_SIZE), 1) == 0
    k_new = jnp.where(first, new[2 * KV_PAIR:3 * KV_PAIR], 0.0).astype(BF16)
    v_new = jnp.where(first, new[3 * KV_PAIR:4 * KV_PAIR], 0.0).astype(BF16)
    o_g = []
    for gi in range(NSA_KV_HEADS):
        scs, masks = [], []
        for n in range(n_blk):
            half = sel_ref[(b * NSA_KV_HEADS + gi) * NSA_TOPK + n] % BLOCKS_PER_PAGE
            scs.append(_dot(qb, blk_refs[gi * n_blk + n][0, :KV_PAIR, :].astype(BF16)))
            masks.append(lane // NSA_BLOCK == half)
        scs.append(_dot(qb, k_new))
        masks.append(lane == 0)
        p = softmax(jnp.concatenate(scs, axis=1), jnp.concatenate(masks, axis=1)).astype(BF16)
        o = _dot_nt(p[:, n_blk * PAGE_SIZE:], v_new)
        for n in range(n_blk):
            o = o + _dot_nt(p[:, n * PAGE_SIZE:(n + 1) * PAGE_SIZE],
                            blk_refs[gi * n_blk + n][0, KV_PAIR:, :].astype(BF16))
        o_g.append(o)
    o_slc = fold(jnp.where(top, o_g[0], o_g[1]))

    wp = winp_ref[0]
    w = wp.shape[1]
    wl = lax.broadcasted_iota(jnp.int32, wp.shape, 1)
    shifted = jnp.where(wl == w - 1, new[4 * KV_PAIR:], pltpu.roll(wp, w - 1, 1))
    wino_ref[0] = shifted
    sc = _dot(qb, shifted[:KV_PAIR].astype(BF16))
    p = softmax(sc, jnp.full((1, w), True)).astype(BF16)
    o_win = fold(_dot_nt(p, shifted[KV_PAIR:].astype(BF16)))

    ga = ga_ref[0]
    o_ref[0] = ga[0] * ocmp_ref[0] + ga[1] * o_slc + ga[2] * o_win


def _nsa_sel_decode(sel_flat, page_table_flat, q_a, o_cmp, ga, new_kv, win_past_t, cache_t, *, dec_batch, n_pages):
    n_blk = NSA_TOPK - 1
    w = win_past_t.shape[2]

    def blk_map(gi, n):
        def f(b, sel, pt):
            blk = sel[(b * NSA_KV_HEADS + gi) * NSA_TOPK + n]
            return (pt[b * n_pages + blk // BLOCKS_PER_PAGE], 1, 0)
        return f

    grid_spec = pltpu.PrefetchScalarGridSpec(
        num_scalar_prefetch=2, grid=(dec_batch,),
        in_specs=[pl.BlockSpec((1, NSA_HEADS, HEAD_DIM), lambda b, sel, pt: (b, 0, 0)),
                  pl.BlockSpec((1, NSA_HEADS, HEAD_DIM), lambda b, sel, pt: (b, 0, 0)),
                  pl.BlockSpec((1, NSA_BRANCHES, NSA_HEADS, 1), lambda b, sel, pt: (b, 0, 0, 0)),
                  pl.BlockSpec((1, KVA_WIDTH, 1), lambda b, sel, pt: (b, 0, 0)),
                  pl.BlockSpec((1, 2 * KV_PAIR, w), lambda b, sel, pt: (b, 0, 0))]
        + [pl.BlockSpec((1, 2 * KV_PAIR, PAGE_SIZE), blk_map(gi, n))
           for gi in range(NSA_KV_HEADS) for n in range(n_blk)],
        out_specs=[pl.BlockSpec((1, NSA_HEADS, HEAD_DIM), lambda b, sel, pt: (b, 0, 0)),
                   pl.BlockSpec((1, 2 * KV_PAIR, w), lambda b, sel, pt: (b, 0, 0))])
    return pl.pallas_call(
        functools.partial(_nsa_sel_decode_kernel, n_blk=n_blk),
        grid_spec=grid_spec,
        out_shape=[jax.ShapeDtypeStruct((dec_batch, NSA_HEADS, HEAD_DIM), F32),
                   jax.ShapeDtypeStruct((dec_batch, 2 * KV_PAIR, w), F32)],
        compiler_params=pltpu.CompilerParams(dimension_semantics=("arbitrary",), vmem_limit_bytes=VMEM_LIMIT),
        name="nsa_sel_decode",
    )(sel_flat, page_table_flat, q_a, o_cmp, ga, new_kv, win_past_t, *([cache_t] * (NSA_KV_HEADS * n_blk)))


def _feature_major(a, lead):
    nl = len(lead)
    t = jnp.moveaxis(a, nl, -1)
    return t.reshape(lead + (-1, a.shape[nl]))


def _time_major(a_t, feat_shape):
    lead, _, time = a_t.shape
    return jnp.moveaxis(a_t.reshape((lead,) + feat_shape + (time,)), -1, 1)


def _layer(x_prompt, x_sample, nsa_cache, win_cache, sb_cache, page_table, w_in, w_cmp, w_a, w_b, w_o, g_pre, g_post):
    batch, seq, d_model = x_prompt.shape
    dec_batch, dec_seq, _ = x_sample.shape
    n_pages = page_table.shape[1]
    past_len = n_pages * PAGE_SIZE
    n_phys = nsa_cache.shape[0]
    tm, ppg, ppg_cmp = 256, 16, 32
    assert dec_seq == 1 and seq % tm == 0 and seq >= NSA_WINDOW and past_len >= NSA_WINDOW
    assert n_pages % ppg == 0 and n_pages % ppg_cmp == 0 and past_len // NSA_BLOCK >= NSA_TOPK
    assert win_cache.shape[1] == NSA_WINDOW and nsa_cache.shape[1] == PAGE_SIZE

    wt_perm = _permute_w_in_t(w_in, d_model)
    wat, wbt, wot = w_a.T.astype(BF16), w_b.T.astype(BF16), w_o.T.astype(BF16)
    g_pre2, g_post2 = g_pre.reshape(1, d_model), g_post.reshape(1, d_model)
    nsa_feat = (4, NSA_KV_HEADS, HEAD_DIM)
    win_feat = (2, NSA_KV_HEADS, HEAD_DIM)
    sb_feat = (2, SB_HEADS, HEAD_DIM)

    xp = x_prompt.reshape(batch * seq, d_model)
    cos, sin = _rope_angles(jnp.arange(seq, dtype=jnp.int32))
    wp_prompt = _pool_weights(w_cmp, 1, tm // NSA_BLOCK, SUBLANES).reshape(2 * SUBLANES, tm)
    (nsa_t, win_t, sb_t, za_t, zb_t, gm_t, qa_t, ga_t, kb, v_t, pooled, qb_t, kbb, vb_t) = _project_prompt(
        xp, g_pre2, wt_perm, cos.T, sin.T, wp_prompt, batch=batch, seq=seq, tm=tm)
    oa_t = _nsa_prompt(qa_t, kb.reshape(batch, seq, 2 * KV_PAIR), v_t,
                       pooled.reshape(batch, seq // NSA_BLOCK, 2 * KV_PAIR), ga_t, batch=batch, seq=seq)
    ob_t = _sb_prompt(qb_t, kbb.reshape(batch, seq, SB_WIDTH), vb_t, batch=batch, seq=seq)
    y_prompt = _mix_prompt(xp, oa_t, za_t, ob_t, zb_t, gm_t, wat, wbt, wot, g_post2,
                           batch=batch, seq=seq, tm=tm).reshape(batch, seq, d_model)
    nsa_kv_prompt = _time_major(nsa_t, nsa_feat)
    win_kv_prompt = _time_major(win_t[:, :, seq - NSA_WINDOW:], win_feat)
    sb_kv_prompt = _time_major(sb_t, sb_feat)

    xs = x_sample.reshape(dec_batch, d_model)
    tabs_s = _rope_lane_tables(jnp.full((dec_batch,), past_len, jnp.int32))
    (nsa_s, win_s, sb_s, za_s, zb_s, gm_s, qa_s, ga_s, qb_s) = _project_sample(xs, g_pre2, wt_perm, tabs_s)
    pt_flat = page_table.reshape(-1)
    sb_cache_t = _feature_major(sb_cache, (n_phys,))
    nsa_cache_t = _feature_major(nsa_cache, (n_phys,))
    win_cache_t = _feature_major(win_cache, (dec_batch,))
    o_b_s = _sb_decode(pt_flat, qb_s.reshape(dec_batch, SB_HEADS, HEAD_DIM), sb_cache_t,
                       dec_batch=dec_batch, n_pages=n_pages, ppg=ppg)
    qa3 = qa_s.reshape(dec_batch, NSA_HEADS, HEAD_DIM)
    o_cmp_s, sel = _nsa_cmp_decode(pt_flat, qa3, w_cmp, nsa_cache_t, dec_batch=dec_batch, n_pages=n_pages,
                                   ppg=ppg_cmp)
    ga3 = ga_s[:, :GATE_A].reshape(dec_batch, NSA_BRANCHES, NSA_HEADS, 1)
    new_kv = jnp.concatenate([nsa_s, win_s], axis=1).reshape(dec_batch, KVA_WIDTH, 1)
    o_a_s, win_out_t = _nsa_sel_decode(sel[:, :, :, 0].reshape(-1), pt_flat, qa3, o_cmp_s, ga3, new_kv,
                                       win_cache_t, nsa_cache_t, dec_batch=dec_batch, n_pages=n_pages)
    y_sample = _mix_sample(xs, o_a_s.reshape(dec_batch, NSA_WIDTH), za_s, o_b_s.reshape(dec_batch, SB_WIDTH), zb_s,
                           gm_s, wat, wbt, wot, g_post2).reshape(dec_batch, 1, d_model)
    nsa_kv_sample = nsa_s.reshape((dec_batch, 1) + nsa_feat)
    win_kv_sample = _time_major(win_out_t, win_feat)
    sb_kv_sample = sb_s.reshape((dec_batch, 1) + sb_feat)
    return (y_prompt, y_sample, nsa_kv_prompt, win_kv_prompt, sb_kv_prompt, nsa_kv_sample, win_kv_sample,
            sb_kv_sample)


def kernel(x_prompt, x_sample, cache_nsa_kv, cache_nsa_win_kv, cache_sb_kv, page_table, w_in, w_cmp, w_branch_a,
           w_branch_b, w_out, g_pre, g_post):
    hp, hs = x_prompt, x_sample
    caches = [[] for _ in range(6)]
    for layer in range(w_in.shape[0]):
        outs = _layer(hp, hs, cache_nsa_kv[layer], cache_nsa_win_kv[layer], cache_sb_kv[layer], page_table,
                      w_in[layer], w_cmp[layer], w_branch_a[layer], w_branch_b[layer], w_out[layer],
                      g_pre[layer], g_post[layer])
        hp, hs = outs[0], outs[1]
        for acc, o in zip(caches, outs[2:]):
            acc.append(o)
    return (hp, hs) + tuple(jnp.stack(c) for c in caches)
```

```python
import functools

import jax
import jax.numpy as jnp
from jax import lax
from jax.experimental import pallas as pl
from jax.experimental.pallas import tpu as pltpu

HEAD_DIM = 64
ROT_DIM = HEAD_DIM // 4
ROPE_THETA = 500000.0
NSA_HEADS = 8
NSA_KV_HEADS = 2
NSA_GROUP = NSA_HEADS // NSA_KV_HEADS
NSA_BRANCHES = 3
NSA_BLOCK = 64
NSA_TOPK = 16
NSA_WINDOW = 512
NSA_WIDTH = NSA_HEADS * HEAD_DIM
SB_HEADS = 8
SB_WIDTH = SB_HEADS * HEAD_DIM
N_MERGE = 2
PAGE_SIZE = 128
RMS_EPS = 1e-6
NEG_INF = -1e30
FORCE_SCORE = 1e3
SCALE = HEAD_DIM ** -0.5
LOG2E = 1.4426950408889634

LANES = 128
SUBLANES = 8
KV_PAIR = NSA_KV_HEADS * HEAD_DIM
assert KV_PAIR == LANES and PAGE_SIZE == LANES and PAGE_SIZE % NSA_BLOCK == 0
KVA_WIDTH = 2 * NSA_BRANCHES * KV_PAIR
GATE_A = NSA_BRANCHES * NSA_HEADS
BLOCKS_PER_PAGE = PAGE_SIZE // NSA_BLOCK
VMEM_LIMIT = 56 * 1024 * 1024

C_QA = 0
C_KVA = C_QA + NSA_WIDTH
C_ZA = C_KVA + KVA_WIDTH
C_QKVB = C_ZA + NSA_WIDTH
C_ZB = C_QKVB + 3 * SB_WIDTH
C_GM = C_ZB + SB_WIDTH

BF16 = jnp.bfloat16
F32 = jnp.float32


def _dot(a, b):
    return jnp.dot(a, b, preferred_element_type=F32)


def _dot_tn(a, b):
    return lax.dot_general(a, b, (((0,), (0,)), ((), ())), preferred_element_type=F32)


def _dot_nt(a, b):
    return lax.dot_general(a, b, (((1,), (1,)), ((), ())), preferred_element_type=F32)


def _sigmoid(x):
    return 1.0 / (1.0 + jnp.exp(-x))


def _rms_scale(x, g):
    return x * lax.rsqrt(jnp.mean(x * x, axis=-1, keepdims=True) + RMS_EPS) * g


def _rope_rows(v, cos, sin):
    half = ROT_DIM // 2
    parts = []
    for base in range(0, v.shape[0], HEAD_DIM):
        x1, x2 = v[base:base + half], v[base + half:base + 2 * half]
        parts += [x1 * cos - x2 * sin, x2 * cos + x1 * sin, v[base + 2 * half:base + HEAD_DIM]]
    return jnp.concatenate(parts, axis=0)


def _proj_prompt_kernel(x_ref, g_ref, wt_ref, cos_ref, sin_ref, wp_ref,
                        nsa_ref, win_ref, sb_ref, za_ref, zb_ref, gm_ref, qa_ref, ga_ref, kb_ref, vt_ref,
                        pool_ref, qb_ref, kbb_ref, vbt_ref, *, d_model):
    hb = _rms_scale(x_ref[...], g_ref[...]).astype(BF16)
    cos, sin = cos_ref[...], sin_ref[...]

    def seg(lo, width):
        return _dot_nt(wt_ref[lo:lo + width, :], hb)

    qa_ref[0] = (_rope_rows(seg(C_QA, NSA_WIDTH), cos, sin) * SCALE).astype(BF16)
    kv = seg(C_KVA, KVA_WIDTH)
    cmp_k = _rope_rows(kv[0 * LANES:1 * LANES], cos, sin)
    cmp_v = kv[1 * LANES:2 * LANES]
    slc_k = _rope_rows(kv[2 * LANES:3 * LANES], cos, sin)
    slc_v = kv[3 * LANES:4 * LANES]
    win_k = _rope_rows(kv[4 * LANES:5 * LANES], cos, sin)
    win_v = kv[5 * LANES:6 * LANES]
    nsa_ref[0] = jnp.concatenate([cmp_k, cmp_v, slc_k, slc_v], axis=0)
    win_ref[0] = jnp.concatenate([win_k, win_v], axis=0)
    kb_ref[...] = jnp.concatenate([slc_k.T, win_k.T], axis=1).astype(BF16)
    vt_ref[0] = jnp.concatenate([slc_v, win_v], axis=0).astype(BF16)
    wp = wp_ref[...]
    nblk = pool_ref.shape[1]
    pooled = jnp.concatenate([_dot_nt(wp[:SUBLANES], cmp_k.astype(BF16)),
                              _dot_nt(wp[SUBLANES:], cmp_v.astype(BF16))], axis=1)
    pool_ref[0] = pooled[:nblk]

    za = seg(C_ZA, NSA_WIDTH)
    za_ref[0] = (za * _sigmoid(za)).astype(za_ref.dtype)
    qkvb = seg(C_QKVB, 3 * SB_WIDTH)
    qb_ref[0] = (qkvb[:SB_WIDTH] * SCALE).astype(BF16)
    sb_ref[0] = qkvb[SB_WIDTH:]
    kbb_ref[...] = qkvb[SB_WIDTH:2 * SB_WIDTH].T.astype(BF16)
    vbt_ref[0] = qkvb[2 * SB_WIDTH:].astype(BF16)
    zb = seg(C_ZB, SB_WIDTH)
    zb_ref[0] = (zb * _sigmoid(zb)).astype(zb_ref.dtype)
    gm_ref[0] = _sigmoid(seg(C_GM, N_MERGE * d_model)).astype(gm_ref.dtype)
    ga_ref[0] = _sigmoid(seg(C_GM + N_MERGE * d_model, LANES))


def _project_prompt(x2d, g_pre, wt_perm, cos_t, sin_t, wp, *, batch, seq, tm):
    m, d_model = x2d.shape
    nt = seq // tm
    n_rows = wt_perm.shape[0]
    nblk = tm // NSA_BLOCK
    row = lambda i: (i, 0)
    full = lambda i: (0, 0)
    tab = lambda i: (0, i % nt)
    tr = lambda i: (i // nt, 0, i % nt)
    half = ROT_DIM // 2
    in_specs = [pl.BlockSpec((tm, d_model), row), pl.BlockSpec((1, d_model), full),
                pl.BlockSpec((n_rows, d_model), full),
                pl.BlockSpec((half, tm), tab), pl.BlockSpec((half, tm), tab),
                pl.BlockSpec((2 * SUBLANES, tm), full)]
    sds = jax.ShapeDtypeStruct

    def feat(width, dtype):
        return sds((batch, width, seq), dtype), pl.BlockSpec((1, width, tm), tr)

    def rows(width, dtype):
        return sds((m, width), dtype), pl.BlockSpec((tm, width), row)

    outs = [feat(4 * KV_PAIR, F32), feat(2 * KV_PAIR, F32), feat(2 * SB_WIDTH, F32),
            feat(NSA_WIDTH, BF16), feat(SB_WIDTH, BF16), feat(N_MERGE * d_model, BF16),
            feat(NSA_WIDTH, BF16), feat(LANES, F32), rows(2 * KV_PAIR, BF16), feat(2 * KV_PAIR, BF16),
            (sds((m // tm, nblk, 2 * KV_PAIR), F32), pl.BlockSpec((1, nblk, 2 * KV_PAIR), lambda i: (i, 0, 0))),
            feat(SB_WIDTH, BF16), rows(SB_WIDTH, BF16), feat(SB_WIDTH, BF16)]
    return pl.pallas_call(
        functools.partial(_proj_prompt_kernel, d_model=d_model),
        grid=(m // tm,), in_specs=in_specs, out_specs=[o[1] for o in outs], out_shape=[o[0] for o in outs],
        compiler_params=pltpu.CompilerParams(dimension_semantics=("arbitrary",), vmem_limit_bytes=VMEM_LIMIT),
        name="proj_prompt",
    )(x2d, g_pre, wt_perm, cos_t, sin_t, wp)


def _rope_lanes(v, c, s_up, s_dn):
    half = ROT_DIM // 2
    outs = []
    for j in range(v.shape[1] // LANES):
        blk = v[:, j * LANES:(j + 1) * LANES]
        outs.append(blk * c + pltpu.roll(blk, LANES - half, 1) * s_up + pltpu.roll(blk, half, 1) * s_dn)
    return outs[0] if len(outs) == 1 else jnp.concatenate(outs, axis=1)


def _proj_sample_kernel(x_ref, g_ref, wt_ref, cos_ref, sup_ref, sdn_ref,
                        nsa_ref, win_ref, sbkv_ref, za_ref, zb_ref, gm_ref, qa_ref, ga_ref, qb_ref, *, d_model):
    hb = _rms_scale(x_ref[...], g_ref[...]).astype(BF16)
    cos, s_up, s_dn = cos_ref[...], sup_ref[...], sdn_ref[...]

    def seg(lo, width):
        return _dot_nt(hb, wt_ref[lo:lo + width, :])

    qa_ref[...] = _rope_lanes(seg(C_QA, NSA_WIDTH), cos, s_up, s_dn) * SCALE
    kv = seg(C_KVA, KVA_WIDTH)
    parts = []
    for j in range(2 * NSA_BRANCHES):
        blk = kv[:, j * LANES:(j + 1) * LANES]
        parts.append(_rope_lanes(blk, cos, s_up, s_dn) if j % 2 == 0 else blk)
    nsa_ref[...] = jnp.concatenate(parts[:4], axis=1)
    win_ref[...] = jnp.concatenate(parts[4:], axis=1)
    za = seg(C_ZA, NSA_WIDTH)
    za_ref[...] = za * _sigmoid(za)
    qkvb = seg(C_QKVB, 3 * SB_WIDTH)
    qb_ref[...] = qkvb[:, :SB_WIDTH] * SCALE
    sbkv_ref[...] = qkvb[:, SB_WIDTH:]
    zb = seg(C_ZB, SB_WIDTH)
    zb_ref[...] = zb * _sigmoid(zb)
    gm_ref[...] = _sigmoid(seg(C_GM, N_MERGE * d_model))
    ga_ref[...] = _sigmoid(seg(C_GM + N_MERGE * d_model, LANES))


def _project_sample(x2d, g_pre, wt_perm, tables):
    m, d_model = x2d.shape
    full = lambda i: (0, 0)
    widths = [4 * KV_PAIR, 2 * KV_PAIR, 2 * SB_WIDTH, NSA_WIDTH, SB_WIDTH, N_MERGE * d_model,
              NSA_WIDTH, LANES, SB_WIDTH]
    return pl.pallas_call(
        functools.partial(_proj_sample_kernel, d_model=d_model),
        grid=(1,),
        in_specs=[pl.BlockSpec((m, d_model), full), pl.BlockSpec((1, d_model), full),
                  pl.BlockSpec(wt_perm.shape, full)] + [pl.BlockSpec((m, LANES), full)] * 3,
        out_specs=[pl.BlockSpec((m, w), full) for w in widths],
        out_shape=[jax.ShapeDtypeStruct((m, w), F32) for w in widths],
        compiler_params=pltpu.CompilerParams(dimension_semantics=("arbitrary",), vmem_limit_bytes=VMEM_LIMIT),
        name="proj_sample",
    )(x2d, g_pre, wt_perm, *tables)


def _rope_angles(pos):
    half = ROT_DIM // 2
    inv_freq = ROPE_THETA ** (-jnp.arange(half, dtype=F32) / half)
    ang = pos.astype(F32)[:, None] * inv_freq[None, :]
    return jnp.cos(ang), jnp.sin(ang)


def _rope_lane_tables(pos):
    cos, sin = _rope_angles(pos)
    n = pos.shape[0]
    half = ROT_DIM // 2
    ones = jnp.ones((n, HEAD_DIM - ROT_DIM), F32)
    zeros = jnp.zeros((n, HEAD_DIM - ROT_DIM), F32)
    zh = jnp.zeros((n, half), F32)
    rep = LANES // HEAD_DIM
    return tuple(jnp.tile(t, (1, rep)) for t in (jnp.concatenate([cos, cos, ones], axis=1),
                                                 jnp.concatenate([-sin, zh, zeros], axis=1),
                                                 jnp.concatenate([zh, sin, zeros], axis=1)))


def _permute_w_in_t(w_in, d_model):
    c = [NSA_WIDTH, KVA_WIDTH, NSA_WIDTH, GATE_A, 3 * SB_WIDTH, SB_WIDTH, N_MERGE * d_model]
    o = [0]
    for s in c:
        o.append(o[-1] + s)
    wt = w_in.T
    pad = jnp.zeros((LANES - GATE_A, d_model), w_in.dtype)
    return jnp.concatenate([wt[:o[3]], wt[o[4]:], wt[o[3]:o[4]], pad], axis=0).astype(BF16)


def _pool_weights(w_cmp, n_tiles, blocks_per_tile, rows):
    lane = jnp.arange(blocks_per_tile * NSA_BLOCK)
    owner = jnp.arange(n_tiles)[:, None, None] * blocks_per_tile + (lane // NSA_BLOCK)[None, None, :]
    hit = jnp.arange(rows)[None, :, None] == owner
    w_lane = jnp.tile(w_cmp, (1, blocks_per_tile))
    return jnp.where(hit[None], w_lane[:, None, None, :], 0.0).astype(BF16)


def _nsa_prompt_kernel(q_ref, kslc_ref, kwin_ref, vslc_ref, vwin_ref, pool_ref, ga_ref, o_ref, sel_ref, acc_ref, *,
                       tq, tk, nb):
    i = pl.program_id(1)
    lanes = NSA_GROUP * tq
    kvh = NSA_KV_HEADS
    row = lax.broadcasted_iota(jnp.int32, (KV_PAIR, lanes), 0)
    pos1 = i * tq + lax.broadcasted_iota(jnp.int32, (1, tq), 1)
    pos = jnp.concatenate([pos1] * NSA_GROUP, axis=1)
    blk = lax.broadcasted_iota(jnp.int32, (nb, 1), 0)
    cmask = ((blk + 1) * NSA_BLOCK - 1) <= pos
    cur = pos1 // NSA_BLOCK
    valid = blk <= cur
    forced = (blk == 0) | (blk == cur) | (blk == cur - 1)
    n_sel = min(NSA_TOPK, nb)
    pooled = pool_ref[0]
    kc = pooled[:, :KV_PAIR].astype(BF16)
    vc = pooled[:, KV_PAIR:].astype(BF16)

    qpads, o_cmp = [], []
    for g in range(kvh):
        qt = q_ref[0, g * NSA_GROUP * HEAD_DIM:(g + 1) * NSA_GROUP * HEAD_DIM, :]
        q4 = jnp.concatenate([qt[hh * HEAD_DIM:(hh + 1) * HEAD_DIM] for hh in range(NSA_GROUP)], axis=1)
        qpad = jnp.where(row // HEAD_DIM == g, jnp.concatenate([q4, q4], axis=0), jnp.zeros((), BF16))
        qpads.append(qpad)
        s = jnp.where(cmask, _dot(kc, qpad), NEG_INF)
        e = jnp.exp(s - jnp.max(s, axis=0, keepdims=True))
        p = jnp.where(cmask, e / jnp.sum(e, axis=0, keepdims=True), 0.0)
        o_cmp.append(_dot_tn(vc, p.astype(BF16))[g * HEAD_DIM:(g + 1) * HEAD_DIM])
        imp = p[:, 0:tq]
        for hh in range(1, NSA_GROUP):
            imp = imp + p[:, hh * tq:(hh + 1) * tq]
        score = jnp.where(valid, jnp.where(forced, FORCE_SCORE, imp), NEG_INF)
        for j in range(nb):
            sj = score[j:j + 1, :]
            beats = (score > sj) | ((score == sj) & (blk < j))
            cnt = jnp.sum(jnp.where(beats, 1.0, 0.0), axis=0, keepdims=True)
            sel_ref[g, j] = jnp.where(cnt < n_sel, 0.0, NEG_INF)

    bpt = tk // NSA_BLOCK

    def sel_bias(g, kt):
        rows = [jnp.broadcast_to(sel_ref[g, kt * bpt + r], (NSA_BLOCK, tq)) for r in range(bpt)]
        return jnp.concatenate(rows, axis=0)

    def step(kt, carry, near):
        ms, ls = list(carry[0]), list(carry[1])
        off = pl.multiple_of(kt * tk, tk)
        ks = kslc_ref[0, pl.ds(off, tk), :]
        chains = [(g, g, ks, vslc_ref) for g in range(kvh)]
        biases = [sel_bias(g, kt) for g in range(kvh)]
        if near:
            kw = kwin_ref[0, pl.ds(off, tk), :]
            d = pos1 - (off + lax.broadcasted_iota(jnp.int32, (tk, 1), 0))
            causal = jnp.where(d >= 0, 0.0, NEG_INF)
            window = jnp.where((d >= 0) & (d < NSA_WINDOW), 0.0, NEG_INF)
            chains += [(kvh + g, g, kw, vwin_ref) for g in range(kvh)]
            biases = [bs + causal for bs in biases] + [window] * kvh
        scs = [_dot(kk, qpads[g]) + jnp.concatenate([bs] * NSA_GROUP, axis=1)
               for (_, g, kk, _), bs in zip(chains, biases)]
        m_new = [jnp.maximum(ms[c], jnp.max(sc, axis=0, keepdims=True)) for (c, _, _, _), sc in zip(chains, scs)]
        pps = [jnp.exp(sc - jnp.maximum(mn, 0.5 * NEG_INF)) for sc, mn in zip(scs, m_new)]
        pvs = [_dot(v_ref[0, g * HEAD_DIM:(g + 1) * HEAD_DIM, pl.ds(off, tk)], pp.astype(BF16))
               for (_, g, _, v_ref), pp in zip(chains, pps)]
        for (c, _, _, _), mn, pp, pv in zip(chains, m_new, pps, pvs):
            alpha = jnp.exp(ms[c] - mn)
            acc_ref[c] = alpha * acc_ref[c] + pv
            ls[c] = alpha * ls[c] + jnp.sum(pp, axis=0, keepdims=True)
            ms[c] = mn
        return tuple(ms), tuple(ls)

    n_chain = 2 * kvh
    acc_ref[...] = jnp.zeros_like(acc_ref)
    carry = ((jnp.full((1, lanes), NEG_INF, F32),) * n_chain, (jnp.zeros((1, lanes), F32),) * n_chain)
    first = jnp.maximum(i * tq - (NSA_WINDOW - 1), 0) // tk
    carry = lax.fori_loop(0, first, lambda kt, cr: step(kt, cr, False), carry)
    _, ls = lax.fori_loop(first, (i + 1) * (tq // tk), lambda kt, cr: step(kt, cr, True), carry)

    ga = ga_ref[0]
    for g in range(kvh):
        o_br = [o_cmp[g]]
        for c in (g, kvh + g):
            l = ls[c]
            o_br.append(jnp.where(l > 0.0, acc_ref[c] / jnp.where(l > 0.0, l, 1.0), 0.0))
        for hh in range(NSA_GROUP):
            h = g * NSA_GROUP + hh
            sl = slice(hh * tq, (hh + 1) * tq)
            o = ga[h:h + 1] * o_br[0][:, sl]
            for br in range(1, NSA_BRANCHES):
                o = o + ga[br * NSA_HEADS + h:br * NSA_HEADS + h + 1] * o_br[br][:, sl]
            o_ref[0, h * HEAD_DIM:(h + 1) * HEAD_DIM, :] = o


def _nsa_prompt(qat, kb, vt, pooled, gat, *, batch, seq, tq=256, tk=128):
    assert tq % tk == 0 and seq % tq == 0 and tk % NSA_BLOCK == 0
    nb = seq // NSA_BLOCK
    nq = seq // tq
    return pl.pallas_call(
        functools.partial(_nsa_prompt_kernel, tq=tq, tk=tk, nb=nb),
        grid=(batch, nq),
        in_specs=[pl.BlockSpec((1, NSA_WIDTH, tq), lambda b, i: (b, 0, i)),
                  pl.BlockSpec((1, seq, KV_PAIR), lambda b, i: (b, 0, 0)),
                  pl.BlockSpec((1, seq, KV_PAIR), lambda b, i: (b, 0, 1)),
                  pl.BlockSpec((1, KV_PAIR, seq), lambda b, i: (b, 0, 0)),
                  pl.BlockSpec((1, KV_PAIR, seq), lambda b, i: (b, 1, 0)),
                  pl.BlockSpec((1, nb, 2 * KV_PAIR), lambda b, i: (b, 0, 0)),
                  pl.BlockSpec((1, LANES, tq), lambda b, i: (b, 0, i))],
        out_specs=pl.BlockSpec((1, NSA_WIDTH, tq), lambda b, i: (b, 0, i)),
        out_shape=jax.ShapeDtypeStruct((batch, NSA_WIDTH, seq), F32),
        scratch_shapes=[pltpu.VMEM((NSA_KV_HEADS, nb, 1, tq), F32),
                        pltpu.VMEM((2 * NSA_KV_HEADS, HEAD_DIM, NSA_GROUP * tq), F32)],
        compiler_params=pltpu.CompilerParams(dimension_semantics=("arbitrary",) * 2, vmem_limit_bytes=VMEM_LIMIT),
        name="nsa_prompt",
    )(qat, kb, kb, vt, vt, pooled, gat)


def _softplus(z):
    return jnp.maximum(z, 0.0) + jnp.log(1.0 + jnp.exp2(jnp.abs(z) * -LOG2E))


def _split_bf16(x):
    hi = x.astype(BF16)
    return hi, (x - hi.astype(F32)).astype(BF16)


def _mix_prompt_kernel(x_ref, oa_ref, za_ref, ob_ref, zb_ref, gm_ref, wat_ref, wbt_ref, wot_ref, g_ref, y_ref, *,
                       d_model):
    ya = _dot(wat_ref[...], (oa_ref[0] * za_ref[0]).astype(BF16))
    yb = _dot(wbt_ref[...], (ob_ref[0] * zb_ref[0]).astype(BF16))
    gm = gm_ref[0]
    mixed = gm[:d_model] * ya + gm[d_model:] * yb
    out = _dot(wot_ref[...], mixed.astype(BF16))
    out = out * lax.rsqrt(jnp.mean(out * out, axis=0, keepdims=True) + RMS_EPS)
    y_ref[...] = x_ref[...] + out.T * g_ref[...]


def _mix_prompt(x2d, oat, zat, obt, zbt, gmt, wat, wbt, wot, g_post, *, batch, seq, tm):
    m, d_model = x2d.shape
    nt = seq // tm
    row = lambda i: (i, 0)
    full = lambda i: (0, 0)
    tr = lambda i: (i // nt, 0, i % nt)
    return pl.pallas_call(
        functools.partial(_mix_prompt_kernel, d_model=d_model),
        grid=(m // tm,),
        in_specs=[pl.BlockSpec((tm, d_model), row), pl.BlockSpec((1, NSA_WIDTH, tm), tr),
                  pl.BlockSpec((1, NSA_WIDTH, tm), tr), pl.BlockSpec((1, SB_WIDTH, tm), tr),
                  pl.BlockSpec((1, SB_WIDTH, tm), tr), pl.BlockSpec((1, N_MERGE * d_model, tm), tr),
                  pl.BlockSpec((d_model, NSA_WIDTH), full), pl.BlockSpec((d_model, SB_WIDTH), full),
                  pl.BlockSpec((d_model, d_model), full), pl.BlockSpec((1, d_model), full)],
        out_specs=pl.BlockSpec((tm, d_model), row),
        out_shape=jax.ShapeDtypeStruct((m, d_model), F32),
        compiler_params=pltpu.CompilerParams(dimension_semantics=("arbitrary",), vmem_limit_bytes=VMEM_LIMIT),
        name="mix_prompt",
    )(x2d, oat, zat, obt, zbt, gmt, wat, wbt, wot, g_post)


def _mix_sample_kernel(x_ref, oa_ref, za_ref, ob_ref, zb_ref, gm_ref, wat_ref, wbt_ref, wot_ref, g_ref, y_ref, *,
                       d_model):
    ya = _dot_nt((oa_ref[...] * za_ref[...]).astype(BF16), wat_ref[...])
    yb = _dot_nt((ob_ref[...] * zb_ref[...]).astype(BF16), wbt_ref[...])
    gm = gm_ref[...]
    mixed = gm[:, :d_model] * ya + gm[:, d_model:] * yb
    out = _dot_nt(mixed.astype(BF16), wot_ref[...])
    y_ref[...] = x_ref[...] + _rms_scale(out, g_ref[...])


def _mix_sample(x2d, o_a, za, o_b, zb, gm, wat, wbt, wot, g_post):
    m, d_model = x2d.shape
    args = (x2d, o_a, za, o_b, zb, gm, wat, wbt, wot, g_post)
    return pl.pallas_call(
        functools.partial(_mix_sample_kernel, d_model=d_model),
        grid=(1,),
        in_specs=[pl.BlockSpec(a.shape, lambda i: (0, 0)) for a in args],
        out_specs=pl.BlockSpec((m, d_model), lambda i: (0, 0)),
        out_shape=jax.ShapeDtypeStruct((m, d_model), F32),
        compiler_params=pltpu.CompilerParams(dimension_semantics=("arbitrary",), vmem_limit_bytes=VMEM_LIMIT),
        name="mix_sample",
    )(*args)


def _head_pad(q):
    q2 = jnp.concatenate([q, q], axis=1)
    row = lax.broadcasted_iota(jnp.int32, q2.shape, 0)
    lane = lax.broadcasted_iota(jnp.int32, q2.shape, 1)
    return jnp.where(row // NSA_GROUP == lane // HEAD_DIM, q2, 0.0)


def _sb_fused_kernel(pt_ref, qd_ref, qp_ref, k_ref, v_ref, *refs, ppg, tq, tk, units, steps_per_batch):
    page_refs = refs[:ppg]
    od_ref, op_ref = refs[ppg:ppg + 2]
    drun_ref, dacc_ref, runs_ref, acc_ref, state_ref = refs[ppg + 2:]
    s = pl.program_id(1)
    s_lin = pl.program_id(0) * pl.num_programs(1) + s

    @pl.when(s == 0)
    def _():
        drun_ref[...] = jnp.zeros_like(drun_ref)
        dacc_ref[...] = jnp.zeros_like(dacc_ref)

    q = qd_ref[0]
    qrep = jnp.concatenate([q] * SB_HEADS, axis=1)
    hrow = lax.broadcasted_iota(jnp.int32, qrep.shape, 0)
    hlane = lax.broadcasted_iota(jnp.int32, qrep.shape, 1)
    diag_blocks = hrow == hlane // HEAD_DIM
    qbd = jnp.where(diag_blocks, qrep, 0.0).astype(BF16)
    z = jnp.concatenate([_dot(qbd, pr[0, :SB_WIDTH, :].astype(BF16)) for pr in page_refs], axis=0)
    sp = _softplus(z)
    r = lax.broadcasted_iota(jnp.int32, (PAGE_SIZE, PAGE_SIZE), 0)
    c = lax.broadcasted_iota(jnp.int32, (PAGE_SIZE, PAGE_SIZE), 1)
    lower = jnp.where(r > c, 1.0, 0.0).astype(BF16)
    hi, lo = _split_bf16(sp)
    drop = _dot(hi, lower) + _dot(lo, lower)
    tot = jnp.sum(sp, axis=1, keepdims=True)
    run = drun_ref[...]
    dacc = dacc_ref[...]
    for n in range(ppg):
        sl = slice(n * SB_HEADS, (n + 1) * SB_HEADS)
        a = jnp.exp(z[sl] - sp[sl] - drop[sl] - run)
        dacc = dacc + _dot_nt(a.astype(BF16), page_refs[n][0, SB_WIDTH:, :].astype(BF16))
        run = run + tot[sl]
    drun_ref[...] = run
    dacc_ref[...] = dacc

    @pl.when(s == pl.num_programs(1) - 1)
    def _():
        od_ref[0] = jnp.sum(jnp.where(diag_blocks, dacc, 0.0), axis=0, keepdims=True)

    pairs = SB_HEADS // 2
    lanes = 2 * tq
    diag = tq // tk
    per_batch = diag * units * (units + 1) // 2
    local = s_lin % steps_per_batch
    n_steps = ((local + 1) * per_batch) // steps_per_batch - (local * per_batch) // steps_per_batch

    @pl.when(s_lin == 0)
    def _():
        state_ref[0] = 0
        state_ref[1] = 0

    ur = lax.broadcasted_iota(jnp.int32, (tk, 2 * tk), 0)
    uc = lax.broadcasted_iota(jnp.int32, (tk, 2 * tk), 1) % tk
    upper2 = jnp.where(uc > ur, 1.0, 0.0).astype(BF16)
    z0 = jnp.zeros((HEAD_DIM, tq), BF16)

    def tile_step(_, carry):
        i = state_ref[0]
        j = state_ref[1]
        n_t = (i + 1) * diag
        off = pl.multiple_of((n_t - 1 - j) * tk, tk)
        qoff = pl.multiple_of(i * tq, tq)

        @pl.when(j == 0)
        def _():
            runs_ref[...] = jnp.zeros_like(runs_ref)
            acc_ref[...] = jnp.zeros_like(acc_ref)

        def compute(masked):
            kk = k_ref[0, pl.ds(off, tk), :]
            qpads = []
            for jp in range(pairs):
                qt = qp_ref[0, jp * LANES:(jp + 1) * LANES, pl.ds(qoff, tq)]
                qpads.append(jnp.concatenate([jnp.concatenate([qt[:HEAD_DIM], z0], axis=0),
                                              jnp.concatenate([z0, qt[HEAD_DIM:]], axis=0)], axis=1))
            if masked:
                pos = qoff + lax.broadcasted_iota(jnp.int32, (1, lanes), 1) % tq
                mask = (off + lax.broadcasted_iota(jnp.int32, (tk, 1), 0)) < pos
            zs = [_dot(kk[:, jp * LANES:(jp + 1) * LANES], qpads[jp]) for jp in range(pairs)]
            sps = [jnp.where(mask, _softplus(zz), 0.0) if masked else _softplus(zz) for zz in zs]
            drops = [_dot(upper2, jnp.concatenate(_split_bf16(spj), axis=0)) + runs_ref[jp]
                     for jp, spj in enumerate(sps)]
            ws = [jnp.exp(zz - spj - dr) for zz, spj, dr in zip(zs, sps, drops)]
            if masked:
                ws = [jnp.where(mask, w, 0.0) for w in ws]
            for h in range(SB_HEADS):
                w = ws[h // 2][:, (h % 2) * tq:(h % 2 + 1) * tq].astype(BF16)
                acc_ref[h] = acc_ref[h] + _dot(v_ref[0, h * HEAD_DIM:(h + 1) * HEAD_DIM, pl.ds(off, tk)], w)
            for jp, spj in enumerate(sps):
                runs_ref[jp] = runs_ref[jp] + jnp.sum(spj, axis=0, keepdims=True)

        @pl.when(j < diag)
        def _():
            compute(True)

        @pl.when(j >= diag)
        def _():
            compute(False)

        last = j == n_t - 1

        @pl.when(last)
        def _():
            for h in range(SB_HEADS):
                op_ref[0, h * HEAD_DIM:(h + 1) * HEAD_DIM, pl.ds(qoff, tq)] = acc_ref[h]

        state_ref[1] = jnp.where(last, 0, j + 1)
        state_ref[0] = jnp.where(last, (i + 1) % units, i)
        return carry

    lax.fori_loop(0, n_steps, tile_step, 0)


def _sb_attention(page_table_flat, q_dec, cache_t, qbt, kbb, vbt, *, dec_batch, n_pages, batch, seq, ppg,
                  tq=256, tk=128):
    steps = n_pages // ppg
    assert tq % tk == 0 and seq % tq == 0 and (dec_batch * steps) % batch == 0
    spb = dec_batch * steps // batch

    def page_map(n):
        return lambda b, s, pt: (pt[b * n_pages + n_pages - 1 - (s * ppg + n)], 0, 0)

    prow = lambda b, s, pt: ((b * steps + s) // spb, 0, 0)
    grid_spec = pltpu.PrefetchScalarGridSpec(
        num_scalar_prefetch=1, grid=(dec_batch, steps),
        in_specs=[pl.BlockSpec((1, SB_HEADS, HEAD_DIM), lambda b, s, pt: (b, 0, 0)),
                  pl.BlockSpec((1, SB_WIDTH, seq), prow), pl.BlockSpec((1, seq, SB_WIDTH), prow),
                  pl.BlockSpec((1, SB_WIDTH, seq), prow)]
        + [pl.BlockSpec((1, 2 * SB_WIDTH, PAGE_SIZE), page_map(n)) for n in range(ppg)],
        out_specs=[pl.BlockSpec((1, 1, SB_WIDTH), lambda b, s, pt: (b, 0, 0)),
                   pl.BlockSpec((1, SB_WIDTH, seq), prow)],
        scratch_shapes=[pltpu.VMEM((SB_HEADS, 1), F32), pltpu.VMEM((SB_HEADS, SB_WIDTH), F32),
                        pltpu.VMEM((SB_HEADS // 2, 1, 2 * tq), F32), pltpu.VMEM((SB_HEADS, HEAD_DIM, tq), F32),
                        pltpu.SMEM((2,), jnp.int32)])
    return pl.pallas_call(
        functools.partial(_sb_fused_kernel, ppg=ppg, tq=tq, tk=tk, units=seq // tq, steps_per_batch=spb),
        grid_spec=grid_spec,
        out_shape=[jax.ShapeDtypeStruct((dec_batch, 1, SB_WIDTH), F32),
                   jax.ShapeDtypeStruct((batch, SB_WIDTH, seq), F32)],
        compiler_params=pltpu.CompilerParams(dimension_semantics=("arbitrary", "arbitrary"),
                                             vmem_limit_bytes=VMEM_LIMIT),
        name="sb_attention",
    )(page_table_flat, q_dec, qbt, kbb, vbt, *([cache_t] * ppg))


def _nsa_cmp_decode_kernel(pt_ref, q_ref, wb_ref, *refs, ppg, nbp):
    page_refs = refs[:ppg]
    ocmp_ref, sel_ref = refs[ppg:ppg + 2]
    pool_ref = refs[ppg + 2]
    s = pl.program_id(1)
    group = LANES // (ppg * BLOCKS_PER_PAGE)
    slot = s % group
    chunk = s // group
    xk = jnp.concatenate([pr[0, :KV_PAIR, :].astype(BF16) for pr in page_refs], axis=1)
    xv = jnp.concatenate([pr[0, KV_PAIR:, :].astype(BF16) for pr in page_refs], axis=1)
    part = jnp.concatenate([_dot(xk, wb_ref[0, slot]), _dot(xv, wb_ref[1, slot])], axis=0)

    @pl.when(slot == 0)
    def _():
        pool_ref[chunk] = part

    @pl.when(slot != 0)
    def _():
        pool_ref[chunk] = pool_ref[chunk] + part

    @pl.when(s == pl.num_programs(1) - 1)
    def _():
        n_chunks = pool_ref.shape[0]
        nl = n_chunks * LANES
        qpad = _head_pad(q_ref[0]).astype(BF16)
        pooled = jnp.concatenate([pool_ref[ch] for ch in range(n_chunks)], axis=1)
        blk_l = lax.broadcasted_iota(jnp.int32, (1, nl), 1)
        live = blk_l < nbp
        sc = jnp.where(live, _dot(qpad, pooled[:KV_PAIR].astype(BF16)), NEG_INF)
        e = jnp.where(live, jnp.exp(sc - jnp.max(sc, axis=1, keepdims=True)), 0.0)
        p = e / jnp.sum(e, axis=1, keepdims=True)
        o = _dot_nt(p.astype(BF16), pooled[KV_PAIR:].astype(BF16))
        hrow = lax.broadcasted_iota(jnp.int32, o.shape, 0)
        ocmp_ref[0] = jnp.where(hrow < NSA_GROUP, o, pltpu.roll(o, HEAD_DIM, 1))[:, :HEAD_DIM]

        blk_s = lax.broadcasted_iota(jnp.int32, (nl, 1), 0)
        eye = lax.broadcasted_iota(jnp.int32, (nl, nl), 0) == lax.broadcasted_iota(jnp.int32, (nl, nl), 1)
        n_sel = NSA_TOPK - 1
        kslot = lax.broadcasted_iota(jnp.int32, (NSA_TOPK, nl), 0)
        for gi in range(NSA_KV_HEADS):
            imp = jnp.sum(p[gi * NSA_GROUP:(gi + 1) * NSA_GROUP], axis=0, keepdims=True)
            forced = (blk_l == 0) | (blk_l == nbp - 1)
            srow = jnp.where(live, jnp.where(forced, FORCE_SCORE, imp), NEG_INF)
            scol = jnp.sum(jnp.where(eye, srow, 0.0), axis=1, keepdims=True)
            beats = (scol > srow) | ((scol == srow) & (blk_s < blk_l))
            rank = jnp.sum(jnp.where(beats, 1.0, 0.0), axis=0, keepdims=True)
            sel = rank < n_sel
            selcol = jnp.sum(jnp.where(eye & sel, 1.0, 0.0), axis=1, keepdims=True) > 0.5
            rank_sel = jnp.sum(jnp.where(selcol & (blk_s < blk_l), 1.0, 0.0), axis=0, keepdims=True)
            onehot = sel & (rank_sel.astype(jnp.int32) == kslot)
            idx = jnp.sum(jnp.where(onehot, blk_l, 0), axis=1, keepdims=True)
            sel_ref[0, gi] = jnp.broadcast_to(idx, (NSA_TOPK, LANES))


def _decode_pool_weights(w_cmp, ppg):
    per_step = ppg * BLOCKS_PER_PAGE
    group = LANES // per_step
    k = jnp.arange(ppg * PAGE_SIZE)
    target = (k // PAGE_SIZE) * BLOCKS_PER_PAGE + (k % PAGE_SIZE) // NSA_BLOCK
    hit = jnp.arange(LANES)[None, None, :] == (jnp.arange(group)[:, None, None] * per_step + target[None, :, None])
    w_row = w_cmp[:, k % NSA_BLOCK]
    return jnp.where(hit[None], w_row[:, None, :, None], 0.0).astype(BF16)


def _nsa_cmp_decode(page_table_flat, q_a, w_cmp, cache_t, *, dec_batch, n_pages, ppg):
    assert LANES % (ppg * BLOCKS_PER_PAGE) == 0
    steps = n_pages // ppg
    nbp = n_pages * BLOCKS_PER_PAGE
    wb = _decode_pool_weights(w_cmp, ppg)

    def page_map(n):
        return lambda b, s, pt: (pt[b * n_pages + s * ppg + n], 0, 0)

    grid_spec = pltpu.PrefetchScalarGridSpec(
        num_scalar_prefetch=1, grid=(dec_batch, steps),
        in_specs=[pl.BlockSpec((1, NSA_HEADS, HEAD_DIM), lambda b, s, pt: (b, 0, 0)),
                  pl.BlockSpec(wb.shape, lambda b, s, pt: (0, 0, 0, 0))]
        + [pl.BlockSpec((1, 2 * KV_PAIR, PAGE_SIZE), page_map(n)) for n in range(ppg)],
        out_specs=[pl.BlockSpec((1, NSA_HEADS, HEAD_DIM), lambda b, s, pt: (b, 0, 0)),
                   pl.BlockSpec((1, NSA_KV_HEADS, NSA_TOPK, LANES), lambda b, s, pt: (b, 0, 0, 0))],
        scratch_shapes=[pltpu.VMEM((pl.cdiv(nbp, LANES), 2 * KV_PAIR, LANES), F32)])
    return pl.pallas_call(
        functools.partial(_nsa_cmp_decode_kernel, ppg=ppg, nbp=nbp),
        grid_spec=grid_spec,
        out_shape=[jax.ShapeDtypeStruct((dec_batch, NSA_HEADS, HEAD_DIM), F32),
                   jax.ShapeDtypeStruct((dec_batch, NSA_KV_HEADS, NSA_TOPK, LANES), jnp.int32)],
        compiler_params=pltpu.CompilerParams(dimension_semantics=("arbitrary", "arbitrary"),
                                             vmem_limit_bytes=VMEM_LIMIT),
        name="nsa_cmp_decode",
    )(page_table_flat, q_a, wb, *([cache_t] * ppg))


def _nsa_sel_decode_kernel(sel_ref, pt_ref, q_ref, ocmp_ref, ga_ref, new_ref, winp_ref, *refs, n_blk):
    blk_refs = refs[:NSA_KV_HEADS * n_blk]
    o_ref, wino_ref = refs[NSA_KV_HEADS * n_blk:]
    b = pl.program_id(0)
    qb = _head_pad(q_ref[0]).astype(BF16)
    new = new_ref[0]
    hrow = lax.broadcasted_iota(jnp.int32, (NSA_HEADS, KV_PAIR), 0)
    top = hrow < NSA_GROUP

    def fold(o):
        return jnp.where(top, o, pltpu.roll(o, HEAD_DIM, 1))[:, :HEAD_DIM]

    def softmax(sc, mask):
        sc = jnp.where(mask, sc, NEG_INF)
        e = jnp.where(mask, jnp.exp(sc - jnp.max(sc, axis=1, keepdims=True)), 0.0)
        return e / jnp.sum(e, axis=1, keepdims=True)

    lane = lax.broadcasted_iota(jnp.int32, (1, PAGE_SIZE), 1)
    first = lax.broadcasted_iota(jnp.int32, (KV_PAIR, PAGE_SIZE), 1) == 0
    k_new = jnp.where(first, new[2 * KV_PAIR:3 * KV_PAIR], 0.0).astype(BF16)
    v_new = jnp.where(first, new[3 * KV_PAIR:4 * KV_PAIR], 0.0).astype(BF16)
    o_g = []
    for gi in range(NSA_KV_HEADS):
        scs, masks = [], []
        for n in range(n_blk):
            half = sel_ref[(b * NSA_KV_HEADS + gi) * NSA_TOPK + n] % BLOCKS_PER_PAGE
            scs.append(_dot(qb, blk_refs[gi * n_blk + n][0, :KV_PAIR, :].astype(BF16)))
            masks.append(lane // NSA_BLOCK == half)
        scs.append(_dot(qb, k_new))
        masks.append(lane == 0)
        p = softmax(jnp.concatenate(scs, axis=1), jnp.concatenate(masks, axis=1)).astype(BF16)
        o = _dot_nt(p[:, n_blk * PAGE_SIZE:], v_new)
        for n in range(n_blk):
            o = o + _dot_nt(p[:, n * PAGE_SIZE:(n + 1) * PAGE_SIZE],
                            blk_refs[gi * n_blk + n][0, KV_PAIR:, :].astype(BF16))
        o_g.append(o)
    o_slc = fold(jnp.where(top, o_g[0], o_g[1]))

    wp = winp_ref[0]
    w = wp.shape[1]
    wl = lax.broadcasted_iota(jnp.int32, wp.shape, 1)
    shifted = jnp.where(wl == w - 1, new[4 * KV_PAIR:], pltpu.roll(wp, w - 1, 1))
    wino_ref[0] = shifted
    sc = _dot(qb, shifted[:KV_PAIR].astype(BF16))
    p = softmax(sc, jnp.full((1, w), True)).astype(BF16)
    o_win = fold(_dot_nt(p, shifted[KV_PAIR:].astype(BF16)))

    ga = ga_ref[0]
    o_ref[0] = ga[0] * ocmp_ref[0] + ga[1] * o_slc + ga[2] * o_win


def _nsa_sel_decode(sel_flat, page_table_flat, q_a, o_cmp, ga, new_kv, win_past_t, cache_t, *, dec_batch, n_pages):
    n_blk = NSA_TOPK - 1
    w = win_past_t.shape[2]

    def blk_map(gi, n):
        def f(b, sel, pt):
            blk = sel[(b * NSA_KV_HEADS + gi) * NSA_TOPK + n]
            return (pt[b * n_pages + blk // BLOCKS_PER_PAGE], 1, 0)
        return f

    grid_spec = pltpu.PrefetchScalarGridSpec(
        num_scalar_prefetch=2, grid=(dec_batch,),
        in_specs=[pl.BlockSpec((1, NSA_HEADS, HEAD_DIM), lambda b, sel, pt: (b, 0, 0)),
                  pl.BlockSpec((1, NSA_HEADS, HEAD_DIM), lambda b, sel, pt: (b, 0, 0)),
                  pl.BlockSpec((1, NSA_BRANCHES, NSA_HEADS, 1), lambda b, sel, pt: (b, 0, 0, 0)),
                  pl.BlockSpec((1, KVA_WIDTH, 1), lambda b, sel, pt: (b, 0, 0)),
                  pl.BlockSpec((1, 2 * KV_PAIR, w), lambda b, sel, pt: (b, 0, 0))]
        + [pl.BlockSpec((1, 2 * KV_PAIR, PAGE_SIZE), blk_map(gi, n))
           for gi in range(NSA_KV_HEADS) for n in range(n_blk)],
        out_specs=[pl.BlockSpec((1, NSA_HEADS, HEAD_DIM), lambda b, sel, pt: (b, 0, 0)),
                   pl.BlockSpec((1, 2 * KV_PAIR, w), lambda b, sel, pt: (b, 0, 0))])
    return pl.pallas_call(
        functools.partial(_nsa_sel_decode_kernel, n_blk=n_blk),
        grid_spec=grid_spec,
        out_shape=[jax.ShapeDtypeStruct((dec_batch, NSA_HEADS, HEAD_DIM), F32),
                   jax.ShapeDtypeStruct((dec_batch, 2 * KV_PAIR, w), F32)],
        compiler_params=pltpu.CompilerParams(dimension_semantics=("arbitrary",), vmem_limit_bytes=VMEM_LIMIT),
        name="nsa_sel_decode",
    )(sel_flat, page_table_flat, q_a, o_cmp, ga, new_kv, win_past_t, *([cache_t] * (NSA_KV_HEADS * n_blk)))


def _feature_major(a, lead):
    nl = len(lead)
    t = jnp.moveaxis(a, nl, -1)
    return t.reshape(lead + (-1, a.shape[nl]))


def _time_major(a_t, feat_shape):
    lead, _, time = a_t.shape
    return jnp.moveaxis(a_t.reshape((lead,) + feat_shape + (time,)), -1, 1)


def _layer(x_prompt, x_sample, nsa_cache, win_cache, sb_cache, page_table, w_in, w_cmp, w_a, w_b, w_o, g_pre, g_post):
    batch, seq, d_model = x_prompt.shape
    dec_batch, dec_seq, _ = x_sample.shape
    n_pages = page_table.shape[1]
    past_len = n_pages * PAGE_SIZE
    n_phys = nsa_cache.shape[0]
    tm, ppg, ppg_cmp = 256, 16, 32
    assert dec_seq == 1 and seq % tm == 0 and seq >= NSA_WINDOW and past_len >= NSA_WINDOW
    assert n_pages % ppg == 0 and n_pages % ppg_cmp == 0 and past_len // NSA_BLOCK >= NSA_TOPK
    assert win_cache.shape[1] == NSA_WINDOW and nsa_cache.shape[1] == PAGE_SIZE

    wt_perm = _permute_w_in_t(w_in, d_model)
    wat, wbt, wot = w_a.T.astype(BF16), w_b.T.astype(BF16), w_o.T.astype(BF16)
    g_pre2, g_post2 = g_pre.reshape(1, d_model), g_post.reshape(1, d_model)
    nsa_feat = (4, NSA_KV_HEADS, HEAD_DIM)
    win_feat = (2, NSA_KV_HEADS, HEAD_DIM)
    sb_feat = (2, SB_HEADS, HEAD_DIM)

    xp = x_prompt.reshape(batch * seq, d_model)
    cos, sin = _rope_angles(jnp.arange(seq, dtype=jnp.int32))
    wp_prompt = _pool_weights(w_cmp, 1, tm // NSA_BLOCK, SUBLANES).reshape(2 * SUBLANES, tm)
    (nsa_t, win_t, sb_t, za_t, zb_t, gm_t, qa_t, ga_t, kb, v_t, pooled, qb_t, kbb, vb_t) = _project_prompt(
        xp, g_pre2, wt_perm, cos.T, sin.T, wp_prompt, batch=batch, seq=seq, tm=tm)
    xs = x_sample.reshape(dec_batch, d_model)
    tabs_s = _rope_lane_tables(jnp.full((dec_batch,), past_len, jnp.int32))
    (nsa_s, win_s, sb_s, za_s, zb_s, gm_s, qa_s, ga_s, qb_s) = _project_sample(xs, g_pre2, wt_perm, tabs_s)
    pt_flat = page_table.reshape(-1)
    sb_cache_t = _feature_major(sb_cache, (n_phys,))
    nsa_cache_t = _feature_major(nsa_cache, (n_phys,))
    win_cache_t = _feature_major(win_cache, (dec_batch,))

    o_b_s, ob_t = _sb_attention(pt_flat, qb_s.reshape(dec_batch, SB_HEADS, HEAD_DIM), sb_cache_t,
                                qb_t, kbb.reshape(batch, seq, SB_WIDTH), vb_t,
                                dec_batch=dec_batch, n_pages=n_pages, batch=batch, seq=seq, ppg=ppg)

    oa_t = _nsa_prompt(qa_t, kb.reshape(batch, seq, 2 * KV_PAIR), v_t,
                       pooled.reshape(batch, seq // NSA_BLOCK, 2 * KV_PAIR), ga_t, batch=batch, seq=seq)
    y_prompt = _mix_prompt(xp, oa_t, za_t, ob_t, zb_t, gm_t, wat, wbt, wot, g_post2,
                           batch=batch, seq=seq, tm=tm).reshape(batch, seq, d_model)
    nsa_kv_prompt = _time_major(nsa_t, nsa_feat)
    win_kv_prompt = _time_major(win_t[:, :, seq - NSA_WINDOW:], win_feat)
    sb_kv_prompt = _time_major(sb_t, sb_feat)

    qa3 = qa_s.reshape(dec_batch, NSA_HEADS, HEAD_DIM)
    o_cmp_s, sel = _nsa_cmp_decode(pt_flat, qa3, w_cmp, nsa_cache_t, dec_batch=dec_batch, n_pages=n_pages,
                                   ppg=ppg_cmp)
    ga3 = ga_s[:, :GATE_A].reshape(dec_batch, NSA_BRANCHES, NSA_HEADS, 1)
    new_kv = jnp.concatenate([nsa_s, win_s], axis=1).reshape(dec_batch, KVA_WIDTH, 1)
    o_a_s, win_out_t = _nsa_sel_decode(sel[:, :, :, 0].reshape(-1), pt_flat, qa3, o_cmp_s, ga3, new_kv,
                                       win_cache_t, nsa_cache_t, dec_batch=dec_batch, n_pages=n_pages)
    y_sample = _mix_sample(xs, o_a_s.reshape(dec_batch, NSA_WIDTH), za_s, o_b_s.reshape(dec_batch, SB_WIDTH), zb_s,
                           gm_s, wat, wbt, wot, g_post2).reshape(dec_batch, 1, d_model)
    nsa_kv_sample = nsa_s.reshape((dec_batch, 1) + nsa_feat)
    win_kv_sample = _time_major(win_out_t, win_feat)
    sb_kv_sample = sb_s.reshape((dec_batch, 1) + sb_feat)
    return (y_prompt, y_sample, nsa_kv_prompt, win_kv_prompt, sb_kv_prompt, nsa_kv_sample, win_kv_sample,
            sb_kv_sample)


def kernel(x_prompt, x_sample, cache_nsa_kv, cache_nsa_win_kv, cache_sb_kv, page_table, w_in, w_cmp, w_branch_a,
           w_branch_b, w_out, g_pre, g_post):
    hp, hs = x_prompt, x_sample
    caches = [[] for _ in range(6)]
    for layer in range(w_in.shape[0]):
        outs = _layer(hp, hs, cache_nsa_kv[layer], cache_nsa_win_kv[layer], cache_sb_kv[layer], page_table,
                      w_in[layer], w_cmp[layer], w_branch_a[layer], w_branch_b[layer], w_out[layer],
                      g_pre[layer], g_post[layer])
        hp, hs = outs[0], outs[1]
        for acc, o in zip(caches, outs[2:]):
            acc.append(o)
    return (hp, hs) + tuple(jnp.stack(c) for c in caches)
```

```python
import functools

import jax
import jax.numpy as jnp
from jax import lax
from jax.experimental import pallas as pl
from jax.experimental.pallas import tpu as pltpu

HEAD_DIM = 64
ROT_DIM = HEAD_DIM // 4
ROPE_THETA = 500000.0
NSA_HEADS = 8
NSA_KV_HEADS = 2
NSA_GROUP = NSA_HEADS // NSA_KV_HEADS
NSA_BRANCHES = 3
NSA_BLOCK = 64
NSA_TOPK = 16
NSA_WINDOW = 512
NSA_WIDTH = NSA_HEADS * HEAD_DIM
SB_HEADS = 8
SB_WIDTH = SB_HEADS * HEAD_DIM
N_MERGE = 2
PAGE_SIZE = 128
RMS_EPS = 1e-6
NEG_INF = -1e30
FORCE_SCORE = 1e3
SCALE = HEAD_DIM ** -0.5
LOG2E = 1.4426950408889634

LANES = 128
SUBLANES = 8
KV_PAIR = NSA_KV_HEADS * HEAD_DIM
assert KV_PAIR == LANES and PAGE_SIZE == LANES and PAGE_SIZE % NSA_BLOCK == 0
KVA_WIDTH = 2 * NSA_BRANCHES * KV_PAIR
GATE_A = NSA_BRANCHES * NSA_HEADS
BLOCKS_PER_PAGE = PAGE_SIZE // NSA_BLOCK
VMEM_LIMIT = 56 * 1024 * 1024

C_QA = 0
C_KVA = C_QA + NSA_WIDTH
C_ZA = C_KVA + KVA_WIDTH
C_QKVB = C_ZA + NSA_WIDTH
C_ZB = C_QKVB + 3 * SB_WIDTH
C_GM = C_ZB + SB_WIDTH

BF16 = jnp.bfloat16
F32 = jnp.float32


def _dot(a, b):
    return jnp.dot(a, b, preferred_element_type=F32)


def _dot_tn(a, b):
    return lax.dot_general(a, b, (((0,), (0,)), ((), ())), preferred_element_type=F32)


def _dot_nt(a, b):
    return lax.dot_general(a, b, (((1,), (1,)), ((), ())), preferred_element_type=F32)


def _sigmoid(x):
    return 1.0 / (1.0 + jnp.exp(-x))


def _rms_scale(x, g):
    return x * lax.rsqrt(jnp.mean(x * x, axis=-1, keepdims=True) + RMS_EPS) * g


def _rope_rows(v, cos, sin):
    half = ROT_DIM // 2
    parts = []
    for base in range(0, v.shape[0], HEAD_DIM):
        x1, x2 = v[base:base + half], v[base + half:base + 2 * half]
        parts += [x1 * cos - x2 * sin, x2 * cos + x1 * sin, v[base + 2 * half:base + HEAD_DIM]]
    return jnp.concatenate(parts, axis=0)


def _proj_prompt_kernel(x_ref, g_ref, wt_ref, cos_ref, sin_ref, wp_ref,
                        nsa_ref, win_ref, sb_ref, za_ref, zb_ref, gm_ref, qa_ref, ga_ref, kb_ref, vt_ref,
                        pool_ref, qb_ref, kbb_ref, vbt_ref, *, d_model):
    hb = _rms_scale(x_ref[...], g_ref[...]).astype(BF16)
    cos, sin = cos_ref[...], sin_ref[...]

    def seg(lo, width):
        return _dot_nt(wt_ref[lo:lo + width, :], hb)

    qa_ref[0] = (_rope_rows(seg(C_QA, NSA_WIDTH), cos, sin) * SCALE).astype(BF16)
    kv = seg(C_KVA, KVA_WIDTH)
    cmp_k = _rope_rows(kv[0 * LANES:1 * LANES], cos, sin)
    cmp_v = kv[1 * LANES:2 * LANES]
    slc_k = _rope_rows(kv[2 * LANES:3 * LANES], cos, sin)
    slc_v = kv[3 * LANES:4 * LANES]
    win_k = _rope_rows(kv[4 * LANES:5 * LANES], cos, sin)
    win_v = kv[5 * LANES:6 * LANES]
    nsa_ref[0] = jnp.concatenate([cmp_k, cmp_v, slc_k, slc_v], axis=0)
    win_ref[0] = jnp.concatenate([win_k, win_v], axis=0)
    kb_ref[...] = jnp.concatenate([slc_k.T, win_k.T], axis=1).astype(BF16)
    vt_ref[0] = jnp.concatenate([slc_v, win_v], axis=0).astype(BF16)
    wp = wp_ref[...]
    nblk = pool_ref.shape[1]
    pooled = jnp.concatenate([_dot_nt(wp[:SUBLANES], cmp_k.astype(BF16)),
                              _dot_nt(wp[SUBLANES:], cmp_v.astype(BF16))], axis=1)
    pool_ref[0] = pooled[:nblk]

    za = seg(C_ZA, NSA_WIDTH)
    za_ref[0] = (za * _sigmoid(za)).astype(za_ref.dtype)
    qkvb = seg(C_QKVB, 3 * SB_WIDTH)
    qb_ref[0] = (qkvb[:SB_WIDTH] * SCALE).astype(BF16)
    sb_ref[0] = qkvb[SB_WIDTH:]
    kbb_ref[...] = qkvb[SB_WIDTH:2 * SB_WIDTH].T.astype(BF16)
    vbt_ref[0] = qkvb[2 * SB_WIDTH:].astype(BF16)
    zb = seg(C_ZB, SB_WIDTH)
    zb_ref[0] = (zb * _sigmoid(zb)).astype(zb_ref.dtype)
    gm_ref[0] = _sigmoid(seg(C_GM, N_MERGE * d_model)).astype(gm_ref.dtype)
    ga_ref[0] = _sigmoid(seg(C_GM + N_MERGE * d_model, LANES))


def _project_prompt(x2d, g_pre, wt_perm, cos_t, sin_t, wp, *, batch, seq, tm):
    m, d_model = x2d.shape
    nt = seq // tm
    n_rows = wt_perm.shape[0]
    nblk = tm // NSA_BLOCK
    row = lambda i: (i, 0)
    full = lambda i: (0, 0)
    tab = lambda i: (0, i % nt)
    tr = lambda i: (i // nt, 0, i % nt)
    half = ROT_DIM // 2
    in_specs = [pl.BlockSpec((tm, d_model), row), pl.BlockSpec((1, d_model), full),
                pl.BlockSpec((n_rows, d_model), full),
                pl.BlockSpec((half, tm), tab), pl.BlockSpec((half, tm), tab),
                pl.BlockSpec((2 * SUBLANES, tm), full)]
    sds = jax.ShapeDtypeStruct

    def feat(width, dtype):
        return sds((batch, width, seq), dtype), pl.BlockSpec((1, width, tm), tr)

    def rows(width, dtype):
        return sds((m, width), dtype), pl.BlockSpec((tm, width), row)

    outs = [feat(4 * KV_PAIR, F32), feat(2 * KV_PAIR, F32), feat(2 * SB_WIDTH, F32),
            feat(NSA_WIDTH, BF16), feat(SB_WIDTH, BF16), feat(N_MERGE * d_model, BF16),
            feat(NSA_WIDTH, BF16), feat(LANES, F32), rows(2 * KV_PAIR, BF16), feat(2 * KV_PAIR, BF16),
            (sds((m // tm, nblk, 2 * KV_PAIR), F32), pl.BlockSpec((1, nblk, 2 * KV_PAIR), lambda i: (i, 0, 0))),
            feat(SB_WIDTH, BF16), rows(SB_WIDTH, BF16), feat(SB_WIDTH, BF16)]
    return pl.pallas_call(
        functools.partial(_proj_prompt_kernel, d_model=d_model),
        grid=(m // tm,), in_specs=in_specs, out_specs=[o[1] for o in outs], out_shape=[o[0] for o in outs],
        compiler_params=pltpu.CompilerParams(dimension_semantics=("arbitrary",), vmem_limit_bytes=VMEM_LIMIT),
        name="proj_prompt",
    )(x2d, g_pre, wt_perm, cos_t, sin_t, wp)


def _rope_lanes(v, c, s_up, s_dn):
    half = ROT_DIM // 2
    outs = []
    for j in range(v.shape[1] // LANES):
        blk = v[:, j * LANES:(j + 1) * LANES]
        outs.append(blk * c + pltpu.roll(blk, LANES - half, 1) * s_up + pltpu.roll(blk, half, 1) * s_dn)
    return outs[0] if len(outs) == 1 else jnp.concatenate(outs, axis=1)


def _proj_sample_kernel(x_ref, g_ref, wt_ref, cos_ref, sup_ref, sdn_ref,
                        nsa_ref, win_ref, sbkv_ref, za_ref, zb_ref, gm_ref, qa_ref, ga_ref, qb_ref, *, d_model):
    hb = _rms_scale(x_ref[...], g_ref[...]).astype(BF16)
    cos, s_up, s_dn = cos_ref[...], sup_ref[...], sdn_ref[...]

    def seg(lo, width):
        return _dot_nt(hb, wt_ref[lo:lo + width, :])

    qa_ref[...] = _rope_lanes(seg(C_QA, NSA_WIDTH), cos, s_up, s_dn) * SCALE
    kv = seg(C_KVA, KVA_WIDTH)
    parts = []
    for j in range(2 * NSA_BRANCHES):
        blk = kv[:, j * LANES:(j + 1) * LANES]
        parts.append(_rope_lanes(blk, cos, s_up, s_dn) if j % 2 == 0 else blk)
    nsa_ref[...] = jnp.concatenate(parts[:4], axis=1)
    win_ref[...] = jnp.concatenate(parts[4:], axis=1)
    za = seg(C_ZA, NSA_WIDTH)
    za_ref[...] = za * _sigmoid(za)
    qkvb = seg(C_QKVB, 3 * SB_WIDTH)
    qb_ref[...] = qkvb[:, :SB_WIDTH] * SCALE
    sbkv_ref[...] = qkvb[:, SB_WIDTH:]
    zb = seg(C_ZB, SB_WIDTH)
    zb_ref[...] = zb * _sigmoid(zb)
    gm_ref[...] = _sigmoid(seg(C_GM, N_MERGE * d_model))
    ga_ref[...] = _sigmoid(seg(C_GM + N_MERGE * d_model, LANES))


def _project_sample(x2d, g_pre, wt_perm, tables):
    m, d_model = x2d.shape
    full = lambda i: (0, 0)
    widths = [4 * KV_PAIR, 2 * KV_PAIR, 2 * SB_WIDTH, NSA_WIDTH, SB_WIDTH, N_MERGE * d_model,
              NSA_WIDTH, LANES, SB_WIDTH]
    return pl.pallas_call(
        functools.partial(_proj_sample_kernel, d_model=d_model),
        grid=(1,),
        in_specs=[pl.BlockSpec((m, d_model), full), pl.BlockSpec((1, d_model), full),
                  pl.BlockSpec(wt_perm.shape, full)] + [pl.BlockSpec((m, LANES), full)] * 3,
        out_specs=[pl.BlockSpec((m, w), full) for w in widths],
        out_shape=[jax.ShapeDtypeStruct((m, w), F32) for w in widths],
        compiler_params=pltpu.CompilerParams(dimension_semantics=("arbitrary",), vmem_limit_bytes=VMEM_LIMIT),
        name="proj_sample",
    )(x2d, g_pre, wt_perm, *tables)


def _rope_angles(pos):
    half = ROT_DIM // 2
    inv_freq = ROPE_THETA ** (-jnp.arange(half, dtype=F32) / half)
    ang = pos.astype(F32)[:, None] * inv_freq[None, :]
    return jnp.cos(ang), jnp.sin(ang)


def _rope_lane_tables(pos):
    cos, sin = _rope_angles(pos)
    n = pos.shape[0]
    half = ROT_DIM // 2
    ones = jnp.ones((n, HEAD_DIM - ROT_DIM), F32)
    zeros = jnp.zeros((n, HEAD_DIM - ROT_DIM), F32)
    zh = jnp.zeros((n, half), F32)
    rep = LANES // HEAD_DIM
    return tuple(jnp.tile(t, (1, rep)) for t in (jnp.concatenate([cos, cos, ones], axis=1),
                                                 jnp.concatenate([-sin, zh, zeros], axis=1),
                                                 jnp.concatenate([zh, sin, zeros], axis=1)))


def _permute_w_in_t(w_in, d_model):
    c = [NSA_WIDTH, KVA_WIDTH, NSA_WIDTH, GATE_A, 3 * SB_WIDTH, SB_WIDTH, N_MERGE * d_model]
    o = [0]
    for s in c:
        o.append(o[-1] + s)
    wt = w_in.T
    pad = jnp.zeros((LANES - GATE_A, d_model), w_in.dtype)
    return jnp.concatenate([wt[:o[3]], wt[o[4]:], wt[o[3]:o[4]], pad], axis=0).astype(BF16)


def _pool_weights(w_cmp, n_tiles, blocks_per_tile, rows):
    lane = jnp.arange(blocks_per_tile * NSA_BLOCK)
    owner = jnp.arange(n_tiles)[:, None, None] * blocks_per_tile + (lane // NSA_BLOCK)[None, None, :]
    hit = jnp.arange(rows)[None, :, None] == owner
    w_lane = jnp.tile(w_cmp, (1, blocks_per_tile))
    return jnp.where(hit[None], w_lane[:, None, None, :], 0.0).astype(BF16)


def _nsa_prompt_kernel(q_ref, kslc_ref, kwin_ref, vslc_ref, vwin_ref, pool_ref, ga_ref, o_ref, sel_ref, acc_ref, *,
                       tq, tk, nb):
    i = pl.program_id(1)
    lanes = NSA_GROUP * tq
    kvh = NSA_KV_HEADS
    row = lax.broadcasted_iota(jnp.int32, (KV_PAIR, lanes), 0)
    pos1 = i * tq + lax.broadcasted_iota(jnp.int32, (1, tq), 1)
    pos = jnp.concatenate([pos1] * NSA_GROUP, axis=1)
    blk = lax.broadcasted_iota(jnp.int32, (nb, 1), 0)
    cmask = ((blk + 1) * NSA_BLOCK - 1) <= pos
    cur = pos1 // NSA_BLOCK
    valid = blk <= cur
    forced = (blk == 0) | (blk == cur) | (blk == cur - 1)
    n_sel = min(NSA_TOPK, nb)
    pooled = pool_ref[0]
    kc = pooled[:, :KV_PAIR].astype(BF16)
    vc = pooled[:, KV_PAIR:].astype(BF16)

    qpads, o_cmp = [], []
    for g in range(kvh):
        qt = q_ref[0, g * NSA_GROUP * HEAD_DIM:(g + 1) * NSA_GROUP * HEAD_DIM, :]
        q4 = jnp.concatenate([qt[hh * HEAD_DIM:(hh + 1) * HEAD_DIM] for hh in range(NSA_GROUP)], axis=1)
        qpad = jnp.where(row // HEAD_DIM == g, jnp.concatenate([q4, q4], axis=0), jnp.zeros((), BF16))
        qpads.append(qpad)
        s = jnp.where(cmask, _dot(kc, qpad), NEG_INF)
        e = jnp.exp(s - jnp.max(s, axis=0, keepdims=True))
        p = jnp.where(cmask, e / jnp.sum(e, axis=0, keepdims=True), 0.0)
        o_cmp.append(_dot_tn(vc, p.astype(BF16))[g * HEAD_DIM:(g + 1) * HEAD_DIM])
        imp = p[:, 0:tq]
        for hh in range(1, NSA_GROUP):
            imp = imp + p[:, hh * tq:(hh + 1) * tq]
        score = jnp.where(valid, jnp.where(forced, FORCE_SCORE, imp), NEG_INF)
        for j in range(nb):
            sj = score[j:j + 1, :]
            beats = (score > sj) | ((score == sj) & (blk < j))
            cnt = jnp.sum(jnp.where(beats, 1.0, 0.0), axis=0, keepdims=True)
            sel_ref[g, j] = jnp.where(cnt < n_sel, 0.0, NEG_INF)

    bpt = tk // NSA_BLOCK

    def sel_bias(g, kt):
        rows = [jnp.broadcast_to(sel_ref[g, kt * bpt + r], (NSA_BLOCK, tq)) for r in range(bpt)]
        return jnp.concatenate(rows, axis=0)

    def step(kt, carry, near):
        ms, ls = list(carry[0]), list(carry[1])
        off = pl.multiple_of(kt * tk, tk)
        ks = kslc_ref[0, pl.ds(off, tk), :]
        chains = [(g, g, ks, vslc_ref) for g in range(kvh)]
        biases = [sel_bias(g, kt) for g in range(kvh)]
        if near:
            kw = kwin_ref[0, pl.ds(off, tk), :]
            d = pos1 - (off + lax.broadcasted_iota(jnp.int32, (tk, 1), 0))
            causal = jnp.where(d >= 0, 0.0, NEG_INF)
            window = jnp.where((d >= 0) & (d < NSA_WINDOW), 0.0, NEG_INF)
            chains += [(kvh + g, g, kw, vwin_ref) for g in range(kvh)]
            biases = [bs + causal for bs in biases] + [window] * kvh
        scs = [_dot(kk, qpads[g]) + jnp.concatenate([bs] * NSA_GROUP, axis=1)
               for (_, g, kk, _), bs in zip(chains, biases)]
        m_new = [jnp.maximum(ms[c], jnp.max(sc, axis=0, keepdims=True)) for (c, _, _, _), sc in zip(chains, scs)]
        pps = [jnp.exp(sc - jnp.maximum(mn, 0.5 * NEG_INF)) for sc, mn in zip(scs, m_new)]
        pvs = [_dot(v_ref[0, g * HEAD_DIM:(g + 1) * HEAD_DIM, pl.ds(off, tk)], pp.astype(BF16))
               for (_, g, _, v_ref), pp in zip(chains, pps)]
        for (c, _, _, _), mn, pp, pv in zip(chains, m_new, pps, pvs):
            alpha = jnp.exp(ms[c] - mn)
            acc_ref[c] = alpha * acc_ref[c] + pv
            ls[c] = alpha * ls[c] + jnp.sum(pp, axis=0, keepdims=True)
            ms[c] = mn
        return tuple(ms), tuple(ls)

    n_chain = 2 * kvh
    acc_ref[...] = jnp.zeros_like(acc_ref)
    carry = ((jnp.full((1, lanes), NEG_INF, F32),) * n_chain, (jnp.zeros((1, lanes), F32),) * n_chain)
    first = jnp.maximum(i * tq - (NSA_WINDOW - 1), 0) // tk
    carry = lax.fori_loop(0, first, lambda kt, cr: step(kt, cr, False), carry)
    _, ls = lax.fori_loop(first, (i + 1) * (tq // tk), lambda kt, cr: step(kt, cr, True), carry)

    ga = ga_ref[0]
    for g in range(kvh):
        o_br = [o_cmp[g]]
        for c in (g, kvh + g):
            l = ls[c]
            o_br.append(jnp.where(l > 0.0, acc_ref[c] / jnp.where(l > 0.0, l, 1.0), 0.0))
        for hh in range(NSA_GROUP):
            h = g * NSA_GROUP + hh
            sl = slice(hh * tq, (hh + 1) * tq)
            o = ga[h:h + 1] * o_br[0][:, sl]
            for br in range(1, NSA_BRANCHES):
                o = o + ga[br * NSA_HEADS + h:br * NSA_HEADS + h + 1] * o_br[br][:, sl]
            o_ref[0, h * HEAD_DIM:(h + 1) * HEAD_DIM, :] = o


def _nsa_prompt(qat, kb, vt, pooled, gat, *, batch, seq, tq=256, tk=128):
    assert tq % tk == 0 and seq % tq == 0 and tk % NSA_BLOCK == 0
    nb = seq // NSA_BLOCK
    nq = seq // tq
    return pl.pallas_call(
        functools.partial(_nsa_prompt_kernel, tq=tq, tk=tk, nb=nb),
        grid=(batch, nq),
        in_specs=[pl.BlockSpec((1, NSA_WIDTH, tq), lambda b, i: (b, 0, i)),
                  pl.BlockSpec((1, seq, KV_PAIR), lambda b, i: (b, 0, 0)),
                  pl.BlockSpec((1, seq, KV_PAIR), lambda b, i: (b, 0, 1)),
                  pl.BlockSpec((1, KV_PAIR, seq), lambda b, i: (b, 0, 0)),
                  pl.BlockSpec((1, KV_PAIR, seq), lambda b, i: (b, 1, 0)),
                  pl.BlockSpec((1, nb, 2 * KV_PAIR), lambda b, i: (b, 0, 0)),
                  pl.BlockSpec((1, LANES, tq), lambda b, i: (b, 0, i))],
        out_specs=pl.BlockSpec((1, NSA_WIDTH, tq), lambda b, i: (b, 0, i)),
        out_shape=jax.ShapeDtypeStruct((batch, NSA_WIDTH, seq), F32),
        scratch_shapes=[pltpu.VMEM((NSA_KV_HEADS, nb, 1, tq), F32),
                        pltpu.VMEM((2 * NSA_KV_HEADS, HEAD_DIM, NSA_GROUP * tq), F32)],
        compiler_params=pltpu.CompilerParams(dimension_semantics=("arbitrary",) * 2, vmem_limit_bytes=VMEM_LIMIT),
        name="nsa_prompt",
    )(qat, kb, kb, vt, vt, pooled, gat)


def _softplus(z):
    return jnp.maximum(z, 0.0) + jnp.log(1.0 + jnp.exp2(jnp.abs(z) * -LOG2E))


def _split_bf16(x):
    hi = x.astype(BF16)
    return hi, (x - hi.astype(F32)).astype(BF16)


def _mix_prompt_kernel(x_ref, oa_ref, za_ref, ob_ref, zb_ref, gm_ref, wat_ref, wbt_ref, wot_ref, g_ref, y_ref, *,
                       d_model):
    ya = _dot(wat_ref[...], (oa_ref[0] * za_ref[0]).astype(BF16))
    yb = _dot(wbt_ref[...], (ob_ref[0] * zb_ref[0]).astype(BF16))
    gm = gm_ref[0]
    mixed = gm[:d_model] * ya + gm[d_model:] * yb
    out = _dot(wot_ref[...], mixed.astype(BF16))
    out = out * lax.rsqrt(jnp.mean(out * out, axis=0, keepdims=True) + RMS_EPS)
    y_ref[...] = x_ref[...] + out.T * g_ref[...]


def _mix_prompt(x2d, oat, zat, obt, zbt, gmt, wat, wbt, wot, g_post, *, batch, seq, tm):
    m, d_model = x2d.shape
    nt = seq // tm
    row = lambda i: (i, 0)
    full = lambda i: (0, 0)
    tr = lambda i: (i // nt, 0, i % nt)
    return pl.pallas_call(
        functools.partial(_mix_prompt_kernel, d_model=d_model),
        grid=(m // tm,),
        in_specs=[pl.BlockSpec((tm, d_model), row), pl.BlockSpec((1, NSA_WIDTH, tm), tr),
                  pl.BlockSpec((1, NSA_WIDTH, tm), tr), pl.BlockSpec((1, SB_WIDTH, tm), tr),
                  pl.BlockSpec((1, SB_WIDTH, tm), tr), pl.BlockSpec((1, N_MERGE * d_model, tm), tr),
                  pl.BlockSpec((d_model, NSA_WIDTH), full), pl.BlockSpec((d_model, SB_WIDTH), full),
                  pl.BlockSpec((d_model, d_model), full), pl.BlockSpec((1, d_model), full)],
        out_specs=pl.BlockSpec((tm, d_model), row),
        out_shape=jax.ShapeDtypeStruct((m, d_model), F32),
        compiler_params=pltpu.CompilerParams(dimension_semantics=("arbitrary",), vmem_limit_bytes=VMEM_LIMIT),
        name="mix_prompt",
    )(x2d, oat, zat, obt, zbt, gmt, wat, wbt, wot, g_post)


def _mix_sample_kernel(x_ref, oa_ref, za_ref, ob_ref, zb_ref, gm_ref, wat_ref, wbt_ref, wot_ref, g_ref, y_ref, *,
                       d_model):
    ya = _dot_nt((oa_ref[...] * za_ref[...]).astype(BF16), wat_ref[...])
    yb = _dot_nt((ob_ref[...] * zb_ref[...]).astype(BF16), wbt_ref[...])
    gm = gm_ref[...]
    mixed = gm[:, :d_model] * ya + gm[:, d_model:] * yb
    out = _dot_nt(mixed.astype(BF16), wot_ref[...])
    y_ref[...] = x_ref[...] + _rms_scale(out, g_ref[...])


def _mix_sample(x2d, o_a, za, o_b, zb, gm, wat, wbt, wot, g_post):
    m, d_model = x2d.shape
    args = (x2d, o_a, za, o_b, zb, gm, wat, wbt, wot, g_post)
    return pl.pallas_call(
        functools.partial(_mix_sample_kernel, d_model=d_model),
        grid=(1,),
        in_specs=[pl.BlockSpec(a.shape, lambda i: (0, 0)) for a in args],
        out_specs=pl.BlockSpec((m, d_model), lambda i: (0, 0)),
        out_shape=jax.ShapeDtypeStruct((m, d_model), F32),
        compiler_params=pltpu.CompilerParams(dimension_semantics=("arbitrary",), vmem_limit_bytes=VMEM_LIMIT),
        name="mix_sample",
    )(*args)


def _head_pad(q):
    q2 = jnp.concatenate([q, q], axis=1)
    row = lax.broadcasted_iota(jnp.int32, q2.shape, 0)
    lane = lax.broadcasted_iota(jnp.int32, q2.shape, 1)
    return jnp.where(row // NSA_GROUP == lane // HEAD_DIM, q2, 0.0)


def _stream_kernel(pt_ref, qd_ref, qp_ref, k_ref, v_ref, qa_ref, wb_ref, *refs, ppg, tq, tk, units, steps_per_batch,
                   nbp):
    page_refs = refs[:ppg]
    cmp_refs = refs[ppg:2 * ppg]
    od_ref, op_ref, ocmp_ref, sel_ref = refs[2 * ppg:2 * ppg + 4]
    drun_ref, dacc_ref, runs_ref, acc_ref, pool_ref, state_ref = refs[2 * ppg + 4:]
    s = pl.program_id(1)
    s_lin = pl.program_id(0) * pl.num_programs(1) + s

    @pl.when(s == 0)
    def _():
        drun_ref[...] = jnp.zeros_like(drun_ref)
        dacc_ref[...] = jnp.zeros_like(dacc_ref)

    @pl.when(s_lin == 0)
    def _():
        state_ref[0] = 0
        state_ref[1] = 0

    hrow = lax.broadcasted_iota(jnp.int32, (SB_HEADS, SB_WIDTH), 0)
    hlane = lax.broadcasted_iota(jnp.int32, (SB_HEADS, SB_WIDTH), 1)
    diag_blocks = hrow == hlane // HEAD_DIM

    def decode_scores():
        q = qd_ref[0]
        qbd = jnp.where(diag_blocks, jnp.concatenate([q] * SB_HEADS, axis=1), 0.0).astype(BF16)
        z = jnp.concatenate([_dot(qbd, pr[0, :SB_WIDTH, :].astype(BF16)) for pr in page_refs], axis=0)
        sp = _softplus(z)
        r = lax.broadcasted_iota(jnp.int32, (PAGE_SIZE, PAGE_SIZE), 0)
        c = lax.broadcasted_iota(jnp.int32, (PAGE_SIZE, PAGE_SIZE), 1)
        lower = jnp.where(r > c, 1.0, 0.0).astype(BF16)
        hi, lo = _split_bf16(sp)
        return z, sp, _dot(hi, lower) + _dot(lo, lower)

    def decode_update(z, sp, drop):
        tot = jnp.sum(sp, axis=1, keepdims=True)
        run = drun_ref[...]
        dacc = dacc_ref[...]
        for n in range(ppg):
            sl = slice(n * SB_HEADS, (n + 1) * SB_HEADS)
            a = jnp.exp(z[sl] - sp[sl] - drop[sl] - run)
            dacc = dacc + _dot_nt(a.astype(BF16), page_refs[n][0, SB_WIDTH:, :].astype(BF16))
            run = run + tot[sl]
        drun_ref[...] = run
        dacc_ref[...] = dacc

    def pool_compressed():
        group = LANES // (ppg * BLOCKS_PER_PAGE)
        slot = s % group
        chunk = s // group
        xk = jnp.concatenate([pr[0, :KV_PAIR, :].astype(BF16) for pr in cmp_refs], axis=1)
        xv = jnp.concatenate([pr[0, KV_PAIR:, :].astype(BF16) for pr in cmp_refs], axis=1)
        part = jnp.concatenate([_dot(xk, wb_ref[0, slot]), _dot(xv, wb_ref[1, slot])], axis=0)
        pool_ref[chunk] = jnp.where(slot == 0, 0.0, pool_ref[chunk]) + part

    pairs = SB_HEADS // 2
    lanes = 2 * tq
    diag = tq // tk
    per_batch = diag * units * (units + 1) // 2
    local = s_lin % steps_per_batch
    n_steps = ((local + 1) * per_batch) // steps_per_batch - (local * per_batch) // steps_per_batch

    ur = lax.broadcasted_iota(jnp.int32, (tk, 2 * tk), 0)
    uc = lax.broadcasted_iota(jnp.int32, (tk, 2 * tk), 1) % tk
    upper2 = jnp.where(uc > ur, 1.0, 0.0).astype(BF16)
    z0 = jnp.zeros((HEAD_DIM, tq), BF16)

    def tile_step(with_decode):
        i = state_ref[0]
        j = state_ref[1]
        n_t = (i + 1) * diag
        off = pl.multiple_of((n_t - 1 - j) * tk, tk)
        qoff = pl.multiple_of(i * tq, tq)

        @pl.when(j == 0)
        def _():
            runs_ref[...] = jnp.zeros_like(runs_ref)
            acc_ref[...] = jnp.zeros_like(acc_ref)

        def compute(masked):
            if with_decode:
                dz, dsp, ddrop = decode_scores()
                pool_compressed()
            kk = k_ref[0, pl.ds(off, tk), :]
            qpads = []
            for jp in range(pairs):
                qt = qp_ref[0, jp * LANES:(jp + 1) * LANES, pl.ds(qoff, tq)]
                qpads.append(jnp.concatenate([jnp.concatenate([qt[:HEAD_DIM], z0], axis=0),
                                              jnp.concatenate([z0, qt[HEAD_DIM:]], axis=0)], axis=1))
            if masked:
                pos = qoff + lax.broadcasted_iota(jnp.int32, (1, lanes), 1) % tq
                mask = (off + lax.broadcasted_iota(jnp.int32, (tk, 1), 0)) < pos
            zs = [_dot(kk[:, jp * LANES:(jp + 1) * LANES], qpads[jp]) for jp in range(pairs)]
            sps = [jnp.where(mask, _softplus(zz), 0.0) if masked else _softplus(zz) for zz in zs]
            drops = [_dot(upper2, jnp.concatenate(_split_bf16(spj), axis=0)) + runs_ref[jp]
                     for jp, spj in enumerate(sps)]
            if with_decode:
                decode_update(dz, dsp, ddrop)
            ws = [jnp.exp(zz - spj - dr) for zz, spj, dr in zip(zs, sps, drops)]
            if masked:
                ws = [jnp.where(mask, w, 0.0) for w in ws]
            for h in range(SB_HEADS):
                w = ws[h // 2][:, (h % 2) * tq:(h % 2 + 1) * tq].astype(BF16)
                acc_ref[h] = acc_ref[h] + _dot(v_ref[0, h * HEAD_DIM:(h + 1) * HEAD_DIM, pl.ds(off, tk)], w)
            for jp, spj in enumerate(sps):
                runs_ref[jp] = runs_ref[jp] + jnp.sum(spj, axis=0, keepdims=True)

        @pl.when(j < diag)
        def _():
            compute(True)

        @pl.when(j >= diag)
        def _():
            compute(False)

        last = j == n_t - 1

        @pl.when(last)
        def _():
            for h in range(SB_HEADS):
                op_ref[0, h * HEAD_DIM:(h + 1) * HEAD_DIM, pl.ds(qoff, tq)] = acc_ref[h]

        state_ref[1] = jnp.where(last, 0, j + 1)
        state_ref[0] = jnp.where(last, (i + 1) % units, i)

    @pl.when(n_steps > 0)
    def _():
        tile_step(True)

    @pl.when(n_steps == 0)
    def _():
        decode_update(*decode_scores())
        pool_compressed()

    def rest(_, carry):
        tile_step(False)
        return carry

    lax.fori_loop(1, n_steps, rest, 0)

    @pl.when(s == pl.num_programs(1) - 1)
    def _():
        od_ref[0] = jnp.sum(jnp.where(diag_blocks, dacc_ref[...], 0.0), axis=0, keepdims=True)
        _compressed_decode(qa_ref[0], pool_ref, ocmp_ref, sel_ref, nbp)


def _stream_attention(page_table_flat, q_dec, sb_cache_t, qbt, kbb, vbt, qa_dec, w_cmp, nsa_cache_t, *, dec_batch,
                      n_pages, batch, seq, ppg, tq=256, tk=128):
    steps = n_pages // ppg
    assert tq % tk == 0 and seq % tq == 0 and (dec_batch * steps) % batch == 0
    assert LANES % (ppg * BLOCKS_PER_PAGE) == 0
    spb = dec_batch * steps // batch
    nbp = n_pages * BLOCKS_PER_PAGE
    wb = _decode_pool_weights(w_cmp, ppg)

    def page_map(n):
        return lambda b, s, pt: (pt[b * n_pages + n_pages - 1 - (s * ppg + n)], 0, 0)

    def cmp_map(n):
        return lambda b, s, pt: (pt[b * n_pages + s * ppg + n], 0, 0)

    prow = lambda b, s, pt: ((b * steps + s) // spb, 0, 0)
    sample = lambda b, s, pt: (b, 0, 0)
    grid_spec = pltpu.PrefetchScalarGridSpec(
        num_scalar_prefetch=1, grid=(dec_batch, steps),
        in_specs=[pl.BlockSpec((1, SB_HEADS, HEAD_DIM), sample),
                  pl.BlockSpec((1, SB_WIDTH, seq), prow), pl.BlockSpec((1, seq, SB_WIDTH), prow),
                  pl.BlockSpec((1, SB_WIDTH, seq), prow),
                  pl.BlockSpec((1, NSA_HEADS, HEAD_DIM), sample),
                  pl.BlockSpec(wb.shape, lambda b, s, pt: (0, 0, 0, 0))]
        + [pl.BlockSpec((1, 2 * SB_WIDTH, PAGE_SIZE), page_map(n)) for n in range(ppg)]
        + [pl.BlockSpec((1, 2 * KV_PAIR, PAGE_SIZE), cmp_map(n)) for n in range(ppg)],
        out_specs=[pl.BlockSpec((1, 1, SB_WIDTH), sample), pl.BlockSpec((1, SB_WIDTH, seq), prow),
                   pl.BlockSpec((1, NSA_HEADS, HEAD_DIM), sample),
                   pl.BlockSpec((1, NSA_KV_HEADS, NSA_TOPK, LANES), lambda b, s, pt: (b, 0, 0, 0))],
        scratch_shapes=[pltpu.VMEM((SB_HEADS, 1), F32), pltpu.VMEM((SB_HEADS, SB_WIDTH), F32),
                        pltpu.VMEM((SB_HEADS // 2, 1, 2 * tq), F32), pltpu.VMEM((SB_HEADS, HEAD_DIM, tq), F32),
                        pltpu.VMEM((pl.cdiv(nbp, LANES), 2 * KV_PAIR, LANES), F32),
                        pltpu.SMEM((2,), jnp.int32)])
    return pl.pallas_call(
        functools.partial(_stream_kernel, ppg=ppg, tq=tq, tk=tk, units=seq // tq, steps_per_batch=spb, nbp=nbp),
        grid_spec=grid_spec,
        out_shape=[jax.ShapeDtypeStruct((dec_batch, 1, SB_WIDTH), F32),
                   jax.ShapeDtypeStruct((batch, SB_WIDTH, seq), F32),
                   jax.ShapeDtypeStruct((dec_batch, NSA_HEADS, HEAD_DIM), F32),
                   jax.ShapeDtypeStruct((dec_batch, NSA_KV_HEADS, NSA_TOPK, LANES), jnp.int32)],
        compiler_params=pltpu.CompilerParams(dimension_semantics=("arbitrary", "arbitrary"),
                                             vmem_limit_bytes=VMEM_LIMIT),
        name="stream_attention",
    )(page_table_flat, q_dec, qbt, kbb, vbt, qa_dec, wb, *([sb_cache_t] * ppg), *([nsa_cache_t] * ppg))


def _compressed_decode(q, pool_ref, ocmp_ref, sel_ref, nbp):
    n_chunks = pool_ref.shape[0]
    nl = n_chunks * LANES
    qpad = _head_pad(q).astype(BF16)
    pooled = jnp.concatenate([pool_ref[ch] for ch in range(n_chunks)], axis=1)
    blk_l = lax.broadcasted_iota(jnp.int32, (1, nl), 1)
    live = blk_l < nbp
    sc = jnp.where(live, _dot(qpad, pooled[:KV_PAIR].astype(BF16)), NEG_INF)
    e = jnp.where(live, jnp.exp(sc - jnp.max(sc, axis=1, keepdims=True)), 0.0)
    p = e / jnp.sum(e, axis=1, keepdims=True)
    o = _dot_nt(p.astype(BF16), pooled[KV_PAIR:].astype(BF16))
    hrow = lax.broadcasted_iota(jnp.int32, o.shape, 0)
    ocmp_ref[0] = jnp.where(hrow < NSA_GROUP, o, pltpu.roll(o, HEAD_DIM, 1))[:, :HEAD_DIM]

    blk_s = lax.broadcasted_iota(jnp.int32, (nl, 1), 0)
    eye = lax.broadcasted_iota(jnp.int32, (nl, nl), 0) == lax.broadcasted_iota(jnp.int32, (nl, nl), 1)
    n_sel = NSA_TOPK - 1
    kslot = lax.broadcasted_iota(jnp.int32, (NSA_TOPK, nl), 0)
    for gi in range(NSA_KV_HEADS):
        imp = jnp.sum(p[gi * NSA_GROUP:(gi + 1) * NSA_GROUP], axis=0, keepdims=True)
        forced = (blk_l == 0) | (blk_l == nbp - 1)
        srow = jnp.where(live, jnp.where(forced, FORCE_SCORE, imp), NEG_INF)
        scol = jnp.sum(jnp.where(eye, srow, 0.0), axis=1, keepdims=True)
        beats = (scol > srow) | ((scol == srow) & (blk_s < blk_l))
        rank = jnp.sum(jnp.where(beats, 1.0, 0.0), axis=0, keepdims=True)
        sel = rank < n_sel
        selcol = jnp.sum(jnp.where(eye & sel, 1.0, 0.0), axis=1, keepdims=True) > 0.5
        rank_sel = jnp.sum(jnp.where(selcol & (blk_s < blk_l), 1.0, 0.0), axis=0, keepdims=True)
        onehot = sel & (rank_sel.astype(jnp.int32) == kslot)
        idx = jnp.sum(jnp.where(onehot, blk_l, 0), axis=1, keepdims=True)
        sel_ref[0, gi] = jnp.broadcast_to(idx, (NSA_TOPK, LANES))


def _decode_pool_weights(w_cmp, ppg):
    per_step = ppg * BLOCKS_PER_PAGE
    group = LANES // per_step
    k = jnp.arange(ppg * PAGE_SIZE)
    target = (k // PAGE_SIZE) * BLOCKS_PER_PAGE + (k % PAGE_SIZE) // NSA_BLOCK
    hit = jnp.arange(LANES)[None, None, :] == (jnp.arange(group)[:, None, None] * per_step + target[None, :, None])
    w_row = w_cmp[:, k % NSA_BLOCK]
    return jnp.where(hit[None], w_row[:, None, :, None], 0.0).astype(BF16)


def _nsa_sel_decode_kernel(sel_ref, pt_ref, q_ref, ocmp_ref, ga_ref, new_ref, winp_ref, *refs, n_blk):
    blk_refs = refs[:NSA_KV_HEADS * n_blk]
    o_ref, wino_ref = refs[NSA_KV_HEADS * n_blk:]
    b = pl.program_id(0)
    qb = _head_pad(q_ref[0]).astype(BF16)
    new = new_ref[0]
    hrow = lax.broadcasted_iota(jnp.int32, (NSA_HEADS, KV_PAIR), 0)
    top = hrow < NSA_GROUP

    def fold(o):
        return jnp.where(top, o, pltpu.roll(o, HEAD_DIM, 1))[:, :HEAD_DIM]

    def softmax(sc, mask):
        sc = jnp.where(mask, sc, NEG_INF)
        e = jnp.where(mask, jnp.exp(sc - jnp.max(sc, axis=1, keepdims=True)), 0.0)
        return e / jnp.sum(e, axis=1, keepdims=True)

    lane = lax.broadcasted_iota(jnp.int32, (1, PAGE_SIZE), 1)
    first = lax.broadcasted_iota(jnp.int32, (KV_PAIR, PAGE_SIZE), 1) == 0
    k_new = jnp.where(first, new[2 * KV_PAIR:3 * KV_PAIR], 0.0).astype(BF16)
    v_new = jnp.where(first, new[3 * KV_PAIR:4 * KV_PAIR], 0.0).astype(BF16)
    o_g = []
    for gi in range(NSA_KV_HEADS):
        scs, masks = [], []
        for n in range(n_blk):
            half = sel_ref[(b * NSA_KV_HEADS + gi) * NSA_TOPK + n] % BLOCKS_PER_PAGE
            scs.append(_dot(qb, blk_refs[gi * n_blk + n][0, :KV_PAIR, :].astype(BF16)))
            masks.append(lane // NSA_BLOCK == half)
        scs.append(_dot(qb, k_new))
        masks.append(lane == 0)
        p = softmax(jnp.concatenate(scs, axis=1), jnp.concatenate(masks, axis=1)).astype(BF16)
        o = _dot_nt(p[:, n_blk * PAGE_SIZE:], v_new)
        for n in range(n_blk):
            o = o + _dot_nt(p[:, n * PAGE_SIZE:(n + 1) * PAGE_SIZE],
                            blk_refs[gi * n_blk + n][0, KV_PAIR:, :].astype(BF16))
        o_g.append(o)
    o_slc = fold(jnp.where(top, o_g[0], o_g[1]))

    wp = winp_ref[0]
    w = wp.shape[1]
    wl = lax.broadcasted_iota(jnp.int32, wp.shape, 1)
    shifted = jnp.where(wl == w - 1, new[4 * KV_PAIR:], pltpu.roll(wp, w - 1, 1))
    wino_ref[0] = shifted
    sc = _dot(qb, shifted[:KV_PAIR].astype(BF16))
    p = softmax(sc, jnp.full((1, w), True)).astype(BF16)
    o_win = fold(_dot_nt(p, shifted[KV_PAIR:].astype(BF16)))

    ga = ga_ref[0]
    o_ref[0] = ga[0] * ocmp_ref[0] + ga[1] * o_slc + ga[2] * o_win


def _nsa_sel_decode(sel_flat, page_table_flat, q_a, o_cmp, ga, new_kv, win_past_t, cache_t, *, dec_batch, n_pages):
    n_blk = NSA_TOPK - 1
    w = win_past_t.shape[2]

    def blk_map(gi, n):
        def f(b, sel, pt):
            blk = sel[(b * NSA_KV_HEADS + gi) * NSA_TOPK + n]
            return (pt[b * n_pages + blk // BLOCKS_PER_PAGE], 1, 0)
        return f

    grid_spec = pltpu.PrefetchScalarGridSpec(
        num_scalar_prefetch=2, grid=(dec_batch,),
        in_specs=[pl.BlockSpec((1, NSA_HEADS, HEAD_DIM), lambda b, sel, pt: (b, 0, 0)),
                  pl.BlockSpec((1, NSA_HEADS, HEAD_DIM), lambda b, sel, pt: (b, 0, 0)),
                  pl.BlockSpec((1, NSA_BRANCHES, NSA_HEADS, 1), lambda b, sel, pt: (b, 0, 0, 0)),
                  pl.BlockSpec((1, KVA_WIDTH, 1), lambda b, sel, pt: (b, 0, 0)),
                  pl.BlockSpec((1, 2 * KV_PAIR, w), lambda b, sel, pt: (b, 0, 0))]
        + [pl.BlockSpec((1, 2 * KV_PAIR, PAGE_SIZE), blk_map(gi, n))
           for gi in range(NSA_KV_HEADS) for n in range(n_blk)],
        out_specs=[pl.BlockSpec((1, NSA_HEADS, HEAD_DIM), lambda b, sel, pt: (b, 0, 0)),
                   pl.BlockSpec((1, 2 * KV_PAIR, w), lambda b, sel, pt: (b, 0, 0))])
    return pl.pallas_call(
        functools.partial(_nsa_sel_decode_kernel, n_blk=n_blk),
        grid_spec=grid_spec,
        out_shape=[jax.ShapeDtypeStruct((dec_batch, NSA_HEADS, HEAD_DIM), F32),
                   jax.ShapeDtypeStruct((dec_batch, 2 * KV_PAIR, w), F32)],
        compiler_params=pltpu.CompilerParams(dimension_semantics=("arbitrary",), vmem_limit_bytes=VMEM_LIMIT),
        name="nsa_sel_decode",
    )(sel_flat, page_table_flat, q_a, o_cmp, ga, new_kv, win_past_t, *([cache_t] * (NSA_KV_HEADS * n_blk)))


def _feature_major(a, lead):
    nl = len(lead)
    t = jnp.moveaxis(a, nl, -1)
    return t.reshape(lead + (-1, a.shape[nl]))


def _time_major(a_t, feat_shape):
    lead, _, time = a_t.shape
    return jnp.moveaxis(a_t.reshape((lead,) + feat_shape + (time,)), -1, 1)


def _layer(x_prompt, x_sample, nsa_cache, win_cache, sb_cache, page_table, w_in, w_cmp, w_a, w_b, w_o, g_pre, g_post):
    batch, seq, d_model = x_prompt.shape
    dec_batch, dec_seq, _ = x_sample.shape
    n_pages = page_table.shape[1]
    past_len = n_pages * PAGE_SIZE
    n_phys = nsa_cache.shape[0]
    tm, ppg = 256, 16
    assert dec_seq == 1 and seq % tm == 0 and seq >= NSA_WINDOW and past_len >= NSA_WINDOW
    assert n_pages % ppg == 0 and past_len // NSA_BLOCK >= NSA_TOPK
    assert win_cache.shape[1] == NSA_WINDOW and nsa_cache.shape[1] == PAGE_SIZE

    wt_perm = _permute_w_in_t(w_in, d_model)
    wat, wbt, wot = w_a.T.astype(BF16), w_b.T.astype(BF16), w_o.T.astype(BF16)
    g_pre2, g_post2 = g_pre.reshape(1, d_model), g_post.reshape(1, d_model)
    nsa_feat = (4, NSA_KV_HEADS, HEAD_DIM)
    win_feat = (2, NSA_KV_HEADS, HEAD_DIM)
    sb_feat = (2, SB_HEADS, HEAD_DIM)

    xp = x_prompt.reshape(batch * seq, d_model)
    cos, sin = _rope_angles(jnp.arange(seq, dtype=jnp.int32))
    wp_prompt = _pool_weights(w_cmp, 1, tm // NSA_BLOCK, SUBLANES).reshape(2 * SUBLANES, tm)
    (nsa_t, win_t, sb_t, za_t, zb_t, gm_t, qa_t, ga_t, kb, v_t, pooled, qb_t, kbb, vb_t) = _project_prompt(
        xp, g_pre2, wt_perm, cos.T, sin.T, wp_prompt, batch=batch, seq=seq, tm=tm)
    xs = x_sample.reshape(dec_batch, d_model)
    tabs_s = _rope_lane_tables(jnp.full((dec_batch,), past_len, jnp.int32))
    (nsa_s, win_s, sb_s, za_s, zb_s, gm_s, qa_s, ga_s, qb_s) = _project_sample(xs, g_pre2, wt_perm, tabs_s)
    pt_flat = page_table.reshape(-1)
    sb_cache_t = _feature_major(sb_cache, (n_phys,))
    nsa_cache_t = _feature_major(nsa_cache, (n_phys,))
    win_cache_t = _feature_major(win_cache, (dec_batch,))

    qa3 = qa_s.reshape(dec_batch, NSA_HEADS, HEAD_DIM)
    o_b_s, ob_t, o_cmp_s, sel = _stream_attention(
        pt_flat, qb_s.reshape(dec_batch, SB_HEADS, HEAD_DIM), sb_cache_t, qb_t, kbb.reshape(batch, seq, SB_WIDTH), vb_t,
        qa3, w_cmp, nsa_cache_t, dec_batch=dec_batch, n_pages=n_pages, batch=batch, seq=seq, ppg=ppg)

    oa_t = _nsa_prompt(qa_t, kb.reshape(batch, seq, 2 * KV_PAIR), v_t,
                       pooled.reshape(batch, seq // NSA_BLOCK, 2 * KV_PAIR), ga_t, batch=batch, seq=seq)
    y_prompt = _mix_prompt(xp, oa_t, za_t, ob_t, zb_t, gm_t, wat, wbt, wot, g_post2,
                           batch=batch, seq=seq, tm=tm).reshape(batch, seq, d_model)
    nsa_kv_prompt = _time_major(nsa_t, nsa_feat)
    win_kv_prompt = _time_major(win_t[:, :, seq - NSA_WINDOW:], win_feat)
    sb_kv_prompt = _time_major(sb_t, sb_feat)

    ga3 = ga_s[:, :GATE_A].reshape(dec_batch, NSA_BRANCHES, NSA_HEADS, 1)
    new_kv = jnp.concatenate([nsa_s, win_s], axis=1).reshape(dec_batch, KVA_WIDTH, 1)
    o_a_s, win_out_t = _nsa_sel_decode(sel[:, :, :, 0].reshape(-1), pt_flat, qa3, o_cmp_s, ga3, new_kv,
                                       win_cache_t, nsa_cache_t, dec_batch=dec_batch, n_pages=n_pages)
    y_sample = _mix_sample(xs, o_a_s.reshape(dec_batch, NSA_WIDTH), za_s, o_b_s.reshape(dec_batch, SB_WIDTH), zb_s,
                           gm_s, wat, wbt, wot, g_post2).reshape(dec_batch, 1, d_model)
    nsa_kv_sample = nsa_s.reshape((dec_batch, 1) + nsa_feat)
    win_kv_sample = _time_major(win_out_t, win_feat)
    sb_kv_sample = sb_s.reshape((dec_batch, 1) + sb_feat)
    return (y_prompt, y_sample, nsa_kv_prompt, win_kv_prompt, sb_kv_prompt, nsa_kv_sample, win_kv_sample,
            sb_kv_sample)


def kernel(x_prompt, x_sample, cache_nsa_kv, cache_nsa_win_kv, cache_sb_kv, page_table, w_in, w_cmp, w_branch_a,
           w_branch_b, w_out, g_pre, g_post):
    hp, hs = x_prompt, x_sample
    caches = [[] for _ in range(6)]
    for layer in range(w_in.shape[0]):
        outs = _layer(hp, hs, cache_nsa_kv[layer], cache_nsa_win_kv[layer], cache_sb_kv[layer], page_table,
                      w_in[layer], w_cmp[layer], w_branch_a[layer], w_branch_b[layer], w_out[layer],
                      g_pre[layer], g_post[layer])
        hp, hs = outs[0], outs[1]
        for acc, o in zip(caches, outs[2:]):
            acc.append(o)
    return (hp, hs) + tuple(jnp.stack(c) for c in caches)
```

```python
import functools

import jax
import jax.numpy as jnp
from jax import lax
from jax.experimental import pallas as pl
from jax.experimental.pallas import tpu as pltpu

HEAD_DIM = 64
ROT_DIM = HEAD_DIM // 4
ROPE_THETA = 500000.0
NSA_HEADS = 8
NSA_KV_HEADS = 2
NSA_GROUP = NSA_HEADS // NSA_KV_HEADS
NSA_BRANCHES = 3
NSA_BLOCK = 64
NSA_TOPK = 16
NSA_WINDOW = 512
NSA_WIDTH = NSA_HEADS * HEAD_DIM
SB_HEADS = 8
SB_WIDTH = SB_HEADS * HEAD_DIM
N_MERGE = 2
PAGE_SIZE = 128
RMS_EPS = 1e-6
NEG_INF = -1e30
FORCE_SCORE = 1e3
SCALE = HEAD_DIM ** -0.5
LOG2E = 1.4426950408889634

LANES = 128
SUBLANES = 8
KV_PAIR = NSA_KV_HEADS * HEAD_DIM
assert KV_PAIR == LANES and PAGE_SIZE == LANES and PAGE_SIZE % NSA_BLOCK == 0
KVA_WIDTH = 2 * NSA_BRANCHES * KV_PAIR
GATE_A = NSA_BRANCHES * NSA_HEADS
BLOCKS_PER_PAGE = PAGE_SIZE // NSA_BLOCK
VMEM_LIMIT = 56 * 1024 * 1024

C_QA = 0
C_KVA = C_QA + NSA_WIDTH
C_ZA = C_KVA + KVA_WIDTH
C_QKVB = C_ZA + NSA_WIDTH
C_ZB = C_QKVB + 3 * SB_WIDTH
C_GM = C_ZB + SB_WIDTH

BF16 = jnp.bfloat16
F32 = jnp.float32


def _dot(a, b):
    return jnp.dot(a, b, preferred_element_type=F32)


def _dot_tn(a, b):
    return lax.dot_general(a, b, (((0,), (0,)), ((), ())), preferred_element_type=F32)


def _dot_nt(a, b):
    return lax.dot_general(a, b, (((1,), (1,)), ((), ())), preferred_element_type=F32)


def _sigmoid(x):
    return 1.0 / (1.0 + jnp.exp(-x))


def _rms_scale(x, g):
    return x * lax.rsqrt(jnp.mean(x * x, axis=-1, keepdims=True) + RMS_EPS) * g


def _rope_rows(v, cos, sin):
    half = ROT_DIM // 2
    parts = []
    for base in range(0, v.shape[0], HEAD_DIM):
        x1, x2 = v[base:base + half], v[base + half:base + 2 * half]
        parts += [x1 * cos - x2 * sin, x2 * cos + x1 * sin, v[base + 2 * half:base + HEAD_DIM]]
    return jnp.concatenate(parts, axis=0)


def _proj_prompt_kernel(x_ref, g_ref, wt_ref, cos_ref, sin_ref, wp_ref,
                        nsa_ref, win_ref, sb_ref, za_ref, zb_ref, gm_ref, qa_ref, ga_ref, kb_ref, vt_ref,
                        pool_ref, qb_ref, kbb_ref, vbt_ref, *, d_model):
    hb = _rms_scale(x_ref[...], g_ref[...]).astype(BF16)
    cos, sin = cos_ref[...], sin_ref[...]

    def seg(lo, width):
        return _dot_nt(wt_ref[lo:lo + width, :], hb)

    qa_ref[0] = (_rope_rows(seg(C_QA, NSA_WIDTH), cos, sin) * (SCALE * LOG2E)).astype(BF16)
    kv = seg(C_KVA, KVA_WIDTH)
    cmp_k = _rope_rows(kv[0 * LANES:1 * LANES], cos, sin)
    cmp_v = kv[1 * LANES:2 * LANES]
    slc_k = _rope_rows(kv[2 * LANES:3 * LANES], cos, sin)
    slc_v = kv[3 * LANES:4 * LANES]
    win_k = _rope_rows(kv[4 * LANES:5 * LANES], cos, sin)
    win_v = kv[5 * LANES:6 * LANES]
    nsa_ref[0] = jnp.concatenate([cmp_k, cmp_v, slc_k, slc_v], axis=0)
    win_ref[0] = jnp.concatenate([win_k, win_v], axis=0)
    kb_ref[...] = jnp.concatenate([slc_k.T, win_k.T], axis=1).astype(BF16)
    vt_ref[0] = jnp.concatenate([slc_v, win_v], axis=0).astype(BF16)
    wp = wp_ref[...]
    nblk = pool_ref.shape[1]
    pooled = jnp.concatenate([_dot_nt(wp[:SUBLANES], cmp_k.astype(BF16)),
                              _dot_nt(wp[SUBLANES:], cmp_v.astype(BF16))], axis=1)
    pool_ref[0] = pooled[:nblk]

    za = seg(C_ZA, NSA_WIDTH)
    za_ref[0] = (za * _sigmoid(za)).astype(za_ref.dtype)
    qkvb = seg(C_QKVB, 3 * SB_WIDTH)
    qb_ref[0] = (qkvb[:SB_WIDTH] * SCALE).astype(BF16)
    sb_ref[0] = qkvb[SB_WIDTH:]
    kbb_ref[...] = qkvb[SB_WIDTH:2 * SB_WIDTH].T.astype(BF16)
    vbt_ref[0] = qkvb[2 * SB_WIDTH:].astype(BF16)
    zb = seg(C_ZB, SB_WIDTH)
    zb_ref[0] = (zb * _sigmoid(zb)).astype(zb_ref.dtype)
    gm_ref[0] = _sigmoid(seg(C_GM, N_MERGE * d_model)).astype(gm_ref.dtype)
    ga_ref[0] = _sigmoid(seg(C_GM + N_MERGE * d_model, LANES))


def _project_prompt(x2d, g_pre, wt_perm, cos_t, sin_t, wp, *, batch, seq, tm):
    m, d_model = x2d.shape
    nt = seq // tm
    n_rows = wt_perm.shape[0]
    nblk = tm // NSA_BLOCK
    row = lambda i: (i, 0)
    full = lambda i: (0, 0)
    tab = lambda i: (0, i % nt)
    tr = lambda i: (i // nt, 0, i % nt)
    half = ROT_DIM // 2
    in_specs = [pl.BlockSpec((tm, d_model), row), pl.BlockSpec((1, d_model), full),
                pl.BlockSpec((n_rows, d_model), full),
                pl.BlockSpec((half, tm), tab), pl.BlockSpec((half, tm), tab),
                pl.BlockSpec((2 * SUBLANES, tm), full)]
    sds = jax.ShapeDtypeStruct

    def feat(width, dtype):
        return sds((batch, width, seq), dtype), pl.BlockSpec((1, width, tm), tr)

    def rows(width, dtype):
        return sds((m, width), dtype), pl.BlockSpec((tm, width), row)

    outs = [feat(4 * KV_PAIR, F32), feat(2 * KV_PAIR, F32), feat(2 * SB_WIDTH, F32),
            feat(NSA_WIDTH, BF16), feat(SB_WIDTH, BF16), feat(N_MERGE * d_model, BF16),
            feat(NSA_WIDTH, BF16), feat(LANES, F32), rows(2 * KV_PAIR, BF16), feat(2 * KV_PAIR, BF16),
            (sds((m // tm, nblk, 2 * KV_PAIR), F32), pl.BlockSpec((1, nblk, 2 * KV_PAIR), lambda i: (i, 0, 0))),
            feat(SB_WIDTH, BF16), rows(SB_WIDTH, BF16), feat(SB_WIDTH, BF16)]
    return pl.pallas_call(
        functools.partial(_proj_prompt_kernel, d_model=d_model),
        grid=(m // tm,), in_specs=in_specs, out_specs=[o[1] for o in outs], out_shape=[o[0] for o in outs],
        compiler_params=pltpu.CompilerParams(dimension_semantics=("arbitrary",), vmem_limit_bytes=VMEM_LIMIT),
        name="proj_prompt",
    )(x2d, g_pre, wt_perm, cos_t, sin_t, wp)


def _rope_lanes(v, c, s_up, s_dn):
    half = ROT_DIM // 2
    outs = []
    for j in range(v.shape[1] // LANES):
        blk = v[:, j * LANES:(j + 1) * LANES]
        outs.append(blk * c + pltpu.roll(blk, LANES - half, 1) * s_up + pltpu.roll(blk, half, 1) * s_dn)
    return outs[0] if len(outs) == 1 else jnp.concatenate(outs, axis=1)


def _proj_sample_kernel(x_ref, g_ref, wt_ref, cos_ref, sup_ref, sdn_ref,
                        nsa_ref, win_ref, sbkv_ref, za_ref, zb_ref, gm_ref, qa_ref, ga_ref, qb_ref, *, d_model):
    hb = _rms_scale(x_ref[...], g_ref[...]).astype(BF16)
    cos, s_up, s_dn = cos_ref[...], sup_ref[...], sdn_ref[...]

    def seg(lo, width):
        return _dot_nt(hb, wt_ref[lo:lo + width, :])

    qa_ref[...] = _rope_lanes(seg(C_QA, NSA_WIDTH), cos, s_up, s_dn) * SCALE
    kv = seg(C_KVA, KVA_WIDTH)
    parts = []
    for j in range(2 * NSA_BRANCHES):
        blk = kv[:, j * LANES:(j + 1) * LANES]
        parts.append(_rope_lanes(blk, cos, s_up, s_dn) if j % 2 == 0 else blk)
    nsa_ref[...] = jnp.concatenate(parts[:4], axis=1)
    win_ref[...] = jnp.concatenate(parts[4:], axis=1)
    za = seg(C_ZA, NSA_WIDTH)
    za_ref[...] = za * _sigmoid(za)
    qkvb = seg(C_QKVB, 3 * SB_WIDTH)
    qb_ref[...] = qkvb[:, :SB_WIDTH] * SCALE
    sbkv_ref[...] = qkvb[:, SB_WIDTH:]
    zb = seg(C_ZB, SB_WIDTH)
    zb_ref[...] = zb * _sigmoid(zb)
    gm_ref[...] = _sigmoid(seg(C_GM, N_MERGE * d_model))
    ga_ref[...] = _sigmoid(seg(C_GM + N_MERGE * d_model, LANES))


def _project_sample(x2d, g_pre, wt_perm, tables):
    m, d_model = x2d.shape
    full = lambda i: (0, 0)
    widths = [4 * KV_PAIR, 2 * KV_PAIR, 2 * SB_WIDTH, NSA_WIDTH, SB_WIDTH, N_MERGE * d_model,
              NSA_WIDTH, LANES, SB_WIDTH]
    return pl.pallas_call(
        functools.partial(_proj_sample_kernel, d_model=d_model),
        grid=(1,),
        in_specs=[pl.BlockSpec((m, d_model), full), pl.BlockSpec((1, d_model), full),
                  pl.BlockSpec(wt_perm.shape, full)] + [pl.BlockSpec((m, LANES), full)] * 3,
        out_specs=[pl.BlockSpec((m, w), full) for w in widths],
        out_shape=[jax.ShapeDtypeStruct((m, w), F32) for w in widths],
        compiler_params=pltpu.CompilerParams(dimension_semantics=("arbitrary",), vmem_limit_bytes=VMEM_LIMIT),
        name="proj_sample",
    )(x2d, g_pre, wt_perm, *tables)


def _rope_angles(pos):
    half = ROT_DIM // 2
    inv_freq = ROPE_THETA ** (-jnp.arange(half, dtype=F32) / half)
    ang = pos.astype(F32)[:, None] * inv_freq[None, :]
    return jnp.cos(ang), jnp.sin(ang)


def _rope_lane_tables(pos):
    cos, sin = _rope_angles(pos)
    n = pos.shape[0]
    half = ROT_DIM // 2
    ones = jnp.ones((n, HEAD_DIM - ROT_DIM), F32)
    zeros = jnp.zeros((n, HEAD_DIM - ROT_DIM), F32)
    zh = jnp.zeros((n, half), F32)
    rep = LANES // HEAD_DIM
    return tuple(jnp.tile(t, (1, rep)) for t in (jnp.concatenate([cos, cos, ones], axis=1),
                                                 jnp.concatenate([-sin, zh, zeros], axis=1),
                                                 jnp.concatenate([zh, sin, zeros], axis=1)))


def _permute_w_in_t(w_in, d_model):
    c = [NSA_WIDTH, KVA_WIDTH, NSA_WIDTH, GATE_A, 3 * SB_WIDTH, SB_WIDTH, N_MERGE * d_model]
    o = [0]
    for s in c:
        o.append(o[-1] + s)
    wt = w_in.T
    pad = jnp.zeros((LANES - GATE_A, d_model), w_in.dtype)
    return jnp.concatenate([wt[:o[3]], wt[o[4]:], wt[o[3]:o[4]], pad], axis=0).astype(BF16)


def _pool_weights(w_cmp, n_tiles, blocks_per_tile, rows):
    lane = jnp.arange(blocks_per_tile * NSA_BLOCK)
    owner = jnp.arange(n_tiles)[:, None, None] * blocks_per_tile + (lane // NSA_BLOCK)[None, None, :]
    hit = jnp.arange(rows)[None, :, None] == owner
    w_lane = jnp.tile(w_cmp, (1, blocks_per_tile))
    return jnp.where(hit[None], w_lane[:, None, None, :], 0.0).astype(BF16)


def _nsa_prompt_kernel(q_ref, kslc_ref, kwin_ref, vslc_ref, vwin_ref, pool_ref, ga_ref, o_ref, sel_ref, acc_ref, *,
                       tq, tk, nb):
    i = pl.program_id(1)
    lanes = NSA_GROUP * tq
    kvh = NSA_KV_HEADS
    row = lax.broadcasted_iota(jnp.int32, (KV_PAIR, lanes), 0)
    pos1 = i * tq + lax.broadcasted_iota(jnp.int32, (1, tq), 1)
    pos = jnp.concatenate([pos1] * NSA_GROUP, axis=1)
    blk = lax.broadcasted_iota(jnp.int32, (nb, 1), 0)
    cmask = ((blk + 1) * NSA_BLOCK - 1) <= pos
    cur = pos1 // NSA_BLOCK
    valid = blk <= cur
    forced = (blk == 0) | (blk == cur) | (blk == cur - 1)
    n_sel = min(NSA_TOPK, nb)
    pooled = pool_ref[0]
    kc = pooled[:, :KV_PAIR].astype(BF16)
    vc = pooled[:, KV_PAIR:].astype(BF16)

    qpads, o_cmp = [], []
    for g in range(kvh):
        qt = q_ref[0, g * NSA_GROUP * HEAD_DIM:(g + 1) * NSA_GROUP * HEAD_DIM, :]
        q4 = jnp.concatenate([qt[hh * HEAD_DIM:(hh + 1) * HEAD_DIM] for hh in range(NSA_GROUP)], axis=1)
        qpad = jnp.where(row // HEAD_DIM == g, jnp.concatenate([q4, q4], axis=0), jnp.zeros((), BF16))
        qpads.append(qpad)
        s = jnp.where(cmask, _dot(kc, qpad), NEG_INF)
        e = jnp.exp2(s - jnp.max(s, axis=0, keepdims=True))
        p = jnp.where(cmask, e / jnp.sum(e, axis=0, keepdims=True), 0.0)
        o_cmp.append(_dot_tn(vc, p.astype(BF16))[g * HEAD_DIM:(g + 1) * HEAD_DIM])
        imp = p[:, 0:tq]
        for hh in range(1, NSA_GROUP):
            imp = imp + p[:, hh * tq:(hh + 1) * tq]
        score = jnp.where(valid, jnp.where(forced, FORCE_SCORE, imp), NEG_INF)
        for j in range(nb):
            sj = score[j:j + 1, :]
            beats = (score > sj) | ((score == sj) & (blk < j))
            cnt = jnp.sum(jnp.where(beats, 1.0, 0.0), axis=0, keepdims=True)
            sel_ref[g, j] = jnp.where(cnt < n_sel, 0.0, NEG_INF)

    bpt = tk // NSA_BLOCK

    def sel_bias(g, kt):
        rows = [jnp.broadcast_to(sel_ref[g, kt * bpt + r], (NSA_BLOCK, tq)) for r in range(bpt)]
        return jnp.concatenate(rows, axis=0)

    def step(kt, carry, near):
        ms, ls = list(carry[0]), list(carry[1])
        off = pl.multiple_of(kt * tk, tk)
        ks = kslc_ref[0, pl.ds(off, tk), :]
        chains = [(g, g, ks, vslc_ref) for g in range(kvh)]
        biases = [sel_bias(g, kt) for g in range(kvh)]
        if near:
            kw = kwin_ref[0, pl.ds(off, tk), :]
            d = pos1 - (off + lax.broadcasted_iota(jnp.int32, (tk, 1), 0))
            causal = jnp.where(d >= 0, 0.0, NEG_INF)
            window = jnp.where((d >= 0) & (d < NSA_WINDOW), 0.0, NEG_INF)
            chains += [(kvh + g, g, kw, vwin_ref) for g in range(kvh)]
            biases = [bs + causal for bs in biases] + [window] * kvh
        scs = [_dot(kk, qpads[g]) + jnp.concatenate([bs] * NSA_GROUP, axis=1)
               for (_, g, kk, _), bs in zip(chains, biases)]
        m_new = [jnp.maximum(ms[c], jnp.max(sc, axis=0, keepdims=True)) for (c, _, _, _), sc in zip(chains, scs)]
        pps = [jnp.exp2(sc - jnp.maximum(mn, 0.5 * NEG_INF)) for sc, mn in zip(scs, m_new)]
        pvs = [_dot(v_ref[0, g * HEAD_DIM:(g + 1) * HEAD_DIM, pl.ds(off, tk)], pp.astype(BF16))
               for (_, g, _, v_ref), pp in zip(chains, pps)]
        for (c, _, _, _), mn, pp, pv in zip(chains, m_new, pps, pvs):
            alpha = jnp.exp2(ms[c] - mn)
            acc_ref[c] = alpha * acc_ref[c] + pv
            ls[c] = alpha * ls[c] + jnp.sum(pp, axis=0, keepdims=True)
            ms[c] = mn
        return tuple(ms), tuple(ls)

    n_chain = 2 * kvh
    acc_ref[...] = jnp.zeros_like(acc_ref)
    carry = ((jnp.full((1, lanes), NEG_INF, F32),) * n_chain, (jnp.zeros((1, lanes), F32),) * n_chain)
    first = jnp.maximum(i * tq - (NSA_WINDOW - 1), 0) // tk
    carry = lax.fori_loop(0, first, lambda kt, cr: step(kt, cr, False), carry)
    _, ls = lax.fori_loop(first, (i + 1) * (tq // tk), lambda kt, cr: step(kt, cr, True), carry)

    ga = ga_ref[0]
    for g in range(kvh):
        o_br = [o_cmp[g]]
        for c in (g, kvh + g):
            l = ls[c]
            o_br.append(jnp.where(l > 0.0, acc_ref[c] / jnp.where(l > 0.0, l, 1.0), 0.0))
        for hh in range(NSA_GROUP):
            h = g * NSA_GROUP + hh
            sl = slice(hh * tq, (hh + 1) * tq)
            o = ga[h:h + 1] * o_br[0][:, sl]
            for br in range(1, NSA_BRANCHES):
                o = o + ga[br * NSA_HEADS + h:br * NSA_HEADS + h + 1] * o_br[br][:, sl]
            o_ref[0, h * HEAD_DIM:(h + 1) * HEAD_DIM, :] = o


def _nsa_prompt(qat, kb, vt, pooled, gat, *, batch, seq, tq=256, tk=128):
    assert tq % tk == 0 and seq % tq == 0 and tk % NSA_BLOCK == 0
    nb = seq // NSA_BLOCK
    nq = seq // tq
    return pl.pallas_call(
        functools.partial(_nsa_prompt_kernel, tq=tq, tk=tk, nb=nb),
        grid=(batch, nq),
        in_specs=[pl.BlockSpec((1, NSA_WIDTH, tq), lambda b, i: (b, 0, i)),
                  pl.BlockSpec((1, seq, KV_PAIR), lambda b, i: (b, 0, 0)),
                  pl.BlockSpec((1, seq, KV_PAIR), lambda b, i: (b, 0, 1)),
                  pl.BlockSpec((1, KV_PAIR, seq), lambda b, i: (b, 0, 0)),
                  pl.BlockSpec((1, KV_PAIR, seq), lambda b, i: (b, 1, 0)),
                  pl.BlockSpec((1, nb, 2 * KV_PAIR), lambda b, i: (b, 0, 0)),
                  pl.BlockSpec((1, LANES, tq), lambda b, i: (b, 0, i))],
        out_specs=pl.BlockSpec((1, NSA_WIDTH, tq), lambda b, i: (b, 0, i)),
        out_shape=jax.ShapeDtypeStruct((batch, NSA_WIDTH, seq), F32),
        scratch_shapes=[pltpu.VMEM((NSA_KV_HEADS, nb, 1, tq), F32),
                        pltpu.VMEM((2 * NSA_KV_HEADS, HEAD_DIM, NSA_GROUP * tq), F32)],
        compiler_params=pltpu.CompilerParams(dimension_semantics=("arbitrary",) * 2, vmem_limit_bytes=VMEM_LIMIT),
        name="nsa_prompt",
    )(qat, kb, kb, vt, vt, pooled, gat)


def _softplus(z):
    return jnp.maximum(z, 0.0) + jnp.log(1.0 + jnp.exp2(jnp.abs(z) * -LOG2E))


def _split_bf16(x):
    hi = x.astype(BF16)
    return hi, (x - hi.astype(F32)).astype(BF16)


def _mix_prompt_kernel(x_ref, oa_ref, za_ref, ob_ref, zb_ref, gm_ref, wat_ref, wbt_ref, wot_ref, g_ref, y_ref, *,
                       d_model, chain):
    cols = [slice(c, c + chain) for c in range(0, x_ref.shape[0], chain)]
    wat, wbt, wot = wat_ref[...], wbt_ref[...], wot_ref[...]
    ga = [(oa_ref[0, :, c] * za_ref[0, :, c]).astype(BF16) for c in cols]
    gb = [(ob_ref[0, :, c] * zb_ref[0, :, c]).astype(BF16) for c in cols]
    ya = [_dot(wat, v) for v in ga]
    yb = [_dot(wbt, v) for v in gb]
    mixed = [(gm_ref[0, :d_model, c] * a + gm_ref[0, d_model:, c] * b).astype(BF16) for c, a, b in zip(cols, ya, yb)]
    outs = [_dot(wot, v) for v in mixed]
    outs = [o * lax.rsqrt(jnp.mean(o * o, axis=0, keepdims=True) + RMS_EPS) for o in outs]
    for c, o in zip(cols, outs):
        y_ref[c, :] = x_ref[c, :] + o.T * g_ref[...]


def _mix_prompt(x2d, oat, zat, obt, zbt, gmt, wat, wbt, wot, g_post, *, batch, seq, tm, chain=256):
    m, d_model = x2d.shape
    assert seq % tm == 0 and tm % chain == 0
    nt = seq // tm
    row = lambda i: (i, 0)
    full = lambda i: (0, 0)
    tr = lambda i: (i // nt, 0, i % nt)
    return pl.pallas_call(
        functools.partial(_mix_prompt_kernel, d_model=d_model, chain=chain),
        grid=(m // tm,),
        in_specs=[pl.BlockSpec((tm, d_model), row), pl.BlockSpec((1, NSA_WIDTH, tm), tr),
                  pl.BlockSpec((1, NSA_WIDTH, tm), tr), pl.BlockSpec((1, SB_WIDTH, tm), tr),
                  pl.BlockSpec((1, SB_WIDTH, tm), tr), pl.BlockSpec((1, N_MERGE * d_model, tm), tr),
                  pl.BlockSpec((d_model, NSA_WIDTH), full), pl.BlockSpec((d_model, SB_WIDTH), full),
                  pl.BlockSpec((d_model, d_model), full), pl.BlockSpec((1, d_model), full)],
        out_specs=pl.BlockSpec((tm, d_model), row),
        out_shape=jax.ShapeDtypeStruct((m, d_model), F32),
        compiler_params=pltpu.CompilerParams(dimension_semantics=("arbitrary",), vmem_limit_bytes=VMEM_LIMIT),
        name="mix_prompt",
    )(x2d, oat, zat, obt, zbt, gmt, wat, wbt, wot, g_post)


def _mix_sample_kernel(x_ref, oa_ref, za_ref, ob_ref, zb_ref, gm_ref, wat_ref, wbt_ref, wot_ref, g_ref, y_ref, *,
                       d_model):
    ya = _dot_nt((oa_ref[...] * za_ref[...]).astype(BF16), wat_ref[...])
    yb = _dot_nt((ob_ref[...] * zb_ref[...]).astype(BF16), wbt_ref[...])
    gm = gm_ref[...]
    mixed = gm[:, :d_model] * ya + gm[:, d_model:] * yb
    out = _dot_nt(mixed.astype(BF16), wot_ref[...])
    y_ref[...] = x_ref[...] + _rms_scale(out, g_ref[...])


def _mix_sample(x2d, o_a, za, o_b, zb, gm, wat, wbt, wot, g_post):
    m, d_model = x2d.shape
    args = (x2d, o_a, za, o_b, zb, gm, wat, wbt, wot, g_post)
    return pl.pallas_call(
        functools.partial(_mix_sample_kernel, d_model=d_model),
        grid=(1,),
        in_specs=[pl.BlockSpec(a.shape, lambda i: (0, 0)) for a in args],
        out_specs=pl.BlockSpec((m, d_model), lambda i: (0, 0)),
        out_shape=jax.ShapeDtypeStruct((m, d_model), F32),
        compiler_params=pltpu.CompilerParams(dimension_semantics=("arbitrary",), vmem_limit_bytes=VMEM_LIMIT),
        name="mix_sample",
    )(*args)


def _head_pad(q):
    q2 = jnp.concatenate([q, q], axis=1)
    row = lax.broadcasted_iota(jnp.int32, q2.shape, 0)
    lane = lax.broadcasted_iota(jnp.int32, q2.shape, 1)
    return jnp.where(row // NSA_GROUP == lane // HEAD_DIM, q2, 0.0)


def _stream_kernel(pt_ref, qd_ref, qp_ref, k_ref, v_ref, qa_ref, wb_ref, *refs, ppg, tq, tk, units, steps_per_batch,
                   nbp):
    page_refs = refs[:ppg]
    cmp_refs = refs[ppg:2 * ppg]
    od_ref, op_ref, ocmp_ref, sel_ref = refs[2 * ppg:2 * ppg + 4]
    drun_ref, dacc_ref, runs_ref, acc_ref, pool_ref, state_ref = refs[2 * ppg + 4:]
    s = pl.program_id(1)
    s_lin = pl.program_id(0) * pl.num_programs(1) + s

    @pl.when(s == 0)
    def _():
        drun_ref[...] = jnp.zeros_like(drun_ref)
        dacc_ref[...] = jnp.zeros_like(dacc_ref)

    @pl.when(s_lin == 0)
    def _():
        state_ref[0] = 0
        state_ref[1] = 0

    hrow = lax.broadcasted_iota(jnp.int32, (SB_HEADS, SB_WIDTH), 0)
    hlane = lax.broadcasted_iota(jnp.int32, (SB_HEADS, SB_WIDTH), 1)
    diag_blocks = hrow == hlane // HEAD_DIM

    def decode_scores():
        q = qd_ref[0]
        qbd = jnp.where(diag_blocks, jnp.concatenate([q] * SB_HEADS, axis=1), 0.0).astype(BF16)
        z = jnp.concatenate([_dot(qbd, pr[0, :SB_WIDTH, :].astype(BF16)) for pr in page_refs], axis=0)
        sp = _softplus(z)
        r = lax.broadcasted_iota(jnp.int32, (PAGE_SIZE, PAGE_SIZE), 0)
        c = lax.broadcasted_iota(jnp.int32, (PAGE_SIZE, PAGE_SIZE), 1)
        lower = jnp.where(r > c, 1.0, 0.0).astype(BF16)
        hi, lo = _split_bf16(sp)
        return z, sp, _dot(hi, lower) + _dot(lo, lower)

    def decode_update(z, sp, drop):
        tot = jnp.sum(sp, axis=1, keepdims=True)
        run = drun_ref[...]
        dacc = dacc_ref[...]
        for n in range(ppg):
            sl = slice(n * SB_HEADS, (n + 1) * SB_HEADS)
            a = jnp.exp(z[sl] - sp[sl] - drop[sl] - run)
            dacc = dacc + _dot_nt(a.astype(BF16), page_refs[n][0, SB_WIDTH:, :].astype(BF16))
            run = run + tot[sl]
        drun_ref[...] = run
        dacc_ref[...] = dacc

    def pool_compressed():
        group = LANES // (ppg * BLOCKS_PER_PAGE)
        slot = s % group
        chunk = s // group
        xk = jnp.concatenate([pr[0, :KV_PAIR, :].astype(BF16) for pr in cmp_refs], axis=1)
        xv = jnp.concatenate([pr[0, KV_PAIR:, :].astype(BF16) for pr in cmp_refs], axis=1)
        part = jnp.concatenate([_dot(xk, wb_ref[0, slot]), _dot(xv, wb_ref[1, slot])], axis=0)
        pool_ref[chunk] = jnp.where(slot == 0, 0.0, pool_ref[chunk]) + part

    pairs = SB_HEADS // 2
    lanes = 2 * tq
    diag = tq // tk
    per_batch = diag * units * (units + 1) // 2
    local = s_lin % steps_per_batch
    n_steps = ((local + 1) * per_batch) // steps_per_batch - (local * per_batch) // steps_per_batch

    ur = lax.broadcasted_iota(jnp.int32, (tk, 2 * tk), 0)
    uc = lax.broadcasted_iota(jnp.int32, (tk, 2 * tk), 1) % tk
    upper2 = jnp.where(uc > ur, 1.0, 0.0).astype(BF16)
    z0 = jnp.zeros((HEAD_DIM, tq), BF16)

    def tile_step(with_decode):
        i = state_ref[0]
        j = state_ref[1]
        n_t = (i + 1) * diag
        off = pl.multiple_of((n_t - 1 - j) * tk, tk)
        qoff = pl.multiple_of(i * tq, tq)

        @pl.when(j == 0)
        def _():
            runs_ref[...] = jnp.zeros_like(runs_ref)
            acc_ref[...] = jnp.zeros_like(acc_ref)

        def compute(masked):
            if with_decode:
                dz, dsp, ddrop = decode_scores()
                pool_compressed()
            kk = k_ref[0, pl.ds(off, tk), :]
            qpads = []
            for jp in range(pairs):
                qt = qp_ref[0, jp * LANES:(jp + 1) * LANES, pl.ds(qoff, tq)]
                qpads.append(jnp.concatenate([jnp.concatenate([qt[:HEAD_DIM], z0], axis=0),
                                              jnp.concatenate([z0, qt[HEAD_DIM:]], axis=0)], axis=1))
            if masked:
                pos = qoff + lax.broadcasted_iota(jnp.int32, (1, lanes), 1) % tq
                mask = (off + lax.broadcasted_iota(jnp.int32, (tk, 1), 0)) < pos
            zs = [_dot(kk[:, jp * LANES:(jp + 1) * LANES], qpads[jp]) for jp in range(pairs)]
            sps = [jnp.where(mask, _softplus(zz), 0.0) if masked else _softplus(zz) for zz in zs]
            drops = [_dot(upper2, jnp.concatenate(_split_bf16(spj), axis=0)) + runs_ref[jp]
                     for jp, spj in enumerate(sps)]
            if with_decode:
                decode_update(dz, dsp, ddrop)
            ws = [jnp.exp(zz - spj - dr) for zz, spj, dr in zip(zs, sps, drops)]
            if masked:
                ws = [jnp.where(mask, w, 0.0) for w in ws]
            for h in range(SB_HEADS):
                w = ws[h // 2][:, (h % 2) * tq:(h % 2 + 1) * tq].astype(BF16)
                acc_ref[h] = acc_ref[h] + _dot(v_ref[0, h * HEAD_DIM:(h + 1) * HEAD_DIM, pl.ds(off, tk)], w)
            for jp, spj in enumerate(sps):
                runs_ref[jp] = runs_ref[jp] + jnp.sum(spj, axis=0, keepdims=True)

        @pl.when(j < diag)
        def _():
            compute(True)

        @pl.when(j >= diag)
        def _():
            compute(False)

        last = j == n_t - 1

        @pl.when(last)
        def _():
            for h in range(SB_HEADS):
                op_ref[0, h * HEAD_DIM:(h + 1) * HEAD_DIM, pl.ds(qoff, tq)] = acc_ref[h]

        state_ref[1] = jnp.where(last, 0, j + 1)
        state_ref[0] = jnp.where(last, (i + 1) % units, i)

    @pl.when(n_steps > 0)
    def _():
        tile_step(True)

    @pl.when(n_steps == 0)
    def _():
        decode_update(*decode_scores())
        pool_compressed()

    def rest(_, carry):
        tile_step(False)
        return carry

    lax.fori_loop(1, n_steps, rest, 0)

    @pl.when(s == pl.num_programs(1) - 1)
    def _():
        od_ref[0] = jnp.sum(jnp.where(diag_blocks, dacc_ref[...], 0.0), axis=0, keepdims=True)
        _compressed_decode(qa_ref[0], pool_ref, ocmp_ref, sel_ref, nbp)


def _stream_attention(page_table_flat, q_dec, sb_cache_t, qbt, kbb, vbt, qa_dec, w_cmp, nsa_cache_t, *, dec_batch,
                      n_pages, batch, seq, ppg, tq=256, tk=128):
    steps = n_pages // ppg
    assert tq % tk == 0 and seq % tq == 0 and (dec_batch * steps) % batch == 0
    assert LANES % (ppg * BLOCKS_PER_PAGE) == 0
    spb = dec_batch * steps // batch
    nbp = n_pages * BLOCKS_PER_PAGE
    wb = _decode_pool_weights(w_cmp, ppg)

    def page_map(n):
        return lambda b, s, pt: (pt[b * n_pages + n_pages - 1 - (s * ppg + n)], 0, 0)

    def cmp_map(n):
        return lambda b, s, pt: (pt[b * n_pages + s * ppg + n], 0, 0)

    prow = lambda b, s, pt: ((b * steps + s) // spb, 0, 0)
    sample = lambda b, s, pt: (b, 0, 0)
    grid_spec = pltpu.PrefetchScalarGridSpec(
        num_scalar_prefetch=1, grid=(dec_batch, steps),
        in_specs=[pl.BlockSpec((1, SB_HEADS, HEAD_DIM), sample),
                  pl.BlockSpec((1, SB_WIDTH, seq), prow), pl.BlockSpec((1, seq, SB_WIDTH), prow),
                  pl.BlockSpec((1, SB_WIDTH, seq), prow),
                  pl.BlockSpec((1, NSA_HEADS, HEAD_DIM), sample),
                  pl.BlockSpec(wb.shape, lambda b, s, pt: (0, 0, 0, 0))]
        + [pl.BlockSpec((1, 2 * SB_WIDTH, PAGE_SIZE), page_map(n)) for n in range(ppg)]
        + [pl.BlockSpec((1, 2 * KV_PAIR, PAGE_SIZE), cmp_map(n)) for n in range(ppg)],
        out_specs=[pl.BlockSpec((1, 1, SB_WIDTH), sample), pl.BlockSpec((1, SB_WIDTH, seq), prow),
                   pl.BlockSpec((1, NSA_HEADS, HEAD_DIM), sample),
                   pl.BlockSpec((1, NSA_KV_HEADS, NSA_TOPK, LANES), lambda b, s, pt: (b, 0, 0, 0))],
        scratch_shapes=[pltpu.VMEM((SB_HEADS, 1), F32), pltpu.VMEM((SB_HEADS, SB_WIDTH), F32),
                        pltpu.VMEM((SB_HEADS // 2, 1, 2 * tq), F32), pltpu.VMEM((SB_HEADS, HEAD_DIM, tq), F32),
                        pltpu.VMEM((pl.cdiv(nbp, LANES), 2 * KV_PAIR, LANES), F32),
                        pltpu.SMEM((2,), jnp.int32)])
    return pl.pallas_call(
        functools.partial(_stream_kernel, ppg=ppg, tq=tq, tk=tk, units=seq // tq, steps_per_batch=spb, nbp=nbp),
        grid_spec=grid_spec,
        out_shape=[jax.ShapeDtypeStruct((dec_batch, 1, SB_WIDTH), F32),
                   jax.ShapeDtypeStruct((batch, SB_WIDTH, seq), F32),
                   jax.ShapeDtypeStruct((dec_batch, NSA_HEADS, HEAD_DIM), F32),
                   jax.ShapeDtypeStruct((dec_batch, NSA_KV_HEADS, NSA_TOPK, LANES), jnp.int32)],
        compiler_params=pltpu.CompilerParams(dimension_semantics=("arbitrary", "arbitrary"),
                                             vmem_limit_bytes=VMEM_LIMIT),
        name="stream_attention",
    )(page_table_flat, q_dec, qbt, kbb, vbt, qa_dec, wb, *([sb_cache_t] * ppg), *([nsa_cache_t] * ppg))


def _compressed_decode(q, pool_ref, ocmp_ref, sel_ref, nbp):
    n_chunks = pool_ref.shape[0]
    nl = n_chunks * LANES
    qpad = _head_pad(q).astype(BF16)
    pooled = jnp.concatenate([pool_ref[ch] for ch in range(n_chunks)], axis=1)
    blk_l = lax.broadcasted_iota(jnp.int32, (1, nl), 1)
    live = blk_l < nbp
    sc = jnp.where(live, _dot(qpad, pooled[:KV_PAIR].astype(BF16)), NEG_INF)
    e = jnp.where(live, jnp.exp(sc - jnp.max(sc, axis=1, keepdims=True)), 0.0)
    p = e / jnp.sum(e, axis=1, keepdims=True)
    o = _dot_nt(p.astype(BF16), pooled[KV_PAIR:].astype(BF16))
    hrow = lax.broadcasted_iota(jnp.int32, o.shape, 0)
    ocmp_ref[0] = jnp.where(hrow < NSA_GROUP, o, pltpu.roll(o, HEAD_DIM, 1))[:, :HEAD_DIM]

    blk_s = lax.broadcasted_iota(jnp.int32, (nl, 1), 0)
    eye = lax.broadcasted_iota(jnp.int32, (nl, nl), 0) == lax.broadcasted_iota(jnp.int32, (nl, nl), 1)
    n_sel = NSA_TOPK - 1
    kslot = lax.broadcasted_iota(jnp.int32, (NSA_TOPK, nl), 0)
    for gi in range(NSA_KV_HEADS):
        imp = jnp.sum(p[gi * NSA_GROUP:(gi + 1) * NSA_GROUP], axis=0, keepdims=True)
        forced = (blk_l == 0) | (blk_l == nbp - 1)
        srow = jnp.where(live, jnp.where(forced, FORCE_SCORE, imp), NEG_INF)
        scol = jnp.sum(jnp.where(eye, srow, 0.0), axis=1, keepdims=True)
        beats = (scol > srow) | ((scol == srow) & (blk_s < blk_l))
        rank = jnp.sum(jnp.where(beats, 1.0, 0.0), axis=0, keepdims=True)
        sel = rank < n_sel
        selcol = jnp.sum(jnp.where(eye & sel, 1.0, 0.0), axis=1, keepdims=True) > 0.5
        rank_sel = jnp.sum(jnp.where(selcol & (blk_s < blk_l), 1.0, 0.0), axis=0, keepdims=True)
        onehot = sel & (rank_sel.astype(jnp.int32) == kslot)
        idx = jnp.sum(jnp.where(onehot, blk_l, 0), axis=1, keepdims=True)
        sel_ref[0, gi] = jnp.broadcast_to(idx, (NSA_TOPK, LANES))


def _decode_pool_weights(w_cmp, ppg):
    per_step = ppg * BLOCKS_PER_PAGE
    group = LANES // per_step
    k = jnp.arange(ppg * PAGE_SIZE)
    target = (k // PAGE_SIZE) * BLOCKS_PER_PAGE + (k % PAGE_SIZE) // NSA_BLOCK
    hit = jnp.arange(LANES)[None, None, :] == (jnp.arange(group)[:, None, None] * per_step + target[None, :, None])
    w_row = jnp.tile(w_cmp, (1, ppg * BLOCKS_PER_PAGE))
    return jnp.where(hit[None], w_row[:, None, :, None], 0.0).astype(BF16)


def _nsa_sel_decode_kernel(sel_ref, pt_ref, q_ref, ocmp_ref, ga_ref, new_ref, winp_ref, *refs, n_blk):
    blk_refs = refs[:NSA_KV_HEADS * n_blk]
    o_ref, wino_ref = refs[NSA_KV_HEADS * n_blk:]
    b = pl.program_id(0)
    qb = _head_pad(q_ref[0]).astype(BF16)
    new = new_ref[0]
    hrow = lax.broadcasted_iota(jnp.int32, (NSA_HEADS, KV_PAIR), 0)
    top = hrow < NSA_GROUP

    def fold(o):
        return jnp.where(top, o, pltpu.roll(o, HEAD_DIM, 1))[:, :HEAD_DIM]

    def softmax(sc, mask):
        sc = jnp.where(mask, sc, NEG_INF)
        e = jnp.where(mask, jnp.exp(sc - jnp.max(sc, axis=1, keepdims=True)), 0.0)
        return e / jnp.sum(e, axis=1, keepdims=True)

    lane = lax.broadcasted_iota(jnp.int32, (1, PAGE_SIZE), 1)
    first = lax.broadcasted_iota(jnp.int32, (KV_PAIR, PAGE_SIZE), 1) == 0
    k_new = jnp.where(first, new[2 * KV_PAIR:3 * KV_PAIR], 0.0).astype(BF16)
    v_new = jnp.where(first, new[3 * KV_PAIR:4 * KV_PAIR], 0.0).astype(BF16)
    o_g = []
    for gi in range(NSA_KV_HEADS):
        scs, masks = [], []
        for n in range(n_blk):
            half = sel_ref[(b * NSA_KV_HEADS + gi) * NSA_TOPK + n] % BLOCKS_PER_PAGE
            scs.append(_dot(qb, blk_refs[gi * n_blk + n][0, :KV_PAIR, :].astype(BF16)))
            masks.append(lane // NSA_BLOCK == half)
        scs.append(_dot(qb, k_new))
        masks.append(lane == 0)
        p = softmax(jnp.concatenate(scs, axis=1), jnp.concatenate(masks, axis=1)).astype(BF16)
        o = _dot_nt(p[:, n_blk * PAGE_SIZE:], v_new)
        for n in range(n_blk):
            o = o + _dot_nt(p[:, n * PAGE_SIZE:(n + 1) * PAGE_SIZE],
                            blk_refs[gi * n_blk + n][0, KV_PAIR:, :].astype(BF16))
        o_g.append(o)
    o_slc = fold(jnp.where(top, o_g[0], o_g[1]))

    wp = winp_ref[0]
    w = wp.shape[1]
    wl = lax.broadcasted_iota(jnp.int32, wp.shape, 1)
    shifted = jnp.where(wl == w - 1, new[4 * KV_PAIR:], pltpu.roll(wp, w - 1, 1))
    wino_ref[0] = shifted
    sc = _dot(qb, shifted[:KV_PAIR].astype(BF16))
    p = softmax(sc, jnp.full((1, w), True)).astype(BF16)
    o_win = fold(_dot_nt(p, shifted[KV_PAIR:].astype(BF16)))

    ga = ga_ref[0]
    o_ref[0] = ga[0] * ocmp_ref[0] + ga[1] * o_slc + ga[2] * o_win


def _nsa_sel_decode(sel_flat, page_table_flat, q_a, o_cmp, ga, new_kv, win_past_t, cache_t, *, dec_batch, n_pages):
    n_blk = NSA_TOPK - 1
    w = win_past_t.shape[2]

    def blk_map(gi, n):
        def f(b, sel, pt):
            blk = sel[(b * NSA_KV_HEADS + gi) * NSA_TOPK + n]
            return (pt[b * n_pages + blk // BLOCKS_PER_PAGE], 1, 0)
        return f

    grid_spec = pltpu.PrefetchScalarGridSpec(
        num_scalar_prefetch=2, grid=(dec_batch,),
        in_specs=[pl.BlockSpec((1, NSA_HEADS, HEAD_DIM), lambda b, sel, pt: (b, 0, 0)),
                  pl.BlockSpec((1, NSA_HEADS, HEAD_DIM), lambda b, sel, pt: (b, 0, 0)),
                  pl.BlockSpec((1, NSA_BRANCHES, NSA_HEADS, 1), lambda b, sel, pt: (b, 0, 0, 0)),
                  pl.BlockSpec((1, KVA_WIDTH, 1), lambda b, sel, pt: (b, 0, 0)),
                  pl.BlockSpec((1, 2 * KV_PAIR, w), lambda b, sel, pt: (b, 0, 0))]
        + [pl.BlockSpec((1, 2 * KV_PAIR, PAGE_SIZE), blk_map(gi, n))
           for gi in range(NSA_KV_HEADS) for n in range(n_blk)],
        out_specs=[pl.BlockSpec((1, NSA_HEADS, HEAD_DIM), lambda b, sel, pt: (b, 0, 0)),
                   pl.BlockSpec((1, 2 * KV_PAIR, w), lambda b, sel, pt: (b, 0, 0))])
    return pl.pallas_call(
        functools.partial(_nsa_sel_decode_kernel, n_blk=n_blk),
        grid_spec=grid_spec,
        out_shape=[jax.ShapeDtypeStruct((dec_batch, NSA_HEADS, HEAD_DIM), F32),
                   jax.ShapeDtypeStruct((dec_batch, 2 * KV_PAIR, w), F32)],
        compiler_params=pltpu.CompilerParams(dimension_semantics=("arbitrary",), vmem_limit_bytes=VMEM_LIMIT),
        name="nsa_sel_decode",
    )(sel_flat, page_table_flat, q_a, o_cmp, ga, new_kv, win_past_t, *([cache_t] * (NSA_KV_HEADS * n_blk)))


def _feature_major(a, lead):
    nl = len(lead)
    t = jnp.moveaxis(a, nl, -1)
    return t.reshape(lead + (-1, a.shape[nl]))


def _time_major(a_t, feat_shape):
    lead, _, time = a_t.shape
    return jnp.moveaxis(a_t.reshape((lead,) + feat_shape + (time,)), -1, 1)


def _layer(x_prompt, x_sample, nsa_cache, win_cache, sb_cache, page_table, w_in, w_cmp, w_a, w_b, w_o, g_pre, g_post):
    batch, seq, d_model = x_prompt.shape
    dec_batch, dec_seq, _ = x_sample.shape
    n_pages = page_table.shape[1]
    past_len = n_pages * PAGE_SIZE
    n_phys = nsa_cache.shape[0]
    tm, ppg = 256, 16
    assert dec_seq == 1 and seq % tm == 0 and seq >= NSA_WINDOW and past_len >= NSA_WINDOW
    assert n_pages % ppg == 0 and past_len // NSA_BLOCK >= NSA_TOPK
    assert win_cache.shape[1] == NSA_WINDOW and nsa_cache.shape[1] == PAGE_SIZE

    wt_perm = _permute_w_in_t(w_in, d_model)
    wat, wbt, wot = w_a.T.astype(BF16), w_b.T.astype(BF16), w_o.T.astype(BF16)
    g_pre2, g_post2 = g_pre.reshape(1, d_model), g_post.reshape(1, d_model)
    nsa_feat = (4, NSA_KV_HEADS, HEAD_DIM)
    win_feat = (2, NSA_KV_HEADS, HEAD_DIM)
    sb_feat = (2, SB_HEADS, HEAD_DIM)

    xp = x_prompt.reshape(batch * seq, d_model)
    cos, sin = _rope_angles(jnp.arange(seq, dtype=jnp.int32))
    wp_prompt = _pool_weights(w_cmp, 1, tm // NSA_BLOCK, SUBLANES).reshape(2 * SUBLANES, tm)
    (nsa_t, win_t, sb_t, za_t, zb_t, gm_t, qa_t, ga_t, kb, v_t, pooled, qb_t, kbb, vb_t) = _project_prompt(
        xp, g_pre2, wt_perm, cos.T, sin.T, wp_prompt, batch=batch, seq=seq, tm=tm)
    xs = x_sample.reshape(dec_batch, d_model)
    tabs_s = _rope_lane_tables(jnp.full((dec_batch,), past_len, jnp.int32))
    (nsa_s, win_s, sb_s, za_s, zb_s, gm_s, qa_s, ga_s, qb_s) = _project_sample(xs, g_pre2, wt_perm, tabs_s)
    pt_flat = page_table.reshape(-1)
    sb_cache_t = _feature_major(sb_cache, (n_phys,))
    nsa_cache_t = _feature_major(nsa_cache, (n_phys,))
    win_cache_t = _feature_major(win_cache, (dec_batch,))

    qa3 = qa_s.reshape(dec_batch, NSA_HEADS, HEAD_DIM)
    o_b_s, ob_t, o_cmp_s, sel = _stream_attention(
        pt_flat, qb_s.reshape(dec_batch, SB_HEADS, HEAD_DIM), sb_cache_t, qb_t, kbb.reshape(batch, seq, SB_WIDTH), vb_t,
        qa3, w_cmp, nsa_cache_t, dec_batch=dec_batch, n_pages=n_pages, batch=batch, seq=seq, ppg=ppg)

    oa_t = _nsa_prompt(qa_t, kb.reshape(batch, seq, 2 * KV_PAIR), v_t,
                       pooled.reshape(batch, seq // NSA_BLOCK, 2 * KV_PAIR), ga_t, batch=batch, seq=seq)
    y_prompt = _mix_prompt(xp, oa_t, za_t, ob_t, zb_t, gm_t, wat, wbt, wot, g_post2,
                           batch=batch, seq=seq, tm=2 * tm).reshape(batch, seq, d_model)
    nsa_kv_prompt = _time_major(nsa_t, nsa_feat)
    win_kv_prompt = _time_major(win_t[:, :, seq - NSA_WINDOW:], win_feat)
    sb_kv_prompt = _time_major(sb_t, sb_feat)

    ga3 = ga_s[:, :GATE_A].reshape(dec_batch, NSA_BRANCHES, NSA_HEADS, 1)
    new_kv = jnp.concatenate([nsa_s, win_s], axis=1).reshape(dec_batch, KVA_WIDTH, 1)
    o_a_s, win_out_t = _nsa_sel_decode(sel[:, :, :, 0].reshape(-1), pt_flat, qa3, o_cmp_s, ga3, new_kv,
                                       win_cache_t, nsa_cache_t, dec_batch=dec_batch, n_pages=n_pages)
    y_sample = _mix_sample(xs, o_a_s.reshape(dec_batch, NSA_WIDTH), za_s, o_b_s.reshape(dec_batch, SB_WIDTH), zb_s,
                           gm_s, wat, wbt, wot, g_post2).reshape(dec_batch, 1, d_model)
    nsa_kv_sample = nsa_s.reshape((dec_batch, 1) + nsa_feat)
    win_kv_sample = _time_major(win_out_t, win_feat)
    sb_kv_sample = sb_s.reshape((dec_batch, 1) + sb_feat)
    return (y_prompt, y_sample, nsa_kv_prompt, win_kv_prompt, sb_kv_prompt, nsa_kv_sample, win_kv_sample,
            sb_kv_sample)


def kernel(x_prompt, x_sample, cache_nsa_kv, cache_nsa_win_kv, cache_sb_kv, page_table, w_in, w_cmp, w_branch_a,
           w_branch_b, w_out, g_pre, g_post):
    hp, hs = x_prompt, x_sample
    caches = [[] for _ in range(6)]
    for layer in range(w_in.shape[0]):
        outs = _layer(hp, hs, cache_nsa_kv[layer], cache_nsa_win_kv[layer], cache_sb_kv[layer], page_table,
                      w_in[layer], w_cmp[layer], w_branch_a[layer], w_branch_b[layer], w_out[layer],
                      g_pre[layer], g_post[layer])
        hp, hs = outs[0], outs[1]
        for acc, o in zip(caches, outs[2:]):
            acc.append(o)
    return (hp, hs) + tuple(jnp.stack(c) for c in caches)
```

```python
import functools

import jax
import jax.numpy as jnp
from jax import lax
from jax.experimental import pallas as pl
from jax.experimental.pallas import tpu as pltpu

HEAD_DIM = 64
ROT_DIM = HEAD_DIM // 4
ROPE_THETA = 500000.0
NSA_HEADS = 8
NSA_KV_HEADS = 2
NSA_GROUP = NSA_HEADS // NSA_KV_HEADS
NSA_BRANCHES = 3
NSA_BLOCK = 64
NSA_TOPK = 16
NSA_WINDOW = 512
NSA_WIDTH = NSA_HEADS * HEAD_DIM
SB_HEADS = 8
SB_WIDTH = SB_HEADS * HEAD_DIM
N_MERGE = 2
PAGE_SIZE = 128
RMS_EPS = 1e-6
NEG_INF = -1e30
FORCE_SCORE = 1e3
SCALE = HEAD_DIM ** -0.5
LOG2E = 1.4426950408889634

LANES = 128
SUBLANES = 8
KV_PAIR = NSA_KV_HEADS * HEAD_DIM
assert KV_PAIR == LANES and PAGE_SIZE == LANES and PAGE_SIZE % NSA_BLOCK == 0
KVA_WIDTH = 2 * NSA_BRANCHES * KV_PAIR
GATE_A = NSA_BRANCHES * NSA_HEADS
BLOCKS_PER_PAGE = PAGE_SIZE // NSA_BLOCK
VMEM_LIMIT = 56 * 1024 * 1024

C_QA = 0
C_KVA = C_QA + NSA_WIDTH
C_ZA = C_KVA + KVA_WIDTH
C_QKVB = C_ZA + NSA_WIDTH
C_ZB = C_QKVB + 3 * SB_WIDTH
C_GM = C_ZB + SB_WIDTH

BF16 = jnp.bfloat16
F32 = jnp.float32


def _dot(a, b):
    return jnp.dot(a, b, preferred_element_type=F32)


def _dot_tn(a, b):
    return lax.dot_general(a, b, (((0,), (0,)), ((), ())), preferred_element_type=F32)


def _dot_nt(a, b):
    return lax.dot_general(a, b, (((1,), (1,)), ((), ())), preferred_element_type=F32)


def _sigmoid(x):
    return 1.0 / (1.0 + jnp.exp(-x))


def _rms_scale(x, g):
    return x * lax.rsqrt(jnp.mean(x * x, axis=-1, keepdims=True) + RMS_EPS) * g


def _rope_rows(v, cos, sin):
    half = ROT_DIM // 2
    parts = []
    for base in range(0, v.shape[0], HEAD_DIM):
        x1, x2 = v[base:base + half], v[base + half:base + 2 * half]
        parts += [x1 * cos - x2 * sin, x2 * cos + x1 * sin, v[base + 2 * half:base + HEAD_DIM]]
    return jnp.concatenate(parts, axis=0)


def _proj_prompt_kernel(x_ref, g_ref, wt_ref, cos_ref, sin_ref, wp_ref,
                        nsa_ref, win_ref, sb_ref, za_ref, zb_ref, gm_ref, qa_ref, ga_ref, kb_ref, vt_ref,
                        pool_ref, qb_ref, kbb_ref, vbt_ref, *, d_model):
    hb = _rms_scale(x_ref[...], g_ref[...]).astype(BF16)
    cos, sin = cos_ref[...], sin_ref[...]

    def seg(lo, width):
        return _dot_nt(wt_ref[lo:lo + width, :], hb)

    qa_ref[0] = (_rope_rows(seg(C_QA, NSA_WIDTH), cos, sin) * (SCALE * LOG2E)).astype(BF16)
    kv = seg(C_KVA, KVA_WIDTH)
    cmp_k = _rope_rows(kv[0 * LANES:1 * LANES], cos, sin)
    cmp_v = kv[1 * LANES:2 * LANES]
    slc_k = _rope_rows(kv[2 * LANES:3 * LANES], cos, sin)
    slc_v = kv[3 * LANES:4 * LANES]
    win_k = _rope_rows(kv[4 * LANES:5 * LANES], cos, sin)
    win_v = kv[5 * LANES:6 * LANES]
    nsa_ref[0] = jnp.concatenate([cmp_k, cmp_v, slc_k, slc_v], axis=0)
    win_ref[0] = jnp.concatenate([win_k, win_v], axis=0)
    kb_ref[...] = jnp.concatenate([slc_k.T, win_k.T], axis=1).astype(BF16)
    vt_ref[0] = jnp.concatenate([slc_v, win_v], axis=0).astype(BF16)
    wp = wp_ref[...]
    nblk = pool_ref.shape[1]
    pooled = jnp.concatenate([_dot_nt(wp[:SUBLANES], cmp_k.astype(BF16)),
                              _dot_nt(wp[SUBLANES:], cmp_v.astype(BF16))], axis=1)
    pool_ref[0] = pooled[:nblk]

    za = seg(C_ZA, NSA_WIDTH)
    za_ref[0] = (za * _sigmoid(za)).astype(za_ref.dtype)
    qkvb = seg(C_QKVB, 3 * SB_WIDTH)
    qb_ref[0] = (qkvb[:SB_WIDTH] * SCALE).astype(BF16)
    sb_ref[0] = qkvb[SB_WIDTH:]
    kbb_ref[...] = qkvb[SB_WIDTH:2 * SB_WIDTH].T.astype(BF16)
    vbt_ref[0] = qkvb[2 * SB_WIDTH:].astype(BF16)
    zb = seg(C_ZB, SB_WIDTH)
    zb_ref[0] = (zb * _sigmoid(zb)).astype(zb_ref.dtype)
    gm_ref[0] = _sigmoid(seg(C_GM, N_MERGE * d_model)).astype(gm_ref.dtype)
    ga_ref[0] = _sigmoid(seg(C_GM + N_MERGE * d_model, LANES))


def _project_prompt(x2d, g_pre, wt_perm, cos_t, sin_t, wp, *, batch, seq, tm):
    m, d_model = x2d.shape
    nt = seq // tm
    n_rows = wt_perm.shape[0]
    nblk = tm // NSA_BLOCK
    row = lambda i: (i, 0)
    full = lambda i: (0, 0)
    tab = lambda i: (0, i % nt)
    tr = lambda i: (i // nt, 0, i % nt)
    half = ROT_DIM // 2
    in_specs = [pl.BlockSpec((tm, d_model), row), pl.BlockSpec((1, d_model), full),
                pl.BlockSpec((n_rows, d_model), full),
                pl.BlockSpec((half, tm), tab), pl.BlockSpec((half, tm), tab),
                pl.BlockSpec((2 * SUBLANES, tm), full)]
    sds = jax.ShapeDtypeStruct

    def feat(width, dtype):
        return sds((batch, width, seq), dtype), pl.BlockSpec((1, width, tm), tr)

    def rows(width, dtype):
        return sds((m, width), dtype), pl.BlockSpec((tm, width), row)

    outs = [feat(4 * KV_PAIR, F32), feat(2 * KV_PAIR, F32), feat(2 * SB_WIDTH, F32),
            feat(NSA_WIDTH, BF16), feat(SB_WIDTH, BF16), feat(N_MERGE * d_model, BF16),
            feat(NSA_WIDTH, BF16), feat(LANES, F32), rows(2 * KV_PAIR, BF16), feat(2 * KV_PAIR, BF16),
            (sds((m // tm, nblk, 2 * KV_PAIR), F32), pl.BlockSpec((1, nblk, 2 * KV_PAIR), lambda i: (i, 0, 0))),
            feat(SB_WIDTH, BF16), rows(SB_WIDTH, BF16), feat(SB_WIDTH, BF16)]
    return pl.pallas_call(
        functools.partial(_proj_prompt_kernel, d_model=d_model),
        grid=(m // tm,), in_specs=in_specs, out_specs=[o[1] for o in outs], out_shape=[o[0] for o in outs],
        compiler_params=pltpu.CompilerParams(dimension_semantics=("arbitrary",), vmem_limit_bytes=VMEM_LIMIT),
        name="proj_prompt",
    )(x2d, g_pre, wt_perm, cos_t, sin_t, wp)


def _rope_lanes(v, c, s_up, s_dn):
    half = ROT_DIM // 2
    outs = []
    for j in range(v.shape[1] // LANES):
        blk = v[:, j * LANES:(j + 1) * LANES]
        outs.append(blk * c + pltpu.roll(blk, LANES - half, 1) * s_up + pltpu.roll(blk, half, 1) * s_dn)
    return outs[0] if len(outs) == 1 else jnp.concatenate(outs, axis=1)


def _proj_sample_kernel(x_ref, g_ref, wt_ref, cos_ref, sup_ref, sdn_ref,
                        nsa_ref, win_ref, sbkv_ref, za_ref, zb_ref, gm_ref, qa_ref, ga_ref, qb_ref, *, d_model):
    hb = _rms_scale(x_ref[...], g_ref[...]).astype(BF16)
    cos, s_up, s_dn = cos_ref[...], sup_ref[...], sdn_ref[...]

    def seg(lo, width):
        return _dot_nt(hb, wt_ref[lo:lo + width, :])

    qa_ref[...] = _rope_lanes(seg(C_QA, NSA_WIDTH), cos, s_up, s_dn) * SCALE
    kv = seg(C_KVA, KVA_WIDTH)
    parts = []
    for j in range(2 * NSA_BRANCHES):
        blk = kv[:, j * LANES:(j + 1) * LANES]
        parts.append(_rope_lanes(blk, cos, s_up, s_dn) if j % 2 == 0 else blk)
    nsa_ref[...] = jnp.concatenate(parts[:4], axis=1)
    win_ref[...] = jnp.concatenate(parts[4:], axis=1)
    za = seg(C_ZA, NSA_WIDTH)
    za_ref[...] = za * _sigmoid(za)
    qkvb = seg(C_QKVB, 3 * SB_WIDTH)
    qb_ref[...] = qkvb[:, :SB_WIDTH] * SCALE
    sbkv_ref[...] = qkvb[:, SB_WIDTH:]
    zb = seg(C_ZB, SB_WIDTH)
    zb_ref[...] = zb * _sigmoid(zb)
    gm_ref[...] = _sigmoid(seg(C_GM, N_MERGE * d_model))
    ga_ref[...] = _sigmoid(seg(C_GM + N_MERGE * d_model, LANES))


def _project_sample(x2d, g_pre, wt_perm, tables):
    m, d_model = x2d.shape
    full = lambda i: (0, 0)
    widths = [4 * KV_PAIR, 2 * KV_PAIR, 2 * SB_WIDTH, NSA_WIDTH, SB_WIDTH, N_MERGE * d_model,
              NSA_WIDTH, LANES, SB_WIDTH]
    return pl.pallas_call(
        functools.partial(_proj_sample_kernel, d_model=d_model),
        grid=(1,),
        in_specs=[pl.BlockSpec((m, d_model), full), pl.BlockSpec((1, d_model), full),
                  pl.BlockSpec(wt_perm.shape, full)] + [pl.BlockSpec((m, LANES), full)] * 3,
        out_specs=[pl.BlockSpec((m, w), full) for w in widths],
        out_shape=[jax.ShapeDtypeStruct((m, w), F32) for w in widths],
        compiler_params=pltpu.CompilerParams(dimension_semantics=("arbitrary",), vmem_limit_bytes=VMEM_LIMIT),
        name="proj_sample",
    )(x2d, g_pre, wt_perm, *tables)


def _rope_angles(pos):
    half = ROT_DIM // 2
    inv_freq = ROPE_THETA ** (-jnp.arange(half, dtype=F32) / half)
    ang = pos.astype(F32)[:, None] * inv_freq[None, :]
    return jnp.cos(ang), jnp.sin(ang)


def _rope_lane_tables(pos):
    cos, sin = _rope_angles(pos)
    n = pos.shape[0]
    half = ROT_DIM // 2
    ones = jnp.ones((n, HEAD_DIM - ROT_DIM), F32)
    zeros = jnp.zeros((n, HEAD_DIM - ROT_DIM), F32)
    zh = jnp.zeros((n, half), F32)
    rep = LANES // HEAD_DIM
    return tuple(jnp.tile(t, (1, rep)) for t in (jnp.concatenate([cos, cos, ones], axis=1),
                                                 jnp.concatenate([-sin, zh, zeros], axis=1),
                                                 jnp.concatenate([zh, sin, zeros], axis=1)))


def _permute_w_in_t(w_in, d_model):
    c = [NSA_WIDTH, KVA_WIDTH, NSA_WIDTH, GATE_A, 3 * SB_WIDTH, SB_WIDTH, N_MERGE * d_model]
    o = [0]
    for s in c:
        o.append(o[-1] + s)
    wt = w_in.T
    pad = jnp.zeros((LANES - GATE_A, d_model), w_in.dtype)
    return jnp.concatenate([wt[:o[3]], wt[o[4]:], wt[o[3]:o[4]], pad], axis=0).astype(BF16)


def _pool_weights(w_cmp, n_tiles, blocks_per_tile, rows):
    lane = jnp.arange(blocks_per_tile * NSA_BLOCK)
    owner = jnp.arange(n_tiles)[:, None, None] * blocks_per_tile + (lane // NSA_BLOCK)[None, None, :]
    hit = jnp.arange(rows)[None, :, None] == owner
    w_lane = jnp.tile(w_cmp, (1, blocks_per_tile))
    return jnp.where(hit[None], w_lane[:, None, None, :], 0.0).astype(BF16)


def _nsa_prompt_kernel(q_ref, kslc_ref, kwin_ref, vslc_ref, vwin_ref, pool_ref, ga_ref, o_ref, sel_ref, acc_ref, *,
                       tq, tk, nb):
    i = pl.program_id(1)
    lanes = NSA_GROUP * tq
    kvh = NSA_KV_HEADS
    row = lax.broadcasted_iota(jnp.int32, (KV_PAIR, lanes), 0)
    pos1 = i * tq + lax.broadcasted_iota(jnp.int32, (1, tq), 1)
    pos = jnp.concatenate([pos1] * NSA_GROUP, axis=1)
    blk = lax.broadcasted_iota(jnp.int32, (nb, 1), 0)
    cmask = ((blk + 1) * NSA_BLOCK - 1) <= pos
    cur = pos1 // NSA_BLOCK
    valid = blk <= cur
    forced = (blk == 0) | (blk == cur) | (blk == cur - 1)
    n_sel = min(NSA_TOPK, nb)
    pooled = pool_ref[0]
    kc = pooled[:, :KV_PAIR].astype(BF16)
    vc = pooled[:, KV_PAIR:].astype(BF16)

    qpads, o_cmp = [], []
    for g in range(kvh):
        qt = q_ref[0, g * NSA_GROUP * HEAD_DIM:(g + 1) * NSA_GROUP * HEAD_DIM, :]
        q4 = jnp.concatenate([qt[hh * HEAD_DIM:(hh + 1) * HEAD_DIM] for hh in range(NSA_GROUP)], axis=1)
        qpad = jnp.where(row // HEAD_DIM == g, jnp.concatenate([q4, q4], axis=0), jnp.zeros((), BF16))
        qpads.append(qpad)
        s = jnp.where(cmask, _dot(kc, qpad), NEG_INF)
        e = jnp.exp2(s - jnp.max(s, axis=0, keepdims=True))
        p = jnp.where(cmask, e / jnp.sum(e, axis=0, keepdims=True), 0.0)
        o_cmp.append(_dot_tn(vc, p.astype(BF16))[g * HEAD_DIM:(g + 1) * HEAD_DIM])
        imp = p[:, 0:tq]
        for hh in range(1, NSA_GROUP):
            imp = imp + p[:, hh * tq:(hh + 1) * tq]
        score = jnp.where(valid, jnp.where(forced, FORCE_SCORE, imp), NEG_INF)
        for j in range(nb):
            sj = score[j:j + 1, :]
            beats = (score > sj) | ((score == sj) & (blk < j))
            cnt = jnp.sum(jnp.where(beats, 1.0, 0.0), axis=0, keepdims=True)
            sel_ref[g, j] = jnp.where(cnt < n_sel, 0.0, NEG_INF)

    bpt = tk // NSA_BLOCK

    def sel_bias(g, kt):
        rows = [jnp.broadcast_to(sel_ref[g, kt * bpt + r], (NSA_BLOCK, tq)) for r in range(bpt)]
        return jnp.concatenate(rows, axis=0)

    def step(kt, carry, near):
        ms, ls = list(carry[0]), list(carry[1])
        off = pl.multiple_of(kt * tk, tk)
        ks = kslc_ref[0, pl.ds(off, tk), :]
        chains = [(g, g, ks, vslc_ref) for g in range(kvh)]
        biases = [sel_bias(g, kt) for g in range(kvh)]
        if near:
            kw = kwin_ref[0, pl.ds(off, tk), :]
            d = pos1 - (off + lax.broadcasted_iota(jnp.int32, (tk, 1), 0))
            causal = jnp.where(d >= 0, 0.0, NEG_INF)
            window = jnp.where((d >= 0) & (d < NSA_WINDOW), 0.0, NEG_INF)
            chains += [(kvh + g, g, kw, vwin_ref) for g in range(kvh)]
            biases = [bs + causal for bs in biases] + [window] * kvh
        scs = [_dot(kk, qpads[g]) + jnp.concatenate([bs] * NSA_GROUP, axis=1)
               for (_, g, kk, _), bs in zip(chains, biases)]
        m_new = [jnp.maximum(ms[c], jnp.max(sc, axis=0, keepdims=True)) for (c, _, _, _), sc in zip(chains, scs)]
        pps = [jnp.exp2(sc - jnp.maximum(mn, 0.5 * NEG_INF)) for sc, mn in zip(scs, m_new)]
        pvs = [_dot(v_ref[0, g * HEAD_DIM:(g + 1) * HEAD_DIM, pl.ds(off, tk)], pp.astype(BF16))
               for (_, g, _, v_ref), pp in zip(chains, pps)]
        for (c, _, _, _), mn, pp, pv in zip(chains, m_new, pps, pvs):
            alpha = jnp.exp2(ms[c] - mn)
            acc_ref[c] = alpha * acc_ref[c] + pv
            ls[c] = alpha * ls[c] + jnp.sum(pp, axis=0, keepdims=True)
            ms[c] = mn
        return tuple(ms), tuple(ls)

    n_chain = 2 * kvh
    acc_ref[...] = jnp.zeros_like(acc_ref)
    carry = ((jnp.full((1, lanes), NEG_INF, F32),) * n_chain, (jnp.zeros((1, lanes), F32),) * n_chain)
    first = jnp.maximum(i * tq - (NSA_WINDOW - 1), 0) // tk
    carry = lax.fori_loop(0, first, lambda kt, cr: step(kt, cr, False), carry)
    _, ls = lax.fori_loop(first, (i + 1) * (tq // tk), lambda kt, cr: step(kt, cr, True), carry)

    ga = ga_ref[0]
    for g in range(kvh):
        o_br = [o_cmp[g]]
        for c in (g, kvh + g):
            l = ls[c]
            o_br.append(jnp.where(l > 0.0, acc_ref[c] / jnp.where(l > 0.0, l, 1.0), 0.0))
        for hh in range(NSA_GROUP):
            h = g * NSA_GROUP + hh
            sl = slice(hh * tq, (hh + 1) * tq)
            o = ga[h:h + 1] * o_br[0][:, sl]
            for br in range(1, NSA_BRANCHES):
                o = o + ga[br * NSA_HEADS + h:br * NSA_HEADS + h + 1] * o_br[br][:, sl]
            o_ref[0, h * HEAD_DIM:(h + 1) * HEAD_DIM, :] = o


def _nsa_prompt(qat, kb, vt, pooled, gat, *, batch, seq, tq=256, tk=128):
    assert tq % tk == 0 and seq % tq == 0 and tk % NSA_BLOCK == 0
    nb = seq // NSA_BLOCK
    nq = seq // tq
    return pl.pallas_call(
        functools.partial(_nsa_prompt_kernel, tq=tq, tk=tk, nb=nb),
        grid=(batch, nq),
        in_specs=[pl.BlockSpec((1, NSA_WIDTH, tq), lambda b, i: (b, 0, i)),
                  pl.BlockSpec((1, seq, KV_PAIR), lambda b, i: (b, 0, 0)),
                  pl.BlockSpec((1, seq, KV_PAIR), lambda b, i: (b, 0, 1)),
                  pl.BlockSpec((1, KV_PAIR, seq), lambda b, i: (b, 0, 0)),
                  pl.BlockSpec((1, KV_PAIR, seq), lambda b, i: (b, 1, 0)),
                  pl.BlockSpec((1, nb, 2 * KV_PAIR), lambda b, i: (b, 0, 0)),
                  pl.BlockSpec((1, LANES, tq), lambda b, i: (b, 0, i))],
        out_specs=pl.BlockSpec((1, NSA_WIDTH, tq), lambda b, i: (b, 0, i)),
        out_shape=jax.ShapeDtypeStruct((batch, NSA_WIDTH, seq), F32),
        scratch_shapes=[pltpu.VMEM((NSA_KV_HEADS, nb, 1, tq), F32),
                        pltpu.VMEM((2 * NSA_KV_HEADS, HEAD_DIM, NSA_GROUP * tq), F32)],
        compiler_params=pltpu.CompilerParams(dimension_semantics=("arbitrary",) * 2, vmem_limit_bytes=VMEM_LIMIT),
        name="nsa_prompt",
    )(qat, kb, kb, vt, vt, pooled, gat)


def _softplus(z):
    return jnp.maximum(z, 0.0) + jnp.log(1.0 + jnp.exp2(jnp.abs(z) * -LOG2E))


def _split_bf16(x):
    hi = x.astype(BF16)
    return hi, (x - hi.astype(F32)).astype(BF16)


def _mix_prompt_kernel(x_ref, oa_ref, za_ref, ob_ref, zb_ref, gm_ref, wat_ref, wbt_ref, wot_ref, g_ref, y_ref, *,
                       d_model, chain):
    cols = [slice(c, c + chain) for c in range(0, x_ref.shape[0], chain)]
    wat, wbt, wot = wat_ref[...], wbt_ref[...], wot_ref[...]
    ga = [(oa_ref[0, :, c] * za_ref[0, :, c]).astype(BF16) for c in cols]
    gb = [(ob_ref[0, :, c] * zb_ref[0, :, c]).astype(BF16) for c in cols]
    ya = [_dot(wat, v) for v in ga]
    yb = [_dot(wbt, v) for v in gb]
    mixed = [(gm_ref[0, :d_model, c] * a + gm_ref[0, d_model:, c] * b).astype(BF16) for c, a, b in zip(cols, ya, yb)]
    outs = [_dot(wot, v) for v in mixed]
    outs = [o * lax.rsqrt(jnp.mean(o * o, axis=0, keepdims=True) + RMS_EPS) for o in outs]
    for c, o in zip(cols, outs):
        y_ref[c, :] = x_ref[c, :] + o.T * g_ref[...]


def _mix_prompt(x2d, oat, zat, obt, zbt, gmt, wat, wbt, wot, g_post, *, batch, seq, tm, chain=256):
    m, d_model = x2d.shape
    assert seq % tm == 0 and tm % chain == 0
    nt = seq // tm
    row = lambda i: (i, 0)
    full = lambda i: (0, 0)
    tr = lambda i: (i // nt, 0, i % nt)
    return pl.pallas_call(
        functools.partial(_mix_prompt_kernel, d_model=d_model, chain=chain),
        grid=(m // tm,),
        in_specs=[pl.BlockSpec((tm, d_model), row), pl.BlockSpec((1, NSA_WIDTH, tm), tr),
                  pl.BlockSpec((1, NSA_WIDTH, tm), tr), pl.BlockSpec((1, SB_WIDTH, tm), tr),
                  pl.BlockSpec((1, SB_WIDTH, tm), tr), pl.BlockSpec((1, N_MERGE * d_model, tm), tr),
                  pl.BlockSpec((d_model, NSA_WIDTH), full), pl.BlockSpec((d_model, SB_WIDTH), full),
                  pl.BlockSpec((d_model, d_model), full), pl.BlockSpec((1, d_model), full)],
        out_specs=pl.BlockSpec((tm, d_model), row),
        out_shape=jax.ShapeDtypeStruct((m, d_model), F32),
        compiler_params=pltpu.CompilerParams(dimension_semantics=("arbitrary",), vmem_limit_bytes=VMEM_LIMIT),
        name="mix_prompt",
    )(x2d, oat, zat, obt, zbt, gmt, wat, wbt, wot, g_post)


def _mix_sample_kernel(x_ref, oa_ref, za_ref, ob_ref, zb_ref, gm_ref, wat_ref, wbt_ref, wot_ref, g_ref, y_ref, *,
                       d_model):
    ya = _dot_nt((oa_ref[...] * za_ref[...]).astype(BF16), wat_ref[...])
    yb = _dot_nt((ob_ref[...] * zb_ref[...]).astype(BF16), wbt_ref[...])
    gm = gm_ref[...]
    mixed = gm[:, :d_model] * ya + gm[:, d_model:] * yb
    out = _dot_nt(mixed.astype(BF16), wot_ref[...])
    y_ref[...] = x_ref[...] + _rms_scale(out, g_ref[...])


def _mix_sample(x2d, o_a, za, o_b, zb, gm, wat, wbt, wot, g_post):
    m, d_model = x2d.shape
    args = (x2d, o_a, za, o_b, zb, gm, wat, wbt, wot, g_post)
    return pl.pallas_call(
        functools.partial(_mix_sample_kernel, d_model=d_model),
        grid=(1,),
        in_specs=[pl.BlockSpec(a.shape, lambda i: (0, 0)) for a in args],
        out_specs=pl.BlockSpec((m, d_model), lambda i: (0, 0)),
        out_shape=jax.ShapeDtypeStruct((m, d_model), F32),
        compiler_params=pltpu.CompilerParams(dimension_semantics=("arbitrary",), vmem_limit_bytes=VMEM_LIMIT),
        name="mix_sample",
    )(*args)


def _head_pad(q):
    q2 = jnp.concatenate([q, q], axis=1)
    row = lax.broadcasted_iota(jnp.int32, q2.shape, 0)
    lane = lax.broadcasted_iota(jnp.int32, q2.shape, 1)
    return jnp.where(row // NSA_GROUP == lane // HEAD_DIM, q2, 0.0)


def _stream_kernel(pt_ref, qd_ref, qp_ref, k_ref, v_ref, qa_ref, wb_ref, *refs, ppg, tq, tk, sub, units,
                   steps_per_batch, nbp):
    page_refs = refs[:ppg]
    cmp_refs = refs[ppg:2 * ppg]
    od_ref, op_ref, ocmp_ref, sel_ref = refs[2 * ppg:2 * ppg + 4]
    drun_ref, dacc_ref, runs_ref, acc_ref, pool_ref, state_ref = refs[2 * ppg + 4:]
    s = pl.program_id(1)
    s_lin = pl.program_id(0) * pl.num_programs(1) + s

    @pl.when(s == 0)
    def _():
        drun_ref[...] = jnp.zeros_like(drun_ref)
        dacc_ref[...] = jnp.zeros_like(dacc_ref)

    @pl.when(s_lin == 0)
    def _():
        state_ref[0] = 0
        state_ref[1] = 0

    hrow = lax.broadcasted_iota(jnp.int32, (SB_HEADS, SB_WIDTH), 0)
    hlane = lax.broadcasted_iota(jnp.int32, (SB_HEADS, SB_WIDTH), 1)
    diag_blocks = hrow == hlane // HEAD_DIM

    def decode_scores():
        q = qd_ref[0]
        qbd = jnp.where(diag_blocks, jnp.concatenate([q] * SB_HEADS, axis=1), 0.0).astype(BF16)
        z = jnp.concatenate([_dot(qbd, pr[0, :SB_WIDTH, :].astype(BF16)) for pr in page_refs], axis=0)
        sp = _softplus(z)
        r = lax.broadcasted_iota(jnp.int32, (PAGE_SIZE, PAGE_SIZE), 0)
        c = lax.broadcasted_iota(jnp.int32, (PAGE_SIZE, PAGE_SIZE), 1)
        lower = jnp.where(r > c, 1.0, 0.0).astype(BF16)
        hi, lo = _split_bf16(sp)
        return z, sp, _dot(hi, lower) + _dot(lo, lower)

    def decode_update(z, sp, drop):
        tot = jnp.sum(sp, axis=1, keepdims=True)
        run = drun_ref[...]
        dacc = dacc_ref[...]
        for n in range(ppg):
            sl = slice(n * SB_HEADS, (n + 1) * SB_HEADS)
            a = jnp.exp(z[sl] - sp[sl] - drop[sl] - run)
            dacc = dacc + _dot_nt(a.astype(BF16), page_refs[n][0, SB_WIDTH:, :].astype(BF16))
            run = run + tot[sl]
        drun_ref[...] = run
        dacc_ref[...] = dacc

    def pool_compressed():
        group = LANES // (ppg * BLOCKS_PER_PAGE)
        slot = s % group
        chunk = s // group
        xk = jnp.concatenate([pr[0, :KV_PAIR, :].astype(BF16) for pr in cmp_refs], axis=1)
        xv = jnp.concatenate([pr[0, KV_PAIR:, :].astype(BF16) for pr in cmp_refs], axis=1)
        part = jnp.concatenate([_dot(xk, wb_ref[0, slot]), _dot(xv, wb_ref[1, slot])], axis=0)
        pool_ref[chunk] = jnp.where(slot == 0, 0.0, pool_ref[chunk]) + part

    pairs = SB_HEADS // 2
    lanes = 2 * tq
    diag = tq // tk
    per_batch = diag * units * (units + 1) // 2
    local = s_lin % steps_per_batch
    n_steps = ((local + 1) * per_batch) // steps_per_batch - (local * per_batch) // steps_per_batch

    ur = lax.broadcasted_iota(jnp.int32, (sub, 2 * sub), 0)
    uc = lax.broadcasted_iota(jnp.int32, (sub, 2 * sub), 1) % sub
    upper2 = jnp.where(uc > ur, 1.0, 0.0).astype(BF16)
    z0 = jnp.zeros((HEAD_DIM, tq), BF16)

    def later_sum(sp, run):
        parts = []
        for hb in reversed(range(tk // sub)):
            blk = sp[hb * sub:(hb + 1) * sub]
            parts.append(_dot(upper2, jnp.concatenate(_split_bf16(blk), axis=0)) + run)
            run = run + jnp.sum(blk, axis=0, keepdims=True)
        return jnp.concatenate(parts[::-1], axis=0), run

    def tile_step(with_decode):
        i = state_ref[0]
        j = state_ref[1]
        n_t = (i + 1) * diag
        off = pl.multiple_of((n_t - 1 - j) * tk, tk)
        qoff = pl.multiple_of(i * tq, tq)

        @pl.when(j == 0)
        def _():
            runs_ref[...] = jnp.zeros_like(runs_ref)
            acc_ref[...] = jnp.zeros_like(acc_ref)

        def compute(masked):
            if with_decode:
                dz, dsp, ddrop = decode_scores()
                pool_compressed()
            kk = k_ref[0, pl.ds(off, tk), :]
            qpads = []
            for jp in range(pairs):
                qt = qp_ref[0, jp * LANES:(jp + 1) * LANES, pl.ds(qoff, tq)]
                qpads.append(jnp.concatenate([jnp.concatenate([qt[:HEAD_DIM], z0], axis=0),
                                              jnp.concatenate([z0, qt[HEAD_DIM:]], axis=0)], axis=1))
            if masked:
                pos = qoff + lax.broadcasted_iota(jnp.int32, (1, lanes), 1) % tq
                mask = (off + lax.broadcasted_iota(jnp.int32, (tk, 1), 0)) < pos
            zs = [_dot(kk[:, jp * LANES:(jp + 1) * LANES], qpads[jp]) for jp in range(pairs)]
            sps = [jnp.where(mask, _softplus(zz), 0.0) if masked else _softplus(zz) for zz in zs]
            later = [later_sum(spj, runs_ref[jp]) for jp, spj in enumerate(sps)]
            drops = [d for d, _ in later]
            if with_decode:
                decode_update(dz, dsp, ddrop)
            ws = [jnp.exp(zz - spj - dr) for zz, spj, dr in zip(zs, sps, drops)]
            if masked:
                ws = [jnp.where(mask, w, 0.0) for w in ws]
            for h in range(SB_HEADS):
                w = ws[h // 2][:, (h % 2) * tq:(h % 2 + 1) * tq].astype(BF16)
                acc_ref[h] = acc_ref[h] + _dot(v_ref[0, h * HEAD_DIM:(h + 1) * HEAD_DIM, pl.ds(off, tk)], w)
            for jp, (_, run) in enumerate(later):
                runs_ref[jp] = run

        @pl.when(j < diag)
        def _():
            compute(True)

        @pl.when(j >= diag)
        def _():
            compute(False)

        last = j == n_t - 1

        @pl.when(last)
        def _():
            for h in range(SB_HEADS):
                op_ref[0, h * HEAD_DIM:(h + 1) * HEAD_DIM, pl.ds(qoff, tq)] = acc_ref[h]

        state_ref[1] = jnp.where(last, 0, j + 1)
        state_ref[0] = jnp.where(last, (i + 1) % units, i)

    @pl.when(n_steps > 0)
    def _():
        tile_step(True)

    @pl.when(n_steps == 0)
    def _():
        decode_update(*decode_scores())
        pool_compressed()

    def rest(_, carry):
        tile_step(False)
        return carry

    lax.fori_loop(1, n_steps, rest, 0)

    @pl.when(s == pl.num_programs(1) - 1)
    def _():
        od_ref[0] = jnp.sum(jnp.where(diag_blocks, dacc_ref[...], 0.0), axis=0, keepdims=True)
        _compressed_decode(qa_ref[0], pool_ref, ocmp_ref, sel_ref, nbp)


def _stream_attention(page_table_flat, q_dec, sb_cache_t, qbt, kbb, vbt, qa_dec, w_cmp, nsa_cache_t, *, dec_batch,
                      n_pages, batch, seq, ppg, tq=256, tk=256, sub=128):
    steps = n_pages // ppg
    assert tq % tk == 0 and tk % sub == 0 and seq % tq == 0 and (dec_batch * steps) % batch == 0
    assert LANES % (ppg * BLOCKS_PER_PAGE) == 0
    spb = dec_batch * steps // batch
    nbp = n_pages * BLOCKS_PER_PAGE
    wb = _decode_pool_weights(w_cmp, ppg)

    def page_map(n):
        return lambda b, s, pt: (pt[b * n_pages + n_pages - 1 - (s * ppg + n)], 0, 0)

    def cmp_map(n):
        return lambda b, s, pt: (pt[b * n_pages + s * ppg + n], 0, 0)

    prow = lambda b, s, pt: ((b * steps + s) // spb, 0, 0)
    sample = lambda b, s, pt: (b, 0, 0)
    grid_spec = pltpu.PrefetchScalarGridSpec(
        num_scalar_prefetch=1, grid=(dec_batch, steps),
        in_specs=[pl.BlockSpec((1, SB_HEADS, HEAD_DIM), sample),
                  pl.BlockSpec((1, SB_WIDTH, seq), prow), pl.BlockSpec((1, seq, SB_WIDTH), prow),
                  pl.BlockSpec((1, SB_WIDTH, seq), prow),
                  pl.BlockSpec((1, NSA_HEADS, HEAD_DIM), sample),
                  pl.BlockSpec(wb.shape, lambda b, s, pt: (0, 0, 0, 0))]
        + [pl.BlockSpec((1, 2 * SB_WIDTH, PAGE_SIZE), page_map(n)) for n in range(ppg)]
        + [pl.BlockSpec((1, 2 * KV_PAIR, PAGE_SIZE), cmp_map(n)) for n in range(ppg)],
        out_specs=[pl.BlockSpec((1, 1, SB_WIDTH), sample), pl.BlockSpec((1, SB_WIDTH, seq), prow),
                   pl.BlockSpec((1, NSA_HEADS, HEAD_DIM), sample),
                   pl.BlockSpec((1, NSA_KV_HEADS, NSA_TOPK, LANES), lambda b, s, pt: (b, 0, 0, 0))],
        scratch_shapes=[pltpu.VMEM((SB_HEADS, 1), F32), pltpu.VMEM((SB_HEADS, SB_WIDTH), F32),
                        pltpu.VMEM((SB_HEADS // 2, 1, 2 * tq), F32), pltpu.VMEM((SB_HEADS, HEAD_DIM, tq), F32),
                        pltpu.VMEM((pl.cdiv(nbp, LANES), 2 * KV_PAIR, LANES), F32),
                        pltpu.SMEM((2,), jnp.int32)])
    return pl.pallas_call(
        functools.partial(_stream_kernel, ppg=ppg, tq=tq, tk=tk, sub=sub, units=seq // tq, steps_per_batch=spb,
                          nbp=nbp),
        grid_spec=grid_spec,
        out_shape=[jax.ShapeDtypeStruct((dec_batch, 1, SB_WIDTH), F32),
                   jax.ShapeDtypeStruct((batch, SB_WIDTH, seq), F32),
                   jax.ShapeDtypeStruct((dec_batch, NSA_HEADS, HEAD_DIM), F32),
                   jax.ShapeDtypeStruct((dec_batch, NSA_KV_HEADS, NSA_TOPK, LANES), jnp.int32)],
        compiler_params=pltpu.CompilerParams(dimension_semantics=("arbitrary", "arbitrary"),
                                             vmem_limit_bytes=VMEM_LIMIT),
        name="stream_attention",
    )(page_table_flat, q_dec, qbt, kbb, vbt, qa_dec, wb, *([sb_cache_t] * ppg), *([nsa_cache_t] * ppg))


def _compressed_decode(q, pool_ref, ocmp_ref, sel_ref, nbp):
    n_chunks = pool_ref.shape[0]
    nl = n_chunks * LANES
    qpad = _head_pad(q).astype(BF16)
    pooled = jnp.concatenate([pool_ref[ch] for ch in range(n_chunks)], axis=1)
    blk_l = lax.broadcasted_iota(jnp.int32, (1, nl), 1)
    live = blk_l < nbp
    sc = jnp.where(live, _dot(qpad, pooled[:KV_PAIR].astype(BF16)), NEG_INF)
    e = jnp.where(live, jnp.exp(sc - jnp.max(sc, axis=1, keepdims=True)), 0.0)
    p = e / jnp.sum(e, axis=1, keepdims=True)
    o = _dot_nt(p.astype(BF16), pooled[KV_PAIR:].astype(BF16))
    hrow = lax.broadcasted_iota(jnp.int32, o.shape, 0)
    ocmp_ref[0] = jnp.where(hrow < NSA_GROUP, o, pltpu.roll(o, HEAD_DIM, 1))[:, :HEAD_DIM]

    blk_s = lax.broadcasted_iota(jnp.int32, (nl, 1), 0)
    eye = lax.broadcasted_iota(jnp.int32, (nl, nl), 0) == lax.broadcasted_iota(jnp.int32, (nl, nl), 1)
    n_sel = NSA_TOPK - 1
    kslot = lax.broadcasted_iota(jnp.int32, (NSA_TOPK, nl), 0)
    for gi in range(NSA_KV_HEADS):
        imp = jnp.sum(p[gi * NSA_GROUP:(gi + 1) * NSA_GROUP], axis=0, keepdims=True)
        forced = (blk_l == 0) | (blk_l == nbp - 1)
        srow = jnp.where(live, jnp.where(forced, FORCE_SCORE, imp), NEG_INF)
        scol = jnp.sum(jnp.where(eye, srow, 0.0), axis=1, keepdims=True)
        beats = (scol > srow) | ((scol == srow) & (blk_s < blk_l))
        rank = jnp.sum(jnp.where(beats, 1.0, 0.0), axis=0, keepdims=True)
        sel = rank < n_sel
        selcol = jnp.sum(jnp.where(eye & sel, 1.0, 0.0), axis=1, keepdims=True) > 0.5
        rank_sel = jnp.sum(jnp.where(selcol & (blk_s < blk_l), 1.0, 0.0), axis=0, keepdims=True)
        onehot = sel & (rank_sel.astype(jnp.int32) == kslot)
        idx = jnp.sum(jnp.where(onehot, blk_l, 0), axis=1, keepdims=True)
        sel_ref[0, gi] = jnp.broadcast_to(idx, (NSA_TOPK, LANES))


def _decode_pool_weights(w_cmp, ppg):
    per_step = ppg * BLOCKS_PER_PAGE
    group = LANES // per_step
    k = jnp.arange(ppg * PAGE_SIZE)
    target = (k // PAGE_SIZE) * BLOCKS_PER_PAGE + (k % PAGE_SIZE) // NSA_BLOCK
    hit = jnp.arange(LANES)[None, None, :] == (jnp.arange(group)[:, None, None] * per_step + target[None, :, None])
    w_row = jnp.tile(w_cmp, (1, ppg * BLOCKS_PER_PAGE))
    return jnp.where(hit[None], w_row[:, None, :, None], 0.0).astype(BF16)


def _nsa_sel_decode_kernel(sel_ref, pt_ref, q_ref, ocmp_ref, ga_ref, new_ref, winp_ref, *refs, n_blk):
    blk_refs = refs[:NSA_KV_HEADS * n_blk]
    o_ref, wino_ref = refs[NSA_KV_HEADS * n_blk:]
    b = pl.program_id(0)
    qb = _head_pad(q_ref[0]).astype(BF16)
    new = new_ref[0]
    hrow = lax.broadcasted_iota(jnp.int32, (NSA_HEADS, KV_PAIR), 0)
    top = hrow < NSA_GROUP

    def fold(o):
        return jnp.where(top, o, pltpu.roll(o, HEAD_DIM, 1))[:, :HEAD_DIM]

    def softmax(sc, mask):
        sc = jnp.where(mask, sc, NEG_INF)
        e = jnp.where(mask, jnp.exp(sc - jnp.max(sc, axis=1, keepdims=True)), 0.0)
        return e / jnp.sum(e, axis=1, keepdims=True)

    lane = lax.broadcasted_iota(jnp.int32, (1, PAGE_SIZE), 1)
    first = lax.broadcasted_iota(jnp.int32, (KV_PAIR, PAGE_SIZE), 1) == 0
    k_new = jnp.where(first, new[2 * KV_PAIR:3 * KV_PAIR], 0.0).astype(BF16)
    v_new = jnp.where(first, new[3 * KV_PAIR:4 * KV_PAIR], 0.0).astype(BF16)
    o_g = []
    for gi in range(NSA_KV_HEADS):
        scs, masks = [], []
        for n in range(n_blk):
            half = sel_ref[(b * NSA_KV_HEADS + gi) * NSA_TOPK + n] % BLOCKS_PER_PAGE
            scs.append(_dot(qb, blk_refs[gi * n_blk + n][0, :KV_PAIR, :].astype(BF16)))
            masks.append(lane // NSA_BLOCK == half)
        scs.append(_dot(qb, k_new))
        masks.append(lane == 0)
        p = softmax(jnp.concatenate(scs, axis=1), jnp.concatenate(masks, axis=1)).astype(BF16)
        o = _dot_nt(p[:, n_blk * PAGE_SIZE:], v_new)
        for n in range(n_blk):
            o = o + _dot_nt(p[:, n * PAGE_SIZE:(n + 1) * PAGE_SIZE],
                            blk_refs[gi * n_blk + n][0, KV_PAIR:, :].astype(BF16))
        o_g.append(o)
    o_slc = fold(jnp.where(top, o_g[0], o_g[1]))

    wp = winp_ref[0]
    w = wp.shape[1]
    wl = lax.broadcasted_iota(jnp.int32, wp.shape, 1)
    shifted = jnp.where(wl == w - 1, new[4 * KV_PAIR:], pltpu.roll(wp, w - 1, 1))
    wino_ref[0] = shifted
    sc = _dot(qb, shifted[:KV_PAIR].astype(BF16))
    p = softmax(sc, jnp.full((1, w), True)).astype(BF16)
    o_win = fold(_dot_nt(p, shifted[KV_PAIR:].astype(BF16)))

    ga = ga_ref[0]
    o_ref[0] = ga[0] * ocmp_ref[0] + ga[1] * o_slc + ga[2] * o_win


def _nsa_sel_decode(sel_flat, page_table_flat, q_a, o_cmp, ga, new_kv, win_past_t, cache_t, *, dec_batch, n_pages):
    n_blk = NSA_TOPK - 1
    w = win_past_t.shape[2]

    def blk_map(gi, n):
        def f(b, sel, pt):
            blk = sel[(b * NSA_KV_HEADS + gi) * NSA_TOPK + n]
            return (pt[b * n_pages + blk // BLOCKS_PER_PAGE], 1, 0)
        return f

    grid_spec = pltpu.PrefetchScalarGridSpec(
        num_scalar_prefetch=2, grid=(dec_batch,),
        in_specs=[pl.BlockSpec((1, NSA_HEADS, HEAD_DIM), lambda b, sel, pt: (b, 0, 0)),
                  pl.BlockSpec((1, NSA_HEADS, HEAD_DIM), lambda b, sel, pt: (b, 0, 0)),
                  pl.BlockSpec((1, NSA_BRANCHES, NSA_HEADS, 1), lambda b, sel, pt: (b, 0, 0, 0)),
                  pl.BlockSpec((1, KVA_WIDTH, 1), lambda b, sel, pt: (b, 0, 0)),
                  pl.BlockSpec((1, 2 * KV_PAIR, w), lambda b, sel, pt: (b, 0, 0))]
        + [pl.BlockSpec((1, 2 * KV_PAIR, PAGE_SIZE), blk_map(gi, n))
           for gi in range(NSA_KV_HEADS) for n in range(n_blk)],
        out_specs=[pl.BlockSpec((1, NSA_HEADS, HEAD_DIM), lambda b, sel, pt: (b, 0, 0)),
                   pl.BlockSpec((1, 2 * KV_PAIR, w), lambda b, sel, pt: (b, 0, 0))])
    return pl.pallas_call(
        functools.partial(_nsa_sel_decode_kernel, n_blk=n_blk),
        grid_spec=grid_spec,
        out_shape=[jax.ShapeDtypeStruct((dec_batch, NSA_HEADS, HEAD_DIM), F32),
                   jax.ShapeDtypeStruct((dec_batch, 2 * KV_PAIR, w), F32)],
        compiler_params=pltpu.CompilerParams(dimension_semantics=("arbitrary",), vmem_limit_bytes=VMEM_LIMIT),
        name="nsa_sel_decode",
    )(sel_flat, page_table_flat, q_a, o_cmp, ga, new_kv, win_past_t, *([cache_t] * (NSA_KV_HEADS * n_blk)))


def _feature_major(a, lead):
    nl = len(lead)
    t = jnp.moveaxis(a, nl, -1)
    return t.reshape(lead + (-1, a.shape[nl]))


def _time_major(a_t, feat_shape):
    lead, _, time = a_t.shape
    return jnp.moveaxis(a_t.reshape((lead,) + feat_shape + (time,)), -1, 1)


def _layer(x_prompt, x_sample, nsa_cache, win_cache, sb_cache, page_table, w_in, w_cmp, w_a, w_b, w_o, g_pre, g_post):
    batch, seq, d_model = x_prompt.shape
    dec_batch, dec_seq, _ = x_sample.shape
    n_pages = page_table.shape[1]
    past_len = n_pages * PAGE_SIZE
    n_phys = nsa_cache.shape[0]
    tm, ppg = 256, 16
    assert dec_seq == 1 and seq % tm == 0 and seq >= NSA_WINDOW and past_len >= NSA_WINDOW
    assert n_pages % ppg == 0 and past_len // NSA_BLOCK >= NSA_TOPK
    assert win_cache.shape[1] == NSA_WINDOW and nsa_cache.shape[1] == PAGE_SIZE

    wt_perm = _permute_w_in_t(w_in, d_model)
    wat, wbt, wot = w_a.T.astype(BF16), w_b.T.astype(BF16), w_o.T.astype(BF16)
    g_pre2, g_post2 = g_pre.reshape(1, d_model), g_post.reshape(1, d_model)
    nsa_feat = (4, NSA_KV_HEADS, HEAD_DIM)
    win_feat = (2, NSA_KV_HEADS, HEAD_DIM)
    sb_feat = (2, SB_HEADS, HEAD_DIM)

    xp = x_prompt.reshape(batch * seq, d_model)
    cos, sin = _rope_angles(jnp.arange(seq, dtype=jnp.int32))
    wp_prompt = _pool_weights(w_cmp, 1, tm // NSA_BLOCK, SUBLANES).reshape(2 * SUBLANES, tm)
    (nsa_t, win_t, sb_t, za_t, zb_t, gm_t, qa_t, ga_t, kb, v_t, pooled, qb_t, kbb, vb_t) = _project_prompt(
        xp, g_pre2, wt_perm, cos.T, sin.T, wp_prompt, batch=batch, seq=seq, tm=tm)
    xs = x_sample.reshape(dec_batch, d_model)
    tabs_s = _rope_lane_tables(jnp.full((dec_batch,), past_len, jnp.int32))
    (nsa_s, win_s, sb_s, za_s, zb_s, gm_s, qa_s, ga_s, qb_s) = _project_sample(xs, g_pre2, wt_perm, tabs_s)
    pt_flat = page_table.reshape(-1)
    sb_cache_t = _feature_major(sb_cache, (n_phys,))
    nsa_cache_t = _feature_major(nsa_cache, (n_phys,))
    win_cache_t = _feature_major(win_cache, (dec_batch,))

    qa3 = qa_s.reshape(dec_batch, NSA_HEADS, HEAD_DIM)
    o_b_s, ob_t, o_cmp_s, sel = _stream_attention(
        pt_flat, qb_s.reshape(dec_batch, SB_HEADS, HEAD_DIM), sb_cache_t, qb_t, kbb.reshape(batch, seq, SB_WIDTH), vb_t,
        qa3, w_cmp, nsa_cache_t, dec_batch=dec_batch, n_pages=n_pages, batch=batch, seq=seq, ppg=ppg)

    oa_t = _nsa_prompt(qa_t, kb.reshape(batch, seq, 2 * KV_PAIR), v_t,
                       pooled.reshape(batch, seq // NSA_BLOCK, 2 * KV_PAIR), ga_t, batch=batch, seq=seq)
    y_prompt = _mix_prompt(xp, oa_t, za_t, ob_t, zb_t, gm_t, wat, wbt, wot, g_post2,
                           batch=batch, seq=seq, tm=2 * tm).reshape(batch, seq, d_model)
    nsa_kv_prompt = _time_major(nsa_t, nsa_feat)
    win_kv_prompt = _time_major(win_t[:, :, seq - NSA_WINDOW:], win_feat)
    sb_kv_prompt = _time_major(sb_t, sb_feat)

    ga3 = ga_s[:, :GATE_A].reshape(dec_batch, NSA_BRANCHES, NSA_HEADS, 1)
    new_kv = jnp.concatenate([nsa_s, win_s], axis=1).reshape(dec_batch, KVA_WIDTH, 1)
    o_a_s, win_out_t = _nsa_sel_decode(sel[:, :, :, 0].reshape(-1), pt_flat, qa3, o_cmp_s, ga3, new_kv,
                                       win_cache_t, nsa_cache_t, dec_batch=dec_batch, n_pages=n_pages)
    y_sample = _mix_sample(xs, o_a_s.reshape(dec_batch, NSA_WIDTH), za_s, o_b_s.reshape(dec_batch, SB_WIDTH), zb_s,
                           gm_s, wat, wbt, wot, g_post2).reshape(dec_batch, 1, d_model)
    nsa_kv_sample = nsa_s.reshape((dec_batch, 1) + nsa_feat)
    win_kv_sample = _time_major(win_out_t, win_feat)
    sb_kv_sample = sb_s.reshape((dec_batch, 1) + sb_feat)
    return (y_prompt, y_sample, nsa_kv_prompt, win_kv_prompt, sb_kv_prompt, nsa_kv_sample, win_kv_sample,
            sb_kv_sample)


def kernel(x_prompt, x_sample, cache_nsa_kv, cache_nsa_win_kv, cache_sb_kv, page_table, w_in, w_cmp, w_branch_a,
           w_branch_b, w_out, g_pre, g_post):
    hp, hs = x_prompt, x_sample
    caches = [[] for _ in range(6)]
    for layer in range(w_in.shape[0]):
        outs = _layer(hp, hs, cache_nsa_kv[layer], cache_nsa_win_kv[layer], cache_sb_kv[layer], page_table,
                      w_in[layer], w_cmp[layer], w_branch_a[layer], w_branch_b[layer], w_out[layer],
                      g_pre[layer], g_post[layer])
        hp, hs = outs[0], outs[1]
        for acc, o in zip(caches, outs[2:]):
            acc.append(o)
    return (hp, hs) + tuple(jnp.stack(c) for c in caches)
```

```python
import functools

import jax
import jax.numpy as jnp
from jax import lax
from jax.experimental import pallas as pl
from jax.experimental.pallas import tpu as pltpu

HEAD_DIM = 64
ROT_DIM = HEAD_DIM // 4
ROPE_THETA = 500000.0
NSA_HEADS = 8
NSA_KV_HEADS = 2
NSA_GROUP = NSA_HEADS // NSA_KV_HEADS
NSA_BRANCHES = 3
NSA_BLOCK = 64
NSA_TOPK = 16
NSA_WINDOW = 512
NSA_WIDTH = NSA_HEADS * HEAD_DIM
SB_HEADS = 8
SB_WIDTH = SB_HEADS * HEAD_DIM
N_MERGE = 2
PAGE_SIZE = 128
RMS_EPS = 1e-6
NEG_INF = -1e30
FORCE_SCORE = 1e3
SCALE = HEAD_DIM ** -0.5
LOG2E = 1.4426950408889634

LANES = 128
SUBLANES = 8
KV_PAIR = NSA_KV_HEADS * HEAD_DIM
assert KV_PAIR == LANES and PAGE_SIZE == LANES and PAGE_SIZE % NSA_BLOCK == 0
KVA_WIDTH = 2 * NSA_BRANCHES * KV_PAIR
GATE_A = NSA_BRANCHES * NSA_HEADS
BLOCKS_PER_PAGE = PAGE_SIZE // NSA_BLOCK
VMEM_LIMIT = 56 * 1024 * 1024

C_QA = 0
C_KVA = C_QA + NSA_WIDTH
C_ZA = C_KVA + KVA_WIDTH
C_QKVB = C_ZA + NSA_WIDTH
C_ZB = C_QKVB + 3 * SB_WIDTH
C_GM = C_ZB + SB_WIDTH

BF16 = jnp.bfloat16
F32 = jnp.float32


def _dot(a, b):
    return jnp.dot(a, b, preferred_element_type=F32)


def _dot_tn(a, b):
    return lax.dot_general(a, b, (((0,), (0,)), ((), ())), preferred_element_type=F32)


def _dot_nt(a, b):
    return lax.dot_general(a, b, (((1,), (1,)), ((), ())), preferred_element_type=F32)


def _sigmoid(x):
    return 1.0 / (1.0 + jnp.exp(-x))


def _rms_scale(x, g):
    return x * lax.rsqrt(jnp.mean(x * x, axis=-1, keepdims=True) + RMS_EPS) * g


def _rope_rows(v, cos, sin):
    half = ROT_DIM // 2
    parts = []
    for base in range(0, v.shape[0], HEAD_DIM):
        x1, x2 = v[base:base + half], v[base + half:base + 2 * half]
        parts += [x1 * cos - x2 * sin, x2 * cos + x1 * sin, v[base + 2 * half:base + HEAD_DIM]]
    return jnp.concatenate(parts, axis=0)


def _proj_prompt_kernel(x_ref, g_ref, wt_ref, cos_ref, sin_ref, wp_ref,
                        nsa_ref, win_ref, sb_ref, za_ref, zb_ref, gm_ref, qa_ref, ga_ref, kb_ref, vt_ref,
                        pool_ref, qb_ref, kbb_ref, vbt_ref, *, d_model):
    hb = _rms_scale(x_ref[...], g_ref[...]).astype(BF16)
    cos, sin = cos_ref[...], sin_ref[...]

    def seg(lo, width):
        return _dot_nt(wt_ref[lo:lo + width, :], hb)

    qa_ref[0] = (_rope_rows(seg(C_QA, NSA_WIDTH), cos, sin) * (SCALE * LOG2E)).astype(BF16)
    kv = seg(C_KVA, KVA_WIDTH)
    cmp_k = _rope_rows(kv[0 * LANES:1 * LANES], cos, sin)
    cmp_v = kv[1 * LANES:2 * LANES]
    slc_k = _rope_rows(kv[2 * LANES:3 * LANES], cos, sin)
    slc_v = kv[3 * LANES:4 * LANES]
    win_k = _rope_rows(kv[4 * LANES:5 * LANES], cos, sin)
    win_v = kv[5 * LANES:6 * LANES]
    nsa_ref[0] = jnp.concatenate([cmp_k, cmp_v, slc_k, slc_v], axis=0)
    win_ref[0] = jnp.concatenate([win_k, win_v], axis=0)
    kb_ref[...] = jnp.concatenate([slc_k.T, win_k.T], axis=1).astype(BF16)
    vt_ref[0] = jnp.concatenate([slc_v, win_v], axis=0).astype(BF16)
    wp = wp_ref[...]
    nblk = pool_ref.shape[1]
    pooled = jnp.concatenate([_dot_nt(wp[:SUBLANES], cmp_k.astype(BF16)),
                              _dot_nt(wp[SUBLANES:], cmp_v.astype(BF16))], axis=1)
    pool_ref[0] = pooled[:nblk]

    za = seg(C_ZA, NSA_WIDTH)
    za_ref[0] = (za * _sigmoid(za)).astype(za_ref.dtype)
    qkvb = seg(C_QKVB, 3 * SB_WIDTH)
    qb_ref[0] = (qkvb[:SB_WIDTH] * SCALE).astype(BF16)
    sb_ref[0] = qkvb[SB_WIDTH:]
    kbb_ref[...] = qkvb[SB_WIDTH:2 * SB_WIDTH].T.astype(BF16)
    vbt_ref[0] = qkvb[2 * SB_WIDTH:].astype(BF16)
    zb = seg(C_ZB, SB_WIDTH)
    zb_ref[0] = (zb * _sigmoid(zb)).astype(zb_ref.dtype)
    gm_ref[0] = _sigmoid(seg(C_GM, N_MERGE * d_model)).astype(gm_ref.dtype)
    ga_ref[0] = _sigmoid(seg(C_GM + N_MERGE * d_model, LANES))


def _project_prompt(x2d, g_pre, wt_perm, cos_t, sin_t, wp, *, batch, seq, tm):
    m, d_model = x2d.shape
    nt = seq // tm
    n_rows = wt_perm.shape[0]
    nblk = tm // NSA_BLOCK
    row = lambda i: (i, 0)
    full = lambda i: (0, 0)
    tab = lambda i: (0, i % nt)
    tr = lambda i: (i // nt, 0, i % nt)
    half = ROT_DIM // 2
    in_specs = [pl.BlockSpec((tm, d_model), row), pl.BlockSpec((1, d_model), full),
                pl.BlockSpec((n_rows, d_model), full),
                pl.BlockSpec((half, tm), tab), pl.BlockSpec((half, tm), tab),
                pl.BlockSpec((2 * SUBLANES, tm), full)]
    sds = jax.ShapeDtypeStruct

    def feat(width, dtype):
        return sds((batch, width, seq), dtype), pl.BlockSpec((1, width, tm), tr)

    def rows(width, dtype):
        return sds((m, width), dtype), pl.BlockSpec((tm, width), row)

    outs = [feat(4 * KV_PAIR, F32), feat(2 * KV_PAIR, F32), feat(2 * SB_WIDTH, F32),
            feat(NSA_WIDTH, BF16), feat(SB_WIDTH, BF16), feat(N_MERGE * d_model, BF16),
            feat(NSA_WIDTH, BF16), feat(LANES, F32), rows(2 * KV_PAIR, BF16), feat(2 * KV_PAIR, BF16),
            (sds((m // tm, nblk, 2 * KV_PAIR), F32), pl.BlockSpec((1, nblk, 2 * KV_PAIR), lambda i: (i, 0, 0))),
            feat(SB_WIDTH, BF16), rows(SB_WIDTH, BF16), feat(SB_WIDTH, BF16)]
    return pl.pallas_call(
        functools.partial(_proj_prompt_kernel, d_model=d_model),
        grid=(m // tm,), in_specs=in_specs, out_specs=[o[1] for o in outs], out_shape=[o[0] for o in outs],
        compiler_params=pltpu.CompilerParams(dimension_semantics=("arbitrary",), vmem_limit_bytes=VMEM_LIMIT),
        name="proj_prompt",
    )(x2d, g_pre, wt_perm, cos_t, sin_t, wp)


def _rope_lanes(v, c, s_up, s_dn):
    half = ROT_DIM // 2
    outs = []
    for j in range(v.shape[1] // LANES):
        blk = v[:, j * LANES:(j + 1) * LANES]
        outs.append(blk * c + pltpu.roll(blk, LANES - half, 1) * s_up + pltpu.roll(blk, half, 1) * s_dn)
    return outs[0] if len(outs) == 1 else jnp.concatenate(outs, axis=1)


def _proj_sample_kernel(x_ref, g_ref, wt_ref, cos_ref, sup_ref, sdn_ref,
                        nsa_ref, win_ref, sbkv_ref, za_ref, zb_ref, gm_ref, qa_ref, ga_ref, qb_ref, *, d_model):
    hb = _rms_scale(x_ref[...], g_ref[...]).astype(BF16)
    cos, s_up, s_dn = cos_ref[...], sup_ref[...], sdn_ref[...]

    def seg(lo, width):
        return _dot_nt(hb, wt_ref[lo:lo + width, :])

    qa_ref[...] = _rope_lanes(seg(C_QA, NSA_WIDTH), cos, s_up, s_dn) * SCALE
    kv = seg(C_KVA, KVA_WIDTH)
    parts = []
    for j in range(2 * NSA_BRANCHES):
        blk = kv[:, j * LANES:(j + 1) * LANES]
        parts.append(_rope_lanes(blk, cos, s_up, s_dn) if j % 2 == 0 else blk)
    nsa_ref[...] = jnp.concatenate(parts[:4], axis=1)
    win_ref[...] = jnp.concatenate(parts[4:], axis=1)
    za = seg(C_ZA, NSA_WIDTH)
    za_ref[...] = za * _sigmoid(za)
    qkvb = seg(C_QKVB, 3 * SB_WIDTH)
    qb_ref[...] = qkvb[:, :SB_WIDTH] * SCALE
    sbkv_ref[...] = qkvb[:, SB_WIDTH:]
    zb = seg(C_ZB, SB_WIDTH)
    zb_ref[...] = zb * _sigmoid(zb)
    gm_ref[...] = _sigmoid(seg(C_GM, N_MERGE * d_model))
    ga_ref[...] = _sigmoid(seg(C_GM + N_MERGE * d_model, LANES))


def _project_sample(x2d, g_pre, wt_perm, tables):
    m, d_model = x2d.shape
    full = lambda i: (0, 0)
    widths = [4 * KV_PAIR, 2 * KV_PAIR, 2 * SB_WIDTH, NSA_WIDTH, SB_WIDTH, N_MERGE * d_model,
              NSA_WIDTH, LANES, SB_WIDTH]
    return pl.pallas_call(
        functools.partial(_proj_sample_kernel, d_model=d_model),
        grid=(1,),
        in_specs=[pl.BlockSpec((m, d_model), full), pl.BlockSpec((1, d_model), full),
                  pl.BlockSpec(wt_perm.shape, full)] + [pl.BlockSpec((m, LANES), full)] * 3,
        out_specs=[pl.BlockSpec((m, w), full) for w in widths],
        out_shape=[jax.ShapeDtypeStruct((m, w), F32) for w in widths],
        compiler_params=pltpu.CompilerParams(dimension_semantics=("arbitrary",), vmem_limit_bytes=VMEM_LIMIT),
        name="proj_sample",
    )(x2d, g_pre, wt_perm, *tables)


def _rope_angles(pos):
    half = ROT_DIM // 2
    inv_freq = ROPE_THETA ** (-jnp.arange(half, dtype=F32) / half)
    ang = pos.astype(F32)[:, None] * inv_freq[None, :]
    return jnp.cos(ang), jnp.sin(ang)


def _rope_lane_tables(pos):
    cos, sin = _rope_angles(pos)
    n = pos.shape[0]
    half = ROT_DIM // 2
    ones = jnp.ones((n, HEAD_DIM - ROT_DIM), F32)
    zeros = jnp.zeros((n, HEAD_DIM - ROT_DIM), F32)
    zh = jnp.zeros((n, half), F32)
    rep = LANES // HEAD_DIM
    return tuple(jnp.tile(t, (1, rep)) for t in (jnp.concatenate([cos, cos, ones], axis=1),
                                                 jnp.concatenate([-sin, zh, zeros], axis=1),
                                                 jnp.concatenate([zh, sin, zeros], axis=1)))


def _permute_w_in_t(w_in, d_model):
    c = [NSA_WIDTH, KVA_WIDTH, NSA_WIDTH, GATE_A, 3 * SB_WIDTH, SB_WIDTH, N_MERGE * d_model]
    o = [0]
    for s in c:
        o.append(o[-1] + s)
    wt = w_in.T
    pad = jnp.zeros((LANES - GATE_A, d_model), w_in.dtype)
    return jnp.concatenate([wt[:o[3]], wt[o[4]:], wt[o[3]:o[4]], pad], axis=0).astype(BF16)


def _pool_weights(w_cmp, n_tiles, blocks_per_tile, rows):
    lane = jnp.arange(blocks_per_tile * NSA_BLOCK)
    owner = jnp.arange(n_tiles)[:, None, None] * blocks_per_tile + (lane // NSA_BLOCK)[None, None, :]
    hit = jnp.arange(rows)[None, :, None] == owner
    w_lane = jnp.tile(w_cmp, (1, blocks_per_tile))
    return jnp.where(hit[None], w_lane[:, None, None, :], 0.0).astype(BF16)


def _nsa_prompt_kernel(q_ref, kslc_ref, kwin_ref, vslc_ref, vwin_ref, pool_ref, ga_ref, o_ref, sel_ref, acc_ref, *,
                       tq, tk, nb):
    i = pl.program_id(1)
    lanes = NSA_GROUP * tq
    kvh = NSA_KV_HEADS
    row = lax.broadcasted_iota(jnp.int32, (KV_PAIR, lanes), 0)
    pos1 = i * tq + lax.broadcasted_iota(jnp.int32, (1, tq), 1)
    pos = jnp.concatenate([pos1] * NSA_GROUP, axis=1)
    blk = lax.broadcasted_iota(jnp.int32, (nb, 1), 0)
    cmask = ((blk + 1) * NSA_BLOCK - 1) <= pos
    cur = pos1 // NSA_BLOCK
    valid = blk <= cur
    forced = (blk == 0) | (blk == cur) | (blk == cur - 1)
    n_sel = min(NSA_TOPK, nb)
    pooled = pool_ref[0]
    kc = pooled[:, :KV_PAIR].astype(BF16)
    vc = pooled[:, KV_PAIR:].astype(BF16)

    qpads, o_cmp = [], []
    for g in range(kvh):
        qt = q_ref[0, g * NSA_GROUP * HEAD_DIM:(g + 1) * NSA_GROUP * HEAD_DIM, :]
        q4 = jnp.concatenate([qt[hh * HEAD_DIM:(hh + 1) * HEAD_DIM] for hh in range(NSA_GROUP)], axis=1)
        qpad = jnp.where(row // HEAD_DIM == g, jnp.concatenate([q4, q4], axis=0), jnp.zeros((), BF16))
        qpads.append(qpad)
        s = jnp.where(cmask, _dot(kc, qpad), NEG_INF)
        e = jnp.exp2(s - jnp.max(s, axis=0, keepdims=True))
        p = jnp.where(cmask, e / jnp.sum(e, axis=0, keepdims=True), 0.0)
        o_cmp.append(_dot_tn(vc, p.astype(BF16))[g * HEAD_DIM:(g + 1) * HEAD_DIM])
        imp = p[:, 0:tq]
        for hh in range(1, NSA_GROUP):
            imp = imp + p[:, hh * tq:(hh + 1) * tq]
        score = jnp.where(valid, jnp.where(forced, FORCE_SCORE, imp), NEG_INF)
        for j in range(nb):
            sj = score[j:j + 1, :]
            beats = (score > sj) | ((score == sj) & (blk < j))
            cnt = jnp.sum(jnp.where(beats, 1.0, 0.0), axis=0, keepdims=True)
            sel_ref[g, j] = jnp.where(cnt < n_sel, 0.0, NEG_INF)

    bpt = tk // NSA_BLOCK

    def sel_bias(g, kt):
        rows = [jnp.broadcast_to(sel_ref[g, kt * bpt + r], (NSA_BLOCK, tq)) for r in range(bpt)]
        return jnp.concatenate(rows, axis=0)

    def step(kt, carry, near):
        ms, ls = list(carry[0]), list(carry[1])
        off = pl.multiple_of(kt * tk, tk)
        ks = kslc_ref[0, pl.ds(off, tk), :]
        chains = [(g, g, ks, vslc_ref) for g in range(kvh)]
        biases = [sel_bias(g, kt) for g in range(kvh)]
        if near:
            kw = kwin_ref[0, pl.ds(off, tk), :]
            d = pos1 - (off + lax.broadcasted_iota(jnp.int32, (tk, 1), 0))
            causal = jnp.where(d >= 0, 0.0, NEG_INF)
            window = jnp.where((d >= 0) & (d < NSA_WINDOW), 0.0, NEG_INF)
            chains += [(kvh + g, g, kw, vwin_ref) for g in range(kvh)]
            biases = [bs + causal for bs in biases] + [window] * kvh
        scs = [_dot(kk, qpads[g]) + jnp.concatenate([bs] * NSA_GROUP, axis=1)
               for (_, g, kk, _), bs in zip(chains, biases)]
        m_new = [jnp.maximum(ms[c], jnp.max(sc, axis=0, keepdims=True)) for (c, _, _, _), sc in zip(chains, scs)]
        pps = [jnp.exp2(sc - jnp.maximum(mn, 0.5 * NEG_INF)) for sc, mn in zip(scs, m_new)]
        pvs = [_dot(v_ref[0, g * HEAD_DIM:(g + 1) * HEAD_DIM, pl.ds(off, tk)], pp.astype(BF16))
               for (_, g, _, v_ref), pp in zip(chains, pps)]
        for (c, _, _, _), mn, pp, pv in zip(chains, m_new, pps, pvs):
            alpha = jnp.exp2(ms[c] - mn)
            acc_ref[c] = alpha * acc_ref[c] + pv
            ls[c] = alpha * ls[c] + jnp.sum(pp, axis=0, keepdims=True)
            ms[c] = mn
        return tuple(ms), tuple(ls)

    n_chain = 2 * kvh
    acc_ref[...] = jnp.zeros_like(acc_ref)
    carry = ((jnp.full((1, lanes), NEG_INF, F32),) * n_chain, (jnp.zeros((1, lanes), F32),) * n_chain)
    first = jnp.maximum(i * tq - (NSA_WINDOW - 1), 0) // tk
    carry = lax.fori_loop(0, first, lambda kt, cr: step(kt, cr, False), carry)
    _, ls = lax.fori_loop(first, (i + 1) * (tq // tk), lambda kt, cr: step(kt, cr, True), carry)

    ga = ga_ref[0]
    for g in range(kvh):
        o_br = [o_cmp[g]]
        for c in (g, kvh + g):
            l = ls[c]
            o_br.append(jnp.where(l > 0.0, acc_ref[c] / jnp.where(l > 0.0, l, 1.0), 0.0))
        for hh in range(NSA_GROUP):
            h = g * NSA_GROUP + hh
            sl = slice(hh * tq, (hh + 1) * tq)
            o = ga[h:h + 1] * o_br[0][:, sl]
            for br in range(1, NSA_BRANCHES):
                o = o + ga[br * NSA_HEADS + h:br * NSA_HEADS + h + 1] * o_br[br][:, sl]
            o_ref[0, h * HEAD_DIM:(h + 1) * HEAD_DIM, :] = o


def _nsa_prompt(qat, kb, vt, pooled, gat, *, batch, seq, tq=256, tk=256):
    assert tq % tk == 0 and seq % tq == 0 and tk % NSA_BLOCK == 0
    nb = seq // NSA_BLOCK
    nq = seq // tq
    return pl.pallas_call(
        functools.partial(_nsa_prompt_kernel, tq=tq, tk=tk, nb=nb),
        grid=(batch, nq),
        in_specs=[pl.BlockSpec((1, NSA_WIDTH, tq), lambda b, i: (b, 0, i)),
                  pl.BlockSpec((1, seq, KV_PAIR), lambda b, i: (b, 0, 0)),
                  pl.BlockSpec((1, seq, KV_PAIR), lambda b, i: (b, 0, 1)),
                  pl.BlockSpec((1, KV_PAIR, seq), lambda b, i: (b, 0, 0)),
                  pl.BlockSpec((1, KV_PAIR, seq), lambda b, i: (b, 1, 0)),
                  pl.BlockSpec((1, nb, 2 * KV_PAIR), lambda b, i: (b, 0, 0)),
                  pl.BlockSpec((1, LANES, tq), lambda b, i: (b, 0, i))],
        out_specs=pl.BlockSpec((1, NSA_WIDTH, tq), lambda b, i: (b, 0, i)),
        out_shape=jax.ShapeDtypeStruct((batch, NSA_WIDTH, seq), F32),
        scratch_shapes=[pltpu.VMEM((NSA_KV_HEADS, nb, 1, tq), F32),
                        pltpu.VMEM((2 * NSA_KV_HEADS, HEAD_DIM, NSA_GROUP * tq), F32)],
        compiler_params=pltpu.CompilerParams(dimension_semantics=("arbitrary",) * 2, vmem_limit_bytes=VMEM_LIMIT),
        name="nsa_prompt",
    )(qat, kb, kb, vt, vt, pooled, gat)


def _softplus(z):
    return jnp.maximum(z, 0.0) + jnp.log(1.0 + jnp.exp2(jnp.abs(z) * -LOG2E))


def _split_bf16(x):
    hi = x.astype(BF16)
    return hi, (x - hi.astype(F32)).astype(BF16)


def _mix_prompt_kernel(x_ref, oa_ref, za_ref, ob_ref, zb_ref, gm_ref, wat_ref, wbt_ref, wot_ref, g_ref, y_ref, *,
                       d_model, chain):
    cols = [slice(c, c + chain) for c in range(0, x_ref.shape[0], chain)]
    wat, wbt, wot = wat_ref[...], wbt_ref[...], wot_ref[...]
    ga = [(oa_ref[0, :, c] * za_ref[0, :, c]).astype(BF16) for c in cols]
    gb = [(ob_ref[0, :, c] * zb_ref[0, :, c]).astype(BF16) for c in cols]
    ya = [_dot(wat, v) for v in ga]
    yb = [_dot(wbt, v) for v in gb]
    mixed = [(gm_ref[0, :d_model, c] * a + gm_ref[0, d_model:, c] * b).astype(BF16) for c, a, b in zip(cols, ya, yb)]
    outs = [_dot(wot, v) for v in mixed]
    outs = [o * lax.rsqrt(jnp.mean(o * o, axis=0, keepdims=True) + RMS_EPS) for o in outs]
    for c, o in zip(cols, outs):
        y_ref[c, :] = x_ref[c, :] + o.T * g_ref[...]


def _mix_prompt(x2d, oat, zat, obt, zbt, gmt, wat, wbt, wot, g_post, *, batch, seq, tm, chain=256):
    m, d_model = x2d.shape
    assert seq % tm == 0 and tm % chain == 0
    nt = seq // tm
    row = lambda i: (i, 0)
    full = lambda i: (0, 0)
    tr = lambda i: (i // nt, 0, i % nt)
    return pl.pallas_call(
        functools.partial(_mix_prompt_kernel, d_model=d_model, chain=chain),
        grid=(m // tm,),
        in_specs=[pl.BlockSpec((tm, d_model), row), pl.BlockSpec((1, NSA_WIDTH, tm), tr),
                  pl.BlockSpec((1, NSA_WIDTH, tm), tr), pl.BlockSpec((1, SB_WIDTH, tm), tr),
                  pl.BlockSpec((1, SB_WIDTH, tm), tr), pl.BlockSpec((1, N_MERGE * d_model, tm), tr),
                  pl.BlockSpec((d_model, NSA_WIDTH), full), pl.BlockSpec((d_model, SB_WIDTH), full),
                  pl.BlockSpec((d_model, d_model), full), pl.BlockSpec((1, d_model), full)],
        out_specs=pl.BlockSpec((tm, d_model), row),
        out_shape=jax.ShapeDtypeStruct((m, d_model), F32),
        compiler_params=pltpu.CompilerParams(dimension_semantics=("arbitrary",), vmem_limit_bytes=VMEM_LIMIT),
        name="mix_prompt",
    )(x2d, oat, zat, obt, zbt, gmt, wat, wbt, wot, g_post)


def _mix_sample_kernel(x_ref, oa_ref, za_ref, ob_ref, zb_ref, gm_ref, wat_ref, wbt_ref, wot_ref, g_ref, y_ref, *,
                       d_model):
    ya = _dot_nt((oa_ref[...] * za_ref[...]).astype(BF16), wat_ref[...])
    yb = _dot_nt((ob_ref[...] * zb_ref[...]).astype(BF16), wbt_ref[...])
    gm = gm_ref[...]
    mixed = gm[:, :d_model] * ya + gm[:, d_model:] * yb
    out = _dot_nt(mixed.astype(BF16), wot_ref[...])
    y_ref[...] = x_ref[...] + _rms_scale(out, g_ref[...])


def _mix_sample(x2d, o_a, za, o_b, zb, gm, wat, wbt, wot, g_post):
    m, d_model = x2d.shape
    args = (x2d, o_a, za, o_b, zb, gm, wat, wbt, wot, g_post)
    return pl.pallas_call(
        functools.partial(_mix_sample_kernel, d_model=d_model),
        grid=(1,),
        in_specs=[pl.BlockSpec(a.shape, lambda i: (0, 0)) for a in args],
        out_specs=pl.BlockSpec((m, d_model), lambda i: (0, 0)),
        out_shape=jax.ShapeDtypeStruct((m, d_model), F32),
        compiler_params=pltpu.CompilerParams(dimension_semantics=("arbitrary",), vmem_limit_bytes=VMEM_LIMIT),
        name="mix_sample",
    )(*args)


def _head_pad(q):
    q2 = jnp.concatenate([q, q], axis=1)
    row = lax.broadcasted_iota(jnp.int32, q2.shape, 0)
    lane = lax.broadcasted_iota(jnp.int32, q2.shape, 1)
    return jnp.where(row // NSA_GROUP == lane // HEAD_DIM, q2, 0.0)


def _stream_kernel(pt_ref, qd_ref, qp_ref, k_ref, v_ref, qa_ref, wb_ref, *refs, ppg, tq, tk, sub, units,
                   steps_per_batch, nbp):
    page_refs = refs[:ppg]
    cmp_refs = refs[ppg:2 * ppg]
    od_ref, op_ref, ocmp_ref, sel_ref = refs[2 * ppg:2 * ppg + 4]
    drun_ref, dacc_ref, runs_ref, acc_ref, pool_ref, state_ref = refs[2 * ppg + 4:]
    s = pl.program_id(1)
    s_lin = pl.program_id(0) * pl.num_programs(1) + s

    @pl.when(s == 0)
    def _():
        drun_ref[...] = jnp.zeros_like(drun_ref)
        dacc_ref[...] = jnp.zeros_like(dacc_ref)

    @pl.when(s_lin == 0)
    def _():
        state_ref[0] = 0
        state_ref[1] = 0

    hrow = lax.broadcasted_iota(jnp.int32, (SB_HEADS, SB_WIDTH), 0)
    hlane = lax.broadcasted_iota(jnp.int32, (SB_HEADS, SB_WIDTH), 1)
    diag_blocks = hrow == hlane // HEAD_DIM

    def decode_scores():
        q = qd_ref[0]
        qbd = jnp.where(diag_blocks, jnp.concatenate([q] * SB_HEADS, axis=1), 0.0).astype(BF16)
        z = jnp.concatenate([_dot(qbd, pr[0, :SB_WIDTH, :].astype(BF16)) for pr in page_refs], axis=0)
        sp = _softplus(z)
        r = lax.broadcasted_iota(jnp.int32, (PAGE_SIZE, PAGE_SIZE), 0)
        c = lax.broadcasted_iota(jnp.int32, (PAGE_SIZE, PAGE_SIZE), 1)
        lower = jnp.where(r > c, 1.0, 0.0).astype(BF16)
        hi, lo = _split_bf16(sp)
        return z, sp, _dot(hi, lower) + _dot(lo, lower)

    def decode_update(z, sp, drop):
        tot = jnp.sum(sp, axis=1, keepdims=True)
        run = drun_ref[...]
        dacc = dacc_ref[...]
        for n in range(ppg):
            sl = slice(n * SB_HEADS, (n + 1) * SB_HEADS)
            a = jnp.exp(z[sl] - sp[sl] - drop[sl] - run)
            dacc = dacc + _dot_nt(a.astype(BF16), page_refs[n][0, SB_WIDTH:, :].astype(BF16))
            run = run + tot[sl]
        drun_ref[...] = run
        dacc_ref[...] = dacc

    def pool_compressed():
        group = LANES // (ppg * BLOCKS_PER_PAGE)
        slot = s % group
        chunk = s // group
        xk = jnp.concatenate([pr[0, :KV_PAIR, :].astype(BF16) for pr in cmp_refs], axis=1)
        xv = jnp.concatenate([pr[0, KV_PAIR:, :].astype(BF16) for pr in cmp_refs], axis=1)
        part = jnp.concatenate([_dot(xk, wb_ref[0, slot]), _dot(xv, wb_ref[1, slot])], axis=0)
        pool_ref[chunk] = jnp.where(slot == 0, 0.0, pool_ref[chunk]) + part

    pairs = SB_HEADS // 2
    lanes = 2 * tq
    diag = tq // tk
    per_batch = diag * units * (units + 1) // 2
    local = s_lin % steps_per_batch
    n_steps = ((local + 1) * per_batch) // steps_per_batch - (local * per_batch) // steps_per_batch

    ur = lax.broadcasted_iota(jnp.int32, (sub, 2 * sub), 0)
    uc = lax.broadcasted_iota(jnp.int32, (sub, 2 * sub), 1) % sub
    upper2 = jnp.where(uc > ur, 1.0, 0.0).astype(BF16)
    z0 = jnp.zeros((HEAD_DIM, tq), BF16)

    def later_sum(sp, run):
        parts = []
        for hb in reversed(range(tk // sub)):
            blk = sp[hb * sub:(hb + 1) * sub]
            parts.append(_dot(upper2, jnp.concatenate(_split_bf16(blk), axis=0)) + run)
            run = run + jnp.sum(blk, axis=0, keepdims=True)
        return jnp.concatenate(parts[::-1], axis=0), run

    def tile_step(with_decode):
        i = state_ref[0]
        j = state_ref[1]
        n_t = (i + 1) * diag
        off = pl.multiple_of((n_t - 1 - j) * tk, tk)
        qoff = pl.multiple_of(i * tq, tq)

        @pl.when(j == 0)
        def _():
            runs_ref[...] = jnp.zeros_like(runs_ref)
            acc_ref[...] = jnp.zeros_like(acc_ref)

        def compute(masked):
            if with_decode:
                dz, dsp, ddrop = decode_scores()
                pool_compressed()
            kk = k_ref[0, pl.ds(off, tk), :]
            qpads = []
            for jp in range(pairs):
                qt = qp_ref[0, jp * LANES:(jp + 1) * LANES, pl.ds(qoff, tq)]
                qpads.append(jnp.concatenate([jnp.concatenate([qt[:HEAD_DIM], z0], axis=0),
                                              jnp.concatenate([z0, qt[HEAD_DIM:]], axis=0)], axis=1))
            if masked:
                pos = qoff + lax.broadcasted_iota(jnp.int32, (1, lanes), 1) % tq
                mask = (off + lax.broadcasted_iota(jnp.int32, (tk, 1), 0)) < pos
            zs = [_dot(kk[:, jp * LANES:(jp + 1) * LANES], qpads[jp]) for jp in range(pairs)]
            sps = [jnp.where(mask, _softplus(zz), 0.0) if masked else _softplus(zz) for zz in zs]
            later = [later_sum(spj, runs_ref[jp]) for jp, spj in enumerate(sps)]
            drops = [d for d, _ in later]
            if with_decode:
                decode_update(dz, dsp, ddrop)
            ws = [jnp.exp(zz - spj - dr) for zz, spj, dr in zip(zs, sps, drops)]
            if masked:
                ws = [jnp.where(mask, w, 0.0) for w in ws]
            for h in range(SB_HEADS):
                w = ws[h // 2][:, (h % 2) * tq:(h % 2 + 1) * tq].astype(BF16)
                acc_ref[h] = acc_ref[h] + _dot(v_ref[0, h * HEAD_DIM:(h + 1) * HEAD_DIM, pl.ds(off, tk)], w)
            for jp, (_, run) in enumerate(later):
                runs_ref[jp] = run

        @pl.when(j < diag)
        def _():
            compute(True)

        @pl.when(j >= diag)
        def _():
            compute(False)

        last = j == n_t - 1

        @pl.when(last)
        def _():
            for h in range(SB_HEADS):
                op_ref[0, h * HEAD_DIM:(h + 1) * HEAD_DIM, pl.ds(qoff, tq)] = acc_ref[h]

        state_ref[1] = jnp.where(last, 0, j + 1)
        state_ref[0] = jnp.where(last, (i + 1) % units, i)

    @pl.when(n_steps > 0)
    def _():
        tile_step(True)

    @pl.when(n_steps == 0)
    def _():
        decode_update(*decode_scores())
        pool_compressed()

    def rest(_, carry):
        tile_step(False)
        return carry

    lax.fori_loop(1, n_steps, rest, 0)

    @pl.when(s == pl.num_programs(1) - 1)
    def _():
        od_ref[0] = jnp.sum(jnp.where(diag_blocks, dacc_ref[...], 0.0), axis=0, keepdims=True)
        _compressed_decode(qa_ref[0], pool_ref, ocmp_ref, sel_ref, nbp)


def _stream_attention(page_table_flat, q_dec, sb_cache_t, qbt, kbb, vbt, qa_dec, w_cmp, nsa_cache_t, *, dec_batch,
                      n_pages, batch, seq, ppg, tq=256, tk=256, sub=128):
    steps = n_pages // ppg
    assert tq % tk == 0 and tk % sub == 0 and seq % tq == 0 and (dec_batch * steps) % batch == 0
    assert LANES % (ppg * BLOCKS_PER_PAGE) == 0
    spb = dec_batch * steps // batch
    nbp = n_pages * BLOCKS_PER_PAGE
    wb = _decode_pool_weights(w_cmp, ppg)

    def page_map(n):
        return lambda b, s, pt: (pt[b * n_pages + n_pages - 1 - (s * ppg + n)], 0, 0)

    def cmp_map(n):
        return lambda b, s, pt: (pt[b * n_pages + s * ppg + n], 0, 0)

    prow = lambda b, s, pt: ((b * steps + s) // spb, 0, 0)
    sample = lambda b, s, pt: (b, 0, 0)
    grid_spec = pltpu.PrefetchScalarGridSpec(
        num_scalar_prefetch=1, grid=(dec_batch, steps),
        in_specs=[pl.BlockSpec((1, SB_HEADS, HEAD_DIM), sample),
                  pl.BlockSpec((1, SB_WIDTH, seq), prow), pl.BlockSpec((1, seq, SB_WIDTH), prow),
                  pl.BlockSpec((1, SB_WIDTH, seq), prow),
                  pl.BlockSpec((1, NSA_HEADS, HEAD_DIM), sample),
                  pl.BlockSpec(wb.shape, lambda b, s, pt: (0, 0, 0, 0))]
        + [pl.BlockSpec((1, 2 * SB_WIDTH, PAGE_SIZE), page_map(n)) for n in range(ppg)]
        + [pl.BlockSpec((1, 2 * KV_PAIR, PAGE_SIZE), cmp_map(n)) for n in range(ppg)],
        out_specs=[pl.BlockSpec((1, 1, SB_WIDTH), sample), pl.BlockSpec((1, SB_WIDTH, seq), prow),
                   pl.BlockSpec((1, NSA_HEADS, HEAD_DIM), sample),
                   pl.BlockSpec((1, NSA_KV_HEADS, NSA_TOPK, LANES), lambda b, s, pt: (b, 0, 0, 0))],
        scratch_shapes=[pltpu.VMEM((SB_HEADS, 1), F32), pltpu.VMEM((SB_HEADS, SB_WIDTH), F32),
                        pltpu.VMEM((SB_HEADS // 2, 1, 2 * tq), F32), pltpu.VMEM((SB_HEADS, HEAD_DIM, tq), F32),
                        pltpu.VMEM((pl.cdiv(nbp, LANES), 2 * KV_PAIR, LANES), F32),
                        pltpu.SMEM((2,), jnp.int32)])
    return pl.pallas_call(
        functools.partial(_stream_kernel, ppg=ppg, tq=tq, tk=tk, sub=sub, units=seq // tq, steps_per_batch=spb,
                          nbp=nbp),
        grid_spec=grid_spec,
        out_shape=[jax.ShapeDtypeStruct((dec_batch, 1, SB_WIDTH), F32),
                   jax.ShapeDtypeStruct((batch, SB_WIDTH, seq), F32),
                   jax.ShapeDtypeStruct((dec_batch, NSA_HEADS, HEAD_DIM), F32),
                   jax.ShapeDtypeStruct((dec_batch, NSA_KV_HEADS, NSA_TOPK, LANES), jnp.int32)],
        compiler_params=pltpu.CompilerParams(dimension_semantics=("arbitrary", "arbitrary"),
                                             vmem_limit_bytes=VMEM_LIMIT),
        name="stream_attention",
    )(page_table_flat, q_dec, qbt, kbb, vbt, qa_dec, wb, *([sb_cache_t] * ppg), *([nsa_cache_t] * ppg))


def _compressed_decode(q, pool_ref, ocmp_ref, sel_ref, nbp):
    n_chunks = pool_ref.shape[0]
    nl = n_chunks * LANES
    qpad = _head_pad(q).astype(BF16)
    pooled = jnp.concatenate([pool_ref[ch] for ch in range(n_chunks)], axis=1)
    blk_l = lax.broadcasted_iota(jnp.int32, (1, nl), 1)
    live = blk_l < nbp
    sc = jnp.where(live, _dot(qpad, pooled[:KV_PAIR].astype(BF16)), NEG_INF)
    e = jnp.where(live, jnp.exp(sc - jnp.max(sc, axis=1, keepdims=True)), 0.0)
    p = e / jnp.sum(e, axis=1, keepdims=True)
    o = _dot_nt(p.astype(BF16), pooled[KV_PAIR:].astype(BF16))
    hrow = lax.broadcasted_iota(jnp.int32, o.shape, 0)
    ocmp_ref[0] = jnp.where(hrow < NSA_GROUP, o, pltpu.roll(o, HEAD_DIM, 1))[:, :HEAD_DIM]

    blk_s = lax.broadcasted_iota(jnp.int32, (nl, 1), 0)
    eye = lax.broadcasted_iota(jnp.int32, (nl, nl), 0) == lax.broadcasted_iota(jnp.int32, (nl, nl), 1)
    n_sel = NSA_TOPK - 1
    kslot = lax.broadcasted_iota(jnp.int32, (NSA_TOPK, nl), 0)
    for gi in range(NSA_KV_HEADS):
        imp = jnp.sum(p[gi * NSA_GROUP:(gi + 1) * NSA_GROUP], axis=0, keepdims=True)
        forced = (blk_l == 0) | (blk_l == nbp - 1)
        srow = jnp.where(live, jnp.where(forced, FORCE_SCORE, imp), NEG_INF)
        scol = jnp.sum(jnp.where(eye, srow, 0.0), axis=1, keepdims=True)
        beats = (scol > srow) | ((scol == srow) & (blk_s < blk_l))
        rank = jnp.sum(jnp.where(beats, 1.0, 0.0), axis=0, keepdims=True)
        sel = rank < n_sel
        selcol = jnp.sum(jnp.where(eye & sel, 1.0, 0.0), axis=1, keepdims=True) > 0.5
        rank_sel = jnp.sum(jnp.where(selcol & (blk_s < blk_l), 1.0, 0.0), axis=0, keepdims=True)
        onehot = sel & (rank_sel.astype(jnp.int32) == kslot)
        idx = jnp.sum(jnp.where(onehot, blk_l, 0), axis=1, keepdims=True)
        sel_ref[0, gi] = jnp.broadcast_to(idx, (NSA_TOPK, LANES))


def _decode_pool_weights(w_cmp, ppg):
    per_step = ppg * BLOCKS_PER_PAGE
    group = LANES // per_step
    k = jnp.arange(ppg * PAGE_SIZE)
    target = (k // PAGE_SIZE) * BLOCKS_PER_PAGE + (k % PAGE_SIZE) // NSA_BLOCK
    hit = jnp.arange(LANES)[None, None, :] == (jnp.arange(group)[:, None, None] * per_step + target[None, :, None])
    w_row = jnp.tile(w_cmp, (1, ppg * BLOCKS_PER_PAGE))
    return jnp.where(hit[None], w_row[:, None, :, None], 0.0).astype(BF16)


def _nsa_sel_decode_kernel(sel_ref, pt_ref, q_ref, ocmp_ref, ga_ref, new_ref, winp_ref, *refs, n_blk):
    blk_refs = refs[:NSA_KV_HEADS * n_blk]
    o_ref, wino_ref = refs[NSA_KV_HEADS * n_blk:]
    b = pl.program_id(0)
    qb = _head_pad(q_ref[0]).astype(BF16)
    new = new_ref[0]
    hrow = lax.broadcasted_iota(jnp.int32, (NSA_HEADS, KV_PAIR), 0)
    top = hrow < NSA_GROUP

    def fold(o):
        return jnp.where(top, o, pltpu.roll(o, HEAD_DIM, 1))[:, :HEAD_DIM]

    def softmax(sc, mask):
        sc = jnp.where(mask, sc, NEG_INF)
        e = jnp.where(mask, jnp.exp(sc - jnp.max(sc, axis=1, keepdims=True)), 0.0)
        return e / jnp.sum(e, axis=1, keepdims=True)

    lane = lax.broadcasted_iota(jnp.int32, (1, PAGE_SIZE), 1)
    first = lax.broadcasted_iota(jnp.int32, (KV_PAIR, PAGE_SIZE), 1) == 0
    k_new = jnp.where(first, new[2 * KV_PAIR:3 * KV_PAIR], 0.0).astype(BF16)
    v_new = jnp.where(first, new[3 * KV_PAIR:4 * KV_PAIR], 0.0).astype(BF16)
    o_g = []
    for gi in range(NSA_KV_HEADS):
        scs, masks = [], []
        for n in range(n_blk):
            half = sel_ref[(b * NSA_KV_HEADS + gi) * NSA_TOPK + n] % BLOCKS_PER_PAGE
            scs.append(_dot(qb, blk_refs[gi * n_blk + n][0, :KV_PAIR, :].astype(BF16)))
            masks.append(lane // NSA_BLOCK == half)
        scs.append(_dot(qb, k_new))
        masks.append(lane == 0)
        p = softmax(jnp.concatenate(scs, axis=1), jnp.concatenate(masks, axis=1)).astype(BF16)
        o = _dot_nt(p[:, n_blk * PAGE_SIZE:], v_new)
        for n in range(n_blk):
            o = o + _dot_nt(p[:, n * PAGE_SIZE:(n + 1) * PAGE_SIZE],
                            blk_refs[gi * n_blk + n][0, KV_PAIR:, :].astype(BF16))
        o_g.append(o)
    o_slc = fold(jnp.where(top, o_g[0], o_g[1]))

    wp = winp_ref[0]
    w = wp.shape[1]
    wl = lax.broadcasted_iota(jnp.int32, wp.shape, 1)
    shifted = jnp.where(wl == w - 1, new[4 * KV_PAIR:], pltpu.roll(wp, w - 1, 1))
    wino_ref[0] = shifted
    sc = _dot(qb, shifted[:KV_PAIR].astype(BF16))
    p = softmax(sc, jnp.full((1, w), True)).astype(BF16)
    o_win = fold(_dot_nt(p, shifted[KV_PAIR:].astype(BF16)))

    ga = ga_ref[0]
    o_ref[0] = ga[0] * ocmp_ref[0] + ga[1] * o_slc + ga[2] * o_win


def _nsa_sel_decode(sel_flat, page_table_flat, q_a, o_cmp, ga, new_kv, win_past_t, cache_t, *, dec_batch, n_pages):
    n_blk = NSA_TOPK - 1
    w = win_past_t.shape[2]

    def blk_map(gi, n):
        def f(b, sel, pt):
            blk = sel[(b * NSA_KV_HEADS + gi) * NSA_TOPK + n]
            return (pt[b * n_pages + blk // BLOCKS_PER_PAGE], 1, 0)
        return f

    grid_spec = pltpu.PrefetchScalarGridSpec(
        num_scalar_prefetch=2, grid=(dec_batch,),
        in_specs=[pl.BlockSpec((1, NSA_HEADS, HEAD_DIM), lambda b, sel, pt: (b, 0, 0)),
                  pl.BlockSpec((1, NSA_HEADS, HEAD_DIM), lambda b, sel, pt: (b, 0, 0)),
                  pl.BlockSpec((1, NSA_BRANCHES, NSA_HEADS, 1), lambda b, sel, pt: (b, 0, 0, 0)),
                  pl.BlockSpec((1, KVA_WIDTH, 1), lambda b, sel, pt: (b, 0, 0)),
                  pl.BlockSpec((1, 2 * KV_PAIR, w), lambda b, sel, pt: (b, 0, 0))]
        + [pl.BlockSpec((1, 2 * KV_PAIR, PAGE_SIZE), blk_map(gi, n))
           for gi in range(NSA_KV_HEADS) for n in range(n_blk)],
        out_specs=[pl.BlockSpec((1, NSA_HEADS, HEAD_DIM), lambda b, sel, pt: (b, 0, 0)),
                   pl.BlockSpec((1, 2 * KV_PAIR, w), lambda b, sel, pt: (b, 0, 0))])
    return pl.pallas_call(
        functools.partial(_nsa_sel_decode_kernel, n_blk=n_blk),
        grid_spec=grid_spec,
        out_shape=[jax.ShapeDtypeStruct((dec_batch, NSA_HEADS, HEAD_DIM), F32),
                   jax.ShapeDtypeStruct((dec_batch, 2 * KV_PAIR, w), F32)],
        compiler_params=pltpu.CompilerParams(dimension_semantics=("arbitrary",), vmem_limit_bytes=VMEM_LIMIT),
        name="nsa_sel_decode",
    )(sel_flat, page_table_flat, q_a, o_cmp, ga, new_kv, win_past_t, *([cache_t] * (NSA_KV_HEADS * n_blk)))


def _feature_major(a, lead):
    nl = len(lead)
    t = jnp.moveaxis(a, nl, -1)
    return t.reshape(lead + (-1, a.shape[nl]))


def _time_major(a_t, feat_shape):
    lead, _, time = a_t.shape
    return jnp.moveaxis(a_t.reshape((lead,) + feat_shape + (time,)), -1, 1)


def _layer(x_prompt, x_sample, nsa_cache, win_cache, sb_cache, page_table, w_in, w_cmp, w_a, w_b, w_o, g_pre, g_post):
    batch, seq, d_model = x_prompt.shape
    dec_batch, dec_seq, _ = x_sample.shape
    n_pages = page_table.shape[1]
    past_len = n_pages * PAGE_SIZE
    n_phys = nsa_cache.shape[0]
    tm, ppg = 256, 16
    assert dec_seq == 1 and seq % tm == 0 and seq >= NSA_WINDOW and past_len >= NSA_WINDOW
    assert n_pages % ppg == 0 and past_len // NSA_BLOCK >= NSA_TOPK
    assert win_cache.shape[1] == NSA_WINDOW and nsa_cache.shape[1] == PAGE_SIZE

    wt_perm = _permute_w_in_t(w_in, d_model)
    wat, wbt, wot = w_a.T.astype(BF16), w_b.T.astype(BF16), w_o.T.astype(BF16)
    g_pre2, g_post2 = g_pre.reshape(1, d_model), g_post.reshape(1, d_model)
    nsa_feat = (4, NSA_KV_HEADS, HEAD_DIM)
    win_feat = (2, NSA_KV_HEADS, HEAD_DIM)
    sb_feat = (2, SB_HEADS, HEAD_DIM)

    xp = x_prompt.reshape(batch * seq, d_model)
    cos, sin = _rope_angles(jnp.arange(seq, dtype=jnp.int32))
    wp_prompt = _pool_weights(w_cmp, 1, tm // NSA_BLOCK, SUBLANES).reshape(2 * SUBLANES, tm)
    (nsa_t, win_t, sb_t, za_t, zb_t, gm_t, qa_t, ga_t, kb, v_t, pooled, qb_t, kbb, vb_t) = _project_prompt(
        xp, g_pre2, wt_perm, cos.T, sin.T, wp_prompt, batch=batch, seq=seq, tm=tm)
    xs = x_sample.reshape(dec_batch, d_model)
    tabs_s = _rope_lane_tables(jnp.full((dec_batch,), past_len, jnp.int32))
    (nsa_s, win_s, sb_s, za_s, zb_s, gm_s, qa_s, ga_s, qb_s) = _project_sample(xs, g_pre2, wt_perm, tabs_s)
    pt_flat = page_table.reshape(-1)
    sb_cache_t = _feature_major(sb_cache, (n_phys,))
    nsa_cache_t = _feature_major(nsa_cache, (n_phys,))
    win_cache_t = _feature_major(win_cache, (dec_batch,))

    qa3 = qa_s.reshape(dec_batch, NSA_HEADS, HEAD_DIM)
    o_b_s, ob_t, o_cmp_s, sel = _stream_attention(
        pt_flat, qb_s.reshape(dec_batch, SB_HEADS, HEAD_DIM), sb_cache_t, qb_t, kbb.reshape(batch, seq, SB_WIDTH), vb_t,
        qa3, w_cmp, nsa_cache_t, dec_batch=dec_batch, n_pages=n_pages, batch=batch, seq=seq, ppg=ppg)

    oa_t = _nsa_prompt(qa_t, kb.reshape(batch, seq, 2 * KV_PAIR), v_t,
                       pooled.reshape(batch, seq // NSA_BLOCK, 2 * KV_PAIR), ga_t, batch=batch, seq=seq)
    y_prompt = _mix_prompt(xp, oa_t, za_t, ob_t, zb_t, gm_t, wat, wbt, wot, g_post2,
                           batch=batch, seq=seq, tm=2 * tm).reshape(batch, seq, d_model)
    nsa_kv_prompt = _time_major(nsa_t, nsa_feat)
    win_kv_prompt = _time_major(win_t[:, :, seq - NSA_WINDOW:], win_feat)
    sb_kv_prompt = _time_major(sb_t, sb_feat)

    ga3 = ga_s[:, :GATE_A].reshape(dec_batch, NSA_BRANCHES, NSA_HEADS, 1)
    new_kv = jnp.concatenate([nsa_s, win_s], axis=1).reshape(dec_batch, KVA_WIDTH, 1)
    o_a_s, win_out_t = _nsa_sel_decode(sel[:, :, :, 0].reshape(-1), pt_flat, qa3, o_cmp_s, ga3, new_kv,
                                       win_cache_t, nsa_cache_t, dec_batch=dec_batch, n_pages=n_pages)
    y_sample = _mix_sample(xs, o_a_s.reshape(dec_batch, NSA_WIDTH), za_s, o_b_s.reshape(dec_batch, SB_WIDTH), zb_s,
                           gm_s, wat, wbt, wot, g_post2).reshape(dec_batch, 1, d_model)
    nsa_kv_sample = nsa_s.reshape((dec_batch, 1) + nsa_feat)
    win_kv_sample = _time_major(win_out_t, win_feat)
    sb_kv_sample = sb_s.reshape((dec_batch, 1) + sb_feat)
    return (y_prompt, y_sample, nsa_kv_prompt, win_kv_prompt, sb_kv_prompt, nsa_kv_sample, win_kv_sample,
            sb_kv_sample)


def kernel(x_prompt, x_sample, cache_nsa_kv, cache_nsa_win_kv, cache_sb_kv, page_table, w_in, w_cmp, w_branch_a,
           w_branch_b, w_out, g_pre, g_post):
    hp, hs = x_prompt, x_sample
    caches = [[] for _ in range(6)]
    for layer in range(w_in.shape[0]):
        outs = _layer(hp, hs, cache_nsa_kv[layer], cache_nsa_win_kv[layer], cache_sb_kv[layer], page_table,
                      w_in[layer], w_cmp[layer], w_branch_a[layer], w_branch_b[layer], w_out[layer],
                      g_pre[layer], g_post[layer])
        hp, hs = outs[0], outs[1]
        for acc, o in zip(caches, outs[2:]):
            acc.append(o)
    return (hp, hs) + tuple(jnp.stack(c) for c in caches)
```

```python
import functools

import jax
import jax.numpy as jnp
from jax import lax
from jax.experimental import pallas as pl
from jax.experimental.pallas import tpu as pltpu

HEAD_DIM = 64
ROT_DIM = HEAD_DIM // 4
ROPE_THETA = 500000.0
NSA_HEADS = 8
NSA_KV_HEADS = 2
NSA_GROUP = NSA_HEADS // NSA_KV_HEADS
NSA_BRANCHES = 3
NSA_BLOCK = 64
NSA_TOPK = 16
NSA_WINDOW = 512
NSA_WIDTH = NSA_HEADS * HEAD_DIM
SB_HEADS = 8
SB_WIDTH = SB_HEADS * HEAD_DIM
N_MERGE = 2
PAGE_SIZE = 128
RMS_EPS = 1e-6
NEG_INF = -1e30
FORCE_SCORE = 1e3
SCALE = HEAD_DIM ** -0.5
LOG2E = 1.4426950408889634

LANES = 128
SUBLANES = 8
KV_PAIR = NSA_KV_HEADS * HEAD_DIM
assert KV_PAIR == LANES and PAGE_SIZE == LANES and PAGE_SIZE % NSA_BLOCK == 0
KVA_WIDTH = 2 * NSA_BRANCHES * KV_PAIR
GATE_A = NSA_BRANCHES * NSA_HEADS
BLOCKS_PER_PAGE = PAGE_SIZE // NSA_BLOCK
VMEM_LIMIT = 56 * 1024 * 1024

C_QA = 0
C_KVA = C_QA + NSA_WIDTH
C_ZA = C_KVA + KVA_WIDTH
C_QKVB = C_ZA + NSA_WIDTH
C_ZB = C_QKVB + 3 * SB_WIDTH
C_GM = C_ZB + SB_WIDTH

BF16 = jnp.bfloat16
F32 = jnp.float32


def _dot(a, b):
    return jnp.dot(a, b, preferred_element_type=F32)


def _dot_tn(a, b):
    return lax.dot_general(a, b, (((0,), (0,)), ((), ())), preferred_element_type=F32)


def _dot_nt(a, b):
    return lax.dot_general(a, b, (((1,), (1,)), ((), ())), preferred_element_type=F32)


def _sigmoid(x):
    return 1.0 / (1.0 + jnp.exp(-x))


def _rms_scale(x, g):
    return x * lax.rsqrt(jnp.mean(x * x, axis=-1, keepdims=True) + RMS_EPS) * g


def _rope_rows(v, cos, sin):
    half = ROT_DIM // 2
    parts = []
    for base in range(0, v.shape[0], HEAD_DIM):
        x1, x2 = v[base:base + half], v[base + half:base + 2 * half]
        parts += [x1 * cos - x2 * sin, x2 * cos + x1 * sin, v[base + 2 * half:base + HEAD_DIM]]
    return jnp.concatenate(parts, axis=0)


def _proj_prompt_kernel(x_ref, g_ref, wt_ref, cos_ref, sin_ref, wp_ref,
                        nsa_ref, win_ref, sb_ref, za_ref, zb_ref, gm_ref, qa_ref, ga_ref, kb_ref, vt_ref,
                        pool_ref, qb_ref, kbb_ref, vbt_ref, *, d_model):
    hb = _rms_scale(x_ref[...], g_ref[...]).astype(BF16)
    cos, sin = cos_ref[...], sin_ref[...]

    def seg(lo, width):
        return _dot_nt(wt_ref[lo:lo + width, :], hb)

    qa_ref[0] = (_rope_rows(seg(C_QA, NSA_WIDTH), cos, sin) * (SCALE * LOG2E)).astype(BF16)
    kv = seg(C_KVA, KVA_WIDTH)
    cmp_k = _rope_rows(kv[0 * LANES:1 * LANES], cos, sin)
    cmp_v = kv[1 * LANES:2 * LANES]
    slc_k = _rope_rows(kv[2 * LANES:3 * LANES], cos, sin)
    slc_v = kv[3 * LANES:4 * LANES]
    win_k = _rope_rows(kv[4 * LANES:5 * LANES], cos, sin)
    win_v = kv[5 * LANES:6 * LANES]
    nsa_ref[0] = jnp.concatenate([cmp_k, cmp_v, slc_k, slc_v], axis=0)
    win_ref[0] = jnp.concatenate([win_k, win_v], axis=0)
    kb_ref[...] = jnp.concatenate([slc_k.T, win_k.T], axis=1).astype(BF16)
    vt_ref[0] = jnp.concatenate([slc_v, win_v], axis=0).astype(BF16)
    wp = wp_ref[...]
    nblk = pool_ref.shape[1]
    pooled = jnp.concatenate([_dot_nt(wp[:SUBLANES], cmp_k.astype(BF16)),
                              _dot_nt(wp[SUBLANES:], cmp_v.astype(BF16))], axis=1)
    pool_ref[0] = pooled[:nblk]

    za = seg(C_ZA, NSA_WIDTH)
    za_ref[0] = (za * _sigmoid(za)).astype(za_ref.dtype)
    qkvb = seg(C_QKVB, 3 * SB_WIDTH)
    qb_ref[0] = (qkvb[:SB_WIDTH] * SCALE).astype(BF16)
    sb_ref[0] = qkvb[SB_WIDTH:]
    kbb_ref[...] = qkvb[SB_WIDTH:2 * SB_WIDTH].T.astype(BF16)
    vbt_ref[0] = qkvb[2 * SB_WIDTH:].astype(BF16)
    zb = seg(C_ZB, SB_WIDTH)
    zb_ref[0] = (zb * _sigmoid(zb)).astype(zb_ref.dtype)
    gm_ref[0] = _sigmoid(seg(C_GM, N_MERGE * d_model)).astype(gm_ref.dtype)
    ga_ref[0] = _sigmoid(seg(C_GM + N_MERGE * d_model, LANES))


def _project_prompt(x2d, g_pre, wt_perm, cos_t, sin_t, wp, *, batch, seq, tm):
    m, d_model = x2d.shape
    nt = seq // tm
    n_rows = wt_perm.shape[0]
    nblk = tm // NSA_BLOCK
    row = lambda i: (i, 0)
    full = lambda i: (0, 0)
    tab = lambda i: (0, i % nt)
    tr = lambda i: (i // nt, 0, i % nt)
    half = ROT_DIM // 2
    in_specs = [pl.BlockSpec((tm, d_model), row), pl.BlockSpec((1, d_model), full),
                pl.BlockSpec((n_rows, d_model), full),
                pl.BlockSpec((half, tm), tab), pl.BlockSpec((half, tm), tab),
                pl.BlockSpec((2 * SUBLANES, tm), full)]
    sds = jax.ShapeDtypeStruct

    def feat(width, dtype):
        return sds((batch, width, seq), dtype), pl.BlockSpec((1, width, tm), tr)

    def rows(width, dtype):
        return sds((m, width), dtype), pl.BlockSpec((tm, width), row)

    outs = [feat(4 * KV_PAIR, F32), feat(2 * KV_PAIR, F32), feat(2 * SB_WIDTH, F32),
            feat(NSA_WIDTH, BF16), feat(SB_WIDTH, BF16), feat(N_MERGE * d_model, BF16),
            feat(NSA_WIDTH, BF16), feat(LANES, F32), rows(2 * KV_PAIR, BF16), feat(2 * KV_PAIR, BF16),
            (sds((m // tm, nblk, 2 * KV_PAIR), F32), pl.BlockSpec((1, nblk, 2 * KV_PAIR), lambda i: (i, 0, 0))),
            feat(SB_WIDTH, BF16), rows(SB_WIDTH, BF16), feat(SB_WIDTH, BF16)]
    return pl.pallas_call(
        functools.partial(_proj_prompt_kernel, d_model=d_model),
        grid=(m // tm,), in_specs=in_specs, out_specs=[o[1] for o in outs], out_shape=[o[0] for o in outs],
        compiler_params=pltpu.CompilerParams(dimension_semantics=("arbitrary",), vmem_limit_bytes=VMEM_LIMIT),
        name="proj_prompt",
    )(x2d, g_pre, wt_perm, cos_t, sin_t, wp)


def _rope_lanes(v, c, s_up, s_dn):
    half = ROT_DIM // 2
    outs = []
    for j in range(v.shape[1] // LANES):
        blk = v[:, j * LANES:(j + 1) * LANES]
        outs.append(blk * c + pltpu.roll(blk, LANES - half, 1) * s_up + pltpu.roll(blk, half, 1) * s_dn)
    return outs[0] if len(outs) == 1 else jnp.concatenate(outs, axis=1)


def _proj_sample_kernel(x_ref, g_ref, wt_ref, cos_ref, sup_ref, sdn_ref,
                        nsa_ref, win_ref, sbkv_ref, za_ref, zb_ref, gm_ref, qa_ref, ga_ref, qb_ref, *, d_model):
    hb = _rms_scale(x_ref[...], g_ref[...]).astype(BF16)
    cos, s_up, s_dn = cos_ref[...], sup_ref[...], sdn_ref[...]

    def seg(lo, width):
        return _dot_nt(hb, wt_ref[lo:lo + width, :])

    qa_ref[...] = _rope_lanes(seg(C_QA, NSA_WIDTH), cos, s_up, s_dn) * SCALE
    kv = seg(C_KVA, KVA_WIDTH)
    parts = []
    for j in range(2 * NSA_BRANCHES):
        blk = kv[:, j * LANES:(j + 1) * LANES]
        parts.append(_rope_lanes(blk, cos, s_up, s_dn) if j % 2 == 0 else blk)
    nsa_ref[...] = jnp.concatenate(parts[:4], axis=1)
    win_ref[...] = jnp.concatenate(parts[4:], axis=1)
    za = seg(C_ZA, NSA_WIDTH)
    za_ref[...] = za * _sigmoid(za)
    qkvb = seg(C_QKVB, 3 * SB_WIDTH)
    qb_ref[...] = qkvb[:, :SB_WIDTH] * SCALE
    sbkv_ref[...] = qkvb[:, SB_WIDTH:]
    zb = seg(C_ZB, SB_WIDTH)
    zb_ref[...] = zb * _sigmoid(zb)
    gm_ref[...] = _sigmoid(seg(C_GM, N_MERGE * d_model))
    ga_ref[...] = _sigmoid(seg(C_GM + N_MERGE * d_model, LANES))


def _project_sample(x2d, g_pre, wt_perm, tables):
    m, d_model = x2d.shape
    full = lambda i: (0, 0)
    widths = [4 * KV_PAIR, 2 * KV_PAIR, 2 * SB_WIDTH, NSA_WIDTH, SB_WIDTH, N_MERGE * d_model,
              NSA_WIDTH, LANES, SB_WIDTH]
    return pl.pallas_call(
        functools.partial(_proj_sample_kernel, d_model=d_model),
        grid=(1,),
        in_specs=[pl.BlockSpec((m, d_model), full), pl.BlockSpec((1, d_model), full),
                  pl.BlockSpec(wt_perm.shape, full)] + [pl.BlockSpec((m, LANES), full)] * 3,
        out_specs=[pl.BlockSpec((m, w), full) for w in widths],
        out_shape=[jax.ShapeDtypeStruct((m, w), F32) for w in widths],
        compiler_params=pltpu.CompilerParams(dimension_semantics=("arbitrary",), vmem_limit_bytes=VMEM_LIMIT),
        name="proj_sample",
    )(x2d, g_pre, wt_perm, *tables)


def _rope_angles(pos):
    half = ROT_DIM // 2
    inv_freq = ROPE_THETA ** (-jnp.arange(half, dtype=F32) / half)
    ang = pos.astype(F32)[:, None] * inv_freq[None, :]
    return jnp.cos(ang), jnp.sin(ang)


def _rope_lane_tables(pos):
    cos, sin = _rope_angles(pos)
    n = pos.shape[0]
    half = ROT_DIM // 2
    ones = jnp.ones((n, HEAD_DIM - ROT_DIM), F32)
    zeros = jnp.zeros((n, HEAD_DIM - ROT_DIM), F32)
    zh = jnp.zeros((n, half), F32)
    rep = LANES // HEAD_DIM
    return tuple(jnp.tile(t, (1, rep)) for t in (jnp.concatenate([cos, cos, ones], axis=1),
                                                 jnp.concatenate([-sin, zh, zeros], axis=1),
                                                 jnp.concatenate([zh, sin, zeros], axis=1)))


def _permute_w_in_t(w_in, d_model):
    c = [NSA_WIDTH, KVA_WIDTH, NSA_WIDTH, GATE_A, 3 * SB_WIDTH, SB_WIDTH, N_MERGE * d_model]
    o = [0]
    for s in c:
        o.append(o[-1] + s)
    wt = w_in.T
    pad = jnp.zeros((LANES - GATE_A, d_model), w_in.dtype)
    return jnp.concatenate([wt[:o[3]], wt[o[4]:], wt[o[3]:o[4]], pad], axis=0).astype(BF16)


def _pool_weights(w_cmp, n_tiles, blocks_per_tile, rows):
    lane = jnp.arange(blocks_per_tile * NSA_BLOCK)
    owner = jnp.arange(n_tiles)[:, None, None] * blocks_per_tile + (lane // NSA_BLOCK)[None, None, :]
    hit = jnp.arange(rows)[None, :, None] == owner
    w_lane = jnp.tile(w_cmp, (1, blocks_per_tile))
    return jnp.where(hit[None], w_lane[:, None, None, :], 0.0).astype(BF16)


def _nsa_prompt_kernel(q_ref, kslc_ref, kwin_ref, vslc_ref, vwin_ref, pool_ref, ga_ref, o_ref, sel_ref, acc_ref, *,
                       tq, tk, nb):
    i = pl.program_id(1)
    lanes = NSA_GROUP * tq
    kvh = NSA_KV_HEADS
    row = lax.broadcasted_iota(jnp.int32, (KV_PAIR, lanes), 0)
    pos1 = i * tq + lax.broadcasted_iota(jnp.int32, (1, tq), 1)
    pos = jnp.concatenate([pos1] * NSA_GROUP, axis=1)
    blk = lax.broadcasted_iota(jnp.int32, (nb, 1), 0)
    cmask = ((blk + 1) * NSA_BLOCK - 1) <= pos
    cur = pos1 // NSA_BLOCK
    valid = blk <= cur
    forced = (blk == 0) | (blk == cur) | (blk == cur - 1)
    n_sel = min(NSA_TOPK, nb)
    pooled = pool_ref[0]
    kc = pooled[:, :KV_PAIR].astype(BF16)
    vc = pooled[:, KV_PAIR:].astype(BF16)

    qpads, o_cmp = [], []
    for g in range(kvh):
        qt = q_ref[0, g * NSA_GROUP * HEAD_DIM:(g + 1) * NSA_GROUP * HEAD_DIM, :]
        q4 = jnp.concatenate([qt[hh * HEAD_DIM:(hh + 1) * HEAD_DIM] for hh in range(NSA_GROUP)], axis=1)
        qpad = jnp.where(row // HEAD_DIM == g, jnp.concatenate([q4, q4], axis=0), jnp.zeros((), BF16))
        qpads.append(qpad)
        s = jnp.where(cmask, _dot(kc, qpad), NEG_INF)
        e = jnp.exp2(s - jnp.max(s, axis=0, keepdims=True))
        p = jnp.where(cmask, e / jnp.sum(e, axis=0, keepdims=True), 0.0)
        o_cmp.append(_dot_tn(vc, p.astype(BF16))[g * HEAD_DIM:(g + 1) * HEAD_DIM])
        imp = p[:, 0:tq]
        for hh in range(1, NSA_GROUP):
            imp = imp + p[:, hh * tq:(hh + 1) * tq]
        score = jnp.where(valid, jnp.where(forced, FORCE_SCORE, imp), NEG_INF)
        for j in range(nb):
            sj = score[j:j + 1, :]
            beats = (score > sj) | ((score == sj) & (blk < j))
            cnt = jnp.sum(jnp.where(beats, 1.0, 0.0), axis=0, keepdims=True)
            sel_ref[g, j] = jnp.where(cnt < n_sel, 0.0, NEG_INF)

    bpt = tk // NSA_BLOCK

    def sel_bias(g, kt):
        rows = [jnp.broadcast_to(sel_ref[g, kt * bpt + r], (NSA_BLOCK, tq)) for r in range(bpt)]
        return jnp.concatenate(rows, axis=0)

    def step(kt, carry, near):
        ms, ls = list(carry[0]), list(carry[1])
        off = pl.multiple_of(kt * tk, tk)
        ks = kslc_ref[0, pl.ds(off, tk), :]
        chains = [(g, g, ks, vslc_ref) for g in range(kvh)]
        biases = [sel_bias(g, kt) for g in range(kvh)]
        if near:
            kw = kwin_ref[0, pl.ds(off, tk), :]
            d = pos1 - (off + lax.broadcasted_iota(jnp.int32, (tk, 1), 0))
            causal = jnp.where(d >= 0, 0.0, NEG_INF)
            window = jnp.where((d >= 0) & (d < NSA_WINDOW), 0.0, NEG_INF)
            chains += [(kvh + g, g, kw, vwin_ref) for g in range(kvh)]
            biases = [bs + causal for bs in biases] + [window] * kvh
        scs = [_dot(kk, qpads[g]) + jnp.concatenate([bs] * NSA_GROUP, axis=1)
               for (_, g, kk, _), bs in zip(chains, biases)]
        m_new = [jnp.maximum(ms[c], jnp.max(sc, axis=0, keepdims=True)) for (c, _, _, _), sc in zip(chains, scs)]
        pps = [jnp.exp2(sc - jnp.maximum(mn, 0.5 * NEG_INF)) for sc, mn in zip(scs, m_new)]
        pvs = [_dot(v_ref[0, g * HEAD_DIM:(g + 1) * HEAD_DIM, pl.ds(off, tk)], pp.astype(BF16))
               for (_, g, _, v_ref), pp in zip(chains, pps)]
        for (c, _, _, _), mn, pp, pv in zip(chains, m_new, pps, pvs):
            alpha = jnp.exp2(ms[c] - mn)
            acc_ref[c] = alpha * acc_ref[c] + pv
            ls[c] = alpha * ls[c] + jnp.sum(pp, axis=0, keepdims=True)
            ms[c] = mn
        return tuple(ms), tuple(ls)

    n_chain = 2 * kvh
    acc_ref[...] = jnp.zeros_like(acc_ref)
    carry = ((jnp.full((1, lanes), NEG_INF, F32),) * n_chain, (jnp.zeros((1, lanes), F32),) * n_chain)
    first = jnp.maximum(i * tq - (NSA_WINDOW - 1), 0) // tk
    carry = lax.fori_loop(0, first, lambda kt, cr: step(kt, cr, False), carry)
    _, ls = lax.fori_loop(first, (i + 1) * (tq // tk), lambda kt, cr: step(kt, cr, True), carry)

    ga = ga_ref[0]
    for g in range(kvh):
        o_br = [o_cmp[g]]
        for c in (g, kvh + g):
            l = ls[c]
            o_br.append(jnp.where(l > 0.0, acc_ref[c] / jnp.where(l > 0.0, l, 1.0), 0.0))
        for hh in range(NSA_GROUP):
            h = g * NSA_GROUP + hh
            sl = slice(hh * tq, (hh + 1) * tq)
            o = ga[h:h + 1] * o_br[0][:, sl]
            for br in range(1, NSA_BRANCHES):
                o = o + ga[br * NSA_HEADS + h:br * NSA_HEADS + h + 1] * o_br[br][:, sl]
            o_ref[0, h * HEAD_DIM:(h + 1) * HEAD_DIM, :] = o


def _nsa_prompt(qat, kb, vt, pooled, gat, *, batch, seq, tq=256, tk=256):
    assert tq % tk == 0 and seq % tq == 0 and tk % NSA_BLOCK == 0
    nb = seq // NSA_BLOCK
    nq = seq // tq
    return pl.pallas_call(
        functools.partial(_nsa_prompt_kernel, tq=tq, tk=tk, nb=nb),
        grid=(batch, nq),
        in_specs=[pl.BlockSpec((1, NSA_WIDTH, tq), lambda b, i: (b, 0, i)),
                  pl.BlockSpec((1, seq, KV_PAIR), lambda b, i: (b, 0, 0)),
                  pl.BlockSpec((1, seq, KV_PAIR), lambda b, i: (b, 0, 1)),
                  pl.BlockSpec((1, KV_PAIR, seq), lambda b, i: (b, 0, 0)),
                  pl.BlockSpec((1, KV_PAIR, seq), lambda b, i: (b, 1, 0)),
                  pl.BlockSpec((1, nb, 2 * KV_PAIR), lambda b, i: (b, 0, 0)),
                  pl.BlockSpec((1, LANES, tq), lambda b, i: (b, 0, i))],
        out_specs=pl.BlockSpec((1, NSA_WIDTH, tq), lambda b, i: (b, 0, i)),
        out_shape=jax.ShapeDtypeStruct((batch, NSA_WIDTH, seq), F32),
        scratch_shapes=[pltpu.VMEM((NSA_KV_HEADS, nb, 1, tq), F32),
                        pltpu.VMEM((2 * NSA_KV_HEADS, HEAD_DIM, NSA_GROUP * tq), F32)],
        compiler_params=pltpu.CompilerParams(dimension_semantics=("arbitrary",) * 2, vmem_limit_bytes=VMEM_LIMIT),
        name="nsa_prompt",
    )(qat, kb, kb, vt, vt, pooled, gat)


def _softplus(z):
    return jnp.maximum(z, 0.0) + jnp.log(1.0 + jnp.exp2(jnp.abs(z) * -LOG2E))


def _split_bf16(x):
    hi = x.astype(BF16)
    return hi, (x - hi.astype(F32)).astype(BF16)


def _mix_prompt_kernel(x_ref, oa_ref, za_ref, ob_ref, zb_ref, gm_ref, wat_ref, wbt_ref, wot_ref, g_ref, y_ref, *,
                       d_model, chain):
    cols = [slice(c, c + chain) for c in range(0, x_ref.shape[0], chain)]
    wat, wbt, wot = wat_ref[...], wbt_ref[...], wot_ref[...]
    ga = [(oa_ref[0, :, c] * za_ref[0, :, c]).astype(BF16) for c in cols]
    gb = [(ob_ref[0, :, c] * zb_ref[0, :, c]).astype(BF16) for c in cols]
    ya = [_dot(wat, v) for v in ga]
    yb = [_dot(wbt, v) for v in gb]
    mixed = [(gm_ref[0, :d_model, c] * a + gm_ref[0, d_model:, c] * b).astype(BF16) for c, a, b in zip(cols, ya, yb)]
    outs = [_dot(wot, v) for v in mixed]
    outs = [o * lax.rsqrt(jnp.mean(o * o, axis=0, keepdims=True) + RMS_EPS) for o in outs]
    for c, o in zip(cols, outs):
        y_ref[c, :] = x_ref[c, :] + o.T * g_ref[...]


def _mix_prompt(x2d, oat, zat, obt, zbt, gmt, wat, wbt, wot, g_post, *, batch, seq, tm, chain=256):
    m, d_model = x2d.shape
    assert seq % tm == 0 and tm % chain == 0
    nt = seq // tm
    row = lambda i: (i, 0)
    full = lambda i: (0, 0)
    tr = lambda i: (i // nt, 0, i % nt)
    return pl.pallas_call(
        functools.partial(_mix_prompt_kernel, d_model=d_model, chain=chain),
        grid=(m // tm,),
        in_specs=[pl.BlockSpec((tm, d_model), row), pl.BlockSpec((1, NSA_WIDTH, tm), tr),
                  pl.BlockSpec((1, NSA_WIDTH, tm), tr), pl.BlockSpec((1, SB_WIDTH, tm), tr),
                  pl.BlockSpec((1, SB_WIDTH, tm), tr), pl.BlockSpec((1, N_MERGE * d_model, tm), tr),
                  pl.BlockSpec((d_model, NSA_WIDTH), full), pl.BlockSpec((d_model, SB_WIDTH), full),
                  pl.BlockSpec((d_model, d_model), full), pl.BlockSpec((1, d_model), full)],
        out_specs=pl.BlockSpec((tm, d_model), row),
        out_shape=jax.ShapeDtypeStruct((m, d_model), F32),
        compiler_params=pltpu.CompilerParams(dimension_semantics=("arbitrary",), vmem_limit_bytes=VMEM_LIMIT),
        name="mix_prompt",
    )(x2d, oat, zat, obt, zbt, gmt, wat, wbt, wot, g_post)


def _mix_sample_kernel(x_ref, oa_ref, za_ref, ob_ref, zb_ref, gm_ref, wat_ref, wbt_ref, wot_ref, g_ref, y_ref, *,
                       d_model):
    ya = _dot_nt((oa_ref[...] * za_ref[...]).astype(BF16), wat_ref[...])
    yb = _dot_nt((ob_ref[...] * zb_ref[...]).astype(BF16), wbt_ref[...])
    gm = gm_ref[...]
    mixed = gm[:, :d_model] * ya + gm[:, d_model:] * yb
    out = _dot_nt(mixed.astype(BF16), wot_ref[...])
    y_ref[...] = x_ref[...] + _rms_scale(out, g_ref[...])


def _mix_sample(x2d, o_a, za, o_b, zb, gm, wat, wbt, wot, g_post):
    m, d_model = x2d.shape
    args = (x2d, o_a, za, o_b, zb, gm, wat, wbt, wot, g_post)
    return pl.pallas_call(
        functools.partial(_mix_sample_kernel, d_model=d_model),
        grid=(1,),
        in_specs=[pl.BlockSpec(a.shape, lambda i: (0, 0)) for a in args],
        out_specs=pl.BlockSpec((m, d_model), lambda i: (0, 0)),
        out_shape=jax.ShapeDtypeStruct((m, d_model), F32),
        compiler_params=pltpu.CompilerParams(dimension_semantics=("arbitrary",), vmem_limit_bytes=VMEM_LIMIT),
        name="mix_sample",
    )(*args)


def _head_pad(q):
    q2 = jnp.concatenate([q, q], axis=1)
    row = lax.broadcasted_iota(jnp.int32, q2.shape, 0)
    lane = lax.broadcasted_iota(jnp.int32, q2.shape, 1)
    return jnp.where(row // NSA_GROUP == lane // HEAD_DIM, q2, 0.0)


def _stream_kernel(pt_ref, qd_ref, qp_ref, k_ref, v_ref, qa_ref, wb_ref, *refs, ppg, tq, tk, sub, units,
                   steps_per_batch, nbp):
    page_refs = refs[:ppg]
    cmp_refs = refs[ppg:2 * ppg]
    od_ref, op_ref, ocmp_ref, sel_ref = refs[2 * ppg:2 * ppg + 4]
    drun_ref, dacc_ref, runs_ref, acc_ref, pool_ref, state_ref = refs[2 * ppg + 4:]
    s = pl.program_id(1)
    s_lin = pl.program_id(0) * pl.num_programs(1) + s

    @pl.when(s == 0)
    def _():
        drun_ref[...] = jnp.zeros_like(drun_ref)
        dacc_ref[...] = jnp.zeros_like(dacc_ref)

    @pl.when(s_lin == 0)
    def _():
        state_ref[0] = 0
        state_ref[1] = 0

    hrow = lax.broadcasted_iota(jnp.int32, (SB_HEADS, SB_WIDTH), 0)
    hlane = lax.broadcasted_iota(jnp.int32, (SB_HEADS, SB_WIDTH), 1)
    diag_blocks = hrow == hlane // HEAD_DIM

    def decode_scores():
        q = qd_ref[0]
        qbd = jnp.where(diag_blocks, jnp.concatenate([q] * SB_HEADS, axis=1), 0.0).astype(BF16)
        z = jnp.concatenate([_dot(qbd, pr[0, :SB_WIDTH, :].astype(BF16)) for pr in page_refs], axis=0)
        sp = _softplus(z)
        r = lax.broadcasted_iota(jnp.int32, (PAGE_SIZE, PAGE_SIZE), 0)
        c = lax.broadcasted_iota(jnp.int32, (PAGE_SIZE, PAGE_SIZE), 1)
        lower = jnp.where(r > c, 1.0, 0.0).astype(BF16)
        hi, lo = _split_bf16(sp)
        return z, sp, _dot(hi, lower) + _dot(lo, lower)

    def decode_update(z, sp, drop):
        tot = jnp.sum(sp, axis=1, keepdims=True)
        run = drun_ref[...]
        dacc = dacc_ref[...]
        for n in range(ppg):
            sl = slice(n * SB_HEADS, (n + 1) * SB_HEADS)
            a = jnp.exp(z[sl] - sp[sl] - drop[sl] - run)
            dacc = dacc + _dot_nt(a.astype(BF16), page_refs[n][0, SB_WIDTH:, :].astype(BF16))
            run = run + tot[sl]
        drun_ref[...] = run
        dacc_ref[...] = dacc

    def pool_compressed():
        group = LANES // (ppg * BLOCKS_PER_PAGE)
        slot = s % group
        chunk = s // group
        xk = jnp.concatenate([pr[0, :KV_PAIR, :].astype(BF16) for pr in cmp_refs], axis=1)
        xv = jnp.concatenate([pr[0, KV_PAIR:, :].astype(BF16) for pr in cmp_refs], axis=1)
        part = jnp.concatenate([_dot(xk, wb_ref[0, slot]), _dot(xv, wb_ref[1, slot])], axis=0)
        pool_ref[chunk] = jnp.where(slot == 0, 0.0, pool_ref[chunk]) + part

    pairs = SB_HEADS // 2
    lanes = 2 * tq
    diag = tq // tk
    per_batch = diag * units * (units + 1) // 2
    local = s_lin % steps_per_batch
    n_steps = ((local + 1) * per_batch) // steps_per_batch - (local * per_batch) // steps_per_batch

    ur = lax.broadcasted_iota(jnp.int32, (sub, 2 * sub), 0)
    uc = lax.broadcasted_iota(jnp.int32, (sub, 2 * sub), 1) % sub
    upper2 = jnp.where(uc > ur, 1.0, 0.0).astype(BF16)
    z0 = jnp.zeros((HEAD_DIM, tq), BF16)

    def later_sum(sp, run):
        parts = []
        for hb in reversed(range(tk // sub)):
            blk = sp[hb * sub:(hb + 1) * sub]
            parts.append(_dot(upper2, jnp.concatenate(_split_bf16(blk), axis=0)) + run)
            run = run + jnp.sum(blk, axis=0, keepdims=True)
        return jnp.concatenate(parts[::-1], axis=0), run

    def tile_step(with_decode):
        i = state_ref[0]
        j = state_ref[1]
        n_t = (i + 1) * diag
        off = pl.multiple_of((n_t - 1 - j) * tk, tk)
        qoff = pl.multiple_of(i * tq, tq)

        @pl.when(j == 0)
        def _():
            runs_ref[...] = jnp.zeros_like(runs_ref)
            acc_ref[...] = jnp.zeros_like(acc_ref)

        def compute(masked):
            if with_decode:
                dz, dsp, ddrop = decode_scores()
                pool_compressed()
            kk = k_ref[0, pl.ds(off, tk), :]
            qpads = []
            for jp in range(pairs):
                qt = qp_ref[0, jp * LANES:(jp + 1) * LANES, pl.ds(qoff, tq)]
                qpads.append(jnp.concatenate([jnp.concatenate([qt[:HEAD_DIM], z0], axis=0),
                                              jnp.concatenate([z0, qt[HEAD_DIM:]], axis=0)], axis=1))
            if masked:
                pos = qoff + lax.broadcasted_iota(jnp.int32, (1, lanes), 1) % tq
                mask = (off + lax.broadcasted_iota(jnp.int32, (tk, 1), 0)) < pos
            zs = [_dot(kk[:, jp * LANES:(jp + 1) * LANES], qpads[jp]) for jp in range(pairs)]
            sps = [jnp.where(mask, _softplus(zz), 0.0) if masked else _softplus(zz) for zz in zs]
            later = [later_sum(spj, runs_ref[jp]) for jp, spj in enumerate(sps)]
            drops = [d for d, _ in later]
            if with_decode:
                decode_update(dz, dsp, ddrop)
            ws = [jnp.exp(zz - spj - dr) for zz, spj, dr in zip(zs, sps, drops)]
            if masked:
                ws = [jnp.where(mask, w, 0.0) for w in ws]
            for h in range(SB_HEADS):
                w = ws[h // 2][:, (h % 2) * tq:(h % 2 + 1) * tq].astype(BF16)
                acc_ref[h] = acc_ref[h] + _dot(v_ref[0, h * HEAD_DIM:(h + 1) * HEAD_DIM, pl.ds(off, tk)], w)
            for jp, (_, run) in enumerate(later):
                runs_ref[jp] = run

        @pl.when(j < diag)
        def _():
            compute(True)

        @pl.when(j >= diag)
        def _():
            compute(False)

        last = j == n_t - 1

        @pl.when(last)
        def _():
            for h in range(SB_HEADS):
                op_ref[0, h * HEAD_DIM:(h + 1) * HEAD_DIM, pl.ds(qoff, tq)] = acc_ref[h]

        state_ref[1] = jnp.where(last, 0, j + 1)
        state_ref[0] = jnp.where(last, (i + 1) % units, i)

    @pl.when(n_steps > 0)
    def _():
        tile_step(True)

    @pl.when(n_steps == 0)
    def _():
        decode_update(*decode_scores())
        pool_compressed()

    def rest(_, carry):
        tile_step(False)
        return carry

    lax.fori_loop(1, n_steps, rest, 0)

    @pl.when(s == pl.num_programs(1) - 1)
    def _():
        od_ref[0] = jnp.sum(jnp.where(diag_blocks, dacc_ref[...], 0.0), axis=0, keepdims=True)
        _compressed_decode(qa_ref[0], pool_ref, ocmp_ref, sel_ref, nbp)


def _stream_attention(page_table_flat, q_dec, sb_cache_t, qbt, kbb, vbt, qa_dec, w_cmp, nsa_cache_t, *, dec_batch,
                      n_pages, batch, seq, ppg, tq=256, tk=256, sub=128):
    steps = n_pages // ppg
    assert tq % tk == 0 and tk % sub == 0 and seq % tq == 0 and (dec_batch * steps) % batch == 0
    assert LANES % (ppg * BLOCKS_PER_PAGE) == 0
    spb = dec_batch * steps // batch
    nbp = n_pages * BLOCKS_PER_PAGE
    wb = _decode_pool_weights(w_cmp, ppg)

    def page_map(n):
        return lambda b, s, pt: (pt[b * n_pages + n_pages - 1 - (s * ppg + n)], 0, 0)

    def cmp_map(n):
        return lambda b, s, pt: (pt[b * n_pages + s * ppg + n], 0, 0)

    prow = lambda b, s, pt: ((b * steps + s) // spb, 0, 0)
    sample = lambda b, s, pt: (b, 0, 0)
    grid_spec = pltpu.PrefetchScalarGridSpec(
        num_scalar_prefetch=1, grid=(dec_batch, steps),
        in_specs=[pl.BlockSpec((1, SB_HEADS, HEAD_DIM), sample),
                  pl.BlockSpec((1, SB_WIDTH, seq), prow), pl.BlockSpec((1, seq, SB_WIDTH), prow),
                  pl.BlockSpec((1, SB_WIDTH, seq), prow),
                  pl.BlockSpec((1, NSA_HEADS, HEAD_DIM), sample),
                  pl.BlockSpec(wb.shape, lambda b, s, pt: (0, 0, 0, 0))]
        + [pl.BlockSpec((1, 2 * SB_WIDTH, PAGE_SIZE), page_map(n)) for n in range(ppg)]
        + [pl.BlockSpec((1, 2 * KV_PAIR, PAGE_SIZE), cmp_map(n)) for n in range(ppg)],
        out_specs=[pl.BlockSpec((1, 1, SB_WIDTH), sample), pl.BlockSpec((1, SB_WIDTH, seq), prow),
                   pl.BlockSpec((1, NSA_HEADS, HEAD_DIM), sample),
                   pl.BlockSpec((1, NSA_KV_HEADS, NSA_TOPK, LANES), lambda b, s, pt: (b, 0, 0, 0))],
        scratch_shapes=[pltpu.VMEM((SB_HEADS, 1), F32), pltpu.VMEM((SB_HEADS, SB_WIDTH), F32),
                        pltpu.VMEM((SB_HEADS // 2, 1, 2 * tq), F32), pltpu.VMEM((SB_HEADS, HEAD_DIM, tq), F32),
                        pltpu.VMEM((pl.cdiv(nbp, LANES), 2 * KV_PAIR, LANES), F32),
                        pltpu.SMEM((2,), jnp.int32)])
    return pl.pallas_call(
        functools.partial(_stream_kernel, ppg=ppg, tq=tq, tk=tk, sub=sub, units=seq // tq, steps_per_batch=spb,
                          nbp=nbp),
        grid_spec=grid_spec,
        out_shape=[jax.ShapeDtypeStruct((dec_batch, 1, SB_WIDTH), F32),
                   jax.ShapeDtypeStruct((batch, SB_WIDTH, seq), F32),
                   jax.ShapeDtypeStruct((dec_batch, NSA_HEADS, HEAD_DIM), F32),
                   jax.ShapeDtypeStruct((dec_batch, NSA_KV_HEADS, NSA_TOPK, LANES), jnp.int32)],
        compiler_params=pltpu.CompilerParams(dimension_semantics=("arbitrary", "arbitrary"),
                                             vmem_limit_bytes=VMEM_LIMIT),
        name="stream_attention",
    )(page_table_flat, q_dec, qbt, kbb, vbt, qa_dec, wb, *([sb_cache_t] * ppg), *([nsa_cache_t] * ppg))


def _compressed_decode(q, pool_ref, ocmp_ref, sel_ref, nbp):
    n_chunks = pool_ref.shape[0]
    nl = n_chunks * LANES
    qpad = _head_pad(q).astype(BF16)
    pooled = jnp.concatenate([pool_ref[ch] for ch in range(n_chunks)], axis=1)
    blk_l = lax.broadcasted_iota(jnp.int32, (1, nl), 1)
    live = blk_l < nbp
    sc = jnp.where(live, _dot(qpad, pooled[:KV_PAIR].astype(BF16)), NEG_INF)
    e = jnp.where(live, jnp.exp(sc - jnp.max(sc, axis=1, keepdims=True)), 0.0)
    p = e / jnp.sum(e, axis=1, keepdims=True)
    o = _dot_nt(p.astype(BF16), pooled[KV_PAIR:].astype(BF16))
    hrow = lax.broadcasted_iota(jnp.int32, o.shape, 0)
    ocmp_ref[0] = jnp.where(hrow < NSA_GROUP, o, pltpu.roll(o, HEAD_DIM, 1))[:, :HEAD_DIM]

    blk_s = lax.broadcasted_iota(jnp.int32, (nl, 1), 0)
    eye = lax.broadcasted_iota(jnp.int32, (nl, nl), 0) == lax.broadcasted_iota(jnp.int32, (nl, nl), 1)
    n_sel = NSA_TOPK - 1
    kslot = lax.broadcasted_iota(jnp.int32, (NSA_TOPK, nl), 0)
    for gi in range(NSA_KV_HEADS):
        imp = jnp.sum(p[gi * NSA_GROUP:(gi + 1) * NSA_GROUP], axis=0, keepdims=True)
        forced = (blk_l == 0) | (blk_l == nbp - 1)
        srow = jnp.where(live, jnp.where(forced, FORCE_SCORE, imp), NEG_INF)
        scol = jnp.sum(jnp.where(eye, srow, 0.0), axis=1, keepdims=True)
        beats = (scol > srow) | ((scol == srow) & (blk_s < blk_l))
        rank = jnp.sum(jnp.where(beats, 1.0, 0.0), axis=0, keepdims=True)
        sel = rank < n_sel
        selcol = jnp.sum(jnp.where(eye & sel, 1.0, 0.0), axis=1, keepdims=True) > 0.5
        rank_sel = jnp.sum(jnp.where(selcol & (blk_s < blk_l), 1.0, 0.0), axis=0, keepdims=True)
        onehot = sel & (rank_sel.astype(jnp.int32) == kslot)
        idx = jnp.sum(jnp.where(onehot, blk_l, 0), axis=1, keepdims=True)
        sel_ref[0, gi] = jnp.broadcast_to(idx, (NSA_TOPK, LANES))


def _decode_pool_weights(w_cmp, ppg):
    per_step = ppg * BLOCKS_PER_PAGE
    group = LANES // per_step
    k = jnp.arange(ppg * PAGE_SIZE)
    target = (k // PAGE_SIZE) * BLOCKS_PER_PAGE + (k % PAGE_SIZE) // NSA_BLOCK
    hit = jnp.arange(LANES)[None, None, :] == (jnp.arange(group)[:, None, None] * per_step + target[None, :, None])
    w_row = jnp.tile(w_cmp, (1, ppg * BLOCKS_PER_PAGE))
    return jnp.where(hit[None], w_row[:, None, :, None], 0.0).astype(BF16)


def _nsa_sel_decode_kernel(sel_ref, pt_ref, q_ref, ocmp_ref, ga_ref, new_ref, winp_ref, *refs, n_blk):
    blk_refs = refs[:NSA_KV_HEADS * n_blk]
    o_ref, wino_ref = refs[NSA_KV_HEADS * n_blk:]
    b = pl.program_id(0)
    qb = _head_pad(q_ref[0]).astype(BF16)
    new = new_ref[0]
    hrow = lax.broadcasted_iota(jnp.int32, (NSA_HEADS, KV_PAIR), 0)
    top = hrow < NSA_GROUP

    def fold(o):
        return jnp.where(top, o, pltpu.roll(o, HEAD_DIM, 1))[:, :HEAD_DIM]

    def softmax(sc, mask):
        sc = jnp.where(mask, sc, NEG_INF)
        e = jnp.where(mask, jnp.exp(sc - jnp.max(sc, axis=1, keepdims=True)), 0.0)
        return e / jnp.sum(e, axis=1, keepdims=True)

    lane = lax.broadcasted_iota(jnp.int32, (1, PAGE_SIZE), 1)
    first = lax.broadcasted_iota(jnp.int32, (KV_PAIR, PAGE_SIZE), 1) == 0
    k_new = jnp.where(first, new[2 * KV_PAIR:3 * KV_PAIR], 0.0).astype(BF16)
    v_new = jnp.where(first, new[3 * KV_PAIR:4 * KV_PAIR], 0.0).astype(BF16)
    o_g = []
    for gi in range(NSA_KV_HEADS):
        scs, masks = [], []
        for n in range(n_blk):
            half = sel_ref[(b * NSA_KV_HEADS + gi) * NSA_TOPK + n] % BLOCKS_PER_PAGE
            scs.append(_dot(qb, blk_refs[gi * n_blk + n][0, :KV_PAIR, :].astype(BF16)))
            masks.append(lane // NSA_BLOCK == half)
        scs.append(_dot(qb, k_new))
        masks.append(lane == 0)
        p = softmax(jnp.concatenate(scs, axis=1), jnp.concatenate(masks, axis=1)).astype(BF16)
        o = _dot_nt(p[:, n_blk * PAGE_SIZE:], v_new)
        for n in range(n_blk):
            o = o + _dot_nt(p[:, n * PAGE_SIZE:(n + 1) * PAGE_SIZE],
                            blk_refs[gi * n_blk + n][0, KV_PAIR:, :].astype(BF16))
        o_g.append(o)
    o_slc = fold(jnp.where(top, o_g[0], o_g[1]))

    wp = winp_ref[0]
    w = wp.shape[1]
    wl = lax.broadcasted_iota(jnp.int32, wp.shape, 1)
    shifted = jnp.where(wl == w - 1, new[4 * KV_PAIR:], pltpu.roll(wp, w - 1, 1))
    wino_ref[0] = shifted
    sc = _dot(qb, shifted[:KV_PAIR].astype(BF16))
    p = softmax(sc, jnp.full((1, w), True)).astype(BF16)
    o_win = fold(_dot_nt(p, shifted[KV_PAIR:].astype(BF16)))

    ga = ga_ref[0]
    o_ref[0] = ga[0] * ocmp_ref[0] + ga[1] * o_slc + ga[2] * o_win


def _nsa_sel_decode(sel_flat, page_table_flat, q_a, o_cmp, ga, new_kv, win_past_t, cache_t, *, dec_batch, n_pages):
    n_blk = NSA_TOPK - 1
    w = win_past_t.shape[2]

    def blk_map(gi, n):
        def f(b, sel, pt):
            blk = sel[(b * NSA_KV_HEADS + gi) * NSA_TOPK + n]
            return (pt[b * n_pages + blk // BLOCKS_PER_PAGE], 1, 0)
        return f

    grid_spec = pltpu.PrefetchScalarGridSpec(
        num_scalar_prefetch=2, grid=(dec_batch,),
        in_specs=[pl.BlockSpec((1, NSA_HEADS, HEAD_DIM), lambda b, sel, pt: (b, 0, 0)),
                  pl.BlockSpec((1, NSA_HEADS, HEAD_DIM), lambda b, sel, pt: (b, 0, 0)),
                  pl.BlockSpec((1, NSA_BRANCHES, NSA_HEADS, 1), lambda b, sel, pt: (b, 0, 0, 0)),
                  pl.BlockSpec((1, KVA_WIDTH, 1), lambda b, sel, pt: (b, 0, 0)),
                  pl.BlockSpec((1, 2 * KV_PAIR, w), lambda b, sel, pt: (b, 0, 0))]
        + [pl.BlockSpec((1, 2 * KV_PAIR, PAGE_SIZE), blk_map(gi, n))
           for gi in range(NSA_KV_HEADS) for n in range(n_blk)],
        out_specs=[pl.BlockSpec((1, NSA_HEADS, HEAD_DIM), lambda b, sel, pt: (b, 0, 0)),
                   pl.BlockSpec((1, 2 * KV_PAIR, w), lambda b, sel, pt: (b, 0, 0))])
    return pl.pallas_call(
        functools.partial(_nsa_sel_decode_kernel, n_blk=n_blk),
        grid_spec=grid_spec,
        out_shape=[jax.ShapeDtypeStruct((dec_batch, NSA_HEADS, HEAD_DIM), F32),
                   jax.ShapeDtypeStruct((dec_batch, 2 * KV_PAIR, w), F32)],
        compiler_params=pltpu.CompilerParams(dimension_semantics=("arbitrary",), vmem_limit_bytes=VMEM_LIMIT),
        name="nsa_sel_decode",
    )(sel_flat, page_table_flat, q_a, o_cmp, ga, new_kv, win_past_t, *([cache_t] * (NSA_KV_HEADS * n_blk)))


def _feature_major(a, lead):
    nl = len(lead)
    t = jnp.moveaxis(a, nl, -1)
    return t.reshape(lead + (-1, a.shape[nl]))


def _time_major(a_t, feat_shape):
    lead, _, time = a_t.shape
    return jnp.moveaxis(a_t.reshape((lead,) + feat_shape + (time,)), -1, 1)


def _layer(x_prompt, x_sample, nsa_cache, win_cache, sb_cache, page_table, w_in, w_cmp, w_a, w_b, w_o, g_pre, g_post):
    batch, seq, d_model = x_prompt.shape
    dec_batch, dec_seq, _ = x_sample.shape
    n_pages = page_table.shape[1]
    past_len = n_pages * PAGE_SIZE
    n_phys = nsa_cache.shape[0]
    tm, ppg = 256, 16
    assert dec_seq == 1 and seq % tm == 0 and seq >= NSA_WINDOW and past_len >= NSA_WINDOW
    assert n_pages % ppg == 0 and past_len // NSA_BLOCK >= NSA_TOPK
    assert win_cache.shape[1] == NSA_WINDOW and nsa_cache.shape[1] == PAGE_SIZE

    wt_perm = _permute_w_in_t(w_in, d_model)
    wat, wbt, wot = w_a.T.astype(BF16), w_b.T.astype(BF16), w_o.T.astype(BF16)
    g_pre2, g_post2 = g_pre.reshape(1, d_model), g_post.reshape(1, d_model)
    nsa_feat = (4, NSA_KV_HEADS, HEAD_DIM)
    win_feat = (2, NSA_KV_HEADS, HEAD_DIM)
    sb_feat = (2, SB_HEADS, HEAD_DIM)

    xp = x_prompt.reshape(batch * seq, d_model)
    cos, sin = _rope_angles(jnp.arange(seq, dtype=jnp.int32))
    wp_prompt = _pool_weights(w_cmp, 1, tm // NSA_BLOCK, SUBLANES).reshape(2 * SUBLANES, tm)
    (nsa_t, win_t, sb_t, za_t, zb_t, gm_t, qa_t, ga_t, kb, v_t, pooled, qb_t, kbb, vb_t) = _project_prompt(
        xp, g_pre2, wt_perm, cos.T, sin.T, wp_prompt, batch=batch, seq=seq, tm=tm)
    xs = x_sample.reshape(dec_batch, d_model)
    tabs_s = _rope_lane_tables(jnp.full((dec_batch,), past_len, jnp.int32))
    (nsa_s, win_s, sb_s, za_s, zb_s, gm_s, qa_s, ga_s, qb_s) = _project_sample(xs, g_pre2, wt_perm, tabs_s)
    pt_flat = page_table.reshape(-1)
    sb_cache_t = _feature_major(sb_cache, (n_phys,))
    nsa_cache_t = _feature_major(nsa_cache, (n_phys,))
    win_cache_t = _feature_major(win_cache, (dec_batch,))

    qa3 = qa_s.reshape(dec_batch, NSA_HEADS, HEAD_DIM)
    o_b_s, ob_t, o_cmp_s, sel = _stream_attention(
        pt_flat, qb_s.reshape(dec_batch, SB_HEADS, HEAD_DIM), sb_cache_t, qb_t, kbb.reshape(batch, seq, SB_WIDTH), vb_t,
        qa3, w_cmp, nsa_cache_t, dec_batch=dec_batch, n_pages=n_pages, batch=batch, seq=seq, ppg=ppg)

    oa_t = _nsa_prompt(qa_t, kb.reshape(batch, seq, 2 * KV_PAIR), v_t,
                       pooled.reshape(batch, seq // NSA_BLOCK, 2 * KV_PAIR), ga_t, batch=batch, seq=seq)
    y_prompt = _mix_prompt(xp, oa_t, za_t, ob_t, zb_t, gm_t, wat, wbt, wot, g_post2,
                           batch=batch, seq=seq, tm=4 * tm).reshape(batch, seq, d_model)
    nsa_kv_prompt = _time_major(nsa_t, nsa_feat)
    win_kv_prompt = _time_major(win_t[:, :, seq - NSA_WINDOW:], win_feat)
    sb_kv_prompt = _time_major(sb_t, sb_feat)

    ga3 = ga_s[:, :GATE_A].reshape(dec_batch, NSA_BRANCHES, NSA_HEADS, 1)
    new_kv = jnp.concatenate([nsa_s, win_s], axis=1).reshape(dec_batch, KVA_WIDTH, 1)
    o_a_s, win_out_t = _nsa_sel_decode(sel[:, :, :, 0].reshape(-1), pt_flat, qa3, o_cmp_s, ga3, new_kv,
                                       win_cache_t, nsa_cache_t, dec_batch=dec_batch, n_pages=n_pages)
    y_sample = _mix_sample(xs, o_a_s.reshape(dec_batch, NSA_WIDTH), za_s, o_b_s.reshape(dec_batch, SB_WIDTH), zb_s,
                           gm_s, wat, wbt, wot, g_post2).reshape(dec_batch, 1, d_model)
    nsa_kv_sample = nsa_s.reshape((dec_batch, 1) + nsa_feat)
    win_kv_sample = _time_major(win_out_t, win_feat)
    sb_kv_sample = sb_s.reshape((dec_batch, 1) + sb_feat)
    return (y_prompt, y_sample, nsa_kv_prompt, win_kv_prompt, sb_kv_prompt, nsa_kv_sample, win_kv_sample,
            sb_kv_sample)


def kernel(x_prompt, x_sample, cache_nsa_kv, cache_nsa_win_kv, cache_sb_kv, page_table, w_in, w_cmp, w_branch_a,
           w_branch_b, w_out, g_pre, g_post):
    hp, hs = x_prompt, x_sample
    caches = [[] for _ in range(6)]
    for layer in range(w_in.shape[0]):
        outs = _layer(hp, hs, cache_nsa_kv[layer], cache_nsa_win_kv[layer], cache_sb_kv[layer], page_table,
                      w_in[layer], w_cmp[layer], w_branch_a[layer], w_branch_b[layer], w_out[layer],
                      g_pre[layer], g_post[layer])
        hp, hs = outs[0], outs[1]
        for acc, o in zip(caches, outs[2:]):
            acc.append(o)
    return (hp, hs) + tuple(jnp.stack(c) for c in caches)
```

```python
import functools

import jax
import jax.numpy as jnp
from jax import lax
from jax.experimental import pallas as pl
from jax.experimental.pallas import tpu as pltpu

HEAD_DIM = 64
ROT_DIM = HEAD_DIM // 4
ROPE_THETA = 500000.0
NSA_HEADS = 8
NSA_KV_HEADS = 2
NSA_GROUP = NSA_HEADS // NSA_KV_HEADS
NSA_BRANCHES = 3
NSA_BLOCK = 64
NSA_TOPK = 16
NSA_WINDOW = 512
NSA_WIDTH = NSA_HEADS * HEAD_DIM
SB_HEADS = 8
SB_WIDTH = SB_HEADS * HEAD_DIM
N_MERGE = 2
PAGE_SIZE = 128
RMS_EPS = 1e-6
NEG_INF = -1e30
FORCE_SCORE = 1e3
SCALE = HEAD_DIM ** -0.5
LOG2E = 1.4426950408889634

LANES = 128
SUBLANES = 8
KV_PAIR = NSA_KV_HEADS * HEAD_DIM
assert KV_PAIR == LANES and PAGE_SIZE == LANES and PAGE_SIZE % NSA_BLOCK == 0
KVA_WIDTH = 2 * NSA_BRANCHES * KV_PAIR
GATE_A = NSA_BRANCHES * NSA_HEADS
BLOCKS_PER_PAGE = PAGE_SIZE // NSA_BLOCK
VMEM_LIMIT = 56 * 1024 * 1024

C_QA = 0
C_KVA = C_QA + NSA_WIDTH
C_ZA = C_KVA + KVA_WIDTH
C_QKVB = C_ZA + NSA_WIDTH
C_ZB = C_QKVB + 3 * SB_WIDTH
C_GM = C_ZB + SB_WIDTH

BF16 = jnp.bfloat16
F32 = jnp.float32


def _dot(a, b):
    return jnp.dot(a, b, preferred_element_type=F32)


def _dot_tn(a, b):
    return lax.dot_general(a, b, (((0,), (0,)), ((), ())), preferred_element_type=F32)


def _dot_nt(a, b):
    return lax.dot_general(a, b, (((1,), (1,)), ((), ())), preferred_element_type=F32)


def _sigmoid(x):
    return 1.0 / (1.0 + jnp.exp(-x))


def _rms_scale(x, g):
    return x * lax.rsqrt(jnp.mean(x * x, axis=-1, keepdims=True) + RMS_EPS) * g


def _rope_rows(v, cos, sin):
    half = ROT_DIM // 2
    parts = []
    for base in range(0, v.shape[0], HEAD_DIM):
        x1, x2 = v[base:base + half], v[base + half:base + 2 * half]
        parts += [x1 * cos - x2 * sin, x2 * cos + x1 * sin, v[base + 2 * half:base + HEAD_DIM]]
    return jnp.concatenate(parts, axis=0)


def _proj_prompt_kernel(x_ref, g_ref, wt_ref, cos_ref, sin_ref, wp_ref,
                        nsa_ref, win_ref, sb_ref, za_ref, zb_ref, gm_ref, qa_ref, ga_ref, kb_ref, vt_ref,
                        pool_ref, qb_ref, kbb_ref, vbt_ref, *, d_model, chain):
    rows = [slice(c, c + chain) for c in range(0, x_ref.shape[0], chain)]
    hbs = [_rms_scale(x_ref[r, :], g_ref[...]).astype(BF16) for r in rows]
    cs = [(cos_ref[:, r], sin_ref[:, r]) for r in rows]
    wp = wp_ref[...]
    nblk = chain // NSA_BLOCK

    def seg(lo, width):
        w = wt_ref[lo:lo + width, :]
        return [_dot_nt(w, hb) for hb in hbs]

    for r, y, (cos, sin) in zip(rows, seg(C_QA, NSA_WIDTH), cs):
        qa_ref[0, :, r] = (_rope_rows(y, cos, sin) * (SCALE * LOG2E)).astype(BF16)
    for n, (r, kv, (cos, sin)) in enumerate(zip(rows, seg(C_KVA, KVA_WIDTH), cs)):
        cmp_k = _rope_rows(kv[0 * LANES:1 * LANES], cos, sin)
        cmp_v = kv[1 * LANES:2 * LANES]
        slc_k = _rope_rows(kv[2 * LANES:3 * LANES], cos, sin)
        slc_v = kv[3 * LANES:4 * LANES]
        win_k = _rope_rows(kv[4 * LANES:5 * LANES], cos, sin)
        win_v = kv[5 * LANES:6 * LANES]
        nsa_ref[0, :, r] = jnp.concatenate([cmp_k, cmp_v, slc_k, slc_v], axis=0)
        win_ref[0, :, r] = jnp.concatenate([win_k, win_v], axis=0)
        kb_ref[r, :] = jnp.concatenate([slc_k.T, win_k.T], axis=1).astype(BF16)
        vt_ref[0, :, r] = jnp.concatenate([slc_v, win_v], axis=0).astype(BF16)
        pooled = jnp.concatenate([_dot_nt(wp[:SUBLANES], cmp_k.astype(BF16)),
                                  _dot_nt(wp[SUBLANES:], cmp_v.astype(BF16))], axis=1)
        pool_ref[0, n * nblk:(n + 1) * nblk, :] = pooled[:nblk]
    for r, za in zip(rows, seg(C_ZA, NSA_WIDTH)):
        za_ref[0, :, r] = (za * _sigmoid(za)).astype(za_ref.dtype)
    for r, qkvb in zip(rows, seg(C_QKVB, 3 * SB_WIDTH)):
        qb_ref[0, :, r] = (qkvb[:SB_WIDTH] * SCALE).astype(BF16)
        sb_ref[0, :, r] = qkvb[SB_WIDTH:]
        kbb_ref[r, :] = qkvb[SB_WIDTH:2 * SB_WIDTH].T.astype(BF16)
        vbt_ref[0, :, r] = qkvb[2 * SB_WIDTH:].astype(BF16)
    for r, zb in zip(rows, seg(C_ZB, SB_WIDTH)):
        zb_ref[0, :, r] = (zb * _sigmoid(zb)).astype(zb_ref.dtype)
    for r, gm in zip(rows, seg(C_GM, N_MERGE * d_model)):
        gm_ref[0, :, r] = _sigmoid(gm).astype(gm_ref.dtype)
    for r, ga in zip(rows, seg(C_GM + N_MERGE * d_model, LANES)):
        ga_ref[0, :, r] = _sigmoid(ga)


def _project_prompt(x2d, g_pre, wt_perm, cos_t, sin_t, wp, *, batch, seq, tm, chain):
    m, d_model = x2d.shape
    assert tm % chain == 0 and chain % NSA_BLOCK == 0
    nt = seq // tm
    n_rows = wt_perm.shape[0]
    nblk = tm // NSA_BLOCK
    row = lambda i: (i, 0)
    full = lambda i: (0, 0)
    tab = lambda i: (0, i % nt)
    tr = lambda i: (i // nt, 0, i % nt)
    half = ROT_DIM // 2
    in_specs = [pl.BlockSpec((tm, d_model), row), pl.BlockSpec((1, d_model), full),
                pl.BlockSpec((n_rows, d_model), full),
                pl.BlockSpec((half, tm), tab), pl.BlockSpec((half, tm), tab),
                pl.BlockSpec((2 * SUBLANES, chain), full)]
    sds = jax.ShapeDtypeStruct

    def feat(width, dtype):
        return sds((batch, width, seq), dtype), pl.BlockSpec((1, width, tm), tr)

    def rows(width, dtype):
        return sds((m, width), dtype), pl.BlockSpec((tm, width), row)

    outs = [feat(4 * KV_PAIR, F32), feat(2 * KV_PAIR, F32), feat(2 * SB_WIDTH, F32),
            feat(NSA_WIDTH, BF16), feat(SB_WIDTH, BF16), feat(N_MERGE * d_model, BF16),
            feat(NSA_WIDTH, BF16), feat(LANES, F32), rows(2 * KV_PAIR, BF16), feat(2 * KV_PAIR, BF16),
            (sds((m // tm, nblk, 2 * KV_PAIR), F32), pl.BlockSpec((1, nblk, 2 * KV_PAIR), lambda i: (i, 0, 0))),
            feat(SB_WIDTH, BF16), rows(SB_WIDTH, BF16), feat(SB_WIDTH, BF16)]
    return pl.pallas_call(
        functools.partial(_proj_prompt_kernel, d_model=d_model, chain=chain),
        grid=(m // tm,), in_specs=in_specs, out_specs=[o[1] for o in outs], out_shape=[o[0] for o in outs],
        compiler_params=pltpu.CompilerParams(dimension_semantics=("arbitrary",), vmem_limit_bytes=VMEM_LIMIT),
        name="proj_prompt",
    )(x2d, g_pre, wt_perm, cos_t, sin_t, wp)


def _rope_lanes(v, c, s_up, s_dn):
    half = ROT_DIM // 2
    outs = []
    for j in range(v.shape[1] // LANES):
        blk = v[:, j * LANES:(j + 1) * LANES]
        outs.append(blk * c + pltpu.roll(blk, LANES - half, 1) * s_up + pltpu.roll(blk, half, 1) * s_dn)
    return outs[0] if len(outs) == 1 else jnp.concatenate(outs, axis=1)


def _proj_sample_kernel(x_ref, g_ref, wt_ref, cos_ref, sup_ref, sdn_ref,
                        nsa_ref, win_ref, sbkv_ref, za_ref, zb_ref, gm_ref, qa_ref, ga_ref, qb_ref, *, d_model):
    hb = _rms_scale(x_ref[...], g_ref[...]).astype(BF16)
    cos, s_up, s_dn = cos_ref[...], sup_ref[...], sdn_ref[...]

    def seg(lo, width):
        return _dot_nt(hb, wt_ref[lo:lo + width, :])

    qa_ref[...] = _rope_lanes(seg(C_QA, NSA_WIDTH), cos, s_up, s_dn) * SCALE
    kv = seg(C_KVA, KVA_WIDTH)
    parts = []
    for j in range(2 * NSA_BRANCHES):
        blk = kv[:, j * LANES:(j + 1) * LANES]
        parts.append(_rope_lanes(blk, cos, s_up, s_dn) if j % 2 == 0 else blk)
    nsa_ref[...] = jnp.concatenate(parts[:4], axis=1)
    win_ref[...] = jnp.concatenate(parts[4:], axis=1)
    za = seg(C_ZA, NSA_WIDTH)
    za_ref[...] = za * _sigmoid(za)
    qkvb = seg(C_QKVB, 3 * SB_WIDTH)
    qb_ref[...] = qkvb[:, :SB_WIDTH] * SCALE
    sbkv_ref[...] = qkvb[:, SB_WIDTH:]
    zb = seg(C_ZB, SB_WIDTH)
    zb_ref[...] = zb * _sigmoid(zb)
    gm_ref[...] = _sigmoid(seg(C_GM, N_MERGE * d_model))
    ga_ref[...] = _sigmoid(seg(C_GM + N_MERGE * d_model, LANES))


def _project_sample(x2d, g_pre, wt_perm, tables):
    m, d_model = x2d.shape
    full = lambda i: (0, 0)
    widths = [4 * KV_PAIR, 2 * KV_PAIR, 2 * SB_WIDTH, NSA_WIDTH, SB_WIDTH, N_MERGE * d_model,
              NSA_WIDTH, LANES, SB_WIDTH]
    return pl.pallas_call(
        functools.partial(_proj_sample_kernel, d_model=d_model),
        grid=(1,),
        in_specs=[pl.BlockSpec((m, d_model), full), pl.BlockSpec((1, d_model), full),
                  pl.BlockSpec(wt_perm.shape, full)] + [pl.BlockSpec((m, LANES), full)] * 3,
        out_specs=[pl.BlockSpec((m, w), full) for w in widths],
        out_shape=[jax.ShapeDtypeStruct((m, w), F32) for w in widths],
        compiler_params=pltpu.CompilerParams(dimension_semantics=("arbitrary",), vmem_limit_bytes=VMEM_LIMIT),
        name="proj_sample",
    )(x2d, g_pre, wt_perm, *tables)


def _rope_angles(pos):
    half = ROT_DIM // 2
    inv_freq = ROPE_THETA ** (-jnp.arange(half, dtype=F32) / half)
    ang = pos.astype(F32)[:, None] * inv_freq[None, :]
    return jnp.cos(ang), jnp.sin(ang)


def _rope_lane_tables(pos):
    cos, sin = _rope_angles(pos)
    n = pos.shape[0]
    half = ROT_DIM // 2
    ones = jnp.ones((n, HEAD_DIM - ROT_DIM), F32)
    zeros = jnp.zeros((n, HEAD_DIM - ROT_DIM), F32)
    zh = jnp.zeros((n, half), F32)
    rep = LANES // HEAD_DIM
    return tuple(jnp.tile(t, (1, rep)) for t in (jnp.concatenate([cos, cos, ones], axis=1),
                                                 jnp.concatenate([-sin, zh, zeros], axis=1),
                                                 jnp.concatenate([zh, sin, zeros], axis=1)))


def _permute_w_in_t(w_in, d_model):
    c = [NSA_WIDTH, KVA_WIDTH, NSA_WIDTH, GATE_A, 3 * SB_WIDTH, SB_WIDTH, N_MERGE * d_model]
    o = [0]
    for s in c:
        o.append(o[-1] + s)
    wt = w_in.T
    pad = jnp.zeros((LANES - GATE_A, d_model), w_in.dtype)
    return jnp.concatenate([wt[:o[3]], wt[o[4]:], wt[o[3]:o[4]], pad], axis=0).astype(BF16)


def _pool_weights(w_cmp, n_tiles, blocks_per_tile, rows):
    lane = jnp.arange(blocks_per_tile * NSA_BLOCK)
    owner = jnp.arange(n_tiles)[:, None, None] * blocks_per_tile + (lane // NSA_BLOCK)[None, None, :]
    hit = jnp.arange(rows)[None, :, None] == owner
    w_lane = jnp.tile(w_cmp, (1, blocks_per_tile))
    return jnp.where(hit[None], w_lane[:, None, None, :], 0.0).astype(BF16)


def _nsa_prompt_kernel(q_ref, kslc_ref, kwin_ref, vslc_ref, vwin_ref, pool_ref, ga_ref, o_ref, sel_ref, acc_ref, *,
                       tq, tk, nb):
    i = pl.program_id(1)
    lanes = NSA_GROUP * tq
    kvh = NSA_KV_HEADS
    row = lax.broadcasted_iota(jnp.int32, (KV_PAIR, lanes), 0)
    pos1 = i * tq + lax.broadcasted_iota(jnp.int32, (1, tq), 1)
    pos = jnp.concatenate([pos1] * NSA_GROUP, axis=1)
    blk = lax.broadcasted_iota(jnp.int32, (nb, 1), 0)
    cmask = ((blk + 1) * NSA_BLOCK - 1) <= pos
    cur = pos1 // NSA_BLOCK
    valid = blk <= cur
    forced = (blk == 0) | (blk == cur) | (blk == cur - 1)
    n_sel = min(NSA_TOPK, nb)
    pooled = pool_ref[0]
    kc = pooled[:, :KV_PAIR].astype(BF16)
    vc = pooled[:, KV_PAIR:].astype(BF16)

    qpads, o_cmp = [], []
    for g in range(kvh):
        qt = q_ref[0, g * NSA_GROUP * HEAD_DIM:(g + 1) * NSA_GROUP * HEAD_DIM, :]
        q4 = jnp.concatenate([qt[hh * HEAD_DIM:(hh + 1) * HEAD_DIM] for hh in range(NSA_GROUP)], axis=1)
        qpad = jnp.where(row // HEAD_DIM == g, jnp.concatenate([q4, q4], axis=0), jnp.zeros((), BF16))
        qpads.append(qpad)
        s = jnp.where(cmask, _dot(kc, qpad), NEG_INF)
        e = jnp.exp2(s - jnp.max(s, axis=0, keepdims=True))
        p = jnp.where(cmask, e / jnp.sum(e, axis=0, keepdims=True), 0.0)
        o_cmp.append(_dot_tn(vc, p.astype(BF16))[g * HEAD_DIM:(g + 1) * HEAD_DIM])
        imp = p[:, 0:tq]
        for hh in range(1, NSA_GROUP):
            imp = imp + p[:, hh * tq:(hh + 1) * tq]
        score = jnp.where(valid, jnp.where(forced, FORCE_SCORE, imp), NEG_INF)
        for j in range(nb):
            sj = score[j:j + 1, :]
            beats = (score > sj) | ((score == sj) & (blk < j))
            cnt = jnp.sum(jnp.where(beats, 1.0, 0.0), axis=0, keepdims=True)
            sel_ref[g, j] = jnp.where(cnt < n_sel, 0.0, NEG_INF)

    bpt = tk // NSA_BLOCK

    def sel_bias(g, kt):
        rows = [jnp.broadcast_to(sel_ref[g, kt * bpt + r], (NSA_BLOCK, tq)) for r in range(bpt)]
        return jnp.concatenate(rows, axis=0)

    def step(kt, carry, near):
        ms, ls = list(carry[0]), list(carry[1])
        off = pl.multiple_of(kt * tk, tk)
        ks = kslc_ref[0, pl.ds(off, tk), :]
        chains = [(g, g, ks, vslc_ref) for g in range(kvh)]
        biases = [sel_bias(g, kt) for g in range(kvh)]
        if near:
            kw = kwin_ref[0, pl.ds(off, tk), :]
            d = pos1 - (off + lax.broadcasted_iota(jnp.int32, (tk, 1), 0))
            causal = jnp.where(d >= 0, 0.0, NEG_INF)
            window = jnp.where((d >= 0) & (d < NSA_WINDOW), 0.0, NEG_INF)
            chains += [(kvh + g, g, kw, vwin_ref) for g in range(kvh)]
            biases = [bs + causal for bs in biases] + [window] * kvh
        scs = [_dot(kk, qpads[g]) + jnp.concatenate([bs] * NSA_GROUP, axis=1)
               for (_, g, kk, _), bs in zip(chains, biases)]
        m_new = [jnp.maximum(ms[c], jnp.max(sc, axis=0, keepdims=True)) for (c, _, _, _), sc in zip(chains, scs)]
        pps = [jnp.exp2(sc - jnp.maximum(mn, 0.5 * NEG_INF)) for sc, mn in zip(scs, m_new)]
        pvs = [_dot(v_ref[0, g * HEAD_DIM:(g + 1) * HEAD_DIM, pl.ds(off, tk)], pp.astype(BF16))
               for (_, g, _, v_ref), pp in zip(chains, pps)]
        for (c, _, _, _), mn, pp, pv in zip(chains, m_new, pps, pvs):
            alpha = jnp.exp2(ms[c] - mn)
            acc_ref[c] = alpha * acc_ref[c] + pv
            ls[c] = alpha * ls[c] + jnp.sum(pp, axis=0, keepdims=True)
            ms[c] = mn
        return tuple(ms), tuple(ls)

    n_chain = 2 * kvh
    acc_ref[...] = jnp.zeros_like(acc_ref)
    carry = ((jnp.full((1, lanes), NEG_INF, F32),) * n_chain, (jnp.zeros((1, lanes), F32),) * n_chain)
    first = jnp.maximum(i * tq - (NSA_WINDOW - 1), 0) // tk
    carry = lax.fori_loop(0, first, lambda kt, cr: step(kt, cr, False), carry)
    _, ls = lax.fori_loop(first, (i + 1) * (tq // tk), lambda kt, cr: step(kt, cr, True), carry)

    ga = ga_ref[0]
    for g in range(kvh):
        o_br = [o_cmp[g]]
        for c in (g, kvh + g):
            l = ls[c]
            o_br.append(jnp.where(l > 0.0, acc_ref[c] / jnp.where(l > 0.0, l, 1.0), 0.0))
        for hh in range(NSA_GROUP):
            h = g * NSA_GROUP + hh
            sl = slice(hh * tq, (hh + 1) * tq)
            o = ga[h:h + 1] * o_br[0][:, sl]
            for br in range(1, NSA_BRANCHES):
                o = o + ga[br * NSA_HEADS + h:br * NSA_HEADS + h + 1] * o_br[br][:, sl]
            o_ref[0, h * HEAD_DIM:(h + 1) * HEAD_DIM, :] = o


def _nsa_prompt(qat, kb, vt, pooled, gat, *, batch, seq, tq=256, tk=256):
    assert tq % tk == 0 and seq % tq == 0 and tk % NSA_BLOCK == 0
    nb = seq // NSA_BLOCK
    nq = seq // tq
    return pl.pallas_call(
        functools.partial(_nsa_prompt_kernel, tq=tq, tk=tk, nb=nb),
        grid=(batch, nq),
        in_specs=[pl.BlockSpec((1, NSA_WIDTH, tq), lambda b, i: (b, 0, i)),
                  pl.BlockSpec((1, seq, KV_PAIR), lambda b, i: (b, 0, 0)),
                  pl.BlockSpec((1, seq, KV_PAIR), lambda b, i: (b, 0, 1)),
                  pl.BlockSpec((1, KV_PAIR, seq), lambda b, i: (b, 0, 0)),
                  pl.BlockSpec((1, KV_PAIR, seq), lambda b, i: (b, 1, 0)),
                  pl.BlockSpec((1, nb, 2 * KV_PAIR), lambda b, i: (b, 0, 0)),
                  pl.BlockSpec((1, LANES, tq), lambda b, i: (b, 0, i))],
        out_specs=pl.BlockSpec((1, NSA_WIDTH, tq), lambda b, i: (b, 0, i)),
        out_shape=jax.ShapeDtypeStruct((batch, NSA_WIDTH, seq), F32),
        scratch_shapes=[pltpu.VMEM((NSA_KV_HEADS, nb, 1, tq), F32),
                        pltpu.VMEM((2 * NSA_KV_HEADS, HEAD_DIM, NSA_GROUP * tq), F32)],
        compiler_params=pltpu.CompilerParams(dimension_semantics=("arbitrary",) * 2, vmem_limit_bytes=VMEM_LIMIT),
        name="nsa_prompt",
    )(qat, kb, kb, vt, vt, pooled, gat)


def _softplus(z):
    return jnp.maximum(z, 0.0) + jnp.log(1.0 + jnp.exp2(jnp.abs(z) * -LOG2E))


def _split_bf16(x):
    hi = x.astype(BF16)
    return hi, (x - hi.astype(F32)).astype(BF16)


def _mix_prompt_kernel(x_ref, oa_ref, za_ref, ob_ref, zb_ref, gm_ref, wat_ref, wbt_ref, wot_ref, g_ref, y_ref, *,
                       d_model, chain):
    cols = [slice(c, c + chain) for c in range(0, x_ref.shape[0], chain)]
    wat, wbt, wot = wat_ref[...], wbt_ref[...], wot_ref[...]
    ga = [(oa_ref[0, :, c] * za_ref[0, :, c]).astype(BF16) for c in cols]
    gb = [(ob_ref[0, :, c] * zb_ref[0, :, c]).astype(BF16) for c in cols]
    ya = [_dot(wat, v) for v in ga]
    yb = [_dot(wbt, v) for v in gb]
    mixed = [(gm_ref[0, :d_model, c] * a + gm_ref[0, d_model:, c] * b).astype(BF16) for c, a, b in zip(cols, ya, yb)]
    outs = [_dot(wot, v) for v in mixed]
    outs = [o * lax.rsqrt(jnp.mean(o * o, axis=0, keepdims=True) + RMS_EPS) for o in outs]
    for c, o in zip(cols, outs):
        y_ref[c, :] = x_ref[c, :] + o.T * g_ref[...]


def _mix_prompt(x2d, oat, zat, obt, zbt, gmt, wat, wbt, wot, g_post, *, batch, seq, tm, chain=256):
    m, d_model = x2d.shape
    assert seq % tm == 0 and tm % chain == 0
    nt = seq // tm
    row = lambda i: (i, 0)
    full = lambda i: (0, 0)
    tr = lambda i: (i // nt, 0, i % nt)
    return pl.pallas_call(
        functools.partial(_mix_prompt_kernel, d_model=d_model, chain=chain),
        grid=(m // tm,),
        in_specs=[pl.BlockSpec((tm, d_model), row), pl.BlockSpec((1, NSA_WIDTH, tm), tr),
                  pl.BlockSpec((1, NSA_WIDTH, tm), tr), pl.BlockSpec((1, SB_WIDTH, tm), tr),
                  pl.BlockSpec((1, SB_WIDTH, tm), tr), pl.BlockSpec((1, N_MERGE * d_model, tm), tr),
                  pl.BlockSpec((d_model, NSA_WIDTH), full), pl.BlockSpec((d_model, SB_WIDTH), full),
                  pl.BlockSpec((d_model, d_model), full), pl.BlockSpec((1, d_model), full)],
        out_specs=pl.BlockSpec((tm, d_model), row),
        out_shape=jax.ShapeDtypeStruct((m, d_model), F32),
        compiler_params=pltpu.CompilerParams(dimension_semantics=("arbitrary",), vmem_limit_bytes=VMEM_LIMIT),
        name="mix_prompt",
    )(x2d, oat, zat, obt, zbt, gmt, wat, wbt, wot, g_post)


def _mix_sample_kernel(x_ref, oa_ref, za_ref, ob_ref, zb_ref, gm_ref, wat_ref, wbt_ref, wot_ref, g_ref, y_ref, *,
                       d_model):
    ya = _dot_nt((oa_ref[...] * za_ref[...]).astype(BF16), wat_ref[...])
    yb = _dot_nt((ob_ref[...] * zb_ref[...]).astype(BF16), wbt_ref[...])
    gm = gm_ref[...]
    mixed = gm[:, :d_model] * ya + gm[:, d_model:] * yb
    out = _dot_nt(mixed.astype(BF16), wot_ref[...])
    y_ref[...] = x_ref[...] + _rms_scale(out, g_ref[...])


def _mix_sample(x2d, o_a, za, o_b, zb, gm, wat, wbt, wot, g_post):
    m, d_model = x2d.shape
    args = (x2d, o_a, za, o_b, zb, gm, wat, wbt, wot, g_post)
    return pl.pallas_call(
        functools.partial(_mix_sample_kernel, d_model=d_model),
        grid=(1,),
        in_specs=[pl.BlockSpec(a.shape, lambda i: (0, 0)) for a in args],
        out_specs=pl.BlockSpec((m, d_model), lambda i: (0, 0)),
        out_shape=jax.ShapeDtypeStruct((m, d_model), F32),
        compiler_params=pltpu.CompilerParams(dimension_semantics=("arbitrary",), vmem_limit_bytes=VMEM_LIMIT),
        name="mix_sample",
    )(*args)


def _head_pad(q):
    q2 = jnp.concatenate([q, q], axis=1)
    row = lax.broadcasted_iota(jnp.int32, q2.shape, 0)
    lane = lax.broadcasted_iota(jnp.int32, q2.shape, 1)
    return jnp.where(row // NSA_GROUP == lane // HEAD_DIM, q2, 0.0)


def _stream_kernel(pt_ref, qd_ref, qp_ref, k_ref, v_ref, qa_ref, wb_ref, *refs, ppg, tq, tk, sub, units,
                   steps_per_batch, nbp):
    page_refs = refs[:ppg]
    cmp_refs = refs[ppg:2 * ppg]
    od_ref, op_ref, ocmp_ref, sel_ref = refs[2 * ppg:2 * ppg + 4]
    drun_ref, dacc_ref, runs_ref, acc_ref, pool_ref, state_ref = refs[2 * ppg + 4:]
    s = pl.program_id(1)
    s_lin = pl.program_id(0) * pl.num_programs(1) + s

    @pl.when(s == 0)
    def _():
        drun_ref[...] = jnp.zeros_like(drun_ref)
        dacc_ref[...] = jnp.zeros_like(dacc_ref)

    @pl.when(s_lin == 0)
    def _():
        state_ref[0] = 0
        state_ref[1] = 0

    hrow = lax.broadcasted_iota(jnp.int32, (SB_HEADS, SB_WIDTH), 0)
    hlane = lax.broadcasted_iota(jnp.int32, (SB_HEADS, SB_WIDTH), 1)
    diag_blocks = hrow == hlane // HEAD_DIM

    def decode_scores():
        q = qd_ref[0]
        qbd = jnp.where(diag_blocks, jnp.concatenate([q] * SB_HEADS, axis=1), 0.0).astype(BF16)
        z = jnp.concatenate([_dot(qbd, pr[0, :SB_WIDTH, :].astype(BF16)) for pr in page_refs], axis=0)
        sp = _softplus(z)
        r = lax.broadcasted_iota(jnp.int32, (PAGE_SIZE, PAGE_SIZE), 0)
        c = lax.broadcasted_iota(jnp.int32, (PAGE_SIZE, PAGE_SIZE), 1)
        lower = jnp.where(r > c, 1.0, 0.0).astype(BF16)
        hi, lo = _split_bf16(sp)
        return z, sp, _dot(hi, lower) + _dot(lo, lower)

    def decode_update(z, sp, drop):
        tot = jnp.sum(sp, axis=1, keepdims=True)
        run = drun_ref[...]
        dacc = dacc_ref[...]
        for n in range(ppg):
            sl = slice(n * SB_HEADS, (n + 1) * SB_HEADS)
            a = jnp.exp(z[sl] - sp[sl] - drop[sl] - run)
            dacc = dacc + _dot_nt(a.astype(BF16), page_refs[n][0, SB_WIDTH:, :].astype(BF16))
            run = run + tot[sl]
        drun_ref[...] = run
        dacc_ref[...] = dacc

    def pool_compressed():
        group = LANES // (ppg * BLOCKS_PER_PAGE)
        slot = s % group
        chunk = s // group
        xk = jnp.concatenate([pr[0, :KV_PAIR, :].astype(BF16) for pr in cmp_refs], axis=1)
        xv = jnp.concatenate([pr[0, KV_PAIR:, :].astype(BF16) for pr in cmp_refs], axis=1)
        part = jnp.concatenate([_dot(xk, wb_ref[0, slot]), _dot(xv, wb_ref[1, slot])], axis=0)
        pool_ref[chunk] = jnp.where(slot == 0, 0.0, pool_ref[chunk]) + part

    pairs = SB_HEADS // 2
    lanes = 2 * tq
    diag = tq // tk
    per_batch = diag * units * (units + 1) // 2
    local = s_lin % steps_per_batch
    n_steps = ((local + 1) * per_batch) // steps_per_batch - (local * per_batch) // steps_per_batch

    ur = lax.broadcasted_iota(jnp.int32, (sub, 2 * sub), 0)
    uc = lax.broadcasted_iota(jnp.int32, (sub, 2 * sub), 1) % sub
    upper2 = jnp.where(uc > ur, 1.0, 0.0).astype(BF16)
    z0 = jnp.zeros((HEAD_DIM, tq), BF16)

    def later_sum(sp, run):
        parts = []
        for hb in reversed(range(tk // sub)):
            blk = sp[hb * sub:(hb + 1) * sub]
            parts.append(_dot(upper2, jnp.concatenate(_split_bf16(blk), axis=0)) + run)
            run = run + jnp.sum(blk, axis=0, keepdims=True)
        return jnp.concatenate(parts[::-1], axis=0), run

    def tile_step(with_decode):
        i = state_ref[0]
        j = state_ref[1]
        n_t = (i + 1) * diag
        off = pl.multiple_of((n_t - 1 - j) * tk, tk)
        qoff = pl.multiple_of(i * tq, tq)

        @pl.when(j == 0)
        def _():
            runs_ref[...] = jnp.zeros_like(runs_ref)
            acc_ref[...] = jnp.zeros_like(acc_ref)

        def compute(masked):
            if with_decode:
                dz, dsp, ddrop = decode_scores()
                pool_compressed()
            kk = k_ref[0, pl.ds(off, tk), :]
            qpads = []
            for jp in range(pairs):
                qt = qp_ref[0, jp * LANES:(jp + 1) * LANES, pl.ds(qoff, tq)]
                qpads.append(jnp.concatenate([jnp.concatenate([qt[:HEAD_DIM], z0], axis=0),
                                              jnp.concatenate([z0, qt[HEAD_DIM:]], axis=0)], axis=1))
            if masked:
                pos = qoff + lax.broadcasted_iota(jnp.int32, (1, lanes), 1) % tq
                mask = (off + lax.broadcasted_iota(jnp.int32, (tk, 1), 0)) < pos
            zs = [_dot(kk[:, jp * LANES:(jp + 1) * LANES], qpads[jp]) for jp in range(pairs)]
            sps = [jnp.where(mask, _softplus(zz), 0.0) if masked else _softplus(zz) for zz in zs]
            later = [later_sum(spj, runs_ref[jp]) for jp, spj in enumerate(sps)]
            drops = [d for d, _ in later]
            if with_decode:
                decode_update(dz, dsp, ddrop)
            ws = [jnp.exp(zz - spj - dr) for zz, spj, dr in zip(zs, sps, drops)]
            if masked:
                ws = [jnp.where(mask, w, 0.0) for w in ws]
            for h in range(SB_HEADS):
                w = ws[h // 2][:, (h % 2) * tq:(h % 2 + 1) * tq].astype(BF16)
                acc_ref[h] = acc_ref[h] + _dot(v_ref[0, h * HEAD_DIM:(h + 1) * HEAD_DIM, pl.ds(off, tk)], w)
            for jp, (_, run) in enumerate(later):
                runs_ref[jp] = run

        @pl.when(j < diag)
        def _():
            compute(True)

        @pl.when(j >= diag)
        def _():
            compute(False)

        last = j == n_t - 1

        @pl.when(last)
        def _():
            for h in range(SB_HEADS):
                op_ref[0, h * HEAD_DIM:(h + 1) * HEAD_DIM, pl.ds(qoff, tq)] = acc_ref[h]

        state_ref[1] = jnp.where(last, 0, j + 1)
        state_ref[0] = jnp.where(last, (i + 1) % units, i)

    @pl.when(n_steps > 0)
    def _():
        tile_step(True)

    @pl.when(n_steps == 0)
    def _():
        decode_update(*decode_scores())
        pool_compressed()

    def rest(_, carry):
        tile_step(False)
        return carry

    lax.fori_loop(1, n_steps, rest, 0)

    @pl.when(s == pl.num_programs(1) - 1)
    def _():
        od_ref[0] = jnp.sum(jnp.where(diag_blocks, dacc_ref[...], 0.0), axis=0, keepdims=True)
        _compressed_decode(qa_ref[0], pool_ref, ocmp_ref, sel_ref, nbp)


def _stream_attention(page_table_flat, q_dec, sb_cache_t, qbt, kbb, vbt, qa_dec, w_cmp, nsa_cache_t, *, dec_batch,
                      n_pages, batch, seq, ppg, tq=256, tk=256, sub=128):
    steps = n_pages // ppg
    assert tq % tk == 0 and tk % sub == 0 and seq % tq == 0 and (dec_batch * steps) % batch == 0
    assert LANES % (ppg * BLOCKS_PER_PAGE) == 0
    spb = dec_batch * steps // batch
    nbp = n_pages * BLOCKS_PER_PAGE
    wb = _decode_pool_weights(w_cmp, ppg)

    def page_map(n):
        return lambda b, s, pt: (pt[b * n_pages + n_pages - 1 - (s * ppg + n)], 0, 0)

    def cmp_map(n):
        return lambda b, s, pt: (pt[b * n_pages + s * ppg + n], 0, 0)

    prow = lambda b, s, pt: ((b * steps + s) // spb, 0, 0)
    sample = lambda b, s, pt: (b, 0, 0)
    grid_spec = pltpu.PrefetchScalarGridSpec(
        num_scalar_prefetch=1, grid=(dec_batch, steps),
        in_specs=[pl.BlockSpec((1, SB_HEADS, HEAD_DIM), sample),
                  pl.BlockSpec((1, SB_WIDTH, seq), prow), pl.BlockSpec((1, seq, SB_WIDTH), prow),
                  pl.BlockSpec((1, SB_WIDTH, seq), prow),
                  pl.BlockSpec((1, NSA_HEADS, HEAD_DIM), sample),
                  pl.BlockSpec(wb.shape, lambda b, s, pt: (0, 0, 0, 0))]
        + [pl.BlockSpec((1, 2 * SB_WIDTH, PAGE_SIZE), page_map(n)) for n in range(ppg)]
        + [pl.BlockSpec((1, 2 * KV_PAIR, PAGE_SIZE), cmp_map(n)) for n in range(ppg)],
        out_specs=[pl.BlockSpec((1, 1, SB_WIDTH), sample), pl.BlockSpec((1, SB_WIDTH, seq), prow),
                   pl.BlockSpec((1, NSA_HEADS, HEAD_DIM), sample),
                   pl.BlockSpec((1, NSA_KV_HEADS, NSA_TOPK, LANES), lambda b, s, pt: (b, 0, 0, 0))],
        scratch_shapes=[pltpu.VMEM((SB_HEADS, 1), F32), pltpu.VMEM((SB_HEADS, SB_WIDTH), F32),
                        pltpu.VMEM((SB_HEADS // 2, 1, 2 * tq), F32), pltpu.VMEM((SB_HEADS, HEAD_DIM, tq), F32),
                        pltpu.VMEM((pl.cdiv(nbp, LANES), 2 * KV_PAIR, LANES), F32),
                        pltpu.SMEM((2,), jnp.int32)])
    return pl.pallas_call(
        functools.partial(_stream_kernel, ppg=ppg, tq=tq, tk=tk, sub=sub, units=seq // tq, steps_per_batch=spb,
                          nbp=nbp),
        grid_spec=grid_spec,
        out_shape=[jax.ShapeDtypeStruct((dec_batch, 1, SB_WIDTH), F32),
                   jax.ShapeDtypeStruct((batch, SB_WIDTH, seq), F32),
                   jax.ShapeDtypeStruct((dec_batch, NSA_HEADS, HEAD_DIM), F32),
                   jax.ShapeDtypeStruct((dec_batch, NSA_KV_HEADS, NSA_TOPK, LANES), jnp.int32)],
        compiler_params=pltpu.CompilerParams(dimension_semantics=("arbitrary", "arbitrary"),
                                             vmem_limit_bytes=VMEM_LIMIT),
        name="stream_attention",
    )(page_table_flat, q_dec, qbt, kbb, vbt, qa_dec, wb, *([sb_cache_t] * ppg), *([nsa_cache_t] * ppg))


def _compressed_decode(q, pool_ref, ocmp_ref, sel_ref, nbp):
    n_chunks = pool_ref.shape[0]
    nl = n_chunks * LANES
    qpad = _head_pad(q).astype(BF16)
    pooled = jnp.concatenate([pool_ref[ch] for ch in range(n_chunks)], axis=1)
    blk_l = lax.broadcasted_iota(jnp.int32, (1, nl), 1)
    live = blk_l < nbp
    sc = jnp.where(live, _dot(qpad, pooled[:KV_PAIR].astype(BF16)), NEG_INF)
    e = jnp.where(live, jnp.exp(sc - jnp.max(sc, axis=1, keepdims=True)), 0.0)
    p = e / jnp.sum(e, axis=1, keepdims=True)
    o = _dot_nt(p.astype(BF16), pooled[KV_PAIR:].astype(BF16))
    hrow = lax.broadcasted_iota(jnp.int32, o.shape, 0)
    ocmp_ref[0] = jnp.where(hrow < NSA_GROUP, o, pltpu.roll(o, HEAD_DIM, 1))[:, :HEAD_DIM]

    blk_s = lax.broadcasted_iota(jnp.int32, (nl, 1), 0)
    eye = lax.broadcasted_iota(jnp.int32, (nl, nl), 0) == lax.broadcasted_iota(jnp.int32, (nl, nl), 1)
    n_sel = NSA_TOPK - 1
    kslot = lax.broadcasted_iota(jnp.int32, (NSA_TOPK, nl), 0)
    for gi in range(NSA_KV_HEADS):
        imp = jnp.sum(p[gi * NSA_GROUP:(gi + 1) * NSA_GROUP], axis=0, keepdims=True)
        forced = (blk_l == 0) | (blk_l == nbp - 1)
        srow = jnp.where(live, jnp.where(forced, FORCE_SCORE, imp), NEG_INF)
        scol = jnp.sum(jnp.where(eye, srow, 0.0), axis=1, keepdims=True)
        beats = (scol > srow) | ((scol == srow) & (blk_s < blk_l))
        rank = jnp.sum(jnp.where(beats, 1.0, 0.0), axis=0, keepdims=True)
        sel = rank < n_sel
        selcol = jnp.sum(jnp.where(eye & sel, 1.0, 0.0), axis=1, keepdims=True) > 0.5
        rank_sel = jnp.sum(jnp.where(selcol & (blk_s < blk_l), 1.0, 0.0), axis=0, keepdims=True)
        onehot = sel & (rank_sel.astype(jnp.int32) == kslot)
        idx = jnp.sum(jnp.where(onehot, blk_l, 0), axis=1, keepdims=True)
        sel_ref[0, gi] = jnp.broadcast_to(idx, (NSA_TOPK, LANES))


def _decode_pool_weights(w_cmp, ppg):
    per_step = ppg * BLOCKS_PER_PAGE
    group = LANES // per_step
    k = jnp.arange(ppg * PAGE_SIZE)
    target = (k // PAGE_SIZE) * BLOCKS_PER_PAGE + (k % PAGE_SIZE) // NSA_BLOCK
    hit = jnp.arange(LANES)[None, None, :] == (jnp.arange(group)[:, None, None] * per_step + target[None, :, None])
    w_row = jnp.tile(w_cmp, (1, ppg * BLOCKS_PER_PAGE))
    return jnp.where(hit[None], w_row[:, None, :, None], 0.0).astype(BF16)


def _nsa_sel_decode_kernel(sel_ref, pt_ref, q_ref, ocmp_ref, ga_ref, new_ref, winp_ref, *refs, n_blk):
    blk_refs = refs[:NSA_KV_HEADS * n_blk]
    o_ref, wino_ref = refs[NSA_KV_HEADS * n_blk:]
    b = pl.program_id(0)
    qb = _head_pad(q_ref[0]).astype(BF16)
    new = new_ref[0]
    hrow = lax.broadcasted_iota(jnp.int32, (NSA_HEADS, KV_PAIR), 0)
    top = hrow < NSA_GROUP

    def fold(o):
        return jnp.where(top, o, pltpu.roll(o, HEAD_DIM, 1))[:, :HEAD_DIM]

    def softmax(sc, mask):
        sc = jnp.where(mask, sc, NEG_INF)
        e = jnp.where(mask, jnp.exp(sc - jnp.max(sc, axis=1, keepdims=True)), 0.0)
        return e / jnp.sum(e, axis=1, keepdims=True)

    lane = lax.broadcasted_iota(jnp.int32, (1, PAGE_SIZE), 1)
    first = lax.broadcasted_iota(jnp.int32, (KV_PAIR, PAGE_SIZE), 1) == 0
    k_new = jnp.where(first, new[2 * KV_PAIR:3 * KV_PAIR], 0.0).astype(BF16)
    v_new = jnp.where(first, new[3 * KV_PAIR:4 * KV_PAIR], 0.0).astype(BF16)
    o_g = []
    for gi in range(NSA_KV_HEADS):
        scs, masks = [], []
        for n in range(n_blk):
            half = sel_ref[(b * NSA_KV_HEADS + gi) * NSA_TOPK + n] % BLOCKS_PER_PAGE
            scs.append(_dot(qb, blk_refs[gi * n_blk + n][0, :KV_PAIR, :].astype(BF16)))
            masks.append(lane // NSA_BLOCK == half)
        scs.append(_dot(qb, k_new))
        masks.append(lane == 0)
        p = softmax(jnp.concatenate(scs, axis=1), jnp.concatenate(masks, axis=1)).astype(BF16)
        o = _dot_nt(p[:, n_blk * PAGE_SIZE:], v_new)
        for n in range(n_blk):
            o = o + _dot_nt(p[:, n * PAGE_SIZE:(n + 1) * PAGE_SIZE],
                            blk_refs[gi * n_blk + n][0, KV_PAIR:, :].astype(BF16))
        o_g.append(o)
    o_slc = fold(jnp.where(top, o_g[0], o_g[1]))

    wp = winp_ref[0]
    w = wp.shape[1]
    wl = lax.broadcasted_iota(jnp.int32, wp.shape, 1)
    shifted = jnp.where(wl == w - 1, new[4 * KV_PAIR:], pltpu.roll(wp, w - 1, 1))
    wino_ref[0] = shifted
    sc = _dot(qb, shifted[:KV_PAIR].astype(BF16))
    p = softmax(sc, jnp.full((1, w), True)).astype(BF16)
    o_win = fold(_dot_nt(p, shifted[KV_PAIR:].astype(BF16)))

    ga = ga_ref[0]
    o_ref[0] = ga[0] * ocmp_ref[0] + ga[1] * o_slc + ga[2] * o_win


def _nsa_sel_decode(sel_flat, page_table_flat, q_a, o_cmp, ga, new_kv, win_past_t, cache_t, *, dec_batch, n_pages):
    n_blk = NSA_TOPK - 1
    w = win_past_t.shape[2]

    def blk_map(gi, n):
        def f(b, sel, pt):
            blk = sel[(b * NSA_KV_HEADS + gi) * NSA_TOPK + n]
            return (pt[b * n_pages + blk // BLOCKS_PER_PAGE], 1, 0)
        return f

    grid_spec = pltpu.PrefetchScalarGridSpec(
        num_scalar_prefetch=2, grid=(dec_batch,),
        in_specs=[pl.BlockSpec((1, NSA_HEADS, HEAD_DIM), lambda b, sel, pt: (b, 0, 0)),
                  pl.BlockSpec((1, NSA_HEADS, HEAD_DIM), lambda b, sel, pt: (b, 0, 0)),
                  pl.BlockSpec((1, NSA_BRANCHES, NSA_HEADS, 1), lambda b, sel, pt: (b, 0, 0, 0)),
                  pl.BlockSpec((1, KVA_WIDTH, 1), lambda b, sel, pt: (b, 0, 0)),
                  pl.BlockSpec((1, 2 * KV_PAIR, w), lambda b, sel, pt: (b, 0, 0))]
        + [pl.BlockSpec((1, 2 * KV_PAIR, PAGE_SIZE), blk_map(gi, n))
           for gi in range(NSA_KV_HEADS) for n in range(n_blk)],
        out_specs=[pl.BlockSpec((1, NSA_HEADS, HEAD_DIM), lambda b, sel, pt: (b, 0, 0)),
                   pl.BlockSpec((1, 2 * KV_PAIR, w), lambda b, sel, pt: (b, 0, 0))])
    return pl.pallas_call(
        functools.partial(_nsa_sel_decode_kernel, n_blk=n_blk),
        grid_spec=grid_spec,
        out_shape=[jax.ShapeDtypeStruct((dec_batch, NSA_HEADS, HEAD_DIM), F32),
                   jax.ShapeDtypeStruct((dec_batch, 2 * KV_PAIR, w), F32)],
        compiler_params=pltpu.CompilerParams(dimension_semantics=("arbitrary",), vmem_limit_bytes=VMEM_LIMIT),
        name="nsa_sel_decode",
    )(sel_flat, page_table_flat, q_a, o_cmp, ga, new_kv, win_past_t, *([cache_t] * (NSA_KV_HEADS * n_blk)))


def _feature_major(a, lead):
    nl = len(lead)
    t = jnp.moveaxis(a, nl, -1)
    return t.reshape(lead + (-1, a.shape[nl]))


def _time_major(a_t, feat_shape):
    lead, _, time = a_t.shape
    return jnp.moveaxis(a_t.reshape((lead,) + feat_shape + (time,)), -1, 1)


def _layer(x_prompt, x_sample, nsa_cache, win_cache, sb_cache, page_table, w_in, w_cmp, w_a, w_b, w_o, g_pre, g_post):
    batch, seq, d_model = x_prompt.shape
    dec_batch, dec_seq, _ = x_sample.shape
    n_pages = page_table.shape[1]
    past_len = n_pages * PAGE_SIZE
    n_phys = nsa_cache.shape[0]
    tm, ppg = 256, 16
    assert dec_seq == 1 and seq % tm == 0 and seq >= NSA_WINDOW and past_len >= NSA_WINDOW
    assert n_pages % ppg == 0 and past_len // NSA_BLOCK >= NSA_TOPK
    assert win_cache.shape[1] == NSA_WINDOW and nsa_cache.shape[1] == PAGE_SIZE

    wt_perm = _permute_w_in_t(w_in, d_model)
    wat, wbt, wot = w_a.T.astype(BF16), w_b.T.astype(BF16), w_o.T.astype(BF16)
    g_pre2, g_post2 = g_pre.reshape(1, d_model), g_post.reshape(1, d_model)
    nsa_feat = (4, NSA_KV_HEADS, HEAD_DIM)
    win_feat = (2, NSA_KV_HEADS, HEAD_DIM)
    sb_feat = (2, SB_HEADS, HEAD_DIM)

    xp = x_prompt.reshape(batch * seq, d_model)
    cos, sin = _rope_angles(jnp.arange(seq, dtype=jnp.int32))
    wp_prompt = _pool_weights(w_cmp, 1, tm // NSA_BLOCK, SUBLANES).reshape(2 * SUBLANES, tm)
    (nsa_t, win_t, sb_t, za_t, zb_t, gm_t, qa_t, ga_t, kb, v_t, pooled, qb_t, kbb, vb_t) = _project_prompt(
        xp, g_pre2, wt_perm, cos.T, sin.T, wp_prompt, batch=batch, seq=seq, tm=2 * tm, chain=tm)
    xs = x_sample.reshape(dec_batch, d_model)
    tabs_s = _rope_lane_tables(jnp.full((dec_batch,), past_len, jnp.int32))
    (nsa_s, win_s, sb_s, za_s, zb_s, gm_s, qa_s, ga_s, qb_s) = _project_sample(xs, g_pre2, wt_perm, tabs_s)
    pt_flat = page_table.reshape(-1)
    sb_cache_t = _feature_major(sb_cache, (n_phys,))
    nsa_cache_t = _feature_major(nsa_cache, (n_phys,))
    win_cache_t = _feature_major(win_cache, (dec_batch,))

    qa3 = qa_s.reshape(dec_batch, NSA_HEADS, HEAD_DIM)
    o_b_s, ob_t, o_cmp_s, sel = _stream_attention(
        pt_flat, qb_s.reshape(dec_batch, SB_HEADS, HEAD_DIM), sb_cache_t, qb_t, kbb.reshape(batch, seq, SB_WIDTH), vb_t,
        qa3, w_cmp, nsa_cache_t, dec_batch=dec_batch, n_pages=n_pages, batch=batch, seq=seq, ppg=ppg)

    oa_t = _nsa_prompt(qa_t, kb.reshape(batch, seq, 2 * KV_PAIR), v_t,
                       pooled.reshape(batch, seq // NSA_BLOCK, 2 * KV_PAIR), ga_t, batch=batch, seq=seq)
    y_prompt = _mix_prompt(xp, oa_t, za_t, ob_t, zb_t, gm_t, wat, wbt, wot, g_post2,
                           batch=batch, seq=seq, tm=4 * tm).reshape(batch, seq, d_model)
    nsa_kv_prompt = _time_major(nsa_t, nsa_feat)
    win_kv_prompt = _time_major(win_t[:, :, seq - NSA_WINDOW:], win_feat)
    sb_kv_prompt = _time_major(sb_t, sb_feat)

    ga3 = ga_s[:, :GATE_A].reshape(dec_batch, NSA_BRANCHES, NSA_HEADS, 1)
    new_kv = jnp.concatenate([nsa_s, win_s], axis=1).reshape(dec_batch, KVA_WIDTH, 1)
    o_a_s, win_out_t = _nsa_sel_decode(sel[:, :, :, 0].reshape(-1), pt_flat, qa3, o_cmp_s, ga3, new_kv,
                                       win_cache_t, nsa_cache_t, dec_batch=dec_batch, n_pages=n_pages)
    y_sample = _mix_sample(xs, o_a_s.reshape(dec_batch, NSA_WIDTH), za_s, o_b_s.reshape(dec_batch, SB_WIDTH), zb_s,
                           gm_s, wat, wbt, wot, g_post2).reshape(dec_batch, 1, d_model)
    nsa_kv_sample = nsa_s.reshape((dec_batch, 1) + nsa_feat)
    win_kv_sample = _time_major(win_out_t, win_feat)
    sb_kv_sample = sb_s.reshape((dec_batch, 1) + sb_feat)
    return (y_prompt, y_sample, nsa_kv_prompt, win_kv_prompt, sb_kv_prompt, nsa_kv_sample, win_kv_sample,
            sb_kv_sample)


def kernel(x_prompt, x_sample, cache_nsa_kv, cache_nsa_win_kv, cache_sb_kv, page_table, w_in, w_cmp, w_branch_a,
           w_branch_b, w_out, g_pre, g_post):
    hp, hs = x_prompt, x_sample
    caches = [[] for _ in range(6)]
    for layer in range(w_in.shape[0]):
        outs = _layer(hp, hs, cache_nsa_kv[layer], cache_nsa_win_kv[layer], cache_sb_kv[layer], page_table,
                      w_in[layer], w_cmp[layer], w_branch_a[layer], w_branch_b[layer], w_out[layer],
                      g_pre[layer], g_post[layer])
        hp, hs = outs[0], outs[1]
        for acc, o in zip(caches, outs[2:]):
            acc.append(o)
    return (hp, hs) + tuple(jnp.stack(c) for c in caches)
```
